```python
import math
import jax, jax.numpy as jnp
from jax import lax
import numpy as np

D_MODEL = 1024
BATCH = 4
SEQ = 4096
DEPTH = 2

N_MIXERS = 2
EPS = 1e-6
ATT_Q_HEADS = 16
ATT_KV_HEADS = 4
ATT_GROUP = ATT_Q_HEADS // ATT_KV_HEADS
ATT_HEAD_DIM = 64
WINDOW = 128
ATT_BLOCK = 128
REL_BUCKETS = 32
REL_MAX_DIST = 128
RET_HEADS = 4
RET_QK_DIM = D_MODEL // RET_HEADS
RET_V_DIM = 2 * RET_QK_DIM
RET_CHUNK = 128
ROPE_BASE = 10000.0
FFN_HIDDEN = -(-8 * D_MODEL // (3 * 256)) * 256

kernel_name = 'hybrid_swa_sink_retention_encoder'


def rmsnorm(x, g):
    xf = x.astype(jnp.float32)
    y = xf * lax.rsqrt(jnp.mean(xf * xf, axis=-1, keepdims=True) + EPS)
    return (y * g.astype(jnp.float32)).astype(x.dtype)


def t5_bucket(rel):
    nb = REL_BUCKETS // 2
    ret = jnp.where(rel > 0, nb, 0)
    n = jnp.abs(rel)
    max_exact = nb // 2
    nf = jnp.maximum(n, 1).astype(jnp.float32)
    large = max_exact + (jnp.log(nf / max_exact) / math.log(REL_MAX_DIST / max_exact)
                         * (nb - max_exact)).astype(jnp.int32)
    large = jnp.minimum(large, nb - 1)
    return ret + jnp.where(n < max_exact, n, large)


def windowed_gqa_sink(h, w_qkv, w_o, sink, rel_bias):
    B, S, _ = h.shape
    L = ATT_BLOCK
    nb = S // L
    dh = ATT_HEAD_DIM
    qkv = h @ w_qkv
    q, k, v = jnp.split(qkv, [ATT_Q_HEADS * dh, (ATT_Q_HEADS + ATT_KV_HEADS) * dh], axis=-1)
    q = q.reshape(B, nb, L, ATT_KV_HEADS, ATT_GROUP, dh)
    pad = ((0, 0), (L, L), (0, 0), (0, 0))
    kp = jnp.pad(k.reshape(B, S, ATT_KV_HEADS, dh), pad).reshape(B, nb + 2, L, ATT_KV_HEADS, dh)
    vp = jnp.pad(v.reshape(B, S, ATT_KV_HEADS, dh), pad).reshape(B, nb + 2, L, ATT_KV_HEADS, dh)
    k_band = jnp.concatenate([kp[:, :-2], kp[:, 1:-1], kp[:, 2:]], axis=2)
    v_band = jnp.concatenate([vp[:, :-2], vp[:, 1:-1], vp[:, 2:]], axis=2)
    scores = jnp.einsum('bnqhgd,bnkhd->bnhgqk', q, k_band).astype(jnp.float32) * (dh ** -0.5)
    t = jnp.arange(L, dtype=jnp.int32)
    j = jnp.arange(3 * L, dtype=jnp.int32)
    rel = j[None, :] - L - t[:, None]
    bias = rel_bias.astype(jnp.float32)[t5_bucket(rel)]
    bias = jnp.transpose(bias, (2, 0, 1)).reshape(ATT_KV_HEADS, ATT_GROUP, L, 3 * L)
    in_win = jnp.abs(rel) <= WINDOW
    k_abs = jnp.arange(nb, dtype=jnp.int32)[:, None] * L - L + j[None, :]
    in_range = (k_abs >= 0) & (k_abs < S)
    mask = in_win[None] & in_range[:, None, :]
    logits = jnp.where(mask[None, :, None, None], scores + bias[None, None], jnp.float32(-1e30))
    sink_l = jnp.broadcast_to(sink.astype(jnp.float32).reshape(1, 1, ATT_KV_HEADS, ATT_GROUP, 1, 1),
                              logits.shape[:-1] + (1,))
    p = jax.nn.softmax(jnp.concatenate([logits, sink_l], axis=-1), axis=-1)[..., :-1]
    out = jnp.einsum('bnhgqk,bnkhd->bnqhgd', p.astype(v_band.dtype), v_band)
    return out.reshape(B, S, ATT_Q_HEADS * dh) @ w_o


def rotary(x):
    S, d = x.shape[1], x.shape[-1]
    inv = ROPE_BASE ** (-jnp.arange(0, d, 2, dtype=jnp.float32) / d)
    ang = jnp.arange(S, dtype=jnp.float32)[:, None] * inv[None]
    cos = jnp.cos(ang)[None, :, None, :]
    sin = jnp.sin(ang)[None, :, None, :]
    xf = x.astype(jnp.float32)
    x1, x2 = xf[..., : d // 2], xf[..., d // 2:]
    return jnp.concatenate([x1 * cos - x2 * sin, x1 * sin + x2 * cos], axis=-1).astype(x.dtype)


def retention_chunkwise(q, k, v, log_gamma, include_diag):
    B, S, H, dk = q.shape
    dv = v.shape[-1]
    L = RET_CHUNK
    C = S // L
    idx = jnp.arange(L, dtype=jnp.float32)
    diff = idx[:, None] - idx[None, :]
    keep = (diff >= 0) if include_diag else (diff > 0)
    decay_intra = jnp.where(keep[None], jnp.exp(log_gamma[:, None, None] * jnp.maximum(diff, 0.0)[None]), 0.0)
    xi = jnp.exp(log_gamma[:, None] * (idx + 1.0)[None])[None, :, :, None]
    zeta = jnp.exp(log_gamma[:, None] * (L - 1.0 - idx)[None])[None, :, :, None]
    chunk_decay = jnp.exp(log_gamma * L)[None, :, None, None]

    def to_chunks(a):
        return a.reshape(B, C, L, H, a.shape[-1]).transpose(1, 0, 3, 2, 4).astype(jnp.float32)

    def step(state, inp):
        qi, ki, vi = inp
        s = jnp.einsum('bhld,bhmd->bhlm', qi, ki) * decay_intra[None]
        inner = jnp.einsum('bhlm,bhme->bhle', s, vi)
        cross = jnp.einsum('bhld,bhde->bhle', qi * xi, state)
        new_state = state * chunk_decay + jnp.einsum('bhmd,bhme->bhde', ki * zeta, vi)
        return new_state, inner + cross

    state0 = jnp.zeros((B, H, dk, dv), jnp.float32)
    _, out = lax.scan(step, state0, (to_chunks(q), to_chunks(k), to_chunks(v)))
    return out.transpose(1, 0, 3, 2, 4).reshape(B, S, H, dv)


def bidir_retention(h, w_in, w_o, decay_logit_fwd, decay_logit_bwd):
    B, S, _ = h.shape
    H, dk, dv = RET_HEADS, RET_QK_DIM, RET_V_DIM
    proj = h @ w_in
    q, k, v, g = jnp.split(proj, [H * dk, 2 * H * dk, 2 * H * dk + H * dv], axis=-1)
    q = rotary(q.reshape(B, S, H, dk))
    k = rotary(k.reshape(B, S, H, dk) * (dk ** -0.5))
    v = v.reshape(B, S, H, dv)
    lg_f = jax.nn.log_sigmoid(decay_logit_fwd.astype(jnp.float32))
    lg_b = jax.nn.log_sigmoid(decay_logit_bwd.astype(jnp.float32))
    y_f = retention_chunkwise(q, k, v, lg_f, True)
    y_b = retention_chunkwise(q[:, ::-1], k[:, ::-1], v[:, ::-1], lg_b, False)[:, ::-1]
    y = y_f + y_b
    y = y * lax.rsqrt(jnp.mean(y * y, axis=-1, keepdims=True) + EPS)
    y = y.reshape(B, S, H * dv).astype(h.dtype)
    return (jax.nn.silu(g) * y) @ w_o


def swiglu(h, w_in, w_out):
    a, b = jnp.split(h @ w_in, 2, axis=-1)
    return (jax.nn.silu(a) * b) @ w_out


def setup_inputs(seed: int = 0) -> dict:
    key = jax.random.key(seed)
    ks = jax.random.split(key, 20)
    D = D_MODEL
    n_att = (DEPTH + N_MIXERS - 1) // N_MIXERS
    n_ret = DEPTH // N_MIXERS
    f32 = jnp.float32

    def w(k, shape, fan_in, scale=1.0):
        return jax.random.normal(k, shape, f32) * (scale * fan_in ** -0.5)

    att_qkv_w = (ATT_Q_HEADS + 2 * ATT_KV_HEADS) * ATT_HEAD_DIM
    ret_in_w = 2 * RET_HEADS * RET_QK_DIM + 2 * RET_HEADS * RET_V_DIM
    gam = 1.0 - np.exp(np.linspace(math.log(1 / 32), math.log(1 / 512), RET_HEADS))
    base_logit = jnp.asarray(np.log(gam / (1.0 - gam)), f32)
    return {
        'x': jax.random.normal(ks[0], (BATCH, SEQ, D), f32),
        'c': jax.random.normal(ks[1], (BATCH, D), f32),
        'rel_bias': jax.random.normal(ks[2], (REL_BUCKETS, ATT_Q_HEADS), f32) * 0.5,
        'att_w_qkv': w(ks[3], (n_att, D, att_qkv_w), D),
        'att_w_o': w(ks[4], (n_att, ATT_Q_HEADS * ATT_HEAD_DIM, D), ATT_Q_HEADS * ATT_HEAD_DIM),
        'att_sink': jax.random.normal(ks[5], (n_att, ATT_Q_HEADS), f32),
        'ret_w_in': w(ks[6], (n_ret, D, ret_in_w), D),
        'ret_w_o': w(ks[7], (n_ret, RET_HEADS * RET_V_DIM, D), RET_HEADS * RET_V_DIM),
        'ret_decay_fwd': base_logit[None] + 0.1 * jax.random.normal(ks[8], (n_ret, RET_HEADS), f32),
        'ret_decay_bwd': base_logit[None] + 0.1 * jax.random.normal(ks[9], (n_ret, RET_HEADS), f32),
        'ada_w': w(ks[10], (DEPTH, D, 6 * D), D, 0.5),
        'ada_b': 0.02 * jax.random.normal(ks[11], (DEPTH, 6 * D), f32),
        'mix_norm_pre': 1.0 + 0.05 * jax.random.normal(ks[12], (DEPTH, D), f32),
        'mix_norm_post': 1.0 + 0.05 * jax.random.normal(ks[13], (DEPTH, D), f32),
        'ffn_norm_pre': 1.0 + 0.05 * jax.random.normal(ks[14], (DEPTH, D), f32),
        'ffn_norm_post': 1.0 + 0.05 * jax.random.normal(ks[15], (DEPTH, D), f32),
        'ffn_w_in': w(ks[16], (DEPTH, D, 2 * FFN_HIDDEN), D),
        'ffn_w_out': w(ks[17], (DEPTH, FFN_HIDDEN, D), FFN_HIDDEN),
    }


def reference(x, c, rel_bias, att_w_qkv, att_w_o, att_sink, ret_w_in, ret_w_o,
              ret_decay_fwd, ret_decay_bwd, ada_w, ada_b, mix_norm_pre, mix_norm_post,
              ffn_norm_pre, ffn_norm_post, ffn_w_in, ffn_w_out):
    c_act = jax.nn.silu(c)
    for i in range(DEPTH):
        mod = (c_act @ ada_w[i] + ada_b[i])[:, None, :]
        sh1, sc1, g1, sh2, sc2, g2 = jnp.split(mod, 6, axis=-1)
        h = rmsnorm(x, mix_norm_pre[i]) * (1.0 + sc1) + sh1
        j = i // N_MIXERS
        if i % N_MIXERS == 0:
            y = windowed_gqa_sink(h, att_w_qkv[j], att_w_o[j], att_sink[j], rel_bias)
        else:
            y = bidir_retention(h, ret_w_in[j], ret_w_o[j], ret_decay_fwd[j], ret_decay_bwd[j])
        x = x + g1 * rmsnorm(y, mix_norm_post[i])
        h = rmsnorm(x, ffn_norm_pre[i]) * (1.0 + sc2) + sh2
        y = swiglu(h, ffn_w_in[i], ffn_w_out[i])
        x = x + g2 * rmsnorm(y, ffn_norm_post[i])
    return x
```

```python
import functools
import math

import jax
import jax.numpy as jnp
from jax import lax
from jax.experimental import pallas as pl
from jax.experimental.pallas import tpu as pltpu

F32 = jnp.float32
BF16 = jnp.bfloat16

EPS = 1e-6
NEG = -1e30

Q_HEADS = 16
KV_HEADS = 4
GROUP = Q_HEADS // KV_HEADS
HEAD_DIM = 64
ATT_BLOCK = 128
REL_BUCKETS = 32
RET_HEADS = 4
RET_CHUNK = 128
ROPE_BASE = 10000.0
FFN_CHUNK = 256
ROW_TILE = 512
VMEM_LIMIT = 56 * 1024 * 1024


def _silu(x):
    return x * (1.0 / (1.0 + jnp.exp(-x)))


def _rms(xf, g):
    ms = jnp.mean(xf * xf, axis=-1, keepdims=True)
    return (xf * lax.rsqrt(ms + EPS)) * g


def _resident(shape):
    zeros = (0,) * len(shape)
    return pl.BlockSpec(shape, lambda *_: zeros, pipeline_mode=pl.Buffered(1))


def _params(n_axes, vmem=VMEM_LIMIT):
    return pltpu.CompilerParams(
        dimension_semantics=("arbitrary",) * n_axes, vmem_limit_bytes=vmem)


def _ada_kernel(c_ref, w_ref, b_ref, o_ref):
    ca = _silu(c_ref[...])
    o_ref[0] = jnp.dot(ca.astype(BF16), w_ref[0].astype(BF16),
                       preferred_element_type=F32) + b_ref[0]


def _ada_mod(c_pad, ada_w, ada_b):
    depth, d, n = ada_w.shape
    rows = c_pad.shape[0]
    tn = 1536
    return pl.pallas_call(
        _ada_kernel,
        grid=(depth, n // tn),
        in_specs=[
            pl.BlockSpec((rows, d), lambda i, j: (0, 0)),
            pl.BlockSpec((1, d, tn), lambda i, j: (i, 0, j)),
            pl.BlockSpec((1, 1, tn), lambda i, j: (i, 0, j)),
        ],
        out_specs=pl.BlockSpec((1, rows, tn), lambda i, j: (i, 0, j)),
        out_shape=jax.ShapeDtypeStruct((depth, rows, n), F32),
        compiler_params=_params(2),
        name="ada_mod",
    )(c_pad, ada_w, ada_b.reshape(depth, 1, n))


def _mod_spec(layer, slot, tiles_per_batch, n_batch_rows):
    base = layer * n_batch_rows * 6 + slot

    def index(t):
        return (base + (t // tiles_per_batch) * 6, 0, 0)
    return index


def _qkv_kernel(x_ref, g_ref, sc_ref, sh_ref, w_ref, o_ref):
    h = _rms(x_ref[...], g_ref[...]) * (1.0 + sc_ref[0]) + sh_ref[0]
    o_ref[...] = jnp.dot(h.astype(BF16), w_ref[...],
                         preferred_element_type=F32).astype(BF16)


def _qkv_proj(x2d, g_pre, modv, w_qkv, seq, n_batch_rows):
    t, d = x2d.shape
    n = w_qkv.shape[1]
    tm = ROW_TILE
    tpb = seq // tm
    vec = lambda idx: pl.BlockSpec((1, 1, d), idx)
    return pl.pallas_call(
        _qkv_kernel,
        grid=(t // tm,),
        in_specs=[
            pl.BlockSpec((tm, d), lambda i: (i, 0)),
            _resident((1, d)),
            vec(_mod_spec(0, 1, tpb, n_batch_rows)),
            vec(_mod_spec(0, 0, tpb, n_batch_rows)),
            _resident((d, n)),
        ],
        out_specs=pl.BlockSpec((tm, n), lambda i: (i, 0)),
        out_shape=jax.ShapeDtypeStruct((t, n), BF16),
        compiler_params=_params(1),
        name="qkv_proj",
    )(x2d, g_pre, modv, modv, w_qkv)


def _att_tables(rb_ref, bias_ref, mask_ref):
    L = ATT_BLOCK
    t = lax.broadcasted_iota(jnp.int32, (L, 3 * L), 0)
    j = lax.broadcasted_iota(jnp.int32, (L, 3 * L), 1)
    rel = j - L - t
    n = jnp.abs(rel)
    large = jnp.full((L, 3 * L), 8, jnp.int32)
    for thr in (12, 16, 23, 32, 46, 64, 91):
        large = large + (n >= thr).astype(jnp.int32)
    bucket = jnp.where(rel > 0, 16, 0) + jnp.where(n < 8, n, large)
    in_win = n <= L
    mask_ref[0] = (in_win & (j >= L)).astype(F32)
    mask_ref[1] = in_win.astype(F32)
    mask_ref[2] = (in_win & (j < 2 * L)).astype(F32)
    bias_ref[...] = jnp.zeros(bias_ref.shape, F32)

    def body(b, carry):
        eq = bucket == b
        for h in range(KV_HEADS):
            for g in range(GROUP):
                bias_ref[h, g] = jnp.where(eq, rb_ref[b, h * GROUP + g], bias_ref[h, g])
        return carry
    lax.fori_loop(0, REL_BUCKETS, body, 0)


def _att_kernel(rb_ref, q_ref, kp_ref, kc_ref, kn_ref, sink_ref, o_ref, bias_ref, mask_ref):
    b = pl.program_id(0)
    n = pl.program_id(1)
    nb = pl.num_programs(1)

    @pl.when((b == 0) & (n == 0))
    def _():
        _att_tables(rb_ref, bias_ref, mask_ref)

    L = ATT_BLOCK
    dh = HEAD_DIM
    sel = jnp.where(n == 0, 0, jnp.where(n == nb - 1, 2, 1))
    valid = mask_ref[sel] > 0.5
    q = q_ref[0]
    kv = jnp.concatenate([kp_ref[0], kc_ref[0], kn_ref[0]], axis=0)
    outs = []
    for h in range(KV_HEADS):
        k_h = kv[:, h * dh:(h + 1) * dh]
        v_h = kv[:, (KV_HEADS + h) * dh:(KV_HEADS + h + 1) * dh]
        q4 = jnp.concatenate(
            [q[:, (h * GROUP + g) * dh:(h * GROUP + g + 1) * dh] for g in range(GROUP)], axis=0)
        s = lax.dot_general(q4, k_h, (((1,), (1,)), ((), ())), preferred_element_type=F32)
        s = s.reshape(GROUP, L, 3 * L)
        logit = jnp.where(valid[None], s + bias_ref[h], NEG)
        sk = sink_ref[h * GROUP:(h + 1) * GROUP][:, :, :1]
        m = jnp.maximum(jnp.max(logit, axis=-1, keepdims=True), sk)
        e = jnp.exp(logit - m)
        den = jnp.sum(e, axis=-1, keepdims=True) + jnp.exp(sk - m)
        pv = jnp.dot(e.reshape(GROUP * L, 3 * L).astype(BF16), v_h, preferred_element_type=F32)
        o = pv.reshape(GROUP, L, dh) / den
        outs += [o[g] for g in range(GROUP)]
    o_ref[0] = jnp.concatenate(outs, axis=-1).astype(BF16)


def _attention(qkv, rel_bias, sink_b):
    bsz, seq, n = qkv.shape
    L = ATT_BLOCK
    nb = seq // L
    dq = Q_HEADS * HEAD_DIM
    dkv = 2 * KV_HEADS * HEAD_DIM
    kv_col = dq // dkv
    return pl.pallas_call(
        _att_kernel,
        grid=(bsz, nb),
        in_specs=[
            pl.BlockSpec(memory_space=pltpu.SMEM),
            pl.BlockSpec((1, L, dq), lambda b, i: (b, i, 0)),
            pl.BlockSpec((1, L, dkv), lambda b, i: (b, jnp.maximum(i - 1, 0), kv_col)),
            pl.BlockSpec((1, L, dkv), lambda b, i: (b, i, kv_col)),
            pl.BlockSpec((1, L, dkv), lambda b, i: (b, jnp.minimum(i + 1, nb - 1), kv_col)),
            _resident((Q_HEADS, 1, 128)),
        ],
        out_specs=pl.BlockSpec((1, L, dq), lambda b, i: (b, i, 0)),
        out_shape=jax.ShapeDtypeStruct((bsz, seq, dq), BF16),
        scratch_shapes=[
            pltpu.VMEM((KV_HEADS, GROUP, L, 3 * L), F32),
            pltpu.VMEM((3, L, 3 * L), F32),
        ],
        compiler_params=_params(2),
        name="swa_attention",
    )(rel_bias, qkv, qkv, qkv, qkv, sink_b)


def _post_ffn_kernel(*refs, emit_next):
    (a_ref, x_ref, wo_ref, gpost_ref, g1_ref, gpre_ref, sc2_ref, sh2_ref,
     win_ref, wout_ref, gfpost_ref, g2_ref) = refs[:12]
    if emit_next:
        gn_ref, scn_ref, shn_ref, xo_ref, ho_ref = refs[12:]
    else:
        (xo_ref,) = refs[12:]
    hidden = wout_ref.shape[0]

    y = jnp.dot(a_ref[...], wo_ref[...], preferred_element_type=F32)
    x1 = x_ref[...] + g1_ref[0] * _rms(y, gpost_ref[...])
    h = _rms(x1, gpre_ref[...]) * (1.0 + sc2_ref[0]) + sh2_ref[0]
    hb = h.astype(BF16)
    acc = jnp.zeros(x1.shape, F32)
    for c in range(hidden // FFN_CHUNK):
        lo = c * FFN_CHUNK
        a = jnp.dot(hb, win_ref[:, lo:lo + FFN_CHUNK], preferred_element_type=F32)
        b = jnp.dot(hb, win_ref[:, hidden + lo:hidden + lo + FFN_CHUNK],
                    preferred_element_type=F32)
        act = (_silu(a) * b).astype(BF16)
        acc = acc + jnp.dot(act, wout_ref[lo:lo + FFN_CHUNK, :], preferred_element_type=F32)
    x2 = x1 + g2_ref[0] * _rms(acc, gfpost_ref[...])
    xo_ref[...] = x2
    if emit_next:
        hn = _rms(x2, gn_ref[...]) * (1.0 + scn_ref[0]) + shn_ref[0]
        ho_ref[...] = hn.astype(BF16)


def _post_ffn(a2d, x2d, w_o, g_post, g_pre, w_in, w_out, g_fpost, modv, layer,
              seq, n_batch_rows, g_next=None):
    t, d = x2d.shape
    kin = a2d.shape[1]
    tm = ROW_TILE
    tpb = seq // tm
    emit_next = g_next is not None
    vec = lambda lyr, slot: pl.BlockSpec((1, 1, d), _mod_spec(lyr, slot, tpb, n_batch_rows))
    row = lambda width: pl.BlockSpec((tm, width), lambda i: (i, 0))
    in_specs = [
        row(kin), row(d), _resident(w_o.shape), _resident((1, d)), vec(layer, 2),
        _resident((1, d)), vec(layer, 4), vec(layer, 3),
        _resident(w_in.shape), _resident(w_out.shape), _resident((1, d)), vec(layer, 5),
    ]
    args = [a2d, x2d, w_o, g_post, modv, g_pre, modv, modv, w_in, w_out, g_fpost, modv]
    out_specs = [row(d)]
    out_shape = [jax.ShapeDtypeStruct((t, d), F32)]
    if emit_next:
        in_specs += [_resident((1, d)), vec(layer + 1, 1), vec(layer + 1, 0)]
        args += [g_next, modv, modv]
        out_specs.append(row(d))
        out_shape.append(jax.ShapeDtypeStruct((t, d), BF16))
    return pl.pallas_call(
        functools.partial(_post_ffn_kernel, emit_next=emit_next),
        grid=(t // tm,),
        in_specs=in_specs,
        out_specs=out_specs,
        out_shape=out_shape,
        compiler_params=_params(1),
        name="post_ffn_next" if emit_next else "post_ffn",
    )(*args)


def _ret_in_kernel(h_ref, w_ref, cos_ref, sin_ref, zf_ref, zb_ref,
                   q_ref, k_ref, kzf_ref, kzb_ref, v_ref, sg_ref):
    hb = h_ref[...]
    tm = hb.shape[0]
    cos = cos_ref[...]
    sin = sin_ref[...]
    dk = w_ref.shape[0] // RET_HEADS
    half = dk // 2
    nqk = RET_HEADS * dk
    nv = (w_ref.shape[1] - 2 * nqk) // 2

    yq = jnp.dot(hb, w_ref[:, 0:nqk], preferred_element_type=F32)
    yk = jnp.dot(hb, w_ref[:, nqk:2 * nqk], preferred_element_type=F32)
    zf = zf_ref[...][None]
    zb = zb_ref[...][None]
    for h in range(RET_HEADS):
        for off in (0, half):
            lo = h * dk + off
            sign = -1.0 if off == 0 else 1.0
            other = h * dk + (half - off)
            rq = yq[:, lo:lo + half] * cos + sign * (yq[:, other:other + half] * sin)
            rk = yk[:, lo:lo + half] * cos + sign * (yk[:, other:other + half] * sin)
            q_ref[:, lo:lo + half] = rq.astype(BF16)
            k_ref[:, lo:lo + half] = rk.astype(BF16)
            rk3 = rk.reshape(tm // RET_CHUNK, RET_CHUNK, half)
            kzf_ref[:, lo:lo + half] = (rk3 * zf[:, :, lo:lo + half]).reshape(tm, half).astype(BF16)
            kzb_ref[:, lo:lo + half] = (rk3 * zb[:, :, lo:lo + half]).reshape(tm, half).astype(BF16)
    v_ref[...] = jnp.dot(hb, w_ref[:, 2 * nqk:2 * nqk + nv],
                         preferred_element_type=F32).astype(BF16)
    g = jnp.dot(hb, w_ref[:, 2 * nqk + nv:], preferred_element_type=F32)
    sg_ref[...] = _silu(g)


def _ret_in(h2d, w_in, cos, sin, zf, zb, seq):
    t, d = h2d.shape
    tm = ROW_TILE
    tpb = seq // tm
    nqk = RET_HEADS * (d // RET_HEADS)
    nv = (w_in.shape[1] - 2 * nqk) // 2
    row = lambda width: pl.BlockSpec((tm, width), lambda i: (i, 0))
    pos = pl.BlockSpec((tm, cos.shape[1]), lambda i: (i % tpb, 0))
    return pl.pallas_call(
        _ret_in_kernel,
        grid=(t // tm,),
        in_specs=[row(d), _resident(w_in.shape), pos, pos,
                  _resident(zf.shape), _resident(zb.shape)],
        out_specs=[row(nqk), row(nqk), row(nqk), row(nqk), row(nv), row(nv)],
        out_shape=[jax.ShapeDtypeStruct((t, nqk), BF16)] * 4
        + [jax.ShapeDtypeStruct((t, nv), BF16), jax.ShapeDtypeStruct((t, nv), F32)],
        compiler_params=_params(1),
        name="ret_in_proj",
    )(h2d, w_in, cos, sin, zf, zb)


def _state_step(st_ref, h, q_h, kz_h, v_h, xi, cd):
    st = st_ref[h]
    cross = jnp.dot(q_h, st.astype(BF16), preferred_element_type=F32) * xi
    upd = lax.dot_general(kz_h, v_h, (((0,), (0,)), ((), ())), preferred_element_type=F32)
    st_ref[h] = st * cd + upd
    return cross


def _ret_bwd_kernel(cd_ref, q_ref, kz_ref, v_ref, xi_ref, yb_ref, st_ref):
    @pl.when(pl.program_id(1) == 0)
    def _():
        st_ref[...] = jnp.zeros(st_ref.shape, F32)

    q = q_ref[0]
    kz = kz_ref[0]
    v = v_ref[0]
    dk = q.shape[1] // RET_HEADS
    dv = v.shape[1] // RET_HEADS
    for h in range(RET_HEADS):
        yb_ref[0, :, h * dv:(h + 1) * dv] = _state_step(
            st_ref, h, q[:, h * dk:(h + 1) * dk], kz[:, h * dk:(h + 1) * dk],
            v[:, h * dv:(h + 1) * dv], xi_ref[h], cd_ref[h])


def _ret_fwd_kernel(cd_ref, q_ref, k_ref, kz_ref, v_ref, sg_ref, yb_ref, xi_ref, dm_ref,
                    o_ref, st_ref):
    @pl.when(pl.program_id(1) == 0)
    def _():
        st_ref[...] = jnp.zeros(st_ref.shape, F32)

    q = q_ref[0]
    k = k_ref[0]
    kz = kz_ref[0]
    v = v_ref[0]
    dk = q.shape[1] // RET_HEADS
    dv = v.shape[1] // RET_HEADS
    for h in range(RET_HEADS):
        q_h = q[:, h * dk:(h + 1) * dk]
        v_h = v[:, h * dv:(h + 1) * dv]
        s = lax.dot_general(q_h, k[:, h * dk:(h + 1) * dk], (((1,), (1,)), ((), ())),
                            preferred_element_type=F32)
        inner = jnp.dot((s * dm_ref[h]).astype(BF16), v_h, preferred_element_type=F32)
        cross = _state_step(st_ref, h, q_h, kz[:, h * dk:(h + 1) * dk], v_h,
                            xi_ref[h], cd_ref[h])
        y = inner + cross + yb_ref[0, :, h * dv:(h + 1) * dv]
        yn = y * lax.rsqrt(jnp.mean(y * y, axis=-1, keepdims=True) + EPS)
        o_ref[0, :, h * dv:(h + 1) * dv] = (sg_ref[0, :, h * dv:(h + 1) * dv] * yn).astype(BF16)


def _retention(q, k, kzf, kzb, v, sg, xi_f, xi_b, dmat, cd_f, cd_b):
    bsz, seq, nqk = q.shape
    nv = v.shape[2]
    L = RET_CHUNK
    nc = seq // L
    dk = nqk // RET_HEADS
    dv = nv // RET_HEADS
    smem = pl.BlockSpec(memory_space=pltpu.SMEM)
    state = pltpu.VMEM((RET_HEADS, dk, dv), F32)
    rev = lambda width: pl.BlockSpec((1, L, width), lambda b, j: (b, nc - 1 - j, 0))
    fwd = lambda width: pl.BlockSpec((1, L, width), lambda b, j: (b, j, 0))

    yb = pl.pallas_call(
        _ret_bwd_kernel,
        grid=(bsz, nc),
        in_specs=[smem, rev(nqk), rev(nqk), rev(nv), _resident(xi_b.shape)],
        out_specs=rev(nv),
        out_shape=jax.ShapeDtypeStruct((bsz, seq, nv), F32),
        scratch_shapes=[state],
        compiler_params=_params(2),
        name="ret_scan_bwd",
    )(cd_b, q, kzb, v, xi_b)

    return pl.pallas_call(
        _ret_fwd_kernel,
        grid=(bsz, nc),
        in_specs=[smem, fwd(nqk), fwd(nqk), fwd(nqk), fwd(nv), fwd(nv), fwd(nv),
                  _resident(xi_f.shape), _resident(dmat.shape)],
        out_specs=fwd(nv),
        out_shape=jax.ShapeDtypeStruct((bsz, seq, nv), BF16),
        scratch_shapes=[state],
        compiler_params=_params(2),
        name="ret_scan_fwd",
    )(cd_f, q, k, kzf, v, sg, yb, xi_f, dmat)


def _decay_tables(decay_fwd, decay_bwd, dk):
    L = RET_CHUNK
    lg_f = jax.nn.log_sigmoid(decay_fwd.astype(F32))
    lg_b = jax.nn.log_sigmoid(decay_bwd.astype(F32))
    idx = jnp.arange(L, dtype=F32)
    diff = idx[:, None] - idx[None, :]
    dm = jnp.where((diff >= 0)[None],
                   jnp.exp(lg_f[:, None, None] * jnp.maximum(diff, 0.0)[None]),
                   jnp.exp(lg_b[:, None, None] * jnp.maximum(-diff, 0.0)[None]))
    xi_f = jnp.exp(lg_f[:, None] * (idx + 1.0)[None])[:, :, None]
    xi_b = jnp.exp(lg_b[:, None] * (L - idx)[None])[:, :, None]
    zeta_f = jnp.exp(lg_f[:, None] * (L - 1.0 - idx)[None])
    zeta_b = jnp.exp(lg_b[:, None] * idx[None])
    zf = jnp.repeat(zeta_f.T, dk, axis=1)
    zb = jnp.repeat(zeta_b.T, dk, axis=1)
    return dm, xi_f, xi_b, zf, zb, jnp.exp(lg_f * L), jnp.exp(lg_b * L)


def _rope_tables(seq, dk):
    inv = ROPE_BASE ** (-jnp.arange(0, dk, 2, dtype=F32) / dk)
    ang = jnp.arange(seq, dtype=F32)[:, None] * inv[None]
    return jnp.cos(ang), jnp.sin(ang)


def kernel(x, c, rel_bias, att_w_qkv, att_w_o, att_sink, ret_w_in, ret_w_o, ret_decay_fwd,
           ret_decay_bwd, ada_w, ada_b, mix_norm_pre, mix_norm_post, ffn_norm_pre,
           ffn_norm_post, ffn_w_in, ffn_w_out):
    bsz, seq, d = x.shape
    t = bsz * seq
    assert seq % ROW_TILE == 0 and ROW_TILE % RET_CHUNK == 0 and d % RET_HEADS == 0
    assert ada_w.shape[0] == 2 and ffn_w_out.shape[1] % FFN_CHUNK == 0
    rows = -(-bsz // 8) * 8
    vec = lambda a: a.reshape(1, d)

    c_pad = jnp.pad(c, ((0, rows - bsz), (0, 0)))
    mod = _ada_mod(c_pad, ada_w, ada_b)
    modv = mod.reshape(2 * rows * 6, 1, d)

    x2d = x.reshape(t, d)

    nq = Q_HEADS * HEAD_DIM
    qscale = jnp.concatenate([jnp.full((nq,), HEAD_DIM ** -0.5, F32),
                              jnp.ones((att_w_qkv.shape[2] - nq,), F32)])
    w_qkv = (att_w_qkv[0] * qscale[None]).astype(BF16)
    qkv = _qkv_proj(x2d, vec(mix_norm_pre[0]), modv, w_qkv, seq, rows)
    sink_b = jnp.broadcast_to(att_sink[0].astype(F32)[:, None, None], (Q_HEADS, 1, 128))
    att = _attention(qkv.reshape(bsz, seq, -1), rel_bias.astype(F32), sink_b)
    x1, h1 = _post_ffn(att.reshape(t, nq), x2d, att_w_o[0].astype(BF16), vec(mix_norm_post[0]),
                       vec(ffn_norm_pre[0]), ffn_w_in[0].astype(BF16), ffn_w_out[0].astype(BF16),
                       vec(ffn_norm_post[0]), modv, 0, seq, rows, g_next=vec(mix_norm_pre[1]))

    dk = d // RET_HEADS
    nqk = RET_HEADS * dk
    kscale = jnp.concatenate([jnp.ones((nqk,), F32), jnp.full((nqk,), dk ** -0.5, F32),
                              jnp.ones((ret_w_in.shape[2] - 2 * nqk,), F32)])
    w_rin = (ret_w_in[0] * kscale[None]).astype(BF16)
    dm, xi_f, xi_b, zf, zb, cd_f, cd_b = _decay_tables(ret_decay_fwd[0], ret_decay_bwd[0], dk)
    cos, sin = _rope_tables(seq, dk)
    q, k, kzf, kzb, v, sg = _ret_in(h1, w_rin, cos, sin, zf, zb, seq)
    r3 = lambda a: a.reshape(bsz, seq, a.shape[1])
    gated = _retention(r3(q), r3(k), r3(kzf), r3(kzb), r3(v), r3(sg), xi_f, xi_b, dm, cd_f, cd_b)
    (x2,) = _post_ffn(gated.reshape(t, -1), x1, ret_w_o[0].astype(BF16), vec(mix_norm_post[1]),
                      vec(ffn_norm_pre[1]), ffn_w_in[1].astype(BF16), ffn_w_out[1].astype(BF16),
                      vec(ffn_norm_post[1]), modv, 1, seq, rows)
    return x2.reshape(bsz, seq, d)
```

```python
import functools
import math

import jax
import jax.numpy as jnp
from jax import lax
from jax.experimental import pallas as pl
from jax.experimental.pallas import tpu as pltpu

F32 = jnp.float32
BF16 = jnp.bfloat16

EPS = 1e-6
NEG = -1e30
LOG2E = math.log2(math.e)

Q_HEADS = 16
KV_HEADS = 4
GROUP = Q_HEADS // KV_HEADS
HEAD_DIM = 64
ATT_BLOCK = 128
REL_BUCKETS = 32
RET_HEADS = 4
RET_CHUNK = 128
ROPE_BASE = 10000.0
FFN_CHUNK = 256
ROW_TILE = 512
VMEM_LIMIT = 56 * 1024 * 1024


def _silu(x):
    return x * (1.0 / (1.0 + jnp.exp(-x)))


def _rms(xf, g):
    ms = jnp.mean(xf * xf, axis=-1, keepdims=True)
    return (xf * lax.rsqrt(ms + EPS)) * g


def _resident(shape):
    zeros = (0,) * len(shape)
    return pl.BlockSpec(shape, lambda *_: zeros, pipeline_mode=pl.Buffered(1))


def _params(n_axes, vmem=VMEM_LIMIT):
    return pltpu.CompilerParams(
        dimension_semantics=("arbitrary",) * n_axes, vmem_limit_bytes=vmem)


def _ada_kernel(c_ref, w_ref, b_ref, o_ref):
    ca = _silu(c_ref[...])
    o_ref[0] = jnp.dot(ca.astype(BF16), w_ref[0].astype(BF16),
                       preferred_element_type=F32) + b_ref[0]


def _ada_mod(c_pad, ada_w, ada_b):
    depth, d, n = ada_w.shape
    rows = c_pad.shape[0]
    tn = 1536
    return pl.pallas_call(
        _ada_kernel,
        grid=(depth, n // tn),
        in_specs=[
            pl.BlockSpec((rows, d), lambda i, j: (0, 0)),
            pl.BlockSpec((1, d, tn), lambda i, j: (i, 0, j)),
            pl.BlockSpec((1, 1, tn), lambda i, j: (i, 0, j)),
        ],
        out_specs=pl.BlockSpec((1, rows, tn), lambda i, j: (i, 0, j)),
        out_shape=jax.ShapeDtypeStruct((depth, rows, n), F32),
        compiler_params=_params(2),
        name="ada_mod",
    )(c_pad, ada_w, ada_b.reshape(depth, 1, n))


def _mod_spec(layer, slot, tiles_per_batch, n_batch_rows):
    base = layer * n_batch_rows * 6 + slot

    def index(t):
        return (base + (t // tiles_per_batch) * 6, 0, 0)
    return index


def _qkv_kernel(x_ref, g_ref, sc_ref, sh_ref, wq_ref, wk_ref, wvt_ref, q_ref, k_ref, vt_ref):
    h = _rms(x_ref[...], g_ref[...]) * (1.0 + sc_ref[0]) + sh_ref[0]
    hb = h.astype(BF16)
    q_ref[...] = jnp.dot(hb, wq_ref[...], preferred_element_type=F32).astype(BF16)
    k_ref[...] = jnp.dot(hb, wk_ref[...], preferred_element_type=F32).astype(BF16)
    vt_ref[0] = lax.dot_general(wvt_ref[...], hb, (((1,), (1,)), ((), ())),
                                preferred_element_type=F32).astype(BF16)


def _qkv_proj(x2d, g_pre, modv, w_q, w_k, w_vt, seq, n_batch_rows):
    t, d = x2d.shape
    nq = w_q.shape[1]
    nkv = w_k.shape[1]
    tm = ROW_TILE
    tpb = seq // tm
    vec = lambda idx: pl.BlockSpec((1, 1, d), idx)
    return pl.pallas_call(
        _qkv_kernel,
        grid=(t // tm,),
        in_specs=[
            pl.BlockSpec((tm, d), lambda i: (i, 0)),
            _resident((1, d)),
            vec(_mod_spec(0, 1, tpb, n_batch_rows)),
            vec(_mod_spec(0, 0, tpb, n_batch_rows)),
            _resident(w_q.shape), _resident(w_k.shape), _resident(w_vt.shape),
        ],
        out_specs=[
            pl.BlockSpec((tm, nq), lambda i: (i, 0)),
            pl.BlockSpec((tm, nkv), lambda i: (i, 0)),
            pl.BlockSpec((1, nkv, tm), lambda i: (i // tpb, 0, i % tpb)),
        ],
        out_shape=[
            jax.ShapeDtypeStruct((t, nq), BF16),
            jax.ShapeDtypeStruct((t, nkv), BF16),
            jax.ShapeDtypeStruct((t // seq, nkv, seq), BF16),
        ],
        compiler_params=_params(1),
        name="qkv_proj",
    )(x2d, g_pre, modv, modv, w_q, w_k, w_vt)


def _att_tables(rb_ref, b2_ref, m2_ref):
    L = ATT_BLOCK
    rows = 64
    for r in range(3 * L // rows):
        j = lax.broadcasted_iota(jnp.int32, (rows, L), 0) + r * rows
        t = lax.broadcasted_iota(jnp.int32, (rows, L), 1)
        rel = j - L - t
        n = jnp.abs(rel)
        large = jnp.full((rows, L), 8, jnp.int32)
        for thr in (12, 16, 23, 32, 46, 64, 91):
            large = large + (n >= thr).astype(jnp.int32)
        bucket = jnp.where(rel > 0, 16, 0) + jnp.where(n < 8, n, large)
        in_win = n <= L
        visible = (in_win & (j >= L), in_win, in_win & (j < 2 * L))
        piece = slice(r * rows, (r + 1) * rows)
        for kind in range(3):
            m2_ref[kind, piece, :] = jnp.where(visible[kind], LOG2E, 0.0).astype(F32)

        def body(hq, carry):
            acc = jnp.zeros((rows, L), F32)
            for b in range(REL_BUCKETS):
                acc = jnp.where(bucket == b, rb_ref[b, hq], acc)
            acc = acc * LOG2E
            for kind in range(3):
                b2_ref[kind, hq, piece, :] = jnp.where(visible[kind], acc, NEG)
            return carry
        lax.fori_loop(0, Q_HEADS, body, 0)


def _att_kernel(rb_ref, sink_ref, q_ref, kp_ref, kc_ref, kn_ref, vp_ref, vc_ref, vn_ref,
                o_ref, b2_ref, m2_ref):
    b = pl.program_id(0)
    n = pl.program_id(1)
    nb = pl.num_programs(1)

    @pl.when((b == 0) & (n == 0))
    def _():
        _att_tables(rb_ref, b2_ref, m2_ref)

    L = ATT_BLOCK
    dh = HEAD_DIM
    kind = jnp.where(n == 0, 0, jnp.where(n == nb - 1, 2, 1))
    m2 = m2_ref[kind]
    q = q_ref[0]
    kb = jnp.concatenate([kp_ref[0], kc_ref[0], kn_ref[0]], axis=0)
    vt = jnp.concatenate([vp_ref[0], vc_ref[0], vn_ref[0]], axis=1)
    lane = lax.broadcasted_iota(jnp.int32, (L, 2 * dh), 1)
    zero = jnp.zeros((L, 2 * dh), BF16)
    o_rows = []
    for p in range(KV_HEADS // 2):
        k_pair = kb[:, p * 2 * dh:(p + 1) * 2 * dh]
        for e in range(2):
            h = 2 * p + e
            mine = (lane < dh) if e == 0 else (lane >= dh)
            qz = jnp.concatenate(
                [jnp.where(mine, q[:, (p * GROUP + g) * 2 * dh:(p * GROUP + g + 1) * 2 * dh], zero)
                 for g in range(GROUP)], axis=0)
            st = lax.dot_general(k_pair, qz, (((1,), (1,)), ((), ())),
                                 preferred_element_type=F32)
            es, rs = [], []
            for g in range(GROUP):
                hq = h * GROUP + g
                sk2 = jnp.full((1, L), sink_ref[hq], F32) * LOG2E
                l2 = st[:, g * L:(g + 1) * L] * m2 + b2_ref[kind, hq]
                m = jnp.maximum(jnp.max(l2, axis=0, keepdims=True), sk2)
                ex = jnp.exp2(l2 - m)
                den = jnp.sum(ex, axis=0, keepdims=True) + jnp.exp2(sk2 - m)
                es.append(ex.astype(BF16))
                rs.append(1.0 / den)
            et = jnp.concatenate(es, axis=1)
            ot = jnp.dot(vt[h * dh:(h + 1) * dh, :], et, preferred_element_type=F32)
            ot = ot * jnp.concatenate(rs, axis=1)
            o_rows += [ot[:, g * L:(g + 1) * L] for g in range(GROUP)]
    ot_all = jnp.concatenate(o_rows, axis=0)
    o_ref[0] = ot_all.T.astype(BF16)


def _attention(q, k, vt, rel_bias, sink):
    bsz, seq, dq = q.shape
    dkv = k.shape[2]
    L = ATT_BLOCK
    nb = seq // L
    smem = pl.BlockSpec(memory_space=pltpu.SMEM)
    prev = lambda i: jnp.maximum(i - 1, 0)
    nxt = lambda i: jnp.minimum(i + 1, nb - 1)
    return pl.pallas_call(
        _att_kernel,
        grid=(bsz, nb),
        in_specs=[
            smem, smem,
            pl.BlockSpec((1, L, dq), lambda b, i: (b, i, 0)),
            pl.BlockSpec((1, L, dkv), lambda b, i: (b, prev(i), 0)),
            pl.BlockSpec((1, L, dkv), lambda b, i: (b, i, 0)),
            pl.BlockSpec((1, L, dkv), lambda b, i: (b, nxt(i), 0)),
            pl.BlockSpec((1, dkv, L), lambda b, i: (b, 0, prev(i))),
            pl.BlockSpec((1, dkv, L), lambda b, i: (b, 0, i)),
            pl.BlockSpec((1, dkv, L), lambda b, i: (b, 0, nxt(i))),
        ],
        out_specs=pl.BlockSpec((1, L, dq), lambda b, i: (b, i, 0)),
        out_shape=jax.ShapeDtypeStruct((bsz, seq, dq), BF16),
        scratch_shapes=[
            pltpu.VMEM((3, Q_HEADS, 3 * L, L), F32),
            pltpu.VMEM((3, 3 * L, L), F32),
        ],
        compiler_params=_params(2),
        name="swa_attention",
    )(rel_bias, sink, q, k, k, k, vt, vt, vt)


def _post_ffn_kernel(*refs, emit_next):
    (a_ref, x_ref, wo_ref, gpost_ref, g1_ref, gpre_ref, sc2_ref, sh2_ref,
     win_ref, wout_ref, gfpost_ref, g2_ref) = refs[:12]
    if emit_next:
        gn_ref, scn_ref, shn_ref, xo_ref, ho_ref = refs[12:]
    else:
        (xo_ref,) = refs[12:]
    hidden = wout_ref.shape[0]

    y = jnp.dot(a_ref[...], wo_ref[...], preferred_element_type=F32)
    x1 = x_ref[...] + g1_ref[0] * _rms(y, gpost_ref[...])
    h = _rms(x1, gpre_ref[...]) * (1.0 + sc2_ref[0]) + sh2_ref[0]
    hb = h.astype(BF16)
    acc = jnp.zeros(x1.shape, F32)
    for c in range(hidden // FFN_CHUNK):
        lo = c * FFN_CHUNK
        a = jnp.dot(hb, win_ref[:, lo:lo + FFN_CHUNK], preferred_element_type=F32)
        b = jnp.dot(hb, win_ref[:, hidden + lo:hidden + lo + FFN_CHUNK],
                    preferred_element_type=F32)
        act = (_silu(a) * b).astype(BF16)
        acc = acc + jnp.dot(act, wout_ref[lo:lo + FFN_CHUNK, :], preferred_element_type=F32)
    x2 = x1 + g2_ref[0] * _rms(acc, gfpost_ref[...])
    xo_ref[...] = x2
    if emit_next:
        hn = _rms(x2, gn_ref[...]) * (1.0 + scn_ref[0]) + shn_ref[0]
        ho_ref[...] = hn.astype(BF16)


def _post_ffn(a2d, x2d, w_o, g_post, g_pre, w_in, w_out, g_fpost, modv, layer,
              seq, n_batch_rows, g_next=None):
    t, d = x2d.shape
    kin = a2d.shape[1]
    tm = ROW_TILE
    tpb = seq // tm
    emit_next = g_next is not None
    vec = lambda lyr, slot: pl.BlockSpec((1, 1, d), _mod_spec(lyr, slot, tpb, n_batch_rows))
    row = lambda width: pl.BlockSpec((tm, width), lambda i: (i, 0))
    in_specs = [
        row(kin), row(d), _resident(w_o.shape), _resident((1, d)), vec(layer, 2),
        _resident((1, d)), vec(layer, 4), vec(layer, 3),
        _resident(w_in.shape), _resident(w_out.shape), _resident((1, d)), vec(layer, 5),
    ]
    args = [a2d, x2d, w_o, g_post, modv, g_pre, modv, modv, w_in, w_out, g_fpost, modv]
    out_specs = [row(d)]
    out_shape = [jax.ShapeDtypeStruct((t, d), F32)]
    if emit_next:
        in_specs += [_resident((1, d)), vec(layer + 1, 1), vec(layer + 1, 0)]
        args += [g_next, modv, modv]
        out_specs.append(row(d))
        out_shape.append(jax.ShapeDtypeStruct((t, d), BF16))
    return pl.pallas_call(
        functools.partial(_post_ffn_kernel, emit_next=emit_next),
        grid=(t // tm,),
        in_specs=in_specs,
        out_specs=out_specs,
        out_shape=out_shape,
        compiler_params=_params(1),
        name="post_ffn_next" if emit_next else "post_ffn",
    )(*args)


def _ret_in_kernel(h_ref, w_ref, cos_ref, sin_ref, zf_ref, zb_ref,
                   q_ref, k_ref, kzf_ref, kzb_ref, v_ref, sg_ref):
    hb = h_ref[...]
    tm = hb.shape[0]
    cos = cos_ref[...]
    sin = sin_ref[...]
    dk = w_ref.shape[0] // RET_HEADS
    half = dk // 2
    nqk = RET_HEADS * dk
    nv = (w_ref.shape[1] - 2 * nqk) // 2

    yq = jnp.dot(hb, w_ref[:, 0:nqk], preferred_element_type=F32)
    yk = jnp.dot(hb, w_ref[:, nqk:2 * nqk], preferred_element_type=F32)
    zf = zf_ref[...][None]
    zb = zb_ref[...][None]
    for h in range(RET_HEADS):
        for off in (0, half):
            lo = h * dk + off
            sign = -1.0 if off == 0 else 1.0
            other = h * dk + (half - off)
            rq = yq[:, lo:lo + half] * cos + sign * (yq[:, other:other + half] * sin)
            rk = yk[:, lo:lo + half] * cos + sign * (yk[:, other:other + half] * sin)
            q_ref[:, lo:lo + half] = rq.astype(BF16)
            k_ref[:, lo:lo + half] = rk.astype(BF16)
            rk3 = rk.reshape(tm // RET_CHUNK, RET_CHUNK, half)
            kzf_ref[:, lo:lo + half] = (rk3 * zf[:, :, lo:lo + half]).reshape(tm, half).astype(BF16)
            kzb_ref[:, lo:lo + half] = (rk3 * zb[:, :, lo:lo + half]).reshape(tm, half).astype(BF16)
    v_ref[...] = jnp.dot(hb, w_ref[:, 2 * nqk:2 * nqk + nv],
                         preferred_element_type=F32).astype(BF16)
    g = jnp.dot(hb, w_ref[:, 2 * nqk + nv:], preferred_element_type=F32)
    sg_ref[...] = _silu(g)


def _ret_in(h2d, w_in, cos, sin, zf, zb, seq):
    t, d = h2d.shape
    tm = ROW_TILE
    tpb = seq // tm
    nqk = RET_HEADS * (d // RET_HEADS)
    nv = (w_in.shape[1] - 2 * nqk) // 2
    row = lambda width: pl.BlockSpec((tm, width), lambda i: (i, 0))
    pos = pl.BlockSpec((tm, cos.shape[1]), lambda i: (i % tpb, 0))
    return pl.pallas_call(
        _ret_in_kernel,
        grid=(t // tm,),
        in_specs=[row(d), _resident(w_in.shape), pos, pos,
                  _resident(zf.shape), _resident(zb.shape)],
        out_specs=[row(nqk), row(nqk), row(nqk), row(nqk), row(nv), row(nv)],
        out_shape=[jax.ShapeDtypeStruct((t, nqk), BF16)] * 4
        + [jax.ShapeDtypeStruct((t, nv), BF16), jax.ShapeDtypeStruct((t, nv), F32)],
        compiler_params=_params(1),
        name="ret_in_proj",
    )(h2d, w_in, cos, sin, zf, zb)


def _state_step(st_ref, h, q_h, kz_h, v_h, xi, cd):
    st = st_ref[h]
    cross = jnp.dot(q_h, st.astype(BF16), preferred_element_type=F32) * xi
    upd = lax.dot_general(kz_h, v_h, (((0,), (0,)), ((), ())), preferred_element_type=F32)
    st_ref[h] = st * cd + upd
    return cross


def _ret_bwd_kernel(cd_ref, q_ref, kz_ref, v_ref, xi_ref, yb_ref, st_ref):
    @pl.when(pl.program_id(1) == 0)
    def _():
        st_ref[...] = jnp.zeros(st_ref.shape, F32)

    q = q_ref[0]
    kz = kz_ref[0]
    v = v_ref[0]
    dk = q.shape[1] // RET_HEADS
    dv = v.shape[1] // RET_HEADS
    for h in range(RET_HEADS):
        yb_ref[0, :, h * dv:(h + 1) * dv] = _state_step(
            st_ref, h, q[:, h * dk:(h + 1) * dk], kz[:, h * dk:(h + 1) * dk],
            v[:, h * dv:(h + 1) * dv], xi_ref[h], cd_ref[h])


def _ret_fwd_kernel(cd_ref, q_ref, k_ref, kz_ref, v_ref, sg_ref, yb_ref, xi_ref, dm_ref,
                    o_ref, st_ref):
    @pl.when(pl.program_id(1) == 0)
    def _():
        st_ref[...] = jnp.zeros(st_ref.shape, F32)

    q = q_ref[0]
    k = k_ref[0]
    kz = kz_ref[0]
    v = v_ref[0]
    dk = q.shape[1] // RET_HEADS
    dv = v.shape[1] // RET_HEADS
    for h in range(RET_HEADS):
        q_h = q[:, h * dk:(h + 1) * dk]
        v_h = v[:, h * dv:(h + 1) * dv]
        s = lax.dot_general(q_h, k[:, h * dk:(h + 1) * dk], (((1,), (1,)), ((), ())),
                            preferred_element_type=F32)
        inner = jnp.dot((s * dm_ref[h]).astype(BF16), v_h, preferred_element_type=F32)
        cross = _state_step(st_ref, h, q_h, kz[:, h * dk:(h + 1) * dk], v_h,
                            xi_ref[h], cd_ref[h])
        y = inner + cross + yb_ref[0, :, h * dv:(h + 1) * dv]
        yn = y * lax.rsqrt(jnp.mean(y * y, axis=-1, keepdims=True) + EPS)
        o_ref[0, :, h * dv:(h + 1) * dv] = (sg_ref[0, :, h * dv:(h + 1) * dv] * yn).astype(BF16)


def _retention(q, k, kzf, kzb, v, sg, xi_f, xi_b, dmat, cd_f, cd_b):
    bsz, seq, nqk = q.shape
    nv = v.shape[2]
    L = RET_CHUNK
    nc = seq // L
    dk = nqk // RET_HEADS
    dv = nv // RET_HEADS
    smem = pl.BlockSpec(memory_space=pltpu.SMEM)
    state = pltpu.VMEM((RET_HEADS, dk, dv), F32)
    rev = lambda width: pl.BlockSpec((1, L, width), lambda b, j: (b, nc - 1 - j, 0))
    fwd = lambda width: pl.BlockSpec((1, L, width), lambda b, j: (b, j, 0))

    yb = pl.pallas_call(
        _ret_bwd_kernel,
        grid=(bsz, nc),
        in_specs=[smem, rev(nqk), rev(nqk), rev(nv), _resident(xi_b.shape)],
        out_specs=rev(nv),
        out_shape=jax.ShapeDtypeStruct((bsz, seq, nv), F32),
        scratch_shapes=[state],
        compiler_params=_params(2),
        name="ret_scan_bwd",
    )(cd_b, q, kzb, v, xi_b)

    return pl.pallas_call(
        _ret_fwd_kernel,
        grid=(bsz, nc),
        in_specs=[smem, fwd(nqk), fwd(nqk), fwd(nqk), fwd(nv), fwd(nv), fwd(nv),
                  _resident(xi_f.shape), _resident(dmat.shape)],
        out_specs=fwd(nv),
        out_shape=jax.ShapeDtypeStruct((bsz, seq, nv), BF16),
        scratch_shapes=[state],
        compiler_params=_params(2),
        name="ret_scan_fwd",
    )(cd_f, q, k, kzf, v, sg, yb, xi_f, dmat)


def _decay_tables(decay_fwd, decay_bwd, dk):
    L = RET_CHUNK
    lg_f = jax.nn.log_sigmoid(decay_fwd.astype(F32))
    lg_b = jax.nn.log_sigmoid(decay_bwd.astype(F32))
    idx = jnp.arange(L, dtype=F32)
    diff = idx[:, None] - idx[None, :]
    dm = jnp.where((diff >= 0)[None],
                   jnp.exp(lg_f[:, None, None] * jnp.maximum(diff, 0.0)[None]),
                   jnp.exp(lg_b[:, None, None] * jnp.maximum(-diff, 0.0)[None]))
    xi_f = jnp.exp(lg_f[:, None] * (idx + 1.0)[None])[:, :, None]
    xi_b = jnp.exp(lg_b[:, None] * (L - idx)[None])[:, :, None]
    zeta_f = jnp.exp(lg_f[:, None] * (L - 1.0 - idx)[None])
    zeta_b = jnp.exp(lg_b[:, None] * idx[None])
    zf = jnp.repeat(zeta_f.T, dk, axis=1)
    zb = jnp.repeat(zeta_b.T, dk, axis=1)
    return dm, xi_f, xi_b, zf, zb, jnp.exp(lg_f * L), jnp.exp(lg_b * L)


def _rope_tables(seq, dk):
    inv = ROPE_BASE ** (-jnp.arange(0, dk, 2, dtype=F32) / dk)
    ang = jnp.arange(seq, dtype=F32)[:, None] * inv[None]
    return jnp.cos(ang), jnp.sin(ang)


def kernel(x, c, rel_bias, att_w_qkv, att_w_o, att_sink, ret_w_in, ret_w_o, ret_decay_fwd,
           ret_decay_bwd, ada_w, ada_b, mix_norm_pre, mix_norm_post, ffn_norm_pre,
           ffn_norm_post, ffn_w_in, ffn_w_out):
    bsz, seq, d = x.shape
    t = bsz * seq
    assert seq % ROW_TILE == 0 and ROW_TILE % RET_CHUNK == 0 and d % RET_HEADS == 0
    assert ada_w.shape[0] == 2 and ffn_w_out.shape[1] % FFN_CHUNK == 0
    rows = -(-bsz // 8) * 8
    vec = lambda a: a.reshape(1, d)

    c_pad = jnp.pad(c, ((0, rows - bsz), (0, 0)))
    mod = _ada_mod(c_pad, ada_w, ada_b)
    modv = mod.reshape(2 * rows * 6, 1, d)

    x2d = x.reshape(t, d)

    nq = Q_HEADS * HEAD_DIM
    nkv = KV_HEADS * HEAD_DIM
    heads = [(2 * p + e) * GROUP + g for p in range(KV_HEADS // 2) for g in range(GROUP)
             for e in range(2)]
    w_q = att_w_qkv[0][:, :nq].reshape(d, Q_HEADS, HEAD_DIM)[:, jnp.array(heads)].reshape(d, nq)
    w_q = (w_q * HEAD_DIM ** -0.5).astype(BF16)
    w_k = att_w_qkv[0][:, nq:nq + nkv].astype(BF16)
    w_vt = att_w_qkv[0][:, nq + nkv:].T.astype(BF16)
    q, k, vt = _qkv_proj(x2d, vec(mix_norm_pre[0]), modv, w_q, w_k, w_vt, seq, rows)
    att = _attention(q.reshape(bsz, seq, nq), k.reshape(bsz, seq, nkv), vt,
                     rel_bias.astype(F32), att_sink[0].astype(F32))
    x1, h1 = _post_ffn(att.reshape(t, nq), x2d, att_w_o[0].astype(BF16), vec(mix_norm_post[0]),
                       vec(ffn_norm_pre[0]), ffn_w_in[0].astype(BF16), ffn_w_out[0].astype(BF16),
                       vec(ffn_norm_post[0]), modv, 0, seq, rows, g_next=vec(mix_norm_pre[1]))

    dk = d // RET_HEADS
    nqk = RET_HEADS * dk
    kscale = jnp.concatenate([jnp.ones((nqk,), F32), jnp.full((nqk,), dk ** -0.5, F32),
                              jnp.ones((ret_w_in.shape[2] - 2 * nqk,), F32)])
    w_rin = (ret_w_in[0] * kscale[None]).astype(BF16)
    dm, xi_f, xi_b, zf, zb, cd_f, cd_b = _decay_tables(ret_decay_fwd[0], ret_decay_bwd[0], dk)
    cos, sin = _rope_tables(seq, dk)
    q, k, kzf, kzb, v, sg = _ret_in(h1, w_rin, cos, sin, zf, zb, seq)
    r3 = lambda a: a.reshape(bsz, seq, a.shape[1])
    gated = _retention(r3(q), r3(k), r3(kzf), r3(kzb), r3(v), r3(sg), xi_f, xi_b, dm, cd_f, cd_b)
    (x2,) = _post_ffn(gated.reshape(t, -1), x1, ret_w_o[0].astype(BF16), vec(mix_norm_post[1]),
                      vec(ffn_norm_pre[1]), ffn_w_in[1].astype(BF16), ffn_w_out[1].astype(BF16),
                      vec(ffn_norm_post[1]), modv, 1, seq, rows)
    return x2.reshape(bsz, seq, d)
```

```python
import functools
import math

import jax
import jax.numpy as jnp
from jax import lax
from jax.experimental import pallas as pl
from jax.experimental.pallas import tpu as pltpu

F32 = jnp.float32
BF16 = jnp.bfloat16

EPS = 1e-6
NEG = -1e30
LOG2E = math.log2(math.e)

Q_HEADS = 16
KV_HEADS = 4
GROUP = Q_HEADS // KV_HEADS
HEAD_DIM = 64
ATT_BLOCK = 128
REL_BUCKETS = 32
RET_HEADS = 4
RET_CHUNK = 128
ROPE_BASE = 10000.0
FFN_CHUNK = 256
ROW_TILE = 512
VMEM_LIMIT = 56 * 1024 * 1024


def _silu(x):
    return x * (1.0 / (1.0 + jnp.exp(-x)))


def _rms(xf, g):
    ms = jnp.mean(xf * xf, axis=-1, keepdims=True)
    return (xf * lax.rsqrt(ms + EPS)) * g


def _resident(shape):
    zeros = (0,) * len(shape)
    return pl.BlockSpec(shape, lambda *_: zeros, pipeline_mode=pl.Buffered(1))


def _params(n_axes, vmem=VMEM_LIMIT):
    return pltpu.CompilerParams(
        dimension_semantics=("arbitrary",) * n_axes, vmem_limit_bytes=vmem)


def _ada_kernel(c_ref, w_ref, b_ref, o_ref):
    ca = _silu(c_ref[...])
    o_ref[0] = jnp.dot(ca.astype(BF16), w_ref[0].astype(BF16),
                       preferred_element_type=F32) + b_ref[0]


def _ada_mod(c_pad, ada_w, ada_b):
    depth, d, n = ada_w.shape
    rows = c_pad.shape[0]
    tn = 1536
    return pl.pallas_call(
        _ada_kernel,
        grid=(depth, n // tn),
        in_specs=[
            pl.BlockSpec((rows, d), lambda i, j: (0, 0)),
            pl.BlockSpec((1, d, tn), lambda i, j: (i, 0, j)),
            pl.BlockSpec((1, 1, tn), lambda i, j: (i, 0, j)),
        ],
        out_specs=pl.BlockSpec((1, rows, tn), lambda i, j: (i, 0, j)),
        out_shape=jax.ShapeDtypeStruct((depth, rows, n), F32),
        compiler_params=_params(2),
        name="ada_mod",
    )(c_pad, ada_w, ada_b.reshape(depth, 1, n))


def _mod_spec(layer, slot, tiles_per_batch, n_batch_rows):
    base = layer * n_batch_rows * 6 + slot

    def index(t):
        return (base + (t // tiles_per_batch) * 6, 0, 0)
    return index


def _qkv_kernel(x_ref, g_ref, sc_ref, sh_ref, wq_ref, wk_ref, wvt_ref, q_ref, k_ref, vt_ref):
    h = _rms(x_ref[...], g_ref[...]) * (1.0 + sc_ref[0]) + sh_ref[0]
    hb = h.astype(BF16)
    q_ref[...] = jnp.dot(hb, wq_ref[...], preferred_element_type=F32).astype(BF16)
    k_ref[...] = jnp.dot(hb, wk_ref[...], preferred_element_type=F32).astype(BF16)
    vt_ref[0] = lax.dot_general(wvt_ref[...], hb, (((1,), (1,)), ((), ())),
                                preferred_element_type=F32).astype(BF16)


def _qkv_proj(x2d, g_pre, modv, w_q, w_k, w_vt, seq, n_batch_rows):
    t, d = x2d.shape
    nq = w_q.shape[1]
    nkv = w_k.shape[1]
    tm = ROW_TILE
    tpb = seq // tm
    vec = lambda idx: pl.BlockSpec((1, 1, d), idx)
    return pl.pallas_call(
        _qkv_kernel,
        grid=(t // tm,),
        in_specs=[
            pl.BlockSpec((tm, d), lambda i: (i, 0)),
            _resident((1, d)),
            vec(_mod_spec(0, 1, tpb, n_batch_rows)),
            vec(_mod_spec(0, 0, tpb, n_batch_rows)),
            _resident(w_q.shape), _resident(w_k.shape), _resident(w_vt.shape),
        ],
        out_specs=[
            pl.BlockSpec((tm, nq), lambda i: (i, 0)),
            pl.BlockSpec((tm, nkv), lambda i: (i, 0)),
            pl.BlockSpec((1, nkv, tm), lambda i: (i // tpb, 0, i % tpb)),
        ],
        out_shape=[
            jax.ShapeDtypeStruct((t, nq), BF16),
            jax.ShapeDtypeStruct((t, nkv), BF16),
            jax.ShapeDtypeStruct((t // seq, nkv, seq), BF16),
        ],
        compiler_params=_params(1),
        name="qkv_proj",
    )(x2d, g_pre, modv, modv, w_q, w_k, w_vt)


def _att_tables(rb_ref, b2_ref, m2_ref):
    L = ATT_BLOCK
    rows = 64
    for r in range(3 * L // rows):
        j = lax.broadcasted_iota(jnp.int32, (rows, L), 0) + r * rows
        t = lax.broadcasted_iota(jnp.int32, (rows, L), 1)
        rel = j - L - t
        n = jnp.abs(rel)
        large = jnp.full((rows, L), 8, jnp.int32)
        for thr in (12, 16, 23, 32, 46, 64, 91):
            large = large + (n >= thr).astype(jnp.int32)
        bucket = jnp.where(rel > 0, 16, 0) + jnp.where(n < 8, n, large)
        in_win = n <= L
        visible = (in_win & (j >= L), in_win, in_win & (j < 2 * L))
        piece = slice(r * rows, (r + 1) * rows)
        for kind in range(3):
            m2_ref[kind, piece, :] = jnp.where(visible[kind], LOG2E, 0.0).astype(F32)

        def body(hq, carry):
            acc = jnp.zeros((rows, L), F32)
            for b in range(REL_BUCKETS):
                acc = jnp.where(bucket == b, rb_ref[b, hq], acc)
            acc = acc * LOG2E
            for kind in range(3):
                b2_ref[kind, hq, piece, :] = jnp.where(visible[kind], acc, NEG)
            return carry
        lax.fori_loop(0, Q_HEADS, body, 0)


def _att_kernel(rb_ref, sink_ref, q_ref, kp_ref, kc_ref, kn_ref, vp_ref, vc_ref, vn_ref,
                o_ref, b2_ref, m2_ref):
    b = pl.program_id(0)
    n = pl.program_id(1)
    nb = pl.num_programs(1)

    @pl.when((b == 0) & (n == 0))
    def _():
        _att_tables(rb_ref, b2_ref, m2_ref)

    L = ATT_BLOCK
    dh = HEAD_DIM
    kind = jnp.where(n == 0, 0, jnp.where(n == nb - 1, 2, 1))
    m2 = m2_ref[kind]
    q = q_ref[0]
    kb = jnp.concatenate([kp_ref[0], kc_ref[0], kn_ref[0]], axis=0)
    vt = jnp.concatenate([vp_ref[0], vc_ref[0], vn_ref[0]], axis=1)
    lane = lax.broadcasted_iota(jnp.int32, (L, 2 * dh), 1)
    zero = jnp.zeros((L, 2 * dh), BF16)
    o_rows = []
    for p in range(KV_HEADS // 2):
        k_pair = kb[:, p * 2 * dh:(p + 1) * 2 * dh]
        for e in range(2):
            h = 2 * p + e
            mine = (lane < dh) if e == 0 else (lane >= dh)
            qz = jnp.concatenate(
                [jnp.where(mine, q[:, (p * GROUP + g) * 2 * dh:(p * GROUP + g + 1) * 2 * dh], zero)
                 for g in range(GROUP)], axis=0)
            st = lax.dot_general(k_pair, qz, (((1,), (1,)), ((), ())),
                                 preferred_element_type=F32)
            es, rs = [], []
            for g in range(GROUP):
                hq = h * GROUP + g
                sk2 = jnp.full((1, L), sink_ref[hq], F32) * LOG2E
                l2 = st[:, g * L:(g + 1) * L] * m2 + b2_ref[kind, hq]
                m = jnp.maximum(jnp.max(l2, axis=0, keepdims=True), sk2)
                ex = jnp.exp2(l2 - m)
                den = jnp.sum(ex, axis=0, keepdims=True) + jnp.exp2(sk2 - m)
                es.append(ex.astype(BF16))
                rs.append(1.0 / den)
            et = jnp.concatenate(es, axis=1)
            ot = jnp.dot(vt[h * dh:(h + 1) * dh, :], et, preferred_element_type=F32)
            ot = ot * jnp.concatenate(rs, axis=1)
            o_rows += [ot[:, g * L:(g + 1) * L] for g in range(GROUP)]
    ot_all = jnp.concatenate(o_rows, axis=0)
    o_ref[0] = ot_all.T.astype(BF16)


def _attention(q, k, vt, rel_bias, sink):
    bsz, seq, dq = q.shape
    dkv = k.shape[2]
    L = ATT_BLOCK
    nb = seq // L
    smem = pl.BlockSpec(memory_space=pltpu.SMEM)
    prev = lambda i: jnp.maximum(i - 1, 0)
    nxt = lambda i: jnp.minimum(i + 1, nb - 1)
    return pl.pallas_call(
        _att_kernel,
        grid=(bsz, nb),
        in_specs=[
            smem, smem,
            pl.BlockSpec((1, L, dq), lambda b, i: (b, i, 0)),
            pl.BlockSpec((1, L, dkv), lambda b, i: (b, prev(i), 0)),
            pl.BlockSpec((1, L, dkv), lambda b, i: (b, i, 0)),
            pl.BlockSpec((1, L, dkv), lambda b, i: (b, nxt(i), 0)),
            pl.BlockSpec((1, dkv, L), lambda b, i: (b, 0, prev(i))),
            pl.BlockSpec((1, dkv, L), lambda b, i: (b, 0, i)),
            pl.BlockSpec((1, dkv, L), lambda b, i: (b, 0, nxt(i))),
        ],
        out_specs=pl.BlockSpec((1, L, dq), lambda b, i: (b, i, 0)),
        out_shape=jax.ShapeDtypeStruct((bsz, seq, dq), BF16),
        scratch_shapes=[
            pltpu.VMEM((3, Q_HEADS, 3 * L, L), F32),
            pltpu.VMEM((3, 3 * L, L), F32),
        ],
        compiler_params=_params(2),
        name="swa_attention",
    )(rel_bias, sink, q, k, k, k, vt, vt, vt)


def _post_ffn_kernel(*refs, emit_next):
    (a_ref, x_ref, wo_ref, gpost_ref, g1_ref, gpre_ref, sc2_ref, sh2_ref,
     win_ref, wout_ref, gfpost_ref, g2_ref) = refs[:12]
    if emit_next:
        gn_ref, scn_ref, shn_ref, xo_ref, ho_ref = refs[12:]
    else:
        (xo_ref,) = refs[12:]
    hidden = wout_ref.shape[0]

    y = jnp.dot(a_ref[...], wo_ref[...], preferred_element_type=F32)
    x1 = x_ref[...] + g1_ref[0] * _rms(y, gpost_ref[...])
    h = _rms(x1, gpre_ref[...]) * (1.0 + sc2_ref[0]) + sh2_ref[0]
    hb = h.astype(BF16)
    acc = jnp.zeros(x1.shape, F32)
    for c in range(hidden // FFN_CHUNK):
        lo = c * FFN_CHUNK
        a = jnp.dot(hb, win_ref[:, lo:lo + FFN_CHUNK], preferred_element_type=F32)
        b = jnp.dot(hb, win_ref[:, hidden + lo:hidden + lo + FFN_CHUNK],
                    preferred_element_type=F32)
        act = (_silu(a) * b).astype(BF16)
        acc = acc + jnp.dot(act, wout_ref[lo:lo + FFN_CHUNK, :], preferred_element_type=F32)
    x2 = x1 + g2_ref[0] * _rms(acc, gfpost_ref[...])
    xo_ref[...] = x2
    if emit_next:
        hn = _rms(x2, gn_ref[...]) * (1.0 + scn_ref[0]) + shn_ref[0]
        ho_ref[...] = hn.astype(BF16)


def _post_ffn(a2d, x2d, w_o, g_post, g_pre, w_in, w_out, g_fpost, modv, layer,
              seq, n_batch_rows, g_next=None):
    t, d = x2d.shape
    kin = a2d.shape[1]
    tm = ROW_TILE
    tpb = seq // tm
    emit_next = g_next is not None
    vec = lambda lyr, slot: pl.BlockSpec((1, 1, d), _mod_spec(lyr, slot, tpb, n_batch_rows))
    row = lambda width: pl.BlockSpec((tm, width), lambda i: (i, 0))
    in_specs = [
        row(kin), row(d), _resident(w_o.shape), _resident((1, d)), vec(layer, 2),
        _resident((1, d)), vec(layer, 4), vec(layer, 3),
        _resident(w_in.shape), _resident(w_out.shape), _resident((1, d)), vec(layer, 5),
    ]
    args = [a2d, x2d, w_o, g_post, modv, g_pre, modv, modv, w_in, w_out, g_fpost, modv]
    out_specs = [row(d)]
    out_shape = [jax.ShapeDtypeStruct((t, d), F32)]
    if emit_next:
        in_specs += [_resident((1, d)), vec(layer + 1, 1), vec(layer + 1, 0)]
        args += [g_next, modv, modv]
        out_specs.append(row(d))
        out_shape.append(jax.ShapeDtypeStruct((t, d), BF16))
    return pl.pallas_call(
        functools.partial(_post_ffn_kernel, emit_next=emit_next),
        grid=(t // tm,),
        in_specs=in_specs,
        out_specs=out_specs,
        out_shape=out_shape,
        compiler_params=_params(1),
        name="post_ffn_next" if emit_next else "post_ffn",
    )(*args)


def _ret_in_kernel(h_ref, w_ref, cos_ref, sin_ref, zf_ref, zb_ref,
                   q_ref, k_ref, kzf_ref, kzb_ref, v_ref, sg_ref):
    hb = h_ref[...]
    tm = hb.shape[0]
    cos = cos_ref[...]
    sin = sin_ref[...]
    dk = w_ref.shape[0] // RET_HEADS
    half = dk // 2
    nqk = RET_HEADS * dk
    nv = (w_ref.shape[1] - 2 * nqk) // 2

    yq = jnp.dot(hb, w_ref[:, 0:nqk], preferred_element_type=F32)
    yk = jnp.dot(hb, w_ref[:, nqk:2 * nqk], preferred_element_type=F32)
    zf = zf_ref[...][None]
    zb = zb_ref[...][None]
    for h in range(RET_HEADS):
        for off in (0, half):
            lo = h * dk + off
            sign = -1.0 if off == 0 else 1.0
            other = h * dk + (half - off)
            rq = yq[:, lo:lo + half] * cos + sign * (yq[:, other:other + half] * sin)
            rk = yk[:, lo:lo + half] * cos + sign * (yk[:, other:other + half] * sin)
            q_ref[:, lo:lo + half] = rq.astype(BF16)
            k_ref[:, lo:lo + half] = rk.astype(BF16)
            rk3 = rk.reshape(tm // RET_CHUNK, RET_CHUNK, half)
            kzf_ref[:, lo:lo + half] = (rk3 * zf[:, :, lo:lo + half]).reshape(tm, half).astype(BF16)
            kzb_ref[:, lo:lo + half] = (rk3 * zb[:, :, lo:lo + half]).reshape(tm, half).astype(BF16)
    v_ref[...] = jnp.dot(hb, w_ref[:, 2 * nqk:2 * nqk + nv],
                         preferred_element_type=F32).astype(BF16)
    g = jnp.dot(hb, w_ref[:, 2 * nqk + nv:], preferred_element_type=F32)
    sg_ref[...] = _silu(g)


def _ret_in(h2d, w_in, cos, sin, zf, zb, seq):
    t, d = h2d.shape
    tm = ROW_TILE
    tpb = seq // tm
    nqk = RET_HEADS * (d // RET_HEADS)
    nv = (w_in.shape[1] - 2 * nqk) // 2
    row = lambda width: pl.BlockSpec((tm, width), lambda i: (i, 0))
    pos = pl.BlockSpec((tm, cos.shape[1]), lambda i: (i % tpb, 0))
    return pl.pallas_call(
        _ret_in_kernel,
        grid=(t // tm,),
        in_specs=[row(d), _resident(w_in.shape), pos, pos,
                  _resident(zf.shape), _resident(zb.shape)],
        out_specs=[row(nqk), row(nqk), row(nqk), row(nqk), row(nv), row(nv)],
        out_shape=[jax.ShapeDtypeStruct((t, nqk), BF16)] * 4
        + [jax.ShapeDtypeStruct((t, nv), BF16), jax.ShapeDtypeStruct((t, nv), F32)],
        compiler_params=_params(1),
        name="ret_in_proj",
    )(h2d, w_in, cos, sin, zf, zb)


def _state_step(st_ref, h, q_h, kz_h, v_h, xi, cd):
    st = st_ref[h]
    cross = jnp.dot(q_h, st.astype(BF16), preferred_element_type=F32) * xi
    upd = lax.dot_general(kz_h, v_h, (((0,), (0,)), ((), ())), preferred_element_type=F32)
    st_ref[h] = st * cd + upd
    return cross


def _ret_scan_kernel(cdf_ref, cdb_ref, q_ref, k_ref, kzf_ref, kzb_ref, v_ref, sg_ref,
                     xif_ref, xib_ref, dm_ref, o_ref, st_ref, yb_ref):
    j = pl.program_id(1)
    nc = pl.num_programs(1) // 2

    @pl.when((j == 0) | (j == nc))
    def _():
        st_ref[...] = jnp.zeros(st_ref.shape, F32)

    q = q_ref[0]
    v = v_ref[0]
    dk = q.shape[1] // RET_HEADS
    dv = v.shape[1] // RET_HEADS

    @pl.when(j < nc)
    def _():
        kz = kzb_ref[0]
        c = nc - 1 - j
        for h in range(RET_HEADS):
            yb_ref[c, :, h * dv:(h + 1) * dv] = _state_step(
                st_ref, h, q[:, h * dk:(h + 1) * dk], kz[:, h * dk:(h + 1) * dk],
                v[:, h * dv:(h + 1) * dv], xib_ref[h], cdb_ref[h])

    @pl.when(j >= nc)
    def _():
        k = k_ref[0]
        kz = kzf_ref[0]
        c = j - nc
        for h in range(RET_HEADS):
            q_h = q[:, h * dk:(h + 1) * dk]
            v_h = v[:, h * dv:(h + 1) * dv]
            s = lax.dot_general(q_h, k[:, h * dk:(h + 1) * dk], (((1,), (1,)), ((), ())),
                                preferred_element_type=F32)
            inner = jnp.dot((s * dm_ref[h]).astype(BF16), v_h, preferred_element_type=F32)
            cross = _state_step(st_ref, h, q_h, kz[:, h * dk:(h + 1) * dk], v_h,
                                xif_ref[h], cdf_ref[h])
            y = inner + cross + yb_ref[c, :, h * dv:(h + 1) * dv]
            yn = y * lax.rsqrt(jnp.mean(y * y, axis=-1, keepdims=True) + EPS)
            o_ref[0, :, h * dv:(h + 1) * dv] = (
                sg_ref[0, :, h * dv:(h + 1) * dv] * yn).astype(BF16)


def _retention(q, k, kzf, kzb, v, sg, xi_f, xi_b, dmat, cd_f, cd_b):
    bsz, seq, nqk = q.shape
    nv = v.shape[2]
    L = RET_CHUNK
    nc = seq // L
    dk = nqk // RET_HEADS
    dv = nv // RET_HEADS
    smem = pl.BlockSpec(memory_space=pltpu.SMEM)
    both = lambda j: jnp.where(j < nc, nc - 1 - j, j - nc)
    bwd_only = lambda j: jnp.where(j < nc, nc - 1 - j, 0)
    fwd_only = lambda j: jnp.where(j < nc, 0, j - nc)
    blk = lambda width, chunk: pl.BlockSpec((1, L, width), lambda b, j: (b, chunk(j), 0))
    return pl.pallas_call(
        _ret_scan_kernel,
        grid=(bsz, 2 * nc),
        in_specs=[smem, smem, blk(nqk, both), blk(nqk, fwd_only), blk(nqk, fwd_only),
                  blk(nqk, bwd_only), blk(nv, both), blk(nv, fwd_only),
                  _resident(xi_f.shape), _resident(xi_b.shape), _resident(dmat.shape)],
        out_specs=blk(nv, fwd_only),
        out_shape=jax.ShapeDtypeStruct((bsz, seq, nv), BF16),
        scratch_shapes=[pltpu.VMEM((RET_HEADS, dk, dv), F32), pltpu.VMEM((nc, L, nv), F32)],
        compiler_params=_params(2),
        name="ret_scan",
    )(cd_f, cd_b, q, k, kzf, kzb, v, sg, xi_f, xi_b, dmat)


def _decay_tables(decay_fwd, decay_bwd, dk):
    L = RET_CHUNK
    lg_f = jax.nn.log_sigmoid(decay_fwd.astype(F32))
    lg_b = jax.nn.log_sigmoid(decay_bwd.astype(F32))
    idx = jnp.arange(L, dtype=F32)
    diff = idx[:, None] - idx[None, :]
    dm = jnp.where((diff >= 0)[None],
                   jnp.exp(lg_f[:, None, None] * jnp.maximum(diff, 0.0)[None]),
                   jnp.exp(lg_b[:, None, None] * jnp.maximum(-diff, 0.0)[None]))
    xi_f = jnp.exp(lg_f[:, None] * (idx + 1.0)[None])[:, :, None]
    xi_b = jnp.exp(lg_b[:, None] * (L - idx)[None])[:, :, None]
    zeta_f = jnp.exp(lg_f[:, None] * (L - 1.0 - idx)[None])
    zeta_b = jnp.exp(lg_b[:, None] * idx[None])
    zf = jnp.repeat(zeta_f.T, dk, axis=1)
    zb = jnp.repeat(zeta_b.T, dk, axis=1)
    return dm, xi_f, xi_b, zf, zb, jnp.exp(lg_f * L), jnp.exp(lg_b * L)


def _rope_tables(seq, dk):
    inv = ROPE_BASE ** (-jnp.arange(0, dk, 2, dtype=F32) / dk)
    ang = jnp.arange(seq, dtype=F32)[:, None] * inv[None]
    return jnp.cos(ang), jnp.sin(ang)


def kernel(x, c, rel_bias, att_w_qkv, att_w_o, att_sink, ret_w_in, ret_w_o, ret_decay_fwd,
           ret_decay_bwd, ada_w, ada_b, mix_norm_pre, mix_norm_post, ffn_norm_pre,
           ffn_norm_post, ffn_w_in, ffn_w_out):
    bsz, seq, d = x.shape
    t = bsz * seq
    assert seq % ROW_TILE == 0 and ROW_TILE % RET_CHUNK == 0 and d % RET_HEADS == 0
    assert ada_w.shape[0] == 2 and ffn_w_out.shape[1] % FFN_CHUNK == 0
    rows = -(-bsz // 8) * 8
    vec = lambda a: a.reshape(1, d)

    c_pad = jnp.pad(c, ((0, rows - bsz), (0, 0)))
    mod = _ada_mod(c_pad, ada_w, ada_b)
    modv = mod.reshape(2 * rows * 6, 1, d)

    x2d = x.reshape(t, d)

    nq = Q_HEADS * HEAD_DIM
    nkv = KV_HEADS * HEAD_DIM
    heads = [(2 * p + e) * GROUP + g for p in range(KV_HEADS // 2) for g in range(GROUP)
             for e in range(2)]
    w_q = att_w_qkv[0][:, :nq].reshape(d, Q_HEADS, HEAD_DIM)[:, jnp.array(heads)].reshape(d, nq)
    w_q = (w_q * HEAD_DIM ** -0.5).astype(BF16)
    w_k = att_w_qkv[0][:, nq:nq + nkv].astype(BF16)
    w_vt = att_w_qkv[0][:, nq + nkv:].T.astype(BF16)
    q, k, vt = _qkv_proj(x2d, vec(mix_norm_pre[0]), modv, w_q, w_k, w_vt, seq, rows)
    att = _attention(q.reshape(bsz, seq, nq), k.reshape(bsz, seq, nkv), vt,
                     rel_bias.astype(F32), att_sink[0].astype(F32))
    x1, h1 = _post_ffn(att.reshape(t, nq), x2d, att_w_o[0].astype(BF16), vec(mix_norm_post[0]),
                       vec(ffn_norm_pre[0]), ffn_w_in[0].astype(BF16), ffn_w_out[0].astype(BF16),
                       vec(ffn_norm_post[0]), modv, 0, seq, rows, g_next=vec(mix_norm_pre[1]))

    dk = d // RET_HEADS
    nqk = RET_HEADS * dk
    kscale = jnp.concatenate([jnp.ones((nqk,), F32), jnp.full((nqk,), dk ** -0.5, F32),
                              jnp.ones((ret_w_in.shape[2] - 2 * nqk,), F32)])
    w_rin = (ret_w_in[0] * kscale[None]).astype(BF16)
    dm, xi_f, xi_b, zf, zb, cd_f, cd_b = _decay_tables(ret_decay_fwd[0], ret_decay_bwd[0], dk)
    cos, sin = _rope_tables(seq, dk)
    q, k, kzf, kzb, v, sg = _ret_in(h1, w_rin, cos, sin, zf, zb, seq)
    r3 = lambda a: a.reshape(bsz, seq, a.shape[1])
    gated = _retention(r3(q), r3(k), r3(kzf), r3(kzb), r3(v), r3(sg), xi_f, xi_b, dm, cd_f, cd_b)
    (x2,) = _post_ffn(gated.reshape(t, -1), x1, ret_w_o[0].astype(BF16), vec(mix_norm_post[1]),
                      vec(ffn_norm_pre[1]), ffn_w_in[1].astype(BF16), ffn_w_out[1].astype(BF16),
                      vec(ffn_norm_post[1]), modv, 1, seq, rows)
    return x2.reshape(bsz, seq, d)
```

```python
import functools
import math

import jax
import jax.numpy as jnp
from jax import lax
from jax.experimental import pallas as pl
from jax.experimental.pallas import tpu as pltpu

F32 = jnp.float32
BF16 = jnp.bfloat16

EPS = 1e-6
NEG = -1e30
LOG2E = math.log2(math.e)

Q_HEADS = 16
KV_HEADS = 4
GROUP = Q_HEADS // KV_HEADS
HEAD_DIM = 64
ATT_BLOCK = 128
REL_BUCKETS = 32
RET_HEADS = 4
RET_CHUNK = 256
ROPE_BASE = 10000.0
FFN_CHUNK = 256
ROW_TILE = 512
VMEM_LIMIT = 56 * 1024 * 1024


def _silu(x):
    return x * (1.0 / (1.0 + jnp.exp(-x)))


def _rms(xf, g):
    ms = jnp.mean(xf * xf, axis=-1, keepdims=True)
    return (xf * lax.rsqrt(ms + EPS)) * g


def _resident(shape):
    zeros = (0,) * len(shape)
    return pl.BlockSpec(shape, lambda *_: zeros, pipeline_mode=pl.Buffered(1))


def _params(n_axes, vmem=VMEM_LIMIT):
    return pltpu.CompilerParams(
        dimension_semantics=("arbitrary",) * n_axes, vmem_limit_bytes=vmem)


def _ada_kernel(c_ref, w_ref, b_ref, o_ref):
    ca = _silu(c_ref[...])
    o_ref[0] = jnp.dot(ca.astype(BF16), w_ref[0].astype(BF16),
                       preferred_element_type=F32) + b_ref[0]


def _ada_mod(c_pad, ada_w, ada_b):
    depth, d, n = ada_w.shape
    rows = c_pad.shape[0]
    tn = 1536
    return pl.pallas_call(
        _ada_kernel,
        grid=(depth, n // tn),
        in_specs=[
            pl.BlockSpec((rows, d), lambda i, j: (0, 0)),
            pl.BlockSpec((1, d, tn), lambda i, j: (i, 0, j)),
            pl.BlockSpec((1, 1, tn), lambda i, j: (i, 0, j)),
        ],
        out_specs=pl.BlockSpec((1, rows, tn), lambda i, j: (i, 0, j)),
        out_shape=jax.ShapeDtypeStruct((depth, rows, n), F32),
        compiler_params=_params(2),
        name="ada_mod",
    )(c_pad, ada_w, ada_b.reshape(depth, 1, n))


def _mod_spec(layer, slot, tiles_per_batch, n_batch_rows):
    base = layer * n_batch_rows * 6 + slot

    def index(t):
        return (base + (t // tiles_per_batch) * 6, 0, 0)
    return index


def _qkv_kernel(x_ref, g_ref, sc_ref, sh_ref, wq_ref, wk_ref, wvt_ref, q_ref, k_ref, vt_ref):
    h = _rms(x_ref[...], g_ref[...]) * (1.0 + sc_ref[0]) + sh_ref[0]
    hb = h.astype(BF16)
    q_ref[...] = jnp.dot(hb, wq_ref[...], preferred_element_type=F32).astype(BF16)
    k_ref[...] = jnp.dot(hb, wk_ref[...], preferred_element_type=F32).astype(BF16)
    vt_ref[0] = lax.dot_general(wvt_ref[...], hb, (((1,), (1,)), ((), ())),
                                preferred_element_type=F32).astype(BF16)


def _qkv_proj(x2d, g_pre, modv, w_q, w_k, w_vt, seq, n_batch_rows):
    t, d = x2d.shape
    nq = w_q.shape[1]
    nkv = w_k.shape[1]
    tm = ROW_TILE
    tpb = seq // tm
    vec = lambda idx: pl.BlockSpec((1, 1, d), idx)
    return pl.pallas_call(
        _qkv_kernel,
        grid=(t // tm,),
        in_specs=[
            pl.BlockSpec((tm, d), lambda i: (i, 0)),
            _resident((1, d)),
            vec(_mod_spec(0, 1, tpb, n_batch_rows)),
            vec(_mod_spec(0, 0, tpb, n_batch_rows)),
            _resident(w_q.shape), _resident(w_k.shape), _resident(w_vt.shape),
        ],
        out_specs=[
            pl.BlockSpec((tm, nq), lambda i: (i, 0)),
            pl.BlockSpec((tm, nkv), lambda i: (i, 0)),
            pl.BlockSpec((1, nkv, tm), lambda i: (i // tpb, 0, i % tpb)),
        ],
        out_shape=[
            jax.ShapeDtypeStruct((t, nq), BF16),
            jax.ShapeDtypeStruct((t, nkv), BF16),
            jax.ShapeDtypeStruct((t // seq, nkv, seq), BF16),
        ],
        compiler_params=_params(1),
        name="qkv_proj",
    )(x2d, g_pre, modv, modv, w_q, w_k, w_vt)


def _att_tables(rb_ref, b2_ref, m2_ref):
    L = ATT_BLOCK
    rows = 64
    for r in range(3 * L // rows):
        j = lax.broadcasted_iota(jnp.int32, (rows, L), 0) + r * rows
        t = lax.broadcasted_iota(jnp.int32, (rows, L), 1)
        rel = j - L - t
        n = jnp.abs(rel)
        large = jnp.full((rows, L), 8, jnp.int32)
        for thr in (12, 16, 23, 32, 46, 64, 91):
            large = large + (n >= thr).astype(jnp.int32)
        bucket = jnp.where(rel > 0, 16, 0) + jnp.where(n < 8, n, large)
        in_win = n <= L
        visible = (in_win & (j >= L), in_win, in_win & (j < 2 * L))
        piece = slice(r * rows, (r + 1) * rows)
        for kind in range(3):
            m2_ref[kind, piece, :] = jnp.where(visible[kind], LOG2E, 0.0).astype(F32)

        def body(hq, carry):
            acc = jnp.zeros((rows, L), F32)
            for b in range(REL_BUCKETS):
                acc = jnp.where(bucket == b, rb_ref[b, hq], acc)
            acc = acc * LOG2E
            for kind in range(3):
                b2_ref[kind, hq, piece, :] = jnp.where(visible[kind], acc, NEG)
            return carry
        lax.fori_loop(0, Q_HEADS, body, 0)


def _att_kernel(rb_ref, sink_ref, q_ref, kp_ref, kc_ref, kn_ref, vp_ref, vc_ref, vn_ref,
                o_ref, b2_ref, m2_ref):
    b = pl.program_id(0)
    n = pl.program_id(1)
    nb = pl.num_programs(1)

    @pl.when((b == 0) & (n == 0))
    def _():
        _att_tables(rb_ref, b2_ref, m2_ref)

    L = ATT_BLOCK
    dh = HEAD_DIM
    kind = jnp.where(n == 0, 0, jnp.where(n == nb - 1, 2, 1))
    m2 = m2_ref[kind]
    q = q_ref[0]
    kb = jnp.concatenate([kp_ref[0], kc_ref[0], kn_ref[0]], axis=0)
    vt = jnp.concatenate([vp_ref[0], vc_ref[0], vn_ref[0]], axis=1)
    lane = lax.broadcasted_iota(jnp.int32, (L, 2 * dh), 1)
    zero = jnp.zeros((L, 2 * dh), BF16)
    o_rows = []
    for p in range(KV_HEADS // 2):
        k_pair = kb[:, p * 2 * dh:(p + 1) * 2 * dh]
        for e in range(2):
            h = 2 * p + e
            mine = (lane < dh) if e == 0 else (lane >= dh)
            qz = jnp.concatenate(
                [jnp.where(mine, q[:, (p * GROUP + g) * 2 * dh:(p * GROUP + g + 1) * 2 * dh], zero)
                 for g in range(GROUP)], axis=0)
            st = lax.dot_general(k_pair, qz, (((1,), (1,)), ((), ())),
                                 preferred_element_type=F32)
            es, rs = [], []
            for g in range(GROUP):
                hq = h * GROUP + g
                sk2 = jnp.full((1, L), sink_ref[hq], F32) * LOG2E
                l2 = st[:, g * L:(g + 1) * L] * m2 + b2_ref[kind, hq]
                m = jnp.maximum(jnp.max(l2, axis=0, keepdims=True), sk2)
                ex = jnp.exp2(l2 - m)
                den = jnp.sum(ex, axis=0, keepdims=True) + jnp.exp2(sk2 - m)
                es.append(ex.astype(BF16))
                rs.append(1.0 / den)
            et = jnp.concatenate(es, axis=1)
            ot = jnp.dot(vt[h * dh:(h + 1) * dh, :], et, preferred_element_type=F32)
            ot = ot * jnp.concatenate(rs, axis=1)
            o_rows += [ot[:, g * L:(g + 1) * L] for g in range(GROUP)]
    ot_all = jnp.concatenate(o_rows, axis=0)
    o_ref[0] = ot_all.T.astype(BF16)


def _attention(q, k, vt, rel_bias, sink):
    bsz, seq, dq = q.shape
    dkv = k.shape[2]
    L = ATT_BLOCK
    nb = seq // L
    smem = pl.BlockSpec(memory_space=pltpu.SMEM)
    prev = lambda i: jnp.maximum(i - 1, 0)
    nxt = lambda i: jnp.minimum(i + 1, nb - 1)
    return pl.pallas_call(
        _att_kernel,
        grid=(bsz, nb),
        in_specs=[
            smem, smem,
            pl.BlockSpec((1, L, dq), lambda b, i: (b, i, 0)),
            pl.BlockSpec((1, L, dkv), lambda b, i: (b, prev(i), 0)),
            pl.BlockSpec((1, L, dkv), lambda b, i: (b, i, 0)),
            pl.BlockSpec((1, L, dkv), lambda b, i: (b, nxt(i), 0)),
            pl.BlockSpec((1, dkv, L), lambda b, i: (b, 0, prev(i))),
            pl.BlockSpec((1, dkv, L), lambda b, i: (b, 0, i)),
            pl.BlockSpec((1, dkv, L), lambda b, i: (b, 0, nxt(i))),
        ],
        out_specs=pl.BlockSpec((1, L, dq), lambda b, i: (b, i, 0)),
        out_shape=jax.ShapeDtypeStruct((bsz, seq, dq), BF16),
        scratch_shapes=[
            pltpu.VMEM((3, Q_HEADS, 3 * L, L), F32),
            pltpu.VMEM((3, 3 * L, L), F32),
        ],
        compiler_params=_params(2),
        name="swa_attention",
    )(rel_bias, sink, q, k, k, k, vt, vt, vt)


def _post_ffn_kernel(*refs, emit_next):
    (a_ref, x_ref, wo_ref, gpost_ref, g1_ref, gpre_ref, sc2_ref, sh2_ref,
     win_ref, wout_ref, gfpost_ref, g2_ref) = refs[:12]
    if emit_next:
        gn_ref, scn_ref, shn_ref, xo_ref, ho_ref = refs[12:]
    else:
        (xo_ref,) = refs[12:]
    hidden = wout_ref.shape[0]

    y = jnp.dot(a_ref[...], wo_ref[...], preferred_element_type=F32)
    x1 = x_ref[...] + g1_ref[0] * _rms(y, gpost_ref[...])
    h = _rms(x1, gpre_ref[...]) * (1.0 + sc2_ref[0]) + sh2_ref[0]
    hb = h.astype(BF16)
    acc = jnp.zeros(x1.shape, F32)
    for c in range(hidden // FFN_CHUNK):
        lo = c * FFN_CHUNK
        a = jnp.dot(hb, win_ref[:, lo:lo + FFN_CHUNK], preferred_element_type=F32)
        b = jnp.dot(hb, win_ref[:, hidden + lo:hidden + lo + FFN_CHUNK],
                    preferred_element_type=F32)
        act = (_silu(a) * b).astype(BF16)
        acc = acc + jnp.dot(act, wout_ref[lo:lo + FFN_CHUNK, :], preferred_element_type=F32)
    x2 = x1 + g2_ref[0] * _rms(acc, gfpost_ref[...])
    xo_ref[...] = x2
    if emit_next:
        hn = _rms(x2, gn_ref[...]) * (1.0 + scn_ref[0]) + shn_ref[0]
        ho_ref[...] = hn.astype(BF16)


def _post_ffn(a2d, x2d, w_o, g_post, g_pre, w_in, w_out, g_fpost, modv, layer,
              seq, n_batch_rows, g_next=None):
    t, d = x2d.shape
    kin = a2d.shape[1]
    tm = ROW_TILE
    tpb = seq // tm
    emit_next = g_next is not None
    vec = lambda lyr, slot: pl.BlockSpec((1, 1, d), _mod_spec(lyr, slot, tpb, n_batch_rows))
    row = lambda width: pl.BlockSpec((tm, width), lambda i: (i, 0))
    in_specs = [
        row(kin), row(d), _resident(w_o.shape), _resident((1, d)), vec(layer, 2),
        _resident((1, d)), vec(layer, 4), vec(layer, 3),
        _resident(w_in.shape), _resident(w_out.shape), _resident((1, d)), vec(layer, 5),
    ]
    args = [a2d, x2d, w_o, g_post, modv, g_pre, modv, modv, w_in, w_out, g_fpost, modv]
    out_specs = [row(d)]
    out_shape = [jax.ShapeDtypeStruct((t, d), F32)]
    if emit_next:
        in_specs += [_resident((1, d)), vec(layer + 1, 1), vec(layer + 1, 0)]
        args += [g_next, modv, modv]
        out_specs.append(row(d))
        out_shape.append(jax.ShapeDtypeStruct((t, d), BF16))
    return pl.pallas_call(
        functools.partial(_post_ffn_kernel, emit_next=emit_next),
        grid=(t // tm,),
        in_specs=in_specs,
        out_specs=out_specs,
        out_shape=out_shape,
        compiler_params=_params(1),
        name="post_ffn_next" if emit_next else "post_ffn",
    )(*args)


def _ret_in_kernel(h_ref, w_ref, cos_ref, sin_ref, zf_ref, zb_ref,
                   q_ref, k_ref, kzf_ref, kzb_ref, v_ref, sg_ref):
    hb = h_ref[...]
    tm = hb.shape[0]
    cos = cos_ref[...]
    sin = sin_ref[...]
    dk = w_ref.shape[0] // RET_HEADS
    half = dk // 2
    nqk = RET_HEADS * dk
    nv = (w_ref.shape[1] - 2 * nqk) // 2

    yq = jnp.dot(hb, w_ref[:, 0:nqk], preferred_element_type=F32)
    yk = jnp.dot(hb, w_ref[:, nqk:2 * nqk], preferred_element_type=F32)
    zf = zf_ref[...][None]
    zb = zb_ref[...][None]
    for h in range(RET_HEADS):
        for off in (0, half):
            lo = h * dk + off
            sign = -1.0 if off == 0 else 1.0
            other = h * dk + (half - off)
            rq = yq[:, lo:lo + half] * cos + sign * (yq[:, other:other + half] * sin)
            rk = yk[:, lo:lo + half] * cos + sign * (yk[:, other:other + half] * sin)
            q_ref[:, lo:lo + half] = rq.astype(BF16)
            k_ref[:, lo:lo + half] = rk.astype(BF16)
            rk3 = rk.reshape(tm // RET_CHUNK, RET_CHUNK, half)
            kzf_ref[:, lo:lo + half] = (rk3 * zf[:, :, lo:lo + half]).reshape(tm, half).astype(BF16)
            kzb_ref[:, lo:lo + half] = (rk3 * zb[:, :, lo:lo + half]).reshape(tm, half).astype(BF16)
    v_ref[...] = jnp.dot(hb, w_ref[:, 2 * nqk:2 * nqk + nv],
                         preferred_element_type=F32).astype(BF16)
    g = jnp.dot(hb, w_ref[:, 2 * nqk + nv:], preferred_element_type=F32)
    sg_ref[...] = _silu(g)


def _ret_in(h2d, w_in, cos, sin, zf, zb, seq):
    t, d = h2d.shape
    tm = ROW_TILE
    tpb = seq // tm
    nqk = RET_HEADS * (d // RET_HEADS)
    nv = (w_in.shape[1] - 2 * nqk) // 2
    row = lambda width: pl.BlockSpec((tm, width), lambda i: (i, 0))
    pos = pl.BlockSpec((tm, cos.shape[1]), lambda i: (i % tpb, 0))
    return pl.pallas_call(
        _ret_in_kernel,
        grid=(t // tm,),
        in_specs=[row(d), _resident(w_in.shape), pos, pos,
                  _resident(zf.shape), _resident(zb.shape)],
        out_specs=[row(nqk), row(nqk), row(nqk), row(nqk), row(nv), row(nv)],
        out_shape=[jax.ShapeDtypeStruct((t, nqk), BF16)] * 4
        + [jax.ShapeDtypeStruct((t, nv), BF16), jax.ShapeDtypeStruct((t, nv), F32)],
        compiler_params=_params(1),
        name="ret_in_proj",
    )(h2d, w_in, cos, sin, zf, zb)


def _state_step(st_ref, h, q_h, kz_h, v_h, xi, cd):
    st = st_ref[h]
    cross = jnp.dot(q_h, st.astype(BF16), preferred_element_type=F32) * xi
    upd = lax.dot_general(kz_h, v_h, (((0,), (0,)), ((), ())), preferred_element_type=F32)
    st_ref[h] = st * cd + upd
    return cross


def _ret_scan_kernel(cdf_ref, cdb_ref, q_ref, k_ref, kzf_ref, kzb_ref, v_ref, sg_ref,
                     xif_ref, xib_ref, dm_ref, o_ref, st_ref, yb_ref):
    j = pl.program_id(1)
    nc = pl.num_programs(1) // 2

    @pl.when((j == 0) | (j == nc))
    def _():
        st_ref[...] = jnp.zeros(st_ref.shape, F32)

    q = q_ref[0]
    v = v_ref[0]
    dk = q.shape[1] // RET_HEADS
    dv = v.shape[1] // RET_HEADS

    @pl.when(j < nc)
    def _():
        kz = kzb_ref[0]
        c = nc - 1 - j
        for h in range(RET_HEADS):
            yb_ref[c, :, h * dv:(h + 1) * dv] = _state_step(
                st_ref, h, q[:, h * dk:(h + 1) * dk], kz[:, h * dk:(h + 1) * dk],
                v[:, h * dv:(h + 1) * dv], xib_ref[h], cdb_ref[h])

    @pl.when(j >= nc)
    def _():
        k = k_ref[0]
        kz = kzf_ref[0]
        c = j - nc
        for h in range(RET_HEADS):
            q_h = q[:, h * dk:(h + 1) * dk]
            v_h = v[:, h * dv:(h + 1) * dv]
            s = lax.dot_general(q_h, k[:, h * dk:(h + 1) * dk], (((1,), (1,)), ((), ())),
                                preferred_element_type=F32)
            inner = jnp.dot((s * dm_ref[h]).astype(BF16), v_h, preferred_element_type=F32)
            cross = _state_step(st_ref, h, q_h, kz[:, h * dk:(h + 1) * dk], v_h,
                                xif_ref[h], cdf_ref[h])
            y = inner + cross + yb_ref[c, :, h * dv:(h + 1) * dv]
            yn = y * lax.rsqrt(jnp.mean(y * y, axis=-1, keepdims=True) + EPS)
            o_ref[0, :, h * dv:(h + 1) * dv] = (
                sg_ref[0, :, h * dv:(h + 1) * dv] * yn).astype(BF16)


def _retention(q, k, kzf, kzb, v, sg, xi_f, xi_b, dmat, cd_f, cd_b):
    bsz, seq, nqk = q.shape
    nv = v.shape[2]
    L = RET_CHUNK
    nc = seq // L
    dk = nqk // RET_HEADS
    dv = nv // RET_HEADS
    smem = pl.BlockSpec(memory_space=pltpu.SMEM)
    both = lambda j: jnp.where(j < nc, nc - 1 - j, j - nc)
    bwd_only = lambda j: jnp.where(j < nc, nc - 1 - j, 0)
    fwd_only = lambda j: jnp.where(j < nc, 0, j - nc)
    blk = lambda width, chunk: pl.BlockSpec((1, L, width), lambda b, j: (b, chunk(j), 0))
    return pl.pallas_call(
        _ret_scan_kernel,
        grid=(bsz, 2 * nc),
        in_specs=[smem, smem, blk(nqk, both), blk(nqk, fwd_only), blk(nqk, fwd_only),
                  blk(nqk, bwd_only), blk(nv, both), blk(nv, fwd_only),
                  _resident(xi_f.shape), _resident(xi_b.shape), _resident(dmat.shape)],
        out_specs=blk(nv, fwd_only),
        out_shape=jax.ShapeDtypeStruct((bsz, seq, nv), BF16),
        scratch_shapes=[pltpu.VMEM((RET_HEADS, dk, dv), F32), pltpu.VMEM((nc, L, nv), F32)],
        compiler_params=_params(2),
        name="ret_scan",
    )(cd_f, cd_b, q, k, kzf, kzb, v, sg, xi_f, xi_b, dmat)


def _decay_tables(decay_fwd, decay_bwd, dk):
    L = RET_CHUNK
    lg_f = jax.nn.log_sigmoid(decay_fwd.astype(F32))
    lg_b = jax.nn.log_sigmoid(decay_bwd.astype(F32))
    idx = jnp.arange(L, dtype=F32)
    diff = idx[:, None] - idx[None, :]
    dm = jnp.where((diff >= 0)[None],
                   jnp.exp(lg_f[:, None, None] * jnp.maximum(diff, 0.0)[None]),
                   jnp.exp(lg_b[:, None, None] * jnp.maximum(-diff, 0.0)[None]))
    xi_f = jnp.exp(lg_f[:, None] * (idx + 1.0)[None])[:, :, None]
    xi_b = jnp.exp(lg_b[:, None] * (L - idx)[None])[:, :, None]
    zeta_f = jnp.exp(lg_f[:, None] * (L - 1.0 - idx)[None])
    zeta_b = jnp.exp(lg_b[:, None] * idx[None])
    zf = jnp.repeat(zeta_f.T, dk, axis=1)
    zb = jnp.repeat(zeta_b.T, dk, axis=1)
    return dm, xi_f, xi_b, zf, zb, jnp.exp(lg_f * L), jnp.exp(lg_b * L)


def _rope_tables(seq, dk):
    inv = ROPE_BASE ** (-jnp.arange(0, dk, 2, dtype=F32) / dk)
    ang = jnp.arange(seq, dtype=F32)[:, None] * inv[None]
    return jnp.cos(ang), jnp.sin(ang)


def kernel(x, c, rel_bias, att_w_qkv, att_w_o, att_sink, ret_w_in, ret_w_o, ret_decay_fwd,
           ret_decay_bwd, ada_w, ada_b, mix_norm_pre, mix_norm_post, ffn_norm_pre,
           ffn_norm_post, ffn_w_in, ffn_w_out):
    bsz, seq, d = x.shape
    t = bsz * seq
    assert seq % ROW_TILE == 0 and ROW_TILE % RET_CHUNK == 0 and d % RET_HEADS == 0
    assert ada_w.shape[0] == 2 and ffn_w_out.shape[1] % FFN_CHUNK == 0
    rows = -(-bsz // 8) * 8
    vec = lambda a: a.reshape(1, d)

    c_pad = jnp.pad(c, ((0, rows - bsz), (0, 0)))
    mod = _ada_mod(c_pad, ada_w, ada_b)
    modv = mod.reshape(2 * rows * 6, 1, d)

    x2d = x.reshape(t, d)

    nq = Q_HEADS * HEAD_DIM
    nkv = KV_HEADS * HEAD_DIM
    heads = [(2 * p + e) * GROUP + g for p in range(KV_HEADS // 2) for g in range(GROUP)
             for e in range(2)]
    w_q = att_w_qkv[0][:, :nq].reshape(d, Q_HEADS, HEAD_DIM)[:, jnp.array(heads)].reshape(d, nq)
    w_q = (w_q * HEAD_DIM ** -0.5).astype(BF16)
    w_k = att_w_qkv[0][:, nq:nq + nkv].astype(BF16)
    w_vt = att_w_qkv[0][:, nq + nkv:].T.astype(BF16)
    q, k, vt = _qkv_proj(x2d, vec(mix_norm_pre[0]), modv, w_q, w_k, w_vt, seq, rows)
    att = _attention(q.reshape(bsz, seq, nq), k.reshape(bsz, seq, nkv), vt,
                     rel_bias.astype(F32), att_sink[0].astype(F32))
    x1, h1 = _post_ffn(att.reshape(t, nq), x2d, att_w_o[0].astype(BF16), vec(mix_norm_post[0]),
                       vec(ffn_norm_pre[0]), ffn_w_in[0].astype(BF16), ffn_w_out[0].astype(BF16),
                       vec(ffn_norm_post[0]), modv, 0, seq, rows, g_next=vec(mix_norm_pre[1]))

    dk = d // RET_HEADS
    nqk = RET_HEADS * dk
    kscale = jnp.concatenate([jnp.ones((nqk,), F32), jnp.full((nqk,), dk ** -0.5, F32),
                              jnp.ones((ret_w_in.shape[2] - 2 * nqk,), F32)])
    w_rin = (ret_w_in[0] * kscale[None]).astype(BF16)
    dm, xi_f, xi_b, zf, zb, cd_f, cd_b = _decay_tables(ret_decay_fwd[0], ret_decay_bwd[0], dk)
    cos, sin = _rope_tables(seq, dk)
    q, k, kzf, kzb, v, sg = _ret_in(h1, w_rin, cos, sin, zf, zb, seq)
    r3 = lambda a: a.reshape(bsz, seq, a.shape[1])
    gated = _retention(r3(q), r3(k), r3(kzf), r3(kzb), r3(v), r3(sg), xi_f, xi_b, dm, cd_f, cd_b)
    (x2,) = _post_ffn(gated.reshape(t, -1), x1, ret_w_o[0].astype(BF16), vec(mix_norm_post[1]),
                      vec(ffn_norm_pre[1]), ffn_w_in[1].astype(BF16), ffn_w_out[1].astype(BF16),
                      vec(ffn_norm_post[1]), modv, 1, seq, rows)
    return x2.reshape(bsz, seq, d)
```

```python
import functools
import math

import jax
import jax.numpy as jnp
from jax import lax
from jax.experimental import pallas as pl
from jax.experimental.pallas import tpu as pltpu

F32 = jnp.float32
BF16 = jnp.bfloat16

EPS = 1e-6
NEG = -1e30
LOG2E = math.log2(math.e)

Q_HEADS = 16
KV_HEADS = 4
GROUP = Q_HEADS // KV_HEADS
HEAD_DIM = 64
ATT_BLOCK = 128
REL_BUCKETS = 32
RET_HEADS = 4
RET_CHUNK = 256
ROPE_BASE = 10000.0
FFN_CHUNK = 256
ROW_TILE = 512
VMEM_LIMIT = 56 * 1024 * 1024


def _silu(x):
    return x * (1.0 / (1.0 + jnp.exp(-x)))


def _rms(xf, g):
    ms = jnp.mean(xf * xf, axis=-1, keepdims=True)
    return (xf * lax.rsqrt(ms + EPS)) * g


def _resident(shape):
    zeros = (0,) * len(shape)
    return pl.BlockSpec(shape, lambda *_: zeros, pipeline_mode=pl.Buffered(1))


def _params(n_axes, vmem=VMEM_LIMIT):
    return pltpu.CompilerParams(
        dimension_semantics=("arbitrary",) * n_axes, vmem_limit_bytes=vmem)


def _ada_kernel(c_ref, w_ref, b_ref, o_ref):
    ca = _silu(c_ref[...])
    o_ref[0] = jnp.dot(ca.astype(BF16), w_ref[0].astype(BF16),
                       preferred_element_type=F32) + b_ref[0]


def _ada_mod(c_pad, ada_w, ada_b):
    depth, d, n = ada_w.shape
    rows = c_pad.shape[0]
    tn = 1536
    return pl.pallas_call(
        _ada_kernel,
        grid=(depth, n // tn),
        in_specs=[
            pl.BlockSpec((rows, d), lambda i, j: (0, 0)),
            pl.BlockSpec((1, d, tn), lambda i, j: (i, 0, j)),
            pl.BlockSpec((1, 1, tn), lambda i, j: (i, 0, j)),
        ],
        out_specs=pl.BlockSpec((1, rows, tn), lambda i, j: (i, 0, j)),
        out_shape=jax.ShapeDtypeStruct((depth, rows, n), F32),
        compiler_params=_params(2),
        name="ada_mod",
    )(c_pad, ada_w, ada_b.reshape(depth, 1, n))


def _mod_spec(layer, slot, tiles_per_batch, n_batch_rows):
    base = layer * n_batch_rows * 6 + slot

    def index(t):
        return (base + (t // tiles_per_batch) * 6, 0, 0)
    return index


def _qkv_kernel(x_ref, g_ref, sc_ref, sh_ref, wq_ref, wk_ref, wvt_ref, q_ref, k_ref, vt_ref):
    h = _rms(x_ref[...], g_ref[...]) * (1.0 + sc_ref[0]) + sh_ref[0]
    hb = h.astype(BF16)
    q_ref[...] = jnp.dot(hb, wq_ref[...], preferred_element_type=F32).astype(BF16)
    k_ref[...] = jnp.dot(hb, wk_ref[...], preferred_element_type=F32).astype(BF16)
    vt_ref[0] = lax.dot_general(wvt_ref[...], hb, (((1,), (1,)), ((), ())),
                                preferred_element_type=F32).astype(BF16)


def _qkv_proj(x2d, g_pre, modv, w_q, w_k, w_vt, seq, n_batch_rows):
    t, d = x2d.shape
    nq = w_q.shape[1]
    nkv = w_k.shape[1]
    tm = ROW_TILE
    tpb = seq // tm
    vec = lambda idx: pl.BlockSpec((1, 1, d), idx)
    return pl.pallas_call(
        _qkv_kernel,
        grid=(t // tm,),
        in_specs=[
            pl.BlockSpec((tm, d), lambda i: (i, 0)),
            _resident((1, d)),
            vec(_mod_spec(0, 1, tpb, n_batch_rows)),
            vec(_mod_spec(0, 0, tpb, n_batch_rows)),
            _resident(w_q.shape), _resident(w_k.shape), _resident(w_vt.shape),
        ],
        out_specs=[
            pl.BlockSpec((tm, nq), lambda i: (i, 0)),
            pl.BlockSpec((tm, nkv), lambda i: (i, 0)),
            pl.BlockSpec((1, nkv, tm), lambda i: (i // tpb, 0, i % tpb)),
        ],
        out_shape=[
            jax.ShapeDtypeStruct((t, nq), BF16),
            jax.ShapeDtypeStruct((t, nkv), BF16),
            jax.ShapeDtypeStruct((t // seq, nkv, seq), BF16),
        ],
        compiler_params=_params(1),
        name="qkv_proj",
    )(x2d, g_pre, modv, modv, w_q, w_k, w_vt)


def _att_tables(rb_ref, b2_ref, m2_ref):
    L = ATT_BLOCK
    rows = 64
    for r in range(3 * L // rows):
        j = lax.broadcasted_iota(jnp.int32, (rows, L), 0) + r * rows
        t = lax.broadcasted_iota(jnp.int32, (rows, L), 1)
        rel = j - L - t
        n = jnp.abs(rel)
        large = jnp.full((rows, L), 8, jnp.int32)
        for thr in (12, 16, 23, 32, 46, 64, 91):
            large = large + (n >= thr).astype(jnp.int32)
        bucket = jnp.where(rel > 0, 16, 0) + jnp.where(n < 8, n, large)
        in_win = n <= L
        visible = (in_win & (j >= L), in_win, in_win & (j < 2 * L))
        piece = slice(r * rows, (r + 1) * rows)
        for kind in range(3):
            m2_ref[kind, piece, :] = jnp.where(visible[kind], LOG2E, 0.0).astype(F32)

        def body(hq, carry):
            acc = jnp.zeros((rows, L), F32)
            for b in range(REL_BUCKETS):
                acc = jnp.where(bucket == b, rb_ref[b, hq], acc)
            acc = acc * LOG2E
            for kind in range(3):
                b2_ref[kind, hq, piece, :] = jnp.where(visible[kind], acc, NEG)
            return carry
        lax.fori_loop(0, Q_HEADS, body, 0)


def _att_kernel(rb_ref, sink_ref, q_ref, kp_ref, kc_ref, kn_ref, vp_ref, vc_ref, vn_ref,
                o_ref, b2_ref, m2_ref):
    b = pl.program_id(0)
    n = pl.program_id(1)
    nb = pl.num_programs(1)

    @pl.when((b == 0) & (n == 0))
    def _():
        _att_tables(rb_ref, b2_ref, m2_ref)

    L = ATT_BLOCK
    dh = HEAD_DIM
    kind = jnp.where(n == 0, 0, jnp.where(n == nb - 1, 2, 1))
    m2 = m2_ref[kind]
    q = q_ref[0]
    kb = jnp.concatenate([kp_ref[0], kc_ref[0], kn_ref[0]], axis=0)
    vt = jnp.concatenate([vp_ref[0], vc_ref[0], vn_ref[0]], axis=1)
    lane = lax.broadcasted_iota(jnp.int32, (L, 2 * dh), 1)
    zero = jnp.zeros((L, 2 * dh), BF16)

    def scores(h):
        p, e = divmod(h, 2)
        k_pair = kb[:, p * 2 * dh:(p + 1) * 2 * dh]
        mine = (lane < dh) if e == 0 else (lane >= dh)
        qz = jnp.concatenate(
            [jnp.where(mine, q[:, (p * GROUP + g) * 2 * dh:(p * GROUP + g + 1) * 2 * dh], zero)
             for g in range(GROUP)], axis=0)
        return lax.dot_general(k_pair, qz, (((1,), (1,)), ((), ())),
                               preferred_element_type=F32)

    o_rows = []
    st_next = scores(0)
    for h in range(KV_HEADS):
        st = st_next
        if h + 1 < KV_HEADS:
            st_next = scores(h + 1)
        es, rs = [], []
        for g in range(GROUP):
            hq = h * GROUP + g
            sk2 = jnp.full((1, L), sink_ref[hq], F32) * LOG2E
            l2 = st[:, g * L:(g + 1) * L] * m2 + b2_ref[kind, hq]
            m = jnp.maximum(jnp.max(l2, axis=0, keepdims=True), sk2)
            ex = jnp.exp2(l2 - m)
            den = jnp.sum(ex, axis=0, keepdims=True) + jnp.exp2(sk2 - m)
            es.append(ex.astype(BF16))
            rs.append(1.0 / den)
        et = jnp.concatenate(es, axis=1)
        ot = jnp.dot(vt[h * dh:(h + 1) * dh, :], et, preferred_element_type=F32)
        ot = ot * jnp.concatenate(rs, axis=1)
        o_rows += [ot[:, g * L:(g + 1) * L] for g in range(GROUP)]
    ot_all = jnp.concatenate(o_rows, axis=0)
    o_ref[0] = ot_all.T.astype(BF16)


def _attention(q, k, vt, rel_bias, sink):
    bsz, seq, dq = q.shape
    dkv = k.shape[2]
    L = ATT_BLOCK
    nb = seq // L
    smem = pl.BlockSpec(memory_space=pltpu.SMEM)
    prev = lambda i: jnp.maximum(i - 1, 0)
    nxt = lambda i: jnp.minimum(i + 1, nb - 1)
    return pl.pallas_call(
        _att_kernel,
        grid=(bsz, nb),
        in_specs=[
            smem, smem,
            pl.BlockSpec((1, L, dq), lambda b, i: (b, i, 0)),
            pl.BlockSpec((1, L, dkv), lambda b, i: (b, prev(i), 0)),
            pl.BlockSpec((1, L, dkv), lambda b, i: (b, i, 0)),
            pl.BlockSpec((1, L, dkv), lambda b, i: (b, nxt(i), 0)),
            pl.BlockSpec((1, dkv, L), lambda b, i: (b, 0, prev(i))),
            pl.BlockSpec((1, dkv, L), lambda b, i: (b, 0, i)),
            pl.BlockSpec((1, dkv, L), lambda b, i: (b, 0, nxt(i))),
        ],
        out_specs=pl.BlockSpec((1, L, dq), lambda b, i: (b, i, 0)),
        out_shape=jax.ShapeDtypeStruct((bsz, seq, dq), BF16),
        scratch_shapes=[
            pltpu.VMEM((3, Q_HEADS, 3 * L, L), F32),
            pltpu.VMEM((3, 3 * L, L), F32),
        ],
        compiler_params=_params(2),
        name="swa_attention",
    )(rel_bias, sink, q, k, k, k, vt, vt, vt)


def _post_ffn_kernel(*refs, emit_next):
    (a_ref, x_ref, wo_ref, gpost_ref, g1_ref, gpre_ref, sc2_ref, sh2_ref,
     win_ref, wout_ref, gfpost_ref, g2_ref) = refs[:12]
    if emit_next:
        gn_ref, scn_ref, shn_ref, xo_ref, ho_ref = refs[12:]
    else:
        (xo_ref,) = refs[12:]
    hidden = wout_ref.shape[0]

    y = jnp.dot(a_ref[...], wo_ref[...], preferred_element_type=F32)
    x1 = x_ref[...] + g1_ref[0] * _rms(y, gpost_ref[...])
    h = _rms(x1, gpre_ref[...]) * (1.0 + sc2_ref[0]) + sh2_ref[0]
    hb = h.astype(BF16)
    acc = jnp.zeros(x1.shape, F32)
    for c in range(hidden // FFN_CHUNK):
        lo = c * FFN_CHUNK
        a = jnp.dot(hb, win_ref[:, lo:lo + FFN_CHUNK], preferred_element_type=F32)
        b = jnp.dot(hb, win_ref[:, hidden + lo:hidden + lo + FFN_CHUNK],
                    preferred_element_type=F32)
        act = (_silu(a) * b).astype(BF16)
        acc = acc + jnp.dot(act, wout_ref[lo:lo + FFN_CHUNK, :], preferred_element_type=F32)
    x2 = x1 + g2_ref[0] * _rms(acc, gfpost_ref[...])
    xo_ref[...] = x2
    if emit_next:
        hn = _rms(x2, gn_ref[...]) * (1.0 + scn_ref[0]) + shn_ref[0]
        ho_ref[...] = hn.astype(BF16)


def _post_ffn(a2d, x2d, w_o, g_post, g_pre, w_in, w_out, g_fpost, modv, layer,
              seq, n_batch_rows, g_next=None):
    t, d = x2d.shape
    kin = a2d.shape[1]
    tm = ROW_TILE
    tpb = seq // tm
    emit_next = g_next is not None
    vec = lambda lyr, slot: pl.BlockSpec((1, 1, d), _mod_spec(lyr, slot, tpb, n_batch_rows))
    row = lambda width: pl.BlockSpec((tm, width), lambda i: (i, 0))
    in_specs = [
        row(kin), row(d), _resident(w_o.shape), _resident((1, d)), vec(layer, 2),
        _resident((1, d)), vec(layer, 4), vec(layer, 3),
        _resident(w_in.shape), _resident(w_out.shape), _resident((1, d)), vec(layer, 5),
    ]
    args = [a2d, x2d, w_o, g_post, modv, g_pre, modv, modv, w_in, w_out, g_fpost, modv]
    out_specs = [row(d)]
    out_shape = [jax.ShapeDtypeStruct((t, d), F32)]
    if emit_next:
        in_specs += [_resident((1, d)), vec(layer + 1, 1), vec(layer + 1, 0)]
        args += [g_next, modv, modv]
        out_specs.append(row(d))
        out_shape.append(jax.ShapeDtypeStruct((t, d), BF16))
    return pl.pallas_call(
        functools.partial(_post_ffn_kernel, emit_next=emit_next),
        grid=(t // tm,),
        in_specs=in_specs,
        out_specs=out_specs,
        out_shape=out_shape,
        compiler_params=_params(1),
        name="post_ffn_next" if emit_next else "post_ffn",
    )(*args)


def _ret_in_kernel(h_ref, w_ref, cos_ref, sin_ref, zf_ref, zb_ref,
                   q_ref, k_ref, kzf_ref, kzb_ref, v_ref, sg_ref):
    hb = h_ref[...]
    tm = hb.shape[0]
    cos = cos_ref[...]
    sin = sin_ref[...]
    dk = w_ref.shape[0] // RET_HEADS
    half = dk // 2
    nqk = RET_HEADS * dk
    nv = (w_ref.shape[1] - 2 * nqk) // 2

    yq = jnp.dot(hb, w_ref[:, 0:nqk], preferred_element_type=F32)
    yk = jnp.dot(hb, w_ref[:, nqk:2 * nqk], preferred_element_type=F32)
    zf = zf_ref[...][None]
    zb = zb_ref[...][None]
    for h in range(RET_HEADS):
        for off in (0, half):
            lo = h * dk + off
            sign = -1.0 if off == 0 else 1.0
            other = h * dk + (half - off)
            rq = yq[:, lo:lo + half] * cos + sign * (yq[:, other:other + half] * sin)
            rk = yk[:, lo:lo + half] * cos + sign * (yk[:, other:other + half] * sin)
            q_ref[:, lo:lo + half] = rq.astype(BF16)
            k_ref[:, lo:lo + half] = rk.astype(BF16)
            rk3 = rk.reshape(tm // RET_CHUNK, RET_CHUNK, half)
            kzf_ref[:, lo:lo + half] = (rk3 * zf[:, :, lo:lo + half]).reshape(tm, half).astype(BF16)
            kzb_ref[:, lo:lo + half] = (rk3 * zb[:, :, lo:lo + half]).reshape(tm, half).astype(BF16)
    v_ref[...] = jnp.dot(hb, w_ref[:, 2 * nqk:2 * nqk + nv],
                         preferred_element_type=F32).astype(BF16)
    g = jnp.dot(hb, w_ref[:, 2 * nqk + nv:], preferred_element_type=F32)
    sg_ref[...] = _silu(g)


def _ret_in(h2d, w_in, cos, sin, zf, zb, seq):
    t, d = h2d.shape
    tm = ROW_TILE
    tpb = seq // tm
    nqk = RET_HEADS * (d // RET_HEADS)
    nv = (w_in.shape[1] - 2 * nqk) // 2
    row = lambda width: pl.BlockSpec((tm, width), lambda i: (i, 0))
    pos = pl.BlockSpec((tm, cos.shape[1]), lambda i: (i % tpb, 0))
    return pl.pallas_call(
        _ret_in_kernel,
        grid=(t // tm,),
        in_specs=[row(d), _resident(w_in.shape), pos, pos,
                  _resident(zf.shape), _resident(zb.shape)],
        out_specs=[row(nqk), row(nqk), row(nqk), row(nqk), row(nv), row(nv)],
        out_shape=[jax.ShapeDtypeStruct((t, nqk), BF16)] * 4
        + [jax.ShapeDtypeStruct((t, nv), BF16), jax.ShapeDtypeStruct((t, nv), F32)],
        compiler_params=_params(1),
        name="ret_in_proj",
    )(h2d, w_in, cos, sin, zf, zb)


def _state_step(st_ref, h, q_h, kz_h, v_h, xi, cd):
    st = st_ref[h]
    cross = jnp.dot(q_h, st.astype(BF16), preferred_element_type=F32) * xi
    upd = lax.dot_general(kz_h, v_h, (((0,), (0,)), ((), ())), preferred_element_type=F32)
    st_ref[h] = st * cd + upd
    return cross


def _ret_scan_kernel(cdf_ref, cdb_ref, q_ref, k_ref, kzf_ref, kzb_ref, v_ref, sg_ref,
                     xif_ref, xib_ref, dm_ref, o_ref, st_ref, yb_ref):
    j = pl.program_id(1)
    nc = pl.num_programs(1) // 2

    @pl.when((j == 0) | (j == nc))
    def _():
        st_ref[...] = jnp.zeros(st_ref.shape, F32)

    q = q_ref[0]
    v = v_ref[0]
    dk = q.shape[1] // RET_HEADS
    dv = v.shape[1] // RET_HEADS

    @pl.when(j < nc)
    def _():
        kz = kzb_ref[0]
        c = nc - 1 - j
        for h in range(RET_HEADS):
            yb_ref[c, :, h * dv:(h + 1) * dv] = _state_step(
                st_ref, h, q[:, h * dk:(h + 1) * dk], kz[:, h * dk:(h + 1) * dk],
                v[:, h * dv:(h + 1) * dv], xib_ref[h], cdb_ref[h])

    @pl.when(j >= nc)
    def _():
        k = k_ref[0]
        kz = kzf_ref[0]
        c = j - nc
        for h in range(RET_HEADS):
            q_h = q[:, h * dk:(h + 1) * dk]
            v_h = v[:, h * dv:(h + 1) * dv]
            s = lax.dot_general(q_h, k[:, h * dk:(h + 1) * dk], (((1,), (1,)), ((), ())),
                                preferred_element_type=F32)
            inner = jnp.dot((s * dm_ref[h]).astype(BF16), v_h, preferred_element_type=F32)
            cross = _state_step(st_ref, h, q_h, kz[:, h * dk:(h + 1) * dk], v_h,
                                xif_ref[h], cdf_ref[h])
            y = inner + cross + yb_ref[c, :, h * dv:(h + 1) * dv]
            yn = y * lax.rsqrt(jnp.mean(y * y, axis=-1, keepdims=True) + EPS)
            o_ref[0, :, h * dv:(h + 1) * dv] = (
                sg_ref[0, :, h * dv:(h + 1) * dv] * yn).astype(BF16)


def _retention(q, k, kzf, kzb, v, sg, xi_f, xi_b, dmat, cd_f, cd_b):
    bsz, seq, nqk = q.shape
    nv = v.shape[2]
    L = RET_CHUNK
    nc = seq // L
    dk = nqk // RET_HEADS
    dv = nv // RET_HEADS
    smem = pl.BlockSpec(memory_space=pltpu.SMEM)
    both = lambda j: jnp.where(j < nc, nc - 1 - j, j - nc)
    bwd_only = lambda j: jnp.where(j < nc, nc - 1 - j, 0)
    fwd_only = lambda j: jnp.where(j < nc, 0, j - nc)
    blk = lambda width, chunk: pl.BlockSpec((1, L, width), lambda b, j: (b, chunk(j), 0))
    return pl.pallas_call(
        _ret_scan_kernel,
        grid=(bsz, 2 * nc),
        in_specs=[smem, smem, blk(nqk, both), blk(nqk, fwd_only), blk(nqk, fwd_only),
                  blk(nqk, bwd_only), blk(nv, both), blk(nv, fwd_only),
                  _resident(xi_f.shape), _resident(xi_b.shape), _resident(dmat.shape)],
        out_specs=blk(nv, fwd_only),
        out_shape=jax.ShapeDtypeStruct((bsz, seq, nv), BF16),
        scratch_shapes=[pltpu.VMEM((RET_HEADS, dk, dv), F32), pltpu.VMEM((nc, L, nv), F32)],
        compiler_params=_params(2),
        name="ret_scan",
    )(cd_f, cd_b, q, k, kzf, kzb, v, sg, xi_f, xi_b, dmat)


def _decay_tables(decay_fwd, decay_bwd, dk):
    L = RET_CHUNK
    lg_f = jax.nn.log_sigmoid(decay_fwd.astype(F32))
    lg_b = jax.nn.log_sigmoid(decay_bwd.astype(F32))
    idx = jnp.arange(L, dtype=F32)
    diff = idx[:, None] - idx[None, :]
    dm = jnp.where((diff >= 0)[None],
                   jnp.exp(lg_f[:, None, None] * jnp.maximum(diff, 0.0)[None]),
                   jnp.exp(lg_b[:, None, None] * jnp.maximum(-diff, 0.0)[None]))
    xi_f = jnp.exp(lg_f[:, None] * (idx + 1.0)[None])[:, :, None]
    xi_b = jnp.exp(lg_b[:, None] * (L - idx)[None])[:, :, None]
    zeta_f = jnp.exp(lg_f[:, None] * (L - 1.0 - idx)[None])
    zeta_b = jnp.exp(lg_b[:, None] * idx[None])
    zf = jnp.repeat(zeta_f.T, dk, axis=1)
    zb = jnp.repeat(zeta_b.T, dk, axis=1)
    return dm, xi_f, xi_b, zf, zb, jnp.exp(lg_f * L), jnp.exp(lg_b * L)


def _rope_tables(seq, dk):
    inv = ROPE_BASE ** (-jnp.arange(0, dk, 2, dtype=F32) / dk)
    ang = jnp.arange(seq, dtype=F32)[:, None] * inv[None]
    return jnp.cos(ang), jnp.sin(ang)


def kernel(x, c, rel_bias, att_w_qkv, att_w_o, att_sink, ret_w_in, ret_w_o, ret_decay_fwd,
           ret_decay_bwd, ada_w, ada_b, mix_norm_pre, mix_norm_post, ffn_norm_pre,
           ffn_norm_post, ffn_w_in, ffn_w_out):
    bsz, seq, d = x.shape
    t = bsz * seq
    assert seq % ROW_TILE == 0 and ROW_TILE % RET_CHUNK == 0 and d % RET_HEADS == 0
    assert ada_w.shape[0] == 2 and ffn_w_out.shape[1] % FFN_CHUNK == 0
    rows = -(-bsz // 8) * 8
    vec = lambda a: a.reshape(1, d)

    c_pad = jnp.pad(c, ((0, rows - bsz), (0, 0)))
    mod = _ada_mod(c_pad, ada_w, ada_b)
    modv = mod.reshape(2 * rows * 6, 1, d)

    x2d = x.reshape(t, d)

    nq = Q_HEADS * HEAD_DIM
    nkv = KV_HEADS * HEAD_DIM
    heads = [(2 * p + e) * GROUP + g for p in range(KV_HEADS // 2) for g in range(GROUP)
             for e in range(2)]
    w_q = att_w_qkv[0][:, :nq].reshape(d, Q_HEADS, HEAD_DIM)[:, jnp.array(heads)].reshape(d, nq)
    w_q = (w_q * HEAD_DIM ** -0.5).astype(BF16)
    w_k = att_w_qkv[0][:, nq:nq + nkv].astype(BF16)
    w_vt = att_w_qkv[0][:, nq + nkv:].T.astype(BF16)
    q, k, vt = _qkv_proj(x2d, vec(mix_norm_pre[0]), modv, w_q, w_k, w_vt, seq, rows)
    att = _attention(q.reshape(bsz, seq, nq), k.reshape(bsz, seq, nkv), vt,
                     rel_bias.astype(F32), att_sink[0].astype(F32))
    x1, h1 = _post_ffn(att.reshape(t, nq), x2d, att_w_o[0].astype(BF16), vec(mix_norm_post[0]),
                       vec(ffn_norm_pre[0]), ffn_w_in[0].astype(BF16), ffn_w_out[0].astype(BF16),
                       vec(ffn_norm_post[0]), modv, 0, seq, rows, g_next=vec(mix_norm_pre[1]))

    dk = d // RET_HEADS
    nqk = RET_HEADS * dk
    kscale = jnp.concatenate([jnp.ones((nqk,), F32), jnp.full((nqk,), dk ** -0.5, F32),
                              jnp.ones((ret_w_in.shape[2] - 2 * nqk,), F32)])
    w_rin = (ret_w_in[0] * kscale[None]).astype(BF16)
    dm, xi_f, xi_b, zf, zb, cd_f, cd_b = _decay_tables(ret_decay_fwd[0], ret_decay_bwd[0], dk)
    cos, sin = _rope_tables(seq, dk)
    q, k, kzf, kzb, v, sg = _ret_in(h1, w_rin, cos, sin, zf, zb, seq)
    r3 = lambda a: a.reshape(bsz, seq, a.shape[1])
    gated = _retention(r3(q), r3(k), r3(kzf), r3(kzb), r3(v), r3(sg), xi_f, xi_b, dm, cd_f, cd_b)
    (x2,) = _post_ffn(gated.reshape(t, -1), x1, ret_w_o[0].astype(BF16), vec(mix_norm_post[1]),
                      vec(ffn_norm_pre[1]), ffn_w_in[1].astype(BF16), ffn_w_out[1].astype(BF16),
                      vec(ffn_norm_post[1]), modv, 1, seq, rows)
    return x2.reshape(bsz, seq, d)
```

```python
import functools
import math

import jax
import jax.numpy as jnp
from jax import lax
from jax.experimental import pallas as pl
from jax.experimental.pallas import tpu as pltpu

F32 = jnp.float32
BF16 = jnp.bfloat16

EPS = 1e-6
NEG = -1e30
LOG2E = math.log2(math.e)

Q_HEADS = 16
KV_HEADS = 4
GROUP = Q_HEADS // KV_HEADS
HEAD_DIM = 64
ATT_BLOCK = 128
REL_BUCKETS = 32
RET_HEADS = 4
RET_CHUNK = 256
ROPE_BASE = 10000.0
FFN_CHUNK = 256
ROW_TILE = 512
VMEM_LIMIT = 56 * 1024 * 1024


def _silu(x):
    return x * (1.0 / (1.0 + jnp.exp(-x)))


def _rms(xf, g):
    ms = jnp.mean(xf * xf, axis=-1, keepdims=True)
    return (xf * lax.rsqrt(ms + EPS)) * g


def _resident(shape):
    zeros = (0,) * len(shape)
    return pl.BlockSpec(shape, lambda *_: zeros, pipeline_mode=pl.Buffered(1))


def _params(n_axes, vmem=VMEM_LIMIT):
    return pltpu.CompilerParams(
        dimension_semantics=("arbitrary",) * n_axes, vmem_limit_bytes=vmem)


BF16_SUBLANES = 16


def _cast_plan(rows, n_steps):
    per = BF16_SUBLANES
    while rows % per or rows // per > n_steps:
        per += BF16_SUBLANES
    return per, rows // per


def _cast_io(jobs, n_steps, step_of):
    in_specs, out_specs, out_shapes, args, plan = [], [], [], [], []
    for w, layer, scale in jobs:
        _, rows, cols = w.shape
        per, n_cast = _cast_plan(rows, n_steps)
        blk = lambda *idx, n_cast=n_cast: jnp.minimum(step_of(*idx), n_cast - 1)
        in_specs.append(pl.BlockSpec(
            (1, per, cols), lambda *idx, blk=blk, layer=layer: (layer, blk(*idx), 0)))
        args.append(w)
        if scale is not None:
            in_specs.append(_resident(scale.shape))
            args.append(scale)
        out_specs.append(pl.BlockSpec((per, cols), lambda *idx, blk=blk: (blk(*idx), 0)))
        out_shapes.append(jax.ShapeDtypeStruct((rows, cols), BF16))
        plan.append((n_cast, scale is not None))
    return in_specs, out_specs, out_shapes, args, tuple(plan)


def _run_casts(step, plan, in_refs, out_refs):
    in_refs = list(in_refs)
    for (n_cast, scaled), dst in zip(plan, out_refs):
        src = in_refs.pop(0)
        scale = in_refs.pop(0) if scaled else None

        @pl.when(step < n_cast)
        def _(src=src, scale=scale, dst=dst):
            v = src[0]
            if scale is not None:
                v = v * scale[...]
            dst[...] = v.astype(BF16)


def _n_cast_inputs(plan):
    return sum(2 if scaled else 1 for _, scaled in plan)


def _ada_kernel(c_ref, w_ref, b_ref, o_ref):
    ca = _silu(c_ref[...])
    o_ref[0] = jnp.dot(ca.astype(BF16), w_ref[0].astype(BF16),
                       preferred_element_type=F32) + b_ref[0]


def _ada_mod(c_pad, ada_w, ada_b):
    depth, d, n = ada_w.shape
    rows = c_pad.shape[0]
    tn = 1536
    return pl.pallas_call(
        _ada_kernel,
        grid=(depth, n // tn),
        in_specs=[
            pl.BlockSpec((rows, d), lambda i, j: (0, 0)),
            pl.BlockSpec((1, d, tn), lambda i, j: (i, 0, j)),
            pl.BlockSpec((1, 1, tn), lambda i, j: (i, 0, j)),
        ],
        out_specs=pl.BlockSpec((1, rows, tn), lambda i, j: (i, 0, j)),
        out_shape=jax.ShapeDtypeStruct((depth, rows, n), F32),
        compiler_params=_params(2),
        name="ada_mod",
    )(c_pad, ada_w, ada_b.reshape(depth, 1, n))


def _mod_spec(layer, slot, tiles_per_batch, n_batch_rows):
    base = layer * n_batch_rows * 6 + slot

    def index(t):
        return (base + (t // tiles_per_batch) * 6, 0, 0)
    return index


def _qkv_kernel(x_ref, g_ref, sc_ref, sh_ref, wq_ref, wk_ref, wvt_ref, q_ref, k_ref, vt_ref):
    h = _rms(x_ref[...], g_ref[...]) * (1.0 + sc_ref[0]) + sh_ref[0]
    hb = h.astype(BF16)
    q_ref[...] = jnp.dot(hb, wq_ref[...], preferred_element_type=F32).astype(BF16)
    k_ref[...] = jnp.dot(hb, wk_ref[...], preferred_element_type=F32).astype(BF16)
    vt_ref[0] = lax.dot_general(wvt_ref[...], hb, (((1,), (1,)), ((), ())),
                                preferred_element_type=F32).astype(BF16)


def _qkv_proj(x2d, g_pre, modv, w_q, w_k, w_vt, seq, n_batch_rows):
    t, d = x2d.shape
    nq = w_q.shape[1]
    nkv = w_k.shape[1]
    tm = ROW_TILE
    tpb = seq // tm
    vec = lambda idx: pl.BlockSpec((1, 1, d), idx)
    return pl.pallas_call(
        _qkv_kernel,
        grid=(t // tm,),
        in_specs=[
            pl.BlockSpec((tm, d), lambda i: (i, 0)),
            _resident((1, d)),
            vec(_mod_spec(0, 1, tpb, n_batch_rows)),
            vec(_mod_spec(0, 0, tpb, n_batch_rows)),
            _resident(w_q.shape), _resident(w_k.shape), _resident(w_vt.shape),
        ],
        out_specs=[
            pl.BlockSpec((tm, nq), lambda i: (i, 0)),
            pl.BlockSpec((tm, nkv), lambda i: (i, 0)),
            pl.BlockSpec((1, nkv, tm), lambda i: (i // tpb, 0, i % tpb)),
        ],
        out_shape=[
            jax.ShapeDtypeStruct((t, nq), BF16),
            jax.ShapeDtypeStruct((t, nkv), BF16),
            jax.ShapeDtypeStruct((t // seq, nkv, seq), BF16),
        ],
        compiler_params=_params(1),
        name="qkv_proj",
    )(x2d, g_pre, modv, modv, w_q, w_k, w_vt)


def _att_tables(rb_ref, b2_ref, m2_ref):
    L = ATT_BLOCK
    rows = 64
    for r in range(3 * L // rows):
        j = lax.broadcasted_iota(jnp.int32, (rows, L), 0) + r * rows
        t = lax.broadcasted_iota(jnp.int32, (rows, L), 1)
        rel = j - L - t
        n = jnp.abs(rel)
        large = jnp.full((rows, L), 8, jnp.int32)
        for thr in (12, 16, 23, 32, 46, 64, 91):
            large = large + (n >= thr).astype(jnp.int32)
        bucket = jnp.where(rel > 0, 16, 0) + jnp.where(n < 8, n, large)
        in_win = n <= L
        visible = (in_win & (j >= L), in_win, in_win & (j < 2 * L))
        piece = slice(r * rows, (r + 1) * rows)
        for kind in range(3):
            m2_ref[kind, piece, :] = jnp.where(visible[kind], LOG2E, 0.0).astype(F32)

        def body(hq, carry):
            acc = jnp.zeros((rows, L), F32)
            for b in range(REL_BUCKETS):
                acc = jnp.where(bucket == b, rb_ref[b, hq], acc)
            acc = acc * LOG2E
            for kind in range(3):
                b2_ref[kind, hq, piece, :] = jnp.where(visible[kind], acc, NEG)
            return carry
        lax.fori_loop(0, Q_HEADS, body, 0)


def _att_kernel(*refs, cast_plan):
    rb_ref, sink_ref, q_ref, kp_ref, kc_ref, kn_ref, vp_ref, vc_ref, vn_ref = refs[:9]
    n_cast_in = _n_cast_inputs(cast_plan)
    cast_in = refs[9:9 + n_cast_in]
    o_ref = refs[9 + n_cast_in]
    cast_out = refs[10 + n_cast_in:10 + n_cast_in + len(cast_plan)]
    b2_ref, m2_ref = refs[10 + n_cast_in + len(cast_plan):]
    b = pl.program_id(0)
    n = pl.program_id(1)
    nb = pl.num_programs(1)
    _run_casts(b * nb + n, cast_plan, cast_in, cast_out)

    @pl.when((b == 0) & (n == 0))
    def _():
        _att_tables(rb_ref, b2_ref, m2_ref)

    L = ATT_BLOCK
    dh = HEAD_DIM
    kind = jnp.where(n == 0, 0, jnp.where(n == nb - 1, 2, 1))
    m2 = m2_ref[kind]
    q = q_ref[0]
    kb = jnp.concatenate([kp_ref[0], kc_ref[0], kn_ref[0]], axis=0)
    vt = jnp.concatenate([vp_ref[0], vc_ref[0], vn_ref[0]], axis=1)
    lane = lax.broadcasted_iota(jnp.int32, (L, 2 * dh), 1)
    zero = jnp.zeros((L, 2 * dh), BF16)

    def scores(h):
        p, e = divmod(h, 2)
        k_pair = kb[:, p * 2 * dh:(p + 1) * 2 * dh]
        mine = (lane < dh) if e == 0 else (lane >= dh)
        qz = jnp.concatenate(
            [jnp.where(mine, q[:, (p * GROUP + g) * 2 * dh:(p * GROUP + g + 1) * 2 * dh], zero)
             for g in range(GROUP)], axis=0)
        return lax.dot_general(k_pair, qz, (((1,), (1,)), ((), ())),
                               preferred_element_type=F32)

    o_rows = []
    st_next = scores(0)
    for h in range(KV_HEADS):
        st = st_next
        if h + 1 < KV_HEADS:
            st_next = scores(h + 1)
        es, rs = [], []
        for g in range(GROUP):
            hq = h * GROUP + g
            sk2 = jnp.full((1, L), sink_ref[hq], F32) * LOG2E
            l2 = st[:, g * L:(g + 1) * L] * m2 + b2_ref[kind, hq]
            m = jnp.maximum(jnp.max(l2, axis=0, keepdims=True), sk2)
            ex = jnp.exp2(l2 - m)
            den = jnp.sum(ex, axis=0, keepdims=True) + jnp.exp2(sk2 - m)
            es.append(ex.astype(BF16))
            rs.append(1.0 / den)
        et = jnp.concatenate(es, axis=1)
        ot = jnp.dot(vt[h * dh:(h + 1) * dh, :], et, preferred_element_type=F32)
        ot = ot * jnp.concatenate(rs, axis=1)
        o_rows += [ot[:, g * L:(g + 1) * L] for g in range(GROUP)]
    ot_all = jnp.concatenate(o_rows, axis=0)
    o_ref[0] = ot_all.T.astype(BF16)


def _attention(q, k, vt, rel_bias, sink, cast_jobs):
    bsz, seq, dq = q.shape
    dkv = k.shape[2]
    L = ATT_BLOCK
    nb = seq // L
    smem = pl.BlockSpec(memory_space=pltpu.SMEM)
    prev = lambda i: jnp.maximum(i - 1, 0)
    nxt = lambda i: jnp.minimum(i + 1, nb - 1)
    c_in, c_out, c_shapes, c_args, plan = _cast_io(cast_jobs, bsz * nb, lambda b, i: b * nb + i)
    return pl.pallas_call(
        functools.partial(_att_kernel, cast_plan=plan),
        grid=(bsz, nb),
        in_specs=[
            smem, smem,
            pl.BlockSpec((1, L, dq), lambda b, i: (b, i, 0)),
            pl.BlockSpec((1, L, dkv), lambda b, i: (b, prev(i), 0)),
            pl.BlockSpec((1, L, dkv), lambda b, i: (b, i, 0)),
            pl.BlockSpec((1, L, dkv), lambda b, i: (b, nxt(i), 0)),
            pl.BlockSpec((1, dkv, L), lambda b, i: (b, 0, prev(i))),
            pl.BlockSpec((1, dkv, L), lambda b, i: (b, 0, i)),
            pl.BlockSpec((1, dkv, L), lambda b, i: (b, 0, nxt(i))),
        ] + c_in,
        out_specs=[pl.BlockSpec((1, L, dq), lambda b, i: (b, i, 0))] + c_out,
        out_shape=[jax.ShapeDtypeStruct((bsz, seq, dq), BF16)] + c_shapes,
        scratch_shapes=[
            pltpu.VMEM((3, Q_HEADS, 3 * L, L), F32),
            pltpu.VMEM((3, 3 * L, L), F32),
        ],
        compiler_params=_params(2),
        name="swa_attention",
    )(rel_bias, sink, q, k, k, k, vt, vt, vt, *c_args)


def _post_ffn_kernel(*refs, emit_next, cast_plan):
    (a_ref, x_ref, wo_ref, gpost_ref, g1_ref, gpre_ref, sc2_ref, sh2_ref,
     win_ref, wout_ref, gfpost_ref, g2_ref) = refs[:12]
    rest = list(refs[12:])
    if emit_next:
        gn_ref, scn_ref, shn_ref = rest[:3]
        rest = rest[3:]
    n_cast_in = _n_cast_inputs(cast_plan)
    cast_in, rest = rest[:n_cast_in], rest[n_cast_in:]
    xo_ref = rest.pop(0)
    if emit_next:
        ho_ref = rest.pop(0)
    _run_casts(pl.program_id(0), cast_plan, cast_in, rest)
    hidden = wout_ref.shape[0]

    y = jnp.dot(a_ref[...], wo_ref[...], preferred_element_type=F32)
    x1 = x_ref[...] + g1_ref[0] * _rms(y, gpost_ref[...])
    h = _rms(x1, gpre_ref[...]) * (1.0 + sc2_ref[0]) + sh2_ref[0]
    hb = h.astype(BF16)
    acc = jnp.zeros(x1.shape, F32)
    for c in range(hidden // FFN_CHUNK):
        lo = c * FFN_CHUNK
        a = jnp.dot(hb, win_ref[:, lo:lo + FFN_CHUNK], preferred_element_type=F32)
        b = jnp.dot(hb, win_ref[:, hidden + lo:hidden + lo + FFN_CHUNK],
                    preferred_element_type=F32)
        act = (_silu(a) * b).astype(BF16)
        acc = acc + jnp.dot(act, wout_ref[lo:lo + FFN_CHUNK, :], preferred_element_type=F32)
    x2 = x1 + g2_ref[0] * _rms(acc, gfpost_ref[...])
    xo_ref[...] = x2
    if emit_next:
        hn = _rms(x2, gn_ref[...]) * (1.0 + scn_ref[0]) + shn_ref[0]
        ho_ref[...] = hn.astype(BF16)


def _post_ffn(a2d, x2d, w_o, g_post, g_pre, w_in, w_out, g_fpost, modv, layer,
              seq, n_batch_rows, g_next=None, cast_jobs=()):
    t, d = x2d.shape
    kin = a2d.shape[1]
    tm = ROW_TILE
    tpb = seq // tm
    emit_next = g_next is not None
    vec = lambda lyr, slot: pl.BlockSpec((1, 1, d), _mod_spec(lyr, slot, tpb, n_batch_rows))
    row = lambda width: pl.BlockSpec((tm, width), lambda i: (i, 0))
    in_specs = [
        row(kin), row(d), _resident(w_o.shape), _resident((1, d)), vec(layer, 2),
        _resident((1, d)), vec(layer, 4), vec(layer, 3),
        _resident(w_in.shape), _resident(w_out.shape), _resident((1, d)), vec(layer, 5),
    ]
    args = [a2d, x2d, w_o, g_post, modv, g_pre, modv, modv, w_in, w_out, g_fpost, modv]
    out_specs = [row(d)]
    out_shape = [jax.ShapeDtypeStruct((t, d), F32)]
    if emit_next:
        in_specs += [_resident((1, d)), vec(layer + 1, 1), vec(layer + 1, 0)]
        args += [g_next, modv, modv]
        out_specs.append(row(d))
        out_shape.append(jax.ShapeDtypeStruct((t, d), BF16))
    c_in, c_out, c_shapes, c_args, plan = _cast_io(cast_jobs, t // tm, lambda i: i)
    return pl.pallas_call(
        functools.partial(_post_ffn_kernel, emit_next=emit_next, cast_plan=plan),
        grid=(t // tm,),
        in_specs=in_specs + c_in,
        out_specs=out_specs + c_out,
        out_shape=out_shape + c_shapes,
        compiler_params=_params(1),
        name="post_ffn_next" if emit_next else "post_ffn",
    )(*args, *c_args)


def _ret_in_kernel(h_ref, w_ref, cos_ref, sin_ref, zf_ref, zb_ref,
                   q_ref, k_ref, kzf_ref, kzb_ref, v_ref, sg_ref):
    hb = h_ref[...]
    tm = hb.shape[0]
    cos = cos_ref[...]
    sin = sin_ref[...]
    dk = w_ref.shape[0] // RET_HEADS
    half = dk // 2
    nqk = RET_HEADS * dk
    nv = (w_ref.shape[1] - 2 * nqk) // 2

    yq = jnp.dot(hb, w_ref[:, 0:nqk], preferred_element_type=F32)
    yk = jnp.dot(hb, w_ref[:, nqk:2 * nqk], preferred_element_type=F32)
    zf = zf_ref[...][None]
    zb = zb_ref[...][None]
    for h in range(RET_HEADS):
        for off in (0, half):
            lo = h * dk + off
            sign = -1.0 if off == 0 else 1.0
            other = h * dk + (half - off)
            rq = yq[:, lo:lo + half] * cos + sign * (yq[:, other:other + half] * sin)
            rk = yk[:, lo:lo + half] * cos + sign * (yk[:, other:other + half] * sin)
            q_ref[:, lo:lo + half] = rq.astype(BF16)
            k_ref[:, lo:lo + half] = rk.astype(BF16)
            rk3 = rk.reshape(tm // RET_CHUNK, RET_CHUNK, half)
            kzf_ref[:, lo:lo + half] = (rk3 * zf[:, :, lo:lo + half]).reshape(tm, half).astype(BF16)
            kzb_ref[:, lo:lo + half] = (rk3 * zb[:, :, lo:lo + half]).reshape(tm, half).astype(BF16)
    v_ref[...] = jnp.dot(hb, w_ref[:, 2 * nqk:2 * nqk + nv],
                         preferred_element_type=F32).astype(BF16)
    g = jnp.dot(hb, w_ref[:, 2 * nqk + nv:], preferred_element_type=F32)
    sg_ref[...] = _silu(g)


def _ret_in(h2d, w_in, cos, sin, zf, zb, seq):
    t, d = h2d.shape
    tm = ROW_TILE
    tpb = seq // tm
    nqk = RET_HEADS * (d // RET_HEADS)
    nv = (w_in.shape[1] - 2 * nqk) // 2
    row = lambda width: pl.BlockSpec((tm, width), lambda i: (i, 0))
    pos = pl.BlockSpec((tm, cos.shape[1]), lambda i: (i % tpb, 0))
    return pl.pallas_call(
        _ret_in_kernel,
        grid=(t // tm,),
        in_specs=[row(d), _resident(w_in.shape), pos, pos,
                  _resident(zf.shape), _resident(zb.shape)],
        out_specs=[row(nqk), row(nqk), row(nqk), row(nqk), row(nv), row(nv)],
        out_shape=[jax.ShapeDtypeStruct((t, nqk), BF16)] * 4
        + [jax.ShapeDtypeStruct((t, nv), BF16), jax.ShapeDtypeStruct((t, nv), F32)],
        compiler_params=_params(1),
        name="ret_in_proj",
    )(h2d, w_in, cos, sin, zf, zb)


def _state_step(st_ref, h, q_h, kz_h, v_h, xi, cd):
    st = st_ref[h]
    cross = jnp.dot(q_h, st.astype(BF16), preferred_element_type=F32) * xi
    upd = lax.dot_general(kz_h, v_h, (((0,), (0,)), ((), ())), preferred_element_type=F32)
    st_ref[h] = st * cd + upd
    return cross


def _ret_scan_kernel(cdf_ref, cdb_ref, q_ref, k_ref, kzf_ref, kzb_ref, v_ref, sg_ref,
                     xif_ref, xib_ref, dm_ref, o_ref, st_ref, yb_ref):
    j = pl.program_id(1)
    nc = pl.num_programs(1) // 2

    @pl.when((j == 0) | (j == nc))
    def _():
        st_ref[...] = jnp.zeros(st_ref.shape, F32)

    q = q_ref[0]
    v = v_ref[0]
    dk = q.shape[1] // RET_HEADS
    dv = v.shape[1] // RET_HEADS

    @pl.when(j < nc)
    def _():
        kz = kzb_ref[0]
        c = nc - 1 - j
        for h in range(RET_HEADS):
            yb_ref[c, :, h * dv:(h + 1) * dv] = _state_step(
                st_ref, h, q[:, h * dk:(h + 1) * dk], kz[:, h * dk:(h + 1) * dk],
                v[:, h * dv:(h + 1) * dv], xib_ref[h], cdb_ref[h])

    @pl.when(j >= nc)
    def _():
        k = k_ref[0]
        kz = kzf_ref[0]
        c = j - nc
        for h in range(RET_HEADS):
            q_h = q[:, h * dk:(h + 1) * dk]
            v_h = v[:, h * dv:(h + 1) * dv]
            s = lax.dot_general(q_h, k[:, h * dk:(h + 1) * dk], (((1,), (1,)), ((), ())),
                                preferred_element_type=F32)
            inner = jnp.dot((s * dm_ref[h]).astype(BF16), v_h, preferred_element_type=F32)
            cross = _state_step(st_ref, h, q_h, kz[:, h * dk:(h + 1) * dk], v_h,
                                xif_ref[h], cdf_ref[h])
            y = inner + cross + yb_ref[c, :, h * dv:(h + 1) * dv]
            yn = y * lax.rsqrt(jnp.mean(y * y, axis=-1, keepdims=True) + EPS)
            o_ref[0, :, h * dv:(h + 1) * dv] = (
                sg_ref[0, :, h * dv:(h + 1) * dv] * yn).astype(BF16)


def _retention(q, k, kzf, kzb, v, sg, xi_f, xi_b, dmat, cd_f, cd_b):
    bsz, seq, nqk = q.shape
    nv = v.shape[2]
    L = RET_CHUNK
    nc = seq // L
    dk = nqk // RET_HEADS
    dv = nv // RET_HEADS
    smem = pl.BlockSpec(memory_space=pltpu.SMEM)
    both = lambda j: jnp.where(j < nc, nc - 1 - j, j - nc)
    bwd_only = lambda j: jnp.where(j < nc, nc - 1 - j, 0)
    fwd_only = lambda j: jnp.where(j < nc, 0, j - nc)
    blk = lambda width, chunk: pl.BlockSpec((1, L, width), lambda b, j: (b, chunk(j), 0))
    return pl.pallas_call(
        _ret_scan_kernel,
        grid=(bsz, 2 * nc),
        in_specs=[smem, smem, blk(nqk, both), blk(nqk, fwd_only), blk(nqk, fwd_only),
                  blk(nqk, bwd_only), blk(nv, both), blk(nv, fwd_only),
                  _resident(xi_f.shape), _resident(xi_b.shape), _resident(dmat.shape)],
        out_specs=blk(nv, fwd_only),
        out_shape=jax.ShapeDtypeStruct((bsz, seq, nv), BF16),
        scratch_shapes=[pltpu.VMEM((RET_HEADS, dk, dv), F32), pltpu.VMEM((nc, L, nv), F32)],
        compiler_params=_params(2),
        name="ret_scan",
    )(cd_f, cd_b, q, k, kzf, kzb, v, sg, xi_f, xi_b, dmat)


def _decay_tables(decay_fwd, decay_bwd, dk):
    L = RET_CHUNK
    lg_f = jax.nn.log_sigmoid(decay_fwd.astype(F32))
    lg_b = jax.nn.log_sigmoid(decay_bwd.astype(F32))
    idx = jnp.arange(L, dtype=F32)
    diff = idx[:, None] - idx[None, :]
    dm = jnp.where((diff >= 0)[None],
                   jnp.exp(lg_f[:, None, None] * jnp.maximum(diff, 0.0)[None]),
                   jnp.exp(lg_b[:, None, None] * jnp.maximum(-diff, 0.0)[None]))
    xi_f = jnp.exp(lg_f[:, None] * (idx + 1.0)[None])[:, :, None]
    xi_b = jnp.exp(lg_b[:, None] * (L - idx)[None])[:, :, None]
    zeta_f = jnp.exp(lg_f[:, None] * (L - 1.0 - idx)[None])
    zeta_b = jnp.exp(lg_b[:, None] * idx[None])
    zf = jnp.repeat(zeta_f.T, dk, axis=1)
    zb = jnp.repeat(zeta_b.T, dk, axis=1)
    return dm, xi_f, xi_b, zf, zb, jnp.exp(lg_f * L), jnp.exp(lg_b * L)


def _rope_tables(seq, dk):
    inv = ROPE_BASE ** (-jnp.arange(0, dk, 2, dtype=F32) / dk)
    ang = jnp.arange(seq, dtype=F32)[:, None] * inv[None]
    return jnp.cos(ang), jnp.sin(ang)


def kernel(x, c, rel_bias, att_w_qkv, att_w_o, att_sink, ret_w_in, ret_w_o, ret_decay_fwd,
           ret_decay_bwd, ada_w, ada_b, mix_norm_pre, mix_norm_post, ffn_norm_pre,
           ffn_norm_post, ffn_w_in, ffn_w_out):
    bsz, seq, d = x.shape
    t = bsz * seq
    assert seq % ROW_TILE == 0 and ROW_TILE % RET_CHUNK == 0 and d % RET_HEADS == 0
    assert ada_w.shape[0] == 2 and ffn_w_out.shape[1] % FFN_CHUNK == 0
    rows = -(-bsz // 8) * 8
    vec = lambda a: a.reshape(1, d)

    c_pad = jnp.pad(c, ((0, rows - bsz), (0, 0)))
    mod = _ada_mod(c_pad, ada_w, ada_b)
    modv = mod.reshape(2 * rows * 6, 1, d)

    x2d = x.reshape(t, d)

    nq = Q_HEADS * HEAD_DIM
    nkv = KV_HEADS * HEAD_DIM
    heads = [(2 * p + e) * GROUP + g for p in range(KV_HEADS // 2) for g in range(GROUP)
             for e in range(2)]
    w_q = att_w_qkv[0][:, :nq].reshape(d, Q_HEADS, HEAD_DIM)[:, jnp.array(heads)].reshape(d, nq)
    w_q = (w_q * HEAD_DIM ** -0.5).astype(BF16)
    w_k = att_w_qkv[0][:, nq:nq + nkv].astype(BF16)
    w_vt = att_w_qkv[0][:, nq + nkv:].T.astype(BF16)
    q, k, vt = _qkv_proj(x2d, vec(mix_norm_pre[0]), modv, w_q, w_k, w_vt, seq, rows)
    att, w_ao, w_f0in, w_f0out = _attention(
        q.reshape(bsz, seq, nq), k.reshape(bsz, seq, nkv), vt, rel_bias.astype(F32),
        att_sink[0].astype(F32), [(att_w_o, 0, None), (ffn_w_in, 0, None), (ffn_w_out, 0, None)])

    dk = d // RET_HEADS
    nqk = RET_HEADS * dk
    kscale = jnp.concatenate([jnp.ones((nqk,), F32), jnp.full((nqk,), dk ** -0.5, F32),
                              jnp.ones((ret_w_in.shape[2] - 2 * nqk,), F32)])[None]
    x1, h1, w_rin, w_ro, w_f1in, w_f1out = _post_ffn(
        att.reshape(t, nq), x2d, w_ao, vec(mix_norm_post[0]), vec(ffn_norm_pre[0]), w_f0in,
        w_f0out, vec(ffn_norm_post[0]), modv, 0, seq, rows, g_next=vec(mix_norm_pre[1]),
        cast_jobs=[(ret_w_in, 0, kscale), (ret_w_o, 0, None), (ffn_w_in, 1, None),
                   (ffn_w_out, 1, None)])
    dm, xi_f, xi_b, zf, zb, cd_f, cd_b = _decay_tables(ret_decay_fwd[0], ret_decay_bwd[0], dk)
    cos, sin = _rope_tables(seq, dk)
    q, k, kzf, kzb, v, sg = _ret_in(h1, w_rin, cos, sin, zf, zb, seq)
    r3 = lambda a: a.reshape(bsz, seq, a.shape[1])
    gated = _retention(r3(q), r3(k), r3(kzf), r3(kzb), r3(v), r3(sg), xi_f, xi_b, dm, cd_f, cd_b)
    (x2,) = _post_ffn(gated.reshape(t, -1), x1, w_ro, vec(mix_norm_post[1]),
                      vec(ffn_norm_pre[1]), w_f1in, w_f1out, vec(ffn_norm_post[1]), modv, 1,
                      seq, rows)
    return x2.reshape(bsz, seq, d)
```

```python
import functools
import math

import jax
import jax.numpy as jnp
from jax import lax
from jax.experimental import pallas as pl
from jax.experimental.pallas import tpu as pltpu

F32 = jnp.float32
BF16 = jnp.bfloat16

EPS = 1e-6
NEG = -1e30
LOG2E = math.log2(math.e)

Q_HEADS = 16
KV_HEADS = 4
GROUP = Q_HEADS // KV_HEADS
HEAD_DIM = 64
ATT_BLOCK = 128
REL_BUCKETS = 32
RET_HEADS = 4
RET_CHUNK = 256
ROPE_BASE = 10000.0
FFN_CHUNK = 256
ROW_TILE = 512
VMEM_LIMIT = 56 * 1024 * 1024


def _silu(x):
    return x * (1.0 / (1.0 + jnp.exp(-x)))


def _rms(xf, g):
    ms = jnp.mean(xf * xf, axis=-1, keepdims=True)
    return (xf * lax.rsqrt(ms + EPS)) * g


def _resident(shape):
    zeros = (0,) * len(shape)
    return pl.BlockSpec(shape, lambda *_: zeros, pipeline_mode=pl.Buffered(1))


def _params(n_axes, vmem=VMEM_LIMIT):
    return pltpu.CompilerParams(
        dimension_semantics=("arbitrary",) * n_axes, vmem_limit_bytes=vmem)


BF16_SUBLANES = 16


def _cast_plan(rows, n_steps):
    per = BF16_SUBLANES
    while rows % per or rows // per > n_steps:
        per += BF16_SUBLANES
    return per, rows // per


def _cast_io(jobs, n_steps, step_of):
    in_specs, out_specs, out_shapes, args, plan = [], [], [], [], []
    for w, layer, scale in jobs:
        _, rows, cols = w.shape
        per, n_cast = _cast_plan(rows, n_steps)
        blk = lambda *idx, n_cast=n_cast: jnp.minimum(step_of(*idx), n_cast - 1)
        in_specs.append(pl.BlockSpec(
            (1, per, cols), lambda *idx, blk=blk, layer=layer: (layer, blk(*idx), 0)))
        args.append(w)
        if scale is not None:
            in_specs.append(_resident(scale.shape))
            args.append(scale)
        out_specs.append(pl.BlockSpec((per, cols), lambda *idx, blk=blk: (blk(*idx), 0)))
        out_shapes.append(jax.ShapeDtypeStruct((rows, cols), BF16))
        plan.append((n_cast, scale is not None))
    return in_specs, out_specs, out_shapes, args, tuple(plan)


def _run_casts(step, plan, in_refs, out_refs):
    in_refs = list(in_refs)
    for (n_cast, scaled), dst in zip(plan, out_refs):
        src = in_refs.pop(0)
        scale = in_refs.pop(0) if scaled else None

        @pl.when(step < n_cast)
        def _(src=src, scale=scale, dst=dst):
            v = src[0]
            if scale is not None:
                v = v * scale[...]
            dst[...] = v.astype(BF16)


def _n_cast_inputs(plan):
    return sum(2 if scaled else 1 for _, scaled in plan)


def _ada_kernel(c_ref, w_ref, b_ref, o_ref):
    ca = _silu(c_ref[...])
    o_ref[0] = jnp.dot(ca.astype(BF16), w_ref[0].astype(BF16),
                       preferred_element_type=F32) + b_ref[0]


def _ada_mod(c_pad, ada_w, ada_b):
    depth, d, n = ada_w.shape
    rows = c_pad.shape[0]
    tn = 1536
    return pl.pallas_call(
        _ada_kernel,
        grid=(depth, n // tn),
        in_specs=[
            pl.BlockSpec((rows, d), lambda i, j: (0, 0)),
            pl.BlockSpec((1, d, tn), lambda i, j: (i, 0, j)),
            pl.BlockSpec((1, 1, tn), lambda i, j: (i, 0, j)),
        ],
        out_specs=pl.BlockSpec((1, rows, tn), lambda i, j: (i, 0, j)),
        out_shape=jax.ShapeDtypeStruct((depth, rows, n), F32),
        compiler_params=_params(2),
        name="ada_mod",
    )(c_pad, ada_w, ada_b.reshape(depth, 1, n))


def _mod_spec(layer, slot, tiles_per_batch, n_batch_rows):
    base = layer * n_batch_rows * 6 + slot

    def index(t):
        return (base + (t // tiles_per_batch) * 6, 0, 0)
    return index


def _qkv_kernel(x_ref, g_ref, sc_ref, sh_ref, wq_ref, wk_ref, wvt_ref, q_ref, k_ref, vt_ref):
    h = _rms(x_ref[...], g_ref[...]) * (1.0 + sc_ref[0]) + sh_ref[0]
    hb = h.astype(BF16)
    q_ref[...] = jnp.dot(hb, wq_ref[...], preferred_element_type=F32).astype(BF16)
    k_ref[...] = jnp.dot(hb, wk_ref[...], preferred_element_type=F32).astype(BF16)
    vt_ref[0] = lax.dot_general(wvt_ref[...], hb, (((1,), (1,)), ((), ())),
                                preferred_element_type=F32).astype(BF16)


def _qkv_proj(x2d, g_pre, modv, w_q, w_k, w_vt, seq, n_batch_rows):
    t, d = x2d.shape
    nq = w_q.shape[1]
    nkv = w_k.shape[1]
    tm = ROW_TILE
    tpb = seq // tm
    vec = lambda idx: pl.BlockSpec((1, 1, d), idx)
    return pl.pallas_call(
        _qkv_kernel,
        grid=(t // tm,),
        in_specs=[
            pl.BlockSpec((tm, d), lambda i: (i, 0)),
            _resident((1, d)),
            vec(_mod_spec(0, 1, tpb, n_batch_rows)),
            vec(_mod_spec(0, 0, tpb, n_batch_rows)),
            _resident(w_q.shape), _resident(w_k.shape), _resident(w_vt.shape),
        ],
        out_specs=[
            pl.BlockSpec((tm, nq), lambda i: (i, 0)),
            pl.BlockSpec((tm, nkv), lambda i: (i, 0)),
            pl.BlockSpec((1, nkv, tm), lambda i: (i // tpb, 0, i % tpb)),
        ],
        out_shape=[
            jax.ShapeDtypeStruct((t, nq), BF16),
            jax.ShapeDtypeStruct((t, nkv), BF16),
            jax.ShapeDtypeStruct((t // seq, nkv, seq), BF16),
        ],
        compiler_params=_params(1),
        name="qkv_proj",
    )(x2d, g_pre, modv, modv, w_q, w_k, w_vt)


def _att_tables(rb_ref, b2_ref, m2_ref):
    L = ATT_BLOCK
    rows = 64
    for r in range(3 * L // rows):
        j = lax.broadcasted_iota(jnp.int32, (rows, L), 0) + r * rows
        t = lax.broadcasted_iota(jnp.int32, (rows, L), 1)
        rel = j - L - t
        n = jnp.abs(rel)
        large = jnp.full((rows, L), 8, jnp.int32)
        for thr in (12, 16, 23, 32, 46, 64, 91):
            large = large + (n >= thr).astype(jnp.int32)
        bucket = jnp.where(rel > 0, 16, 0) + jnp.where(n < 8, n, large)
        in_win = n <= L
        visible = (in_win & (j >= L), in_win, in_win & (j < 2 * L))
        piece = slice(r * rows, (r + 1) * rows)
        for kind in range(3):
            m2_ref[kind, piece, :] = jnp.where(visible[kind], LOG2E, 0.0).astype(F32)

        def body(hq, carry):
            acc = jnp.zeros((rows, L), F32)
            for b in range(REL_BUCKETS):
                acc = jnp.where(bucket == b, rb_ref[b, hq], acc)
            acc = acc * LOG2E
            for kind in range(3):
                b2_ref[kind, hq, piece, :] = jnp.where(visible[kind], acc, NEG)
            return carry
        lax.fori_loop(0, Q_HEADS, body, 0)


def _att_kernel(*refs, cast_plan):
    rb_ref, sink_ref, q_ref, kp_ref, kc_ref, kn_ref, vp_ref, vc_ref, vn_ref = refs[:9]
    n_cast_in = _n_cast_inputs(cast_plan)
    cast_in = refs[9:9 + n_cast_in]
    o_ref = refs[9 + n_cast_in]
    cast_out = refs[10 + n_cast_in:10 + n_cast_in + len(cast_plan)]
    b2_ref, m2_ref = refs[10 + n_cast_in + len(cast_plan):]
    b = pl.program_id(0)
    n = pl.program_id(1)
    nb = pl.num_programs(1)
    _run_casts(b * nb + n, cast_plan, cast_in, cast_out)

    @pl.when((b == 0) & (n == 0))
    def _():
        _att_tables(rb_ref, b2_ref, m2_ref)

    L = ATT_BLOCK
    dh = HEAD_DIM
    kind = jnp.where(n == 0, 0, jnp.where(n == nb - 1, 2, 1))
    m2 = m2_ref[kind]
    q = q_ref[0]
    kb = jnp.concatenate([kp_ref[0], kc_ref[0], kn_ref[0]], axis=0)
    vt = jnp.concatenate([vp_ref[0], vc_ref[0], vn_ref[0]], axis=1)
    lane = lax.broadcasted_iota(jnp.int32, (L, 2 * dh), 1)
    zero = jnp.zeros((L, 2 * dh), BF16)

    def scores(h):
        p, e = divmod(h, 2)
        k_pair = kb[:, p * 2 * dh:(p + 1) * 2 * dh]
        mine = (lane < dh) if e == 0 else (lane >= dh)
        qz = jnp.concatenate(
            [jnp.where(mine, q[:, (p * GROUP + g) * 2 * dh:(p * GROUP + g + 1) * 2 * dh], zero)
             for g in range(GROUP)], axis=0)
        return lax.dot_general(k_pair, qz, (((1,), (1,)), ((), ())),
                               preferred_element_type=F32)

    o_rows = []
    st_next = scores(0)
    for h in range(KV_HEADS):
        st = st_next
        if h + 1 < KV_HEADS:
            st_next = scores(h + 1)
        es, rs = [], []
        for g in range(GROUP):
            hq = h * GROUP + g
            sk2 = jnp.full((1, L), sink_ref[hq], F32) * LOG2E
            l2 = st[:, g * L:(g + 1) * L] * m2 + b2_ref[kind, hq]
            m = jnp.maximum(jnp.max(l2, axis=0, keepdims=True), sk2)
            ex = jnp.exp2(l2 - m)
            den = jnp.sum(ex, axis=0, keepdims=True) + jnp.exp2(sk2 - m)
            es.append(ex.astype(BF16))
            rs.append(1.0 / den)
        et = jnp.concatenate(es, axis=1)
        ot = jnp.dot(vt[h * dh:(h + 1) * dh, :], et, preferred_element_type=F32)
        ot = ot * jnp.concatenate(rs, axis=1)
        o_rows += [ot[:, g * L:(g + 1) * L] for g in range(GROUP)]
    ot_all = jnp.concatenate(o_rows, axis=0)
    o_ref[0] = ot_all.T.astype(BF16)


def _attention(q, k, vt, rel_bias, sink, cast_jobs):
    bsz, seq, dq = q.shape
    dkv = k.shape[2]
    L = ATT_BLOCK
    nb = seq // L
    smem = pl.BlockSpec(memory_space=pltpu.SMEM)
    prev = lambda i: jnp.maximum(i - 1, 0)
    nxt = lambda i: jnp.minimum(i + 1, nb - 1)
    c_in, c_out, c_shapes, c_args, plan = _cast_io(cast_jobs, bsz * nb, lambda b, i: b * nb + i)
    return pl.pallas_call(
        functools.partial(_att_kernel, cast_plan=plan),
        grid=(bsz, nb),
        in_specs=[
            smem, smem,
            pl.BlockSpec((1, L, dq), lambda b, i: (b, i, 0)),
            pl.BlockSpec((1, L, dkv), lambda b, i: (b, prev(i), 0)),
            pl.BlockSpec((1, L, dkv), lambda b, i: (b, i, 0)),
            pl.BlockSpec((1, L, dkv), lambda b, i: (b, nxt(i), 0)),
            pl.BlockSpec((1, dkv, L), lambda b, i: (b, 0, prev(i))),
            pl.BlockSpec((1, dkv, L), lambda b, i: (b, 0, i)),
            pl.BlockSpec((1, dkv, L), lambda b, i: (b, 0, nxt(i))),
        ] + c_in,
        out_specs=[pl.BlockSpec((1, L, dq), lambda b, i: (b, i, 0))] + c_out,
        out_shape=[jax.ShapeDtypeStruct((bsz, seq, dq), BF16)] + c_shapes,
        scratch_shapes=[
            pltpu.VMEM((3, Q_HEADS, 3 * L, L), F32),
            pltpu.VMEM((3, 3 * L, L), F32),
        ],
        compiler_params=_params(2),
        name="swa_attention",
    )(rel_bias, sink, q, k, k, k, vt, vt, vt, *c_args)


def _post_ffn_kernel(*refs, emit_next, cast_plan):
    (a_ref, x_ref, wo_ref, gpost_ref, g1_ref, gpre_ref, sc2_ref, sh2_ref,
     win_ref, wout_ref, gfpost_ref, g2_ref) = refs[:12]
    rest = list(refs[12:])
    if emit_next:
        gn_ref, scn_ref, shn_ref = rest[:3]
        rest = rest[3:]
    n_cast_in = _n_cast_inputs(cast_plan)
    cast_in, rest = rest[:n_cast_in], rest[n_cast_in:]
    xo_ref = rest.pop(0)
    if emit_next:
        ho_ref = rest.pop(0)
    cast_out = rest[:len(cast_plan)]
    hb_ref, x1_ref, acc_ref, y_ref = rest[len(cast_plan):]
    s = pl.program_id(0)
    n = pl.num_programs(0) - 2
    _run_casts(s, cast_plan, cast_in, cast_out)
    hidden = wout_ref.shape[0]

    def advance():
        hb_ref[1] = hb_ref[0]
        acc_ref[1] = acc_ref[0]
        x1_ref[2] = x1_ref[1]
        x1_ref[1] = x1_ref[0]

    n_chunks = hidden // FFN_CHUNK
    n_slices = 8
    rows_of = lambda r: slice(r * (x_ref.shape[0] // n_slices), (r + 1) * (x_ref.shape[0] // n_slices))

    def a_matmul():
        y_ref[...] = jnp.dot(a_ref[...], wo_ref[...], preferred_element_type=F32)

    def after(v, dep):
        if dep is None:
            return v
        zero = (pltpu.bitcast(dep, jnp.uint32) >> 16) >> 16
        return pltpu.bitcast(pltpu.bitcast(v, jnp.uint32) | zero, F32)

    def a_norm(r, dep=None):
        rows = rows_of(r)
        x1 = x_ref[rows, :] + g1_ref[0] * _rms(after(y_ref[rows, :], dep), gpost_ref[...])
        h = _rms(x1, gpre_ref[...]) * (1.0 + sc2_ref[0]) + sh2_ref[0]
        hb_ref[0, rows, :] = h.astype(BF16)
        x1_ref[0, rows, :] = x1

    def b_chunk(c, hb, acc):
        lo = c * FFN_CHUNK
        a = jnp.dot(hb, win_ref[:, lo:lo + FFN_CHUNK], preferred_element_type=F32)
        b = jnp.dot(hb, win_ref[:, hidden + lo:hidden + lo + FFN_CHUNK],
                    preferred_element_type=F32)
        act = (_silu(a) * b).astype(BF16)
        return acc + jnp.dot(act, wout_ref[lo:lo + FFN_CHUNK, :], preferred_element_type=F32)

    def c_slice(r, dep=None):
        rows = rows_of(r)
        x2 = x1_ref[2, rows, :] + g2_ref[0] * _rms(after(acc_ref[1, rows, :], dep),
                                                   gfpost_ref[...])
        xo_ref[rows, :] = x2
        if emit_next:
            hn = _rms(x2, gn_ref[...]) * (1.0 + scn_ref[0]) + shn_ref[0]
            ho_ref[rows, :] = hn.astype(BF16)

    @pl.when(s == 0)
    def _():
        x1_ref[...] = jnp.zeros(x1_ref.shape, F32)
        acc_ref[...] = jnp.zeros(acc_ref.shape, F32)
        a_matmul()
        for r in range(n_slices):
            a_norm(r)

    @pl.when((s >= 1) & (s <= n))
    def _():
        advance()
        a_matmul()
        hb = hb_ref[1]
        acc = jnp.zeros((hb.shape[0], wout_ref.shape[1]), F32)
        for c in range(n_chunks):
            acc = b_chunk(c, hb, acc)
            dep = acc[0:1, :]
            if c < n_slices:
                c_slice(c, dep)
            if c >= n_chunks - n_slices - 1 and c < n_chunks - 1:
                a_norm(c - (n_chunks - n_slices - 1), dep)
        acc_ref[0] = acc

    @pl.when(s == n + 1)
    def _():
        advance()
        for r in range(n_slices):
            c_slice(r)


def _post_ffn(a2d, x2d, w_o, g_post, g_pre, w_in, w_out, g_fpost, modv, layer,
              seq, n_batch_rows, g_next=None, cast_jobs=()):
    t, d = x2d.shape
    kin = a2d.shape[1]
    tm = ROW_TILE
    tpb = seq // tm
    n = t // tm
    emit_next = g_next is not None
    tile_a = lambda s: jnp.minimum(s, n - 1)
    tile_c = lambda s: jnp.clip(s - 2, 0, n - 1)

    def vec(lyr, slot, tile_of):
        idx = _mod_spec(lyr, slot, tpb, n_batch_rows)
        return pl.BlockSpec((1, 1, d), lambda s: idx(tile_of(s)))
    row = lambda width, tile_of: pl.BlockSpec((tm, width), lambda s: (tile_of(s), 0))
    in_specs = [
        row(kin, tile_a), row(d, tile_a), _resident(w_o.shape), _resident((1, d)),
        vec(layer, 2, tile_a), _resident((1, d)), vec(layer, 4, tile_a), vec(layer, 3, tile_a),
        _resident(w_in.shape), _resident(w_out.shape), _resident((1, d)), vec(layer, 5, tile_c),
    ]
    args = [a2d, x2d, w_o, g_post, modv, g_pre, modv, modv, w_in, w_out, g_fpost, modv]
    out_specs = [row(d, tile_c)]
    out_shape = [jax.ShapeDtypeStruct((t, d), F32)]
    if emit_next:
        in_specs += [_resident((1, d)), vec(layer + 1, 1, tile_c), vec(layer + 1, 0, tile_c)]
        args += [g_next, modv, modv]
        out_specs.append(row(d, tile_c))
        out_shape.append(jax.ShapeDtypeStruct((t, d), BF16))
    c_in, c_out, c_shapes, c_args, plan = _cast_io(cast_jobs, n, lambda s: s)
    return pl.pallas_call(
        functools.partial(_post_ffn_kernel, emit_next=emit_next, cast_plan=plan),
        grid=(n + 2,),
        in_specs=in_specs + c_in,
        out_specs=out_specs + c_out,
        out_shape=out_shape + c_shapes,
        scratch_shapes=[pltpu.VMEM((2, tm, d), BF16), pltpu.VMEM((3, tm, d), F32),
                        pltpu.VMEM((2, tm, d), F32), pltpu.VMEM((tm, d), F32)],
        compiler_params=_params(1),
        name="post_ffn_next" if emit_next else "post_ffn",
    )(*args, *c_args)


def _ret_in_kernel(h_ref, w_ref, cos_ref, sin_ref, zf_ref, zb_ref,
                   q_ref, k_ref, kzf_ref, kzb_ref, v_ref, sg_ref):
    hb = h_ref[...]
    tm = hb.shape[0]
    cos = cos_ref[...]
    sin = sin_ref[...]
    dk = w_ref.shape[0] // RET_HEADS
    half = dk // 2
    nqk = RET_HEADS * dk
    nv = (w_ref.shape[1] - 2 * nqk) // 2

    yq = jnp.dot(hb, w_ref[:, 0:nqk], preferred_element_type=F32)
    yk = jnp.dot(hb, w_ref[:, nqk:2 * nqk], preferred_element_type=F32)
    zf = zf_ref[...][None]
    zb = zb_ref[...][None]
    for h in range(RET_HEADS):
        for off in (0, half):
            lo = h * dk + off
            sign = -1.0 if off == 0 else 1.0
            other = h * dk + (half - off)
            rq = yq[:, lo:lo + half] * cos + sign * (yq[:, other:other + half] * sin)
            rk = yk[:, lo:lo + half] * cos + sign * (yk[:, other:other + half] * sin)
            q_ref[:, lo:lo + half] = rq.astype(BF16)
            k_ref[:, lo:lo + half] = rk.astype(BF16)
            rk3 = rk.reshape(tm // RET_CHUNK, RET_CHUNK, half)
            kzf_ref[:, lo:lo + half] = (rk3 * zf[:, :, lo:lo + half]).reshape(tm, half).astype(BF16)
            kzb_ref[:, lo:lo + half] = (rk3 * zb[:, :, lo:lo + half]).reshape(tm, half).astype(BF16)
    v_ref[...] = jnp.dot(hb, w_ref[:, 2 * nqk:2 * nqk + nv],
                         preferred_element_type=F32).astype(BF16)
    g = jnp.dot(hb, w_ref[:, 2 * nqk + nv:], preferred_element_type=F32)
    sg_ref[...] = _silu(g)


def _ret_in(h2d, w_in, cos, sin, zf, zb, seq):
    t, d = h2d.shape
    tm = ROW_TILE
    tpb = seq // tm
    nqk = RET_HEADS * (d // RET_HEADS)
    nv = (w_in.shape[1] - 2 * nqk) // 2
    row = lambda width: pl.BlockSpec((tm, width), lambda i: (i, 0))
    pos = pl.BlockSpec((tm, cos.shape[1]), lambda i: (i % tpb, 0))
    return pl.pallas_call(
        _ret_in_kernel,
        grid=(t // tm,),
        in_specs=[row(d), _resident(w_in.shape), pos, pos,
                  _resident(zf.shape), _resident(zb.shape)],
        out_specs=[row(nqk), row(nqk), row(nqk), row(nqk), row(nv), row(nv)],
        out_shape=[jax.ShapeDtypeStruct((t, nqk), BF16)] * 4
        + [jax.ShapeDtypeStruct((t, nv), BF16), jax.ShapeDtypeStruct((t, nv), F32)],
        compiler_params=_params(1),
        name="ret_in_proj",
    )(h2d, w_in, cos, sin, zf, zb)


def _state_step(st_ref, h, q_h, kz_h, v_h, xi, cd):
    st = st_ref[h]
    cross = jnp.dot(q_h, st.astype(BF16), preferred_element_type=F32) * xi
    upd = lax.dot_general(kz_h, v_h, (((0,), (0,)), ((), ())), preferred_element_type=F32)
    st_ref[h] = st * cd + upd
    return cross


def _ret_scan_kernel(cdf_ref, cdb_ref, q_ref, k_ref, kzf_ref, kzb_ref, v_ref, sg_ref,
                     xif_ref, xib_ref, dm_ref, o_ref, st_ref, yb_ref):
    j = pl.program_id(1)
    nc = pl.num_programs(1) // 2

    @pl.when((j == 0) | (j == nc))
    def _():
        st_ref[...] = jnp.zeros(st_ref.shape, F32)

    q = q_ref[0]
    v = v_ref[0]
    dk = q.shape[1] // RET_HEADS
    dv = v.shape[1] // RET_HEADS

    @pl.when(j < nc)
    def _():
        kz = kzb_ref[0]
        c = nc - 1 - j
        for h in range(RET_HEADS):
            yb_ref[c, :, h * dv:(h + 1) * dv] = _state_step(
                st_ref, h, q[:, h * dk:(h + 1) * dk], kz[:, h * dk:(h + 1) * dk],
                v[:, h * dv:(h + 1) * dv], xib_ref[h], cdb_ref[h])

    @pl.when(j >= nc)
    def _():
        k = k_ref[0]
        kz = kzf_ref[0]
        c = j - nc
        for h in range(RET_HEADS):
            q_h = q[:, h * dk:(h + 1) * dk]
            v_h = v[:, h * dv:(h + 1) * dv]
            s = lax.dot_general(q_h, k[:, h * dk:(h + 1) * dk], (((1,), (1,)), ((), ())),
                                preferred_element_type=F32)
            inner = jnp.dot((s * dm_ref[h]).astype(BF16), v_h, preferred_element_type=F32)
            cross = _state_step(st_ref, h, q_h, kz[:, h * dk:(h + 1) * dk], v_h,
                                xif_ref[h], cdf_ref[h])
            y = inner + cross + yb_ref[c, :, h * dv:(h + 1) * dv]
            yn = y * lax.rsqrt(jnp.mean(y * y, axis=-1, keepdims=True) + EPS)
            o_ref[0, :, h * dv:(h + 1) * dv] = (
                sg_ref[0, :, h * dv:(h + 1) * dv] * yn).astype(BF16)


def _retention(q, k, kzf, kzb, v, sg, xi_f, xi_b, dmat, cd_f, cd_b):
    bsz, seq, nqk = q.shape
    nv = v.shape[2]
    L = RET_CHUNK
    nc = seq // L
    dk = nqk // RET_HEADS
    dv = nv // RET_HEADS
    smem = pl.BlockSpec(memory_space=pltpu.SMEM)
    both = lambda j: jnp.where(j < nc, nc - 1 - j, j - nc)
    bwd_only = lambda j: jnp.where(j < nc, nc - 1 - j, 0)
    fwd_only = lambda j: jnp.where(j < nc, 0, j - nc)
    blk = lambda width, chunk: pl.BlockSpec((1, L, width), lambda b, j: (b, chunk(j), 0))
    return pl.pallas_call(
        _ret_scan_kernel,
        grid=(bsz, 2 * nc),
        in_specs=[smem, smem, blk(nqk, both), blk(nqk, fwd_only), blk(nqk, fwd_only),
                  blk(nqk, bwd_only), blk(nv, both), blk(nv, fwd_only),
                  _resident(xi_f.shape), _resident(xi_b.shape), _resident(dmat.shape)],
        out_specs=blk(nv, fwd_only),
        out_shape=jax.ShapeDtypeStruct((bsz, seq, nv), BF16),
        scratch_shapes=[pltpu.VMEM((RET_HEADS, dk, dv), F32), pltpu.VMEM((nc, L, nv), F32)],
        compiler_params=_params(2),
        name="ret_scan",
    )(cd_f, cd_b, q, k, kzf, kzb, v, sg, xi_f, xi_b, dmat)


def _decay_tables(decay_fwd, decay_bwd, dk):
    L = RET_CHUNK
    lg_f = jax.nn.log_sigmoid(decay_fwd.astype(F32))
    lg_b = jax.nn.log_sigmoid(decay_bwd.astype(F32))
    idx = jnp.arange(L, dtype=F32)
    diff = idx[:, None] - idx[None, :]
    dm = jnp.where((diff >= 0)[None],
                   jnp.exp(lg_f[:, None, None] * jnp.maximum(diff, 0.0)[None]),
                   jnp.exp(lg_b[:, None, None] * jnp.maximum(-diff, 0.0)[None]))
    xi_f = jnp.exp(lg_f[:, None] * (idx + 1.0)[None])[:, :, None]
    xi_b = jnp.exp(lg_b[:, None] * (L - idx)[None])[:, :, None]
    zeta_f = jnp.exp(lg_f[:, None] * (L - 1.0 - idx)[None])
    zeta_b = jnp.exp(lg_b[:, None] * idx[None])
    zf = jnp.repeat(zeta_f.T, dk, axis=1)
    zb = jnp.repeat(zeta_b.T, dk, axis=1)
    return dm, xi_f, xi_b, zf, zb, jnp.exp(lg_f * L), jnp.exp(lg_b * L)


def _rope_tables(seq, dk):
    inv = ROPE_BASE ** (-jnp.arange(0, dk, 2, dtype=F32) / dk)
    ang = jnp.arange(seq, dtype=F32)[:, None] * inv[None]
    return jnp.cos(ang), jnp.sin(ang)


def kernel(x, c, rel_bias, att_w_qkv, att_w_o, att_sink, ret_w_in, ret_w_o, ret_decay_fwd,
           ret_decay_bwd, ada_w, ada_b, mix_norm_pre, mix_norm_post, ffn_norm_pre,
           ffn_norm_post, ffn_w_in, ffn_w_out):
    bsz, seq, d = x.shape
    t = bsz * seq
    assert seq % ROW_TILE == 0 and ROW_TILE % RET_CHUNK == 0 and d % RET_HEADS == 0
    assert ada_w.shape[0] == 2 and ffn_w_out.shape[1] % FFN_CHUNK == 0
    rows = -(-bsz // 8) * 8
    vec = lambda a: a.reshape(1, d)

    c_pad = jnp.pad(c, ((0, rows - bsz), (0, 0)))
    mod = _ada_mod(c_pad, ada_w, ada_b)
    modv = mod.reshape(2 * rows * 6, 1, d)

    x2d = x.reshape(t, d)

    nq = Q_HEADS * HEAD_DIM
    nkv = KV_HEADS * HEAD_DIM
    heads = [(2 * p + e) * GROUP + g for p in range(KV_HEADS // 2) for g in range(GROUP)
             for e in range(2)]
    w_q = att_w_qkv[0][:, :nq].reshape(d, Q_HEADS, HEAD_DIM)[:, jnp.array(heads)].reshape(d, nq)
    w_q = (w_q * HEAD_DIM ** -0.5).astype(BF16)
    w_k = att_w_qkv[0][:, nq:nq + nkv].astype(BF16)
    w_vt = att_w_qkv[0][:, nq + nkv:].T.astype(BF16)
    q, k, vt = _qkv_proj(x2d, vec(mix_norm_pre[0]), modv, w_q, w_k, w_vt, seq, rows)
    att, w_ao, w_f0in, w_f0out = _attention(
        q.reshape(bsz, seq, nq), k.reshape(bsz, seq, nkv), vt, rel_bias.astype(F32),
        att_sink[0].astype(F32), [(att_w_o, 0, None), (ffn_w_in, 0, None), (ffn_w_out, 0, None)])

    dk = d // RET_HEADS
    nqk = RET_HEADS * dk
    kscale = jnp.concatenate([jnp.ones((nqk,), F32), jnp.full((nqk,), dk ** -0.5, F32),
                              jnp.ones((ret_w_in.shape[2] - 2 * nqk,), F32)])[None]
    x1, h1, w_rin, w_ro, w_f1in, w_f1out = _post_ffn(
        att.reshape(t, nq), x2d, w_ao, vec(mix_norm_post[0]), vec(ffn_norm_pre[0]), w_f0in,
        w_f0out, vec(ffn_norm_post[0]), modv, 0, seq, rows, g_next=vec(mix_norm_pre[1]),
        cast_jobs=[(ret_w_in, 0, kscale), (ret_w_o, 0, None), (ffn_w_in, 1, None),
                   (ffn_w_out, 1, None)])
    dm, xi_f, xi_b, zf, zb, cd_f, cd_b = _decay_tables(ret_decay_fwd[0], ret_decay_bwd[0], dk)
    cos, sin = _rope_tables(seq, dk)
    q, k, kzf, kzb, v, sg = _ret_in(h1, w_rin, cos, sin, zf, zb, seq)
    r3 = lambda a: a.reshape(bsz, seq, a.shape[1])
    gated = _retention(r3(q), r3(k), r3(kzf), r3(kzb), r3(v), r3(sg), xi_f, xi_b, dm, cd_f, cd_b)
    (x2,) = _post_ffn(gated.reshape(t, -1), x1, w_ro, vec(mix_norm_post[1]),
                      vec(ffn_norm_pre[1]), w_f1in, w_f1out, vec(ffn_norm_post[1]), modv, 1,
                      seq, rows)
    return x2.reshape(bsz, seq, d)
```

```python
import functools
import math

import jax
import jax.numpy as jnp
from jax import lax
from jax.experimental import pallas as pl
from jax.experimental.pallas import tpu as pltpu

F32 = jnp.float32
BF16 = jnp.bfloat16

EPS = 1e-6
NEG = -1e30
LOG2E = math.log2(math.e)

Q_HEADS = 16
KV_HEADS = 4
GROUP = Q_HEADS // KV_HEADS
HEAD_DIM = 64
ATT_BLOCK = 128
ATT_STEP_BLOCKS = 2
REL_BUCKETS = 32
RET_HEADS = 4
RET_CHUNK = 256
ROPE_BASE = 10000.0
FFN_CHUNK = 256
ROW_TILE = 512
VMEM_LIMIT = 56 * 1024 * 1024


def _silu(x):
    return x * (1.0 / (1.0 + jnp.exp(-x)))


def _rms(xf, g):
    ms = jnp.mean(xf * xf, axis=-1, keepdims=True)
    return (xf * lax.rsqrt(ms + EPS)) * g


def _resident(shape):
    zeros = (0,) * len(shape)
    return pl.BlockSpec(shape, lambda *_: zeros, pipeline_mode=pl.Buffered(1))


def _params(n_axes, vmem=VMEM_LIMIT):
    return pltpu.CompilerParams(
        dimension_semantics=("arbitrary",) * n_axes, vmem_limit_bytes=vmem)


BF16_SUBLANES = 16


def _cast_plan(rows, n_steps):
    per = BF16_SUBLANES
    while rows % per or rows // per > n_steps:
        per += BF16_SUBLANES
    return per, rows // per


def _cast_io(jobs, n_steps, step_of):
    in_specs, out_specs, out_shapes, args, plan = [], [], [], [], []
    for w, layer, scale in jobs:
        _, rows, cols = w.shape
        per, n_cast = _cast_plan(rows, n_steps)
        blk = lambda *idx, n_cast=n_cast: jnp.minimum(step_of(*idx), n_cast - 1)
        in_specs.append(pl.BlockSpec(
            (1, per, cols), lambda *idx, blk=blk, layer=layer: (layer, blk(*idx), 0)))
        args.append(w)
        if scale is not None:
            in_specs.append(_resident(scale.shape))
            args.append(scale)
        out_specs.append(pl.BlockSpec((per, cols), lambda *idx, blk=blk: (blk(*idx), 0)))
        out_shapes.append(jax.ShapeDtypeStruct((rows, cols), BF16))
        plan.append((n_cast, scale is not None))
    return in_specs, out_specs, out_shapes, args, tuple(plan)


def _run_casts(step, plan, in_refs, out_refs):
    in_refs = list(in_refs)
    for (n_cast, scaled), dst in zip(plan, out_refs):
        src = in_refs.pop(0)
        scale = in_refs.pop(0) if scaled else None

        @pl.when(step < n_cast)
        def _(src=src, scale=scale, dst=dst):
            v = src[0]
            if scale is not None:
                v = v * scale[...]
            dst[...] = v.astype(BF16)


def _n_cast_inputs(plan):
    return sum(2 if scaled else 1 for _, scaled in plan)


def _ada_kernel(c_ref, w_ref, b_ref, o_ref):
    ca = _silu(c_ref[...])
    o_ref[0] = jnp.dot(ca.astype(BF16), w_ref[0].astype(BF16),
                       preferred_element_type=F32) + b_ref[0]


def _ada_mod(c_pad, ada_w, ada_b):
    depth, d, n = ada_w.shape
    rows = c_pad.shape[0]
    tn = 1536
    return pl.pallas_call(
        _ada_kernel,
        grid=(depth, n // tn),
        in_specs=[
            pl.BlockSpec((rows, d), lambda i, j: (0, 0)),
            pl.BlockSpec((1, d, tn), lambda i, j: (i, 0, j)),
            pl.BlockSpec((1, 1, tn), lambda i, j: (i, 0, j)),
        ],
        out_specs=pl.BlockSpec((1, rows, tn), lambda i, j: (i, 0, j)),
        out_shape=jax.ShapeDtypeStruct((depth, rows, n), F32),
        compiler_params=_params(2),
        name="ada_mod",
    )(c_pad, ada_w, ada_b.reshape(depth, 1, n))


def _mod_spec(layer, slot, tiles_per_batch, n_batch_rows):
    base = layer * n_batch_rows * 6 + slot

    def index(t):
        return (base + (t // tiles_per_batch) * 6, 0, 0)
    return index


def _qkv_kernel(x_ref, g_ref, sc_ref, sh_ref, wq_ref, wk_ref, wvt_ref, q_ref, k_ref, vt_ref):
    h = _rms(x_ref[...], g_ref[...]) * (1.0 + sc_ref[0]) + sh_ref[0]
    hb = h.astype(BF16)
    q_ref[...] = jnp.dot(hb, wq_ref[...], preferred_element_type=F32).astype(BF16)
    k_ref[...] = jnp.dot(hb, wk_ref[...], preferred_element_type=F32).astype(BF16)
    vt_ref[0] = lax.dot_general(wvt_ref[...], hb, (((1,), (1,)), ((), ())),
                                preferred_element_type=F32).astype(BF16)


def _qkv_proj(x2d, g_pre, modv, w_q, w_k, w_vt, seq, n_batch_rows):
    t, d = x2d.shape
    nq = w_q.shape[1]
    nkv = w_k.shape[1]
    tm = ROW_TILE
    tpb = seq // tm
    vec = lambda idx: pl.BlockSpec((1, 1, d), idx)
    return pl.pallas_call(
        _qkv_kernel,
        grid=(t // tm,),
        in_specs=[
            pl.BlockSpec((tm, d), lambda i: (i, 0)),
            _resident((1, d)),
            vec(_mod_spec(0, 1, tpb, n_batch_rows)),
            vec(_mod_spec(0, 0, tpb, n_batch_rows)),
            _resident(w_q.shape), _resident(w_k.shape), _resident(w_vt.shape),
        ],
        out_specs=[
            pl.BlockSpec((tm, nq), lambda i: (i, 0)),
            pl.BlockSpec((tm, nkv), lambda i: (i, 0)),
            pl.BlockSpec((1, nkv, tm), lambda i: (i // tpb, 0, i % tpb)),
        ],
        out_shape=[
            jax.ShapeDtypeStruct((t, nq), BF16),
            jax.ShapeDtypeStruct((t, nkv), BF16),
            jax.ShapeDtypeStruct((t // seq, nkv, seq), BF16),
        ],
        compiler_params=_params(1),
        name="qkv_proj",
    )(x2d, g_pre, modv, modv, w_q, w_k, w_vt)


def _att_tables(rb_ref, b2_ref, m2_ref):
    L = ATT_BLOCK
    rows = 64
    for r in range(3 * L // rows):
        j = lax.broadcasted_iota(jnp.int32, (rows, L), 0) + r * rows
        t = lax.broadcasted_iota(jnp.int32, (rows, L), 1)
        rel = j - L - t
        n = jnp.abs(rel)
        large = jnp.full((rows, L), 8, jnp.int32)
        for thr in (12, 16, 23, 32, 46, 64, 91):
            large = large + (n >= thr).astype(jnp.int32)
        bucket = jnp.where(rel > 0, 16, 0) + jnp.where(n < 8, n, large)
        in_win = n <= L
        visible = (in_win & (j >= L), in_win, in_win & (j < 2 * L))
        piece = slice(r * rows, (r + 1) * rows)
        for kind in range(3):
            m2_ref[kind, piece, :] = jnp.where(visible[kind], LOG2E, 0.0).astype(F32)

        def body(hq, carry):
            acc = jnp.zeros((rows, L), F32)
            for b in range(REL_BUCKETS):
                acc = jnp.where(bucket == b, rb_ref[b, hq], acc)
            acc = acc * LOG2E
            for kind in range(3):
                b2_ref[kind, hq, piece, :] = jnp.where(visible[kind], acc, NEG)
            return carry
        lax.fori_loop(0, Q_HEADS, body, 0)


def _att_kernel(*refs, cast_plan):
    rb_ref, sink_ref, q_ref, kp_ref, kc_ref, kn_ref, vp_ref, vc_ref, vn_ref = refs[:9]
    n_cast_in = _n_cast_inputs(cast_plan)
    cast_in = refs[9:9 + n_cast_in]
    o_ref = refs[9 + n_cast_in]
    cast_out = refs[10 + n_cast_in:10 + n_cast_in + len(cast_plan)]
    b2_ref, m2_ref = refs[10 + n_cast_in + len(cast_plan):]
    b = pl.program_id(0)
    i = pl.program_id(1)
    ns = pl.num_programs(1)
    _run_casts(b * ns + i, cast_plan, cast_in, cast_out)

    @pl.when((b == 0) & (i == 0))
    def _():
        _att_tables(rb_ref, b2_ref, m2_ref)

    L = ATT_BLOCK
    dh = HEAD_DIM
    U = ATT_STEP_BLOCKS
    q = q_ref[0]
    kb = jnp.concatenate([kp_ref[0], kc_ref[0], kn_ref[0]], axis=0)
    vt = jnp.concatenate([vp_ref[0], vc_ref[0], vn_ref[0]], axis=1)
    lane = lax.broadcasted_iota(jnp.int32, (L, 2 * dh), 1)
    zero = jnp.zeros((L, 2 * dh), BF16)
    ones = jnp.ones((BF16_SUBLANES, 3 * L), BF16)
    kinds = [1] * U
    kinds[0] = jnp.where(i == 0, 0, 1)
    kinds[U - 1] = jnp.where(i == ns - 1, 2, 1)

    def scores(u, h):
        p, e = divmod(h, 2)
        k_pair = kb[u * L:(u + 3) * L, p * 2 * dh:(p + 1) * 2 * dh]
        mine = (lane < dh) if e == 0 else (lane >= dh)
        qz = jnp.concatenate(
            [jnp.where(mine, q[u * L:(u + 1) * L,
                               (p * GROUP + g) * 2 * dh:(p * GROUP + g + 1) * 2 * dh], zero)
             for g in range(GROUP)], axis=0)
        return lax.dot_general(k_pair, qz, (((1,), (1,)), ((), ())),
                               preferred_element_type=F32)

    units = [(u, h) for u in range(U) for h in range(KV_HEADS)]
    o_rows = []
    st_next = scores(*units[0])
    for idx, (u, h) in enumerate(units):
        st = st_next
        if idx + 1 < len(units):
            st_next = scores(*units[idx + 1])
        m2 = m2_ref[kinds[u]]
        es, sinks = [], []
        for g in range(GROUP):
            hq = h * GROUP + g
            sk2 = jnp.full((1, L), sink_ref[hq], F32) * LOG2E
            l2 = st[:, g * L:(g + 1) * L] * m2 + b2_ref[kinds[u], hq]
            m = jnp.maximum(jnp.max(l2, axis=0, keepdims=True), sk2)
            es.append(jnp.exp2(l2 - m).astype(BF16))
            sinks.append(jnp.exp2(sk2 - m))
        et = jnp.concatenate(es, axis=1)
        va = jnp.concatenate([vt[h * dh:(h + 1) * dh, u * L:(u + 3) * L], ones], axis=0)
        ot = jnp.dot(va, et, preferred_element_type=F32)
        den = ot[dh:dh + 1, :] + jnp.concatenate(sinks, axis=1)
        ot = ot[:dh, :] * (1.0 / den)
        o_rows += [ot[:, g * L:(g + 1) * L] for g in range(GROUP)]
        if h == KV_HEADS - 1:
            ot_all = jnp.concatenate(o_rows, axis=0)
            o_ref[0, u * L:(u + 1) * L, :] = ot_all.T.astype(BF16)
            o_rows = []


def _attention(q, k, vt, rel_bias, sink, cast_jobs):
    bsz, seq, dq = q.shape
    dkv = k.shape[2]
    L = ATT_BLOCK
    nb = seq // L
    U = ATT_STEP_BLOCKS
    ns = nb // U
    smem = pl.BlockSpec(memory_space=pltpu.SMEM)
    prev = lambda i: jnp.maximum(U * i - 1, 0)
    nxt = lambda i: jnp.minimum(U * i + U, nb - 1)
    c_in, c_out, c_shapes, c_args, plan = _cast_io(cast_jobs, bsz * ns, lambda b, i: b * ns + i)
    return pl.pallas_call(
        functools.partial(_att_kernel, cast_plan=plan),
        grid=(bsz, ns),
        in_specs=[
            smem, smem,
            pl.BlockSpec((1, U * L, dq), lambda b, i: (b, i, 0)),
            pl.BlockSpec((1, L, dkv), lambda b, i: (b, prev(i), 0)),
            pl.BlockSpec((1, U * L, dkv), lambda b, i: (b, i, 0)),
            pl.BlockSpec((1, L, dkv), lambda b, i: (b, nxt(i), 0)),
            pl.BlockSpec((1, dkv, L), lambda b, i: (b, 0, prev(i))),
            pl.BlockSpec((1, dkv, U * L), lambda b, i: (b, 0, i)),
            pl.BlockSpec((1, dkv, L), lambda b, i: (b, 0, nxt(i))),
        ] + c_in,
        out_specs=[pl.BlockSpec((1, U * L, dq), lambda b, i: (b, i, 0))] + c_out,
        out_shape=[jax.ShapeDtypeStruct((bsz, seq, dq), BF16)] + c_shapes,
        scratch_shapes=[
            pltpu.VMEM((3, Q_HEADS, 3 * L, L), F32),
            pltpu.VMEM((3, 3 * L, L), F32),
        ],
        compiler_params=_params(2),
        name="swa_attention",
    )(rel_bias, sink, q, k, k, k, vt, vt, vt, *c_args)


def _post_ffn_kernel(*refs, emit_next, cast_plan):
    (a_ref, x_ref, wo_ref, gpost_ref, g1_ref, gpre_ref, sc2_ref, sh2_ref,
     win_ref, wout_ref, gfpost_ref, g2_ref) = refs[:12]
    rest = list(refs[12:])
    if emit_next:
        gn_ref, scn_ref, shn_ref = rest[:3]
        rest = rest[3:]
    n_cast_in = _n_cast_inputs(cast_plan)
    cast_in, rest = rest[:n_cast_in], rest[n_cast_in:]
    xo_ref = rest.pop(0)
    if emit_next:
        ho_ref = rest.pop(0)
    cast_out = rest[:len(cast_plan)]
    hb_ref, x1_ref, acc_ref, y_ref = rest[len(cast_plan):]
    s = pl.program_id(0)
    n = pl.num_programs(0) - 2
    _run_casts(s, cast_plan, cast_in, cast_out)
    hidden = wout_ref.shape[0]

    def advance():
        hb_ref[1] = hb_ref[0]
        acc_ref[1] = acc_ref[0]
        x1_ref[2] = x1_ref[1]
        x1_ref[1] = x1_ref[0]

    n_chunks = hidden // FFN_CHUNK
    n_slices = 8
    rows_of = lambda r: slice(r * (x_ref.shape[0] // n_slices), (r + 1) * (x_ref.shape[0] // n_slices))

    def a_matmul():
        y_ref[...] = jnp.dot(a_ref[...], wo_ref[...], preferred_element_type=F32)

    def after(v, dep):
        if dep is None:
            return v
        zero = (pltpu.bitcast(dep, jnp.uint32) >> 16) >> 16
        return pltpu.bitcast(pltpu.bitcast(v, jnp.uint32) | zero, F32)

    def a_norm(r, dep=None):
        rows = rows_of(r)
        x1 = x_ref[rows, :] + g1_ref[0] * _rms(after(y_ref[rows, :], dep), gpost_ref[...])
        h = _rms(x1, gpre_ref[...]) * (1.0 + sc2_ref[0]) + sh2_ref[0]
        hb_ref[0, rows, :] = h.astype(BF16)
        x1_ref[0, rows, :] = x1

    def b_chunk(c, hb, acc):
        lo = c * FFN_CHUNK
        a = jnp.dot(hb, win_ref[:, lo:lo + FFN_CHUNK], preferred_element_type=F32)
        b = jnp.dot(hb, win_ref[:, hidden + lo:hidden + lo + FFN_CHUNK],
                    preferred_element_type=F32)
        act = (_silu(a) * b).astype(BF16)
        return acc + jnp.dot(act, wout_ref[lo:lo + FFN_CHUNK, :], preferred_element_type=F32)

    def c_slice(r, dep=None):
        rows = rows_of(r)
        x2 = x1_ref[2, rows, :] + g2_ref[0] * _rms(after(acc_ref[1, rows, :], dep),
                                                   gfpost_ref[...])
        xo_ref[rows, :] = x2
        if emit_next:
            hn = _rms(x2, gn_ref[...]) * (1.0 + scn_ref[0]) + shn_ref[0]
            ho_ref[rows, :] = hn.astype(BF16)

    @pl.when(s == 0)
    def _():
        x1_ref[...] = jnp.zeros(x1_ref.shape, F32)
        acc_ref[...] = jnp.zeros(acc_ref.shape, F32)
        a_matmul()
        for r in range(n_slices):
            a_norm(r)

    @pl.when((s >= 1) & (s <= n))
    def _():
        advance()
        a_matmul()
        hb = hb_ref[1]
        acc = jnp.zeros((hb.shape[0], wout_ref.shape[1]), F32)
        for c in range(n_chunks):
            acc = b_chunk(c, hb, acc)
            dep = acc[0:1, :]
            if c < n_slices:
                c_slice(c, dep)
            if c >= n_chunks - n_slices - 1 and c < n_chunks - 1:
                a_norm(c - (n_chunks - n_slices - 1), dep)
        acc_ref[0] = acc

    @pl.when(s == n + 1)
    def _():
        advance()
        for r in range(n_slices):
            c_slice(r)


def _post_ffn(a2d, x2d, w_o, g_post, g_pre, w_in, w_out, g_fpost, modv, layer,
              seq, n_batch_rows, g_next=None, cast_jobs=()):
    t, d = x2d.shape
    kin = a2d.shape[1]
    tm = ROW_TILE
    tpb = seq // tm
    n = t // tm
    emit_next = g_next is not None
    tile_a = lambda s: jnp.minimum(s, n - 1)
    tile_c = lambda s: jnp.clip(s - 2, 0, n - 1)

    def vec(lyr, slot, tile_of):
        idx = _mod_spec(lyr, slot, tpb, n_batch_rows)
        return pl.BlockSpec((1, 1, d), lambda s: idx(tile_of(s)))
    row = lambda width, tile_of: pl.BlockSpec((tm, width), lambda s: (tile_of(s), 0))
    in_specs = [
        row(kin, tile_a), row(d, tile_a), _resident(w_o.shape), _resident((1, d)),
        vec(layer, 2, tile_a), _resident((1, d)), vec(layer, 4, tile_a), vec(layer, 3, tile_a),
        _resident(w_in.shape), _resident(w_out.shape), _resident((1, d)), vec(layer, 5, tile_c),
    ]
    args = [a2d, x2d, w_o, g_post, modv, g_pre, modv, modv, w_in, w_out, g_fpost, modv]
    out_specs = [row(d, tile_c)]
    out_shape = [jax.ShapeDtypeStruct((t, d), F32)]
    if emit_next:
        in_specs += [_resident((1, d)), vec(layer + 1, 1, tile_c), vec(layer + 1, 0, tile_c)]
        args += [g_next, modv, modv]
        out_specs.append(row(d, tile_c))
        out_shape.append(jax.ShapeDtypeStruct((t, d), BF16))
    c_in, c_out, c_shapes, c_args, plan = _cast_io(cast_jobs, n, lambda s: s)
    return pl.pallas_call(
        functools.partial(_post_ffn_kernel, emit_next=emit_next, cast_plan=plan),
        grid=(n + 2,),
        in_specs=in_specs + c_in,
        out_specs=out_specs + c_out,
        out_shape=out_shape + c_shapes,
        scratch_shapes=[pltpu.VMEM((2, tm, d), BF16), pltpu.VMEM((3, tm, d), F32),
                        pltpu.VMEM((2, tm, d), F32), pltpu.VMEM((tm, d), F32)],
        compiler_params=_params(1),
        name="post_ffn_next" if emit_next else "post_ffn",
    )(*args, *c_args)


def _ret_in_kernel(h_ref, w_ref, cos_ref, sin_ref, zf_ref, zb_ref,
                   q_ref, k_ref, kzf_ref, kzb_ref, v_ref, sg_ref):
    hb = h_ref[...]
    tm = hb.shape[0]
    cos = cos_ref[...]
    sin = sin_ref[...]
    dk = w_ref.shape[0] // RET_HEADS
    half = dk // 2
    nqk = RET_HEADS * dk
    nv = (w_ref.shape[1] - 2 * nqk) // 2

    yq = jnp.dot(hb, w_ref[:, 0:nqk], preferred_element_type=F32)
    yk = jnp.dot(hb, w_ref[:, nqk:2 * nqk], preferred_element_type=F32)
    zf = zf_ref[...][None]
    zb = zb_ref[...][None]
    for h in range(RET_HEADS):
        for off in (0, half):
            lo = h * dk + off
            sign = -1.0 if off == 0 else 1.0
            other = h * dk + (half - off)
            rq = yq[:, lo:lo + half] * cos + sign * (yq[:, other:other + half] * sin)
            rk = yk[:, lo:lo + half] * cos + sign * (yk[:, other:other + half] * sin)
            q_ref[:, lo:lo + half] = rq.astype(BF16)
            k_ref[:, lo:lo + half] = rk.astype(BF16)
            rk3 = rk.reshape(tm // RET_CHUNK, RET_CHUNK, half)
            kzf_ref[:, lo:lo + half] = (rk3 * zf[:, :, lo:lo + half]).reshape(tm, half).astype(BF16)
            kzb_ref[:, lo:lo + half] = (rk3 * zb[:, :, lo:lo + half]).reshape(tm, half).astype(BF16)
    v_ref[...] = jnp.dot(hb, w_ref[:, 2 * nqk:2 * nqk + nv],
                         preferred_element_type=F32).astype(BF16)
    g = jnp.dot(hb, w_ref[:, 2 * nqk + nv:], preferred_element_type=F32)
    sg_ref[...] = _silu(g)


def _ret_in(h2d, w_in, cos, sin, zf, zb, seq):
    t, d = h2d.shape
    tm = ROW_TILE
    tpb = seq // tm
    nqk = RET_HEADS * (d // RET_HEADS)
    nv = (w_in.shape[1] - 2 * nqk) // 2
    row = lambda width: pl.BlockSpec((tm, width), lambda i: (i, 0))
    pos = pl.BlockSpec((tm, cos.shape[1]), lambda i: (i % tpb, 0))
    return pl.pallas_call(
        _ret_in_kernel,
        grid=(t // tm,),
        in_specs=[row(d), _resident(w_in.shape), pos, pos,
                  _resident(zf.shape), _resident(zb.shape)],
        out_specs=[row(nqk), row(nqk), row(nqk), row(nqk), row(nv), row(nv)],
        out_shape=[jax.ShapeDtypeStruct((t, nqk), BF16)] * 4
        + [jax.ShapeDtypeStruct((t, nv), BF16), jax.ShapeDtypeStruct((t, nv), F32)],
        compiler_params=_params(1),
        name="ret_in_proj",
    )(h2d, w_in, cos, sin, zf, zb)


def _state_step(st_ref, h, q_h, kz_h, v_h, xi, cd):
    st = st_ref[h]
    cross = jnp.dot(q_h, st.astype(BF16), preferred_element_type=F32) * xi
    upd = lax.dot_general(kz_h, v_h, (((0,), (0,)), ((), ())), preferred_element_type=F32)
    st_ref[h] = st * cd + upd
    return cross


def _ret_scan_kernel(cdf_ref, cdb_ref, q_ref, k_ref, kzf_ref, kzb_ref, v_ref, sg_ref,
                     xif_ref, xib_ref, dm_ref, o_ref, st_ref, yb_ref):
    j = pl.program_id(1)
    nc = pl.num_programs(1) // 2

    @pl.when((j == 0) | (j == nc))
    def _():
        st_ref[...] = jnp.zeros(st_ref.shape, F32)

    q = q_ref[0]
    v = v_ref[0]
    dk = q.shape[1] // RET_HEADS
    dv = v.shape[1] // RET_HEADS

    @pl.when(j < nc)
    def _():
        kz = kzb_ref[0]
        c = nc - 1 - j
        for h in range(RET_HEADS):
            yb_ref[c, :, h * dv:(h + 1) * dv] = _state_step(
                st_ref, h, q[:, h * dk:(h + 1) * dk], kz[:, h * dk:(h + 1) * dk],
                v[:, h * dv:(h + 1) * dv], xib_ref[h], cdb_ref[h])

    @pl.when(j >= nc)
    def _():
        k = k_ref[0]
        kz = kzf_ref[0]
        c = j - nc
        for h in range(RET_HEADS):
            q_h = q[:, h * dk:(h + 1) * dk]
            v_h = v[:, h * dv:(h + 1) * dv]
            s = lax.dot_general(q_h, k[:, h * dk:(h + 1) * dk], (((1,), (1,)), ((), ())),
                                preferred_element_type=F32)
            inner = jnp.dot((s * dm_ref[h]).astype(BF16), v_h, preferred_element_type=F32)
            cross = _state_step(st_ref, h, q_h, kz[:, h * dk:(h + 1) * dk], v_h,
                                xif_ref[h], cdf_ref[h])
            y = inner + cross + yb_ref[c, :, h * dv:(h + 1) * dv]
            yn = y * lax.rsqrt(jnp.mean(y * y, axis=-1, keepdims=True) + EPS)
            o_ref[0, :, h * dv:(h + 1) * dv] = (
                sg_ref[0, :, h * dv:(h + 1) * dv] * yn).astype(BF16)


def _retention(q, k, kzf, kzb, v, sg, xi_f, xi_b, dmat, cd_f, cd_b):
    bsz, seq, nqk = q.shape
    nv = v.shape[2]
    L = RET_CHUNK
    nc = seq // L
    dk = nqk // RET_HEADS
    dv = nv // RET_HEADS
    smem = pl.BlockSpec(memory_space=pltpu.SMEM)
    both = lambda j: jnp.where(j < nc, nc - 1 - j, j - nc)
    bwd_only = lambda j: jnp.where(j < nc, nc - 1 - j, 0)
    fwd_only = lambda j: jnp.where(j < nc, 0, j - nc)
    blk = lambda width, chunk: pl.BlockSpec((1, L, width), lambda b, j: (b, chunk(j), 0))
    return pl.pallas_call(
        _ret_scan_kernel,
        grid=(bsz, 2 * nc),
        in_specs=[smem, smem, blk(nqk, both), blk(nqk, fwd_only), blk(nqk, fwd_only),
                  blk(nqk, bwd_only), blk(nv, both), blk(nv, fwd_only),
                  _resident(xi_f.shape), _resident(xi_b.shape), _resident(dmat.shape)],
        out_specs=blk(nv, fwd_only),
        out_shape=jax.ShapeDtypeStruct((bsz, seq, nv), BF16),
        scratch_shapes=[pltpu.VMEM((RET_HEADS, dk, dv), F32), pltpu.VMEM((nc, L, nv), F32)],
        compiler_params=_params(2),
        name="ret_scan",
    )(cd_f, cd_b, q, k, kzf, kzb, v, sg, xi_f, xi_b, dmat)


def _decay_tables(decay_fwd, decay_bwd, dk):
    L = RET_CHUNK
    lg_f = jax.nn.log_sigmoid(decay_fwd.astype(F32))
    lg_b = jax.nn.log_sigmoid(decay_bwd.astype(F32))
    idx = jnp.arange(L, dtype=F32)
    diff = idx[:, None] - idx[None, :]
    dm = jnp.where((diff >= 0)[None],
                   jnp.exp(lg_f[:, None, None] * jnp.maximum(diff, 0.0)[None]),
                   jnp.exp(lg_b[:, None, None] * jnp.maximum(-diff, 0.0)[None]))
    xi_f = jnp.exp(lg_f[:, None] * (idx + 1.0)[None])[:, :, None]
    xi_b = jnp.exp(lg_b[:, None] * (L - idx)[None])[:, :, None]
    zeta_f = jnp.exp(lg_f[:, None] * (L - 1.0 - idx)[None])
    zeta_b = jnp.exp(lg_b[:, None] * idx[None])
    zf = jnp.repeat(zeta_f.T, dk, axis=1)
    zb = jnp.repeat(zeta_b.T, dk, axis=1)
    return dm, xi_f, xi_b, zf, zb, jnp.exp(lg_f * L), jnp.exp(lg_b * L)


def _rope_tables(seq, dk):
    inv = ROPE_BASE ** (-jnp.arange(0, dk, 2, dtype=F32) / dk)
    ang = jnp.arange(seq, dtype=F32)[:, None] * inv[None]
    return jnp.cos(ang), jnp.sin(ang)


def kernel(x, c, rel_bias, att_w_qkv, att_w_o, att_sink, ret_w_in, ret_w_o, ret_decay_fwd,
           ret_decay_bwd, ada_w, ada_b, mix_norm_pre, mix_norm_post, ffn_norm_pre,
           ffn_norm_post, ffn_w_in, ffn_w_out):
    bsz, seq, d = x.shape
    t = bsz * seq
    assert seq % ROW_TILE == 0 and ROW_TILE % RET_CHUNK == 0 and d % RET_HEADS == 0
    assert ada_w.shape[0] == 2 and ffn_w_out.shape[1] % FFN_CHUNK == 0
    rows = -(-bsz // 8) * 8
    vec = lambda a: a.reshape(1, d)

    c_pad = jnp.pad(c, ((0, rows - bsz), (0, 0)))
    mod = _ada_mod(c_pad, ada_w, ada_b)
    modv = mod.reshape(2 * rows * 6, 1, d)

    x2d = x.reshape(t, d)

    nq = Q_HEADS * HEAD_DIM
    nkv = KV_HEADS * HEAD_DIM
    heads = [(2 * p + e) * GROUP + g for p in range(KV_HEADS // 2) for g in range(GROUP)
             for e in range(2)]
    w_q = att_w_qkv[0][:, :nq].reshape(d, Q_HEADS, HEAD_DIM)[:, jnp.array(heads)].reshape(d, nq)
    w_q = (w_q * HEAD_DIM ** -0.5).astype(BF16)
    w_k = att_w_qkv[0][:, nq:nq + nkv].astype(BF16)
    w_vt = att_w_qkv[0][:, nq + nkv:].T.astype(BF16)
    q, k, vt = _qkv_proj(x2d, vec(mix_norm_pre[0]), modv, w_q, w_k, w_vt, seq, rows)
    att, w_ao, w_f0in, w_f0out = _attention(
        q.reshape(bsz, seq, nq), k.reshape(bsz, seq, nkv), vt, rel_bias.astype(F32),
        att_sink[0].astype(F32), [(att_w_o, 0, None), (ffn_w_in, 0, None), (ffn_w_out, 0, None)])

    dk = d // RET_HEADS
    nqk = RET_HEADS * dk
    kscale = jnp.concatenate([jnp.ones((nqk,), F32), jnp.full((nqk,), dk ** -0.5, F32),
                              jnp.ones((ret_w_in.shape[2] - 2 * nqk,), F32)])[None]
    x1, h1, w_rin, w_ro, w_f1in, w_f1out = _post_ffn(
        att.reshape(t, nq), x2d, w_ao, vec(mix_norm_post[0]), vec(ffn_norm_pre[0]), w_f0in,
        w_f0out, vec(ffn_norm_post[0]), modv, 0, seq, rows, g_next=vec(mix_norm_pre[1]),
        cast_jobs=[(ret_w_in, 0, kscale), (ret_w_o, 0, None), (ffn_w_in, 1, None),
                   (ffn_w_out, 1, None)])
    dm, xi_f, xi_b, zf, zb, cd_f, cd_b = _decay_tables(ret_decay_fwd[0], ret_decay_bwd[0], dk)
    cos, sin = _rope_tables(seq, dk)
    q, k, kzf, kzb, v, sg = _ret_in(h1, w_rin, cos, sin, zf, zb, seq)
    r3 = lambda a: a.reshape(bsz, seq, a.shape[1])
    gated = _retention(r3(q), r3(k), r3(kzf), r3(kzb), r3(v), r3(sg), xi_f, xi_b, dm, cd_f, cd_b)
    (x2,) = _post_ffn(gated.reshape(t, -1), x1, w_ro, vec(mix_norm_post[1]),
                      vec(ffn_norm_pre[1]), w_f1in, w_f1out, vec(ffn_norm_post[1]), modv, 1,
                      seq, rows)
    return x2.reshape(bsz, seq, d)
```

```python
import functools
import math

import jax
import jax.numpy as jnp
from jax import lax
from jax.experimental import pallas as pl
from jax.experimental.pallas import tpu as pltpu

F32 = jnp.float32
BF16 = jnp.bfloat16

EPS = 1e-6
NEG = -1e30
LOG2E = math.log2(math.e)

Q_HEADS = 16
KV_HEADS = 4
GROUP = Q_HEADS // KV_HEADS
HEAD_DIM = 64
ATT_BLOCK = 128
ATT_STEP_BLOCKS = 4
REL_BUCKETS = 32
RET_HEADS = 4
RET_CHUNK = 256
ROPE_BASE = 10000.0
FFN_CHUNK = 256
ROW_TILE = 512
VMEM_LIMIT = 56 * 1024 * 1024


def _silu(x):
    return x * (1.0 / (1.0 + jnp.exp(-x)))


def _rms(xf, g):
    ms = jnp.mean(xf * xf, axis=-1, keepdims=True)
    return (xf * lax.rsqrt(ms + EPS)) * g


def _resident(shape):
    zeros = (0,) * len(shape)
    return pl.BlockSpec(shape, lambda *_: zeros, pipeline_mode=pl.Buffered(1))


def _params(n_axes, vmem=VMEM_LIMIT):
    return pltpu.CompilerParams(
        dimension_semantics=("arbitrary",) * n_axes, vmem_limit_bytes=vmem)


BF16_SUBLANES = 16


def _cast_plan(rows, n_steps):
    per = BF16_SUBLANES
    while rows % per or rows // per > n_steps:
        per += BF16_SUBLANES
    return per, rows // per


def _cast_io(jobs, n_steps, step_of):
    in_specs, out_specs, out_shapes, args, plan = [], [], [], [], []
    for w, layer, scale in jobs:
        _, rows, cols = w.shape
        per, n_cast = _cast_plan(rows, n_steps)
        blk = lambda *idx, n_cast=n_cast: jnp.minimum(step_of(*idx), n_cast - 1)
        in_specs.append(pl.BlockSpec(
            (1, per, cols), lambda *idx, blk=blk, layer=layer: (layer, blk(*idx), 0)))
        args.append(w)
        if scale is not None:
            in_specs.append(_resident(scale.shape))
            args.append(scale)
        out_specs.append(pl.BlockSpec((per, cols), lambda *idx, blk=blk: (blk(*idx), 0)))
        out_shapes.append(jax.ShapeDtypeStruct((rows, cols), BF16))
        plan.append((n_cast, scale is not None))
    return in_specs, out_specs, out_shapes, args, tuple(plan)


def _run_casts(step, plan, in_refs, out_refs):
    in_refs = list(in_refs)
    for (n_cast, scaled), dst in zip(plan, out_refs):
        src = in_refs.pop(0)
        scale = in_refs.pop(0) if scaled else None

        @pl.when(step < n_cast)
        def _(src=src, scale=scale, dst=dst):
            v = src[0]
            if scale is not None:
                v = v * scale[...]
            dst[...] = v.astype(BF16)


def _n_cast_inputs(plan):
    return sum(2 if scaled else 1 for _, scaled in plan)


def _ada_kernel(c_ref, w_ref, b_ref, o_ref):
    ca = _silu(c_ref[...])
    o_ref[0] = jnp.dot(ca.astype(BF16), w_ref[0].astype(BF16),
                       preferred_element_type=F32) + b_ref[0]


def _ada_mod(c_pad, ada_w, ada_b):
    depth, d, n = ada_w.shape
    rows = c_pad.shape[0]
    tn = 1536
    return pl.pallas_call(
        _ada_kernel,
        grid=(depth, n // tn),
        in_specs=[
            pl.BlockSpec((rows, d), lambda i, j: (0, 0)),
            pl.BlockSpec((1, d, tn), lambda i, j: (i, 0, j)),
            pl.BlockSpec((1, 1, tn), lambda i, j: (i, 0, j)),
        ],
        out_specs=pl.BlockSpec((1, rows, tn), lambda i, j: (i, 0, j)),
        out_shape=jax.ShapeDtypeStruct((depth, rows, n), F32),
        compiler_params=_params(2),
        name="ada_mod",
    )(c_pad, ada_w, ada_b.reshape(depth, 1, n))


def _mod_spec(layer, slot, tiles_per_batch, n_batch_rows):
    base = layer * n_batch_rows * 6 + slot

    def index(t):
        return (base + (t // tiles_per_batch) * 6, 0, 0)
    return index


def _qkv_kernel(x_ref, g_ref, sc_ref, sh_ref, wq_ref, wk_ref, wvt_ref, q_ref, k_ref, vt_ref):
    h = _rms(x_ref[...], g_ref[...]) * (1.0 + sc_ref[0]) + sh_ref[0]
    hb = h.astype(BF16)
    q_ref[...] = jnp.dot(hb, wq_ref[...], preferred_element_type=F32).astype(BF16)
    k_ref[...] = jnp.dot(hb, wk_ref[...], preferred_element_type=F32).astype(BF16)
    vt_ref[0] = lax.dot_general(wvt_ref[...], hb, (((1,), (1,)), ((), ())),
                                preferred_element_type=F32).astype(BF16)


def _qkv_proj(x2d, g_pre, modv, w_q, w_k, w_vt, seq, n_batch_rows):
    t, d = x2d.shape
    nq = w_q.shape[1]
    nkv = w_k.shape[1]
    tm = ROW_TILE
    tpb = seq // tm
    vec = lambda idx: pl.BlockSpec((1, 1, d), idx)
    return pl.pallas_call(
        _qkv_kernel,
        grid=(t // tm,),
        in_specs=[
            pl.BlockSpec((tm, d), lambda i: (i, 0)),
            _resident((1, d)),
            vec(_mod_spec(0, 1, tpb, n_batch_rows)),
            vec(_mod_spec(0, 0, tpb, n_batch_rows)),
            _resident(w_q.shape), _resident(w_k.shape), _resident(w_vt.shape),
        ],
        out_specs=[
            pl.BlockSpec((tm, nq), lambda i: (i, 0)),
            pl.BlockSpec((tm, nkv), lambda i: (i, 0)),
            pl.BlockSpec((1, nkv, tm), lambda i: (i // tpb, 0, i % tpb)),
        ],
        out_shape=[
            jax.ShapeDtypeStruct((t, nq), BF16),
            jax.ShapeDtypeStruct((t, nkv), BF16),
            jax.ShapeDtypeStruct((t // seq, nkv, seq), BF16),
        ],
        compiler_params=_params(1),
        name="qkv_proj",
    )(x2d, g_pre, modv, modv, w_q, w_k, w_vt)


def _att_tables(rb_ref, b2_ref, m2_ref):
    L = ATT_BLOCK
    rows = 64
    for r in range(3 * L // rows):
        j = lax.broadcasted_iota(jnp.int32, (rows, L), 0) + r * rows
        t = lax.broadcasted_iota(jnp.int32, (rows, L), 1)
        rel = j - L - t
        n = jnp.abs(rel)
        large = jnp.full((rows, L), 8, jnp.int32)
        for thr in (12, 16, 23, 32, 46, 64, 91):
            large = large + (n >= thr).astype(jnp.int32)
        bucket = jnp.where(rel > 0, 16, 0) + jnp.where(n < 8, n, large)
        in_win = n <= L
        visible = (in_win & (j >= L), in_win, in_win & (j < 2 * L))
        piece = slice(r * rows, (r + 1) * rows)
        for kind in range(3):
            m2_ref[kind, piece, :] = jnp.where(visible[kind], LOG2E, 0.0).astype(F32)

        def body(hq, carry):
            acc = jnp.zeros((rows, L), F32)
            for b in range(REL_BUCKETS):
                acc = jnp.where(bucket == b, rb_ref[b, hq], acc)
            acc = acc * LOG2E
            for kind in range(3):
                b2_ref[kind, hq, piece, :] = jnp.where(visible[kind], acc, NEG)
            return carry
        lax.fori_loop(0, Q_HEADS, body, 0)


def _att_kernel(*refs, cast_plan):
    rb_ref, sink_ref, q_ref, kp_ref, kc_ref, kn_ref, vp_ref, vc_ref, vn_ref = refs[:9]
    n_cast_in = _n_cast_inputs(cast_plan)
    cast_in = refs[9:9 + n_cast_in]
    o_ref = refs[9 + n_cast_in]
    cast_out = refs[10 + n_cast_in:10 + n_cast_in + len(cast_plan)]
    b2_ref, m2_ref = refs[10 + n_cast_in + len(cast_plan):]
    b = pl.program_id(0)
    i = pl.program_id(1)
    ns = pl.num_programs(1)
    _run_casts(b * ns + i, cast_plan, cast_in, cast_out)

    @pl.when((b == 0) & (i == 0))
    def _():
        _att_tables(rb_ref, b2_ref, m2_ref)

    L = ATT_BLOCK
    dh = HEAD_DIM
    U = ATT_STEP_BLOCKS
    q = q_ref[0]
    kb = jnp.concatenate([kp_ref[0], kc_ref[0], kn_ref[0]], axis=0)
    vt = jnp.concatenate([vp_ref[0], vc_ref[0], vn_ref[0]], axis=1)
    lane = lax.broadcasted_iota(jnp.int32, (L, 2 * dh), 1)
    zero = jnp.zeros((L, 2 * dh), BF16)
    ones = jnp.ones((BF16_SUBLANES, 3 * L), BF16)
    kinds = [1] * U
    kinds[0] = jnp.where(i == 0, 0, 1)
    kinds[U - 1] = jnp.where(i == ns - 1, 2, 1)

    def scores(u, h):
        p, e = divmod(h, 2)
        k_pair = kb[u * L:(u + 3) * L, p * 2 * dh:(p + 1) * 2 * dh]
        mine = (lane < dh) if e == 0 else (lane >= dh)
        qz = jnp.concatenate(
            [jnp.where(mine, q[u * L:(u + 1) * L,
                               (p * GROUP + g) * 2 * dh:(p * GROUP + g + 1) * 2 * dh], zero)
             for g in range(GROUP)], axis=0)
        return lax.dot_general(k_pair, qz, (((1,), (1,)), ((), ())),
                               preferred_element_type=F32)

    units = [(u, h) for u in range(U) for h in range(KV_HEADS)]
    o_rows = []
    st_next = scores(*units[0])
    for idx, (u, h) in enumerate(units):
        st = st_next
        if idx + 1 < len(units):
            st_next = scores(*units[idx + 1])
        m2 = m2_ref[kinds[u]]
        es, sinks = [], []
        for g in range(GROUP):
            hq = h * GROUP + g
            sk2 = jnp.full((1, L), sink_ref[hq], F32) * LOG2E
            l2 = st[:, g * L:(g + 1) * L] * m2 + b2_ref[kinds[u], hq]
            m = jnp.maximum(jnp.max(l2, axis=0, keepdims=True), sk2)
            es.append(jnp.exp2(l2 - m).astype(BF16))
            sinks.append(jnp.exp2(sk2 - m))
        et = jnp.concatenate(es, axis=1)
        va = jnp.concatenate([vt[h * dh:(h + 1) * dh, u * L:(u + 3) * L], ones], axis=0)
        ot = jnp.dot(va, et, preferred_element_type=F32)
        den = ot[dh:dh + 1, :] + jnp.concatenate(sinks, axis=1)
        ot = ot[:dh, :] * (1.0 / den)
        o_rows += [ot[:, g * L:(g + 1) * L] for g in range(GROUP)]
        if h == KV_HEADS - 1:
            ot_all = jnp.concatenate(o_rows, axis=0)
            o_ref[0, u * L:(u + 1) * L, :] = ot_all.T.astype(BF16)
            o_rows = []


def _attention(q, k, vt, rel_bias, sink, cast_jobs):
    bsz, seq, dq = q.shape
    dkv = k.shape[2]
    L = ATT_BLOCK
    nb = seq // L
    U = ATT_STEP_BLOCKS
    ns = nb // U
    smem = pl.BlockSpec(memory_space=pltpu.SMEM)
    prev = lambda i: jnp.maximum(U * i - 1, 0)
    nxt = lambda i: jnp.minimum(U * i + U, nb - 1)
    c_in, c_out, c_shapes, c_args, plan = _cast_io(cast_jobs, bsz * ns, lambda b, i: b * ns + i)
    return pl.pallas_call(
        functools.partial(_att_kernel, cast_plan=plan),
        grid=(bsz, ns),
        in_specs=[
            smem, smem,
            pl.BlockSpec((1, U * L, dq), lambda b, i: (b, i, 0)),
            pl.BlockSpec((1, L, dkv), lambda b, i: (b, prev(i), 0)),
            pl.BlockSpec((1, U * L, dkv), lambda b, i: (b, i, 0)),
            pl.BlockSpec((1, L, dkv), lambda b, i: (b, nxt(i), 0)),
            pl.BlockSpec((1, dkv, L), lambda b, i: (b, 0, prev(i))),
            pl.BlockSpec((1, dkv, U * L), lambda b, i: (b, 0, i)),
            pl.BlockSpec((1, dkv, L), lambda b, i: (b, 0, nxt(i))),
        ] + c_in,
        out_specs=[pl.BlockSpec((1, U * L, dq), lambda b, i: (b, i, 0))] + c_out,
        out_shape=[jax.ShapeDtypeStruct((bsz, seq, dq), BF16)] + c_shapes,
        scratch_shapes=[
            pltpu.VMEM((3, Q_HEADS, 3 * L, L), F32),
            pltpu.VMEM((3, 3 * L, L), F32),
        ],
        compiler_params=_params(2),
        name="swa_attention",
    )(rel_bias, sink, q, k, k, k, vt, vt, vt, *c_args)


def _post_ffn_kernel(*refs, emit_next, cast_plan):
    (a_ref, x_ref, wo_ref, gpost_ref, g1_ref, gpre_ref, sc2_ref, sh2_ref,
     win_ref, wout_ref, gfpost_ref, g2_ref) = refs[:12]
    rest = list(refs[12:])
    if emit_next:
        gn_ref, scn_ref, shn_ref = rest[:3]
        rest = rest[3:]
    n_cast_in = _n_cast_inputs(cast_plan)
    cast_in, rest = rest[:n_cast_in], rest[n_cast_in:]
    xo_ref = rest.pop(0)
    if emit_next:
        ho_ref = rest.pop(0)
    cast_out = rest[:len(cast_plan)]
    hb_ref, x1_ref, acc_ref, y_ref = rest[len(cast_plan):]
    s = pl.program_id(0)
    n = pl.num_programs(0) - 2
    _run_casts(s, cast_plan, cast_in, cast_out)
    hidden = wout_ref.shape[0]

    def advance():
        hb_ref[1] = hb_ref[0]
        acc_ref[1] = acc_ref[0]
        x1_ref[2] = x1_ref[1]
        x1_ref[1] = x1_ref[0]

    n_chunks = hidden // FFN_CHUNK
    n_slices = 8
    rows_of = lambda r: slice(r * (x_ref.shape[0] // n_slices), (r + 1) * (x_ref.shape[0] // n_slices))

    def a_matmul():
        y_ref[...] = jnp.dot(a_ref[...], wo_ref[...], preferred_element_type=F32)

    def after(v, dep):
        if dep is None:
            return v
        zero = (pltpu.bitcast(dep, jnp.uint32) >> 16) >> 16
        return v * pltpu.bitcast(zero | jnp.uint32(0x3F800000), F32)

    def a_norm(r, dep=None):
        rows = rows_of(r)
        x1 = x_ref[rows, :] + g1_ref[0] * _rms(after(y_ref[rows, :], dep), gpost_ref[...])
        h = _rms(x1, gpre_ref[...]) * (1.0 + sc2_ref[0]) + sh2_ref[0]
        hb_ref[0, rows, :] = h.astype(BF16)
        x1_ref[0, rows, :] = x1

    def b_chunk(c, hb, acc):
        lo = c * FFN_CHUNK
        a = jnp.dot(hb, win_ref[:, lo:lo + FFN_CHUNK], preferred_element_type=F32)
        b = jnp.dot(hb, win_ref[:, hidden + lo:hidden + lo + FFN_CHUNK],
                    preferred_element_type=F32)
        act = (_silu(a) * b).astype(BF16)
        return acc + jnp.dot(act, wout_ref[lo:lo + FFN_CHUNK, :], preferred_element_type=F32)

    def c_slice(r, dep=None):
        rows = rows_of(r)
        x2 = x1_ref[2, rows, :] + g2_ref[0] * _rms(after(acc_ref[1, rows, :], dep),
                                                   gfpost_ref[...])
        xo_ref[rows, :] = x2
        if emit_next:
            hn = _rms(x2, gn_ref[...]) * (1.0 + scn_ref[0]) + shn_ref[0]
            ho_ref[rows, :] = hn.astype(BF16)

    @pl.when(s == 0)
    def _():
        x1_ref[...] = jnp.zeros(x1_ref.shape, F32)
        acc_ref[...] = jnp.zeros(acc_ref.shape, F32)
        a_matmul()
        for r in range(n_slices):
            a_norm(r)

    @pl.when((s >= 1) & (s <= n))
    def _():
        advance()
        a_matmul()
        hb = hb_ref[1]
        acc = jnp.zeros((hb.shape[0], wout_ref.shape[1]), F32)
        for c in range(n_chunks):
            acc = b_chunk(c, hb, acc)
            dep = acc[0:1, :]
            if c < n_slices:
                c_slice(c, dep)
            if c >= n_chunks - n_slices - 1 and c < n_chunks - 1:
                a_norm(c - (n_chunks - n_slices - 1), dep)
        acc_ref[0] = acc

    @pl.when(s == n + 1)
    def _():
        advance()
        for r in range(n_slices):
            c_slice(r)


def _post_ffn(a2d, x2d, w_o, g_post, g_pre, w_in, w_out, g_fpost, modv, layer,
              seq, n_batch_rows, g_next=None, cast_jobs=()):
    t, d = x2d.shape
    kin = a2d.shape[1]
    tm = ROW_TILE
    tpb = seq // tm
    n = t // tm
    emit_next = g_next is not None
    tile_a = lambda s: jnp.minimum(s, n - 1)
    tile_c = lambda s: jnp.clip(s - 2, 0, n - 1)

    def vec(lyr, slot, tile_of):
        idx = _mod_spec(lyr, slot, tpb, n_batch_rows)
        return pl.BlockSpec((1, 1, d), lambda s: idx(tile_of(s)))
    row = lambda width, tile_of: pl.BlockSpec((tm, width), lambda s: (tile_of(s), 0))
    in_specs = [
        row(kin, tile_a), row(d, tile_a), _resident(w_o.shape), _resident((1, d)),
        vec(layer, 2, tile_a), _resident((1, d)), vec(layer, 4, tile_a), vec(layer, 3, tile_a),
        _resident(w_in.shape), _resident(w_out.shape), _resident((1, d)), vec(layer, 5, tile_c),
    ]
    args = [a2d, x2d, w_o, g_post, modv, g_pre, modv, modv, w_in, w_out, g_fpost, modv]
    out_specs = [row(d, tile_c)]
    out_shape = [jax.ShapeDtypeStruct((t, d), F32)]
    if emit_next:
        in_specs += [_resident((1, d)), vec(layer + 1, 1, tile_c), vec(layer + 1, 0, tile_c)]
        args += [g_next, modv, modv]
        out_specs.append(row(d, tile_c))
        out_shape.append(jax.ShapeDtypeStruct((t, d), BF16))
    c_in, c_out, c_shapes, c_args, plan = _cast_io(cast_jobs, n, lambda s: s)
    return pl.pallas_call(
        functools.partial(_post_ffn_kernel, emit_next=emit_next, cast_plan=plan),
        grid=(n + 2,),
        in_specs=in_specs + c_in,
        out_specs=out_specs + c_out,
        out_shape=out_shape + c_shapes,
        scratch_shapes=[pltpu.VMEM((2, tm, d), BF16), pltpu.VMEM((3, tm, d), F32),
                        pltpu.VMEM((2, tm, d), F32), pltpu.VMEM((tm, d), F32)],
        compiler_params=_params(1),
        name="post_ffn_next" if emit_next else "post_ffn",
    )(*args, *c_args)


def _ret_in_kernel(h_ref, w_ref, cos_ref, sin_ref, zf_ref, zb_ref,
                   q_ref, k_ref, kzf_ref, kzb_ref, v_ref, sg_ref):
    hb = h_ref[...]
    tm = hb.shape[0]
    cos = cos_ref[...]
    sin = sin_ref[...]
    dk = w_ref.shape[0] // RET_HEADS
    half = dk // 2
    nqk = RET_HEADS * dk
    nv = (w_ref.shape[1] - 2 * nqk) // 2

    yq = jnp.dot(hb, w_ref[:, 0:nqk], preferred_element_type=F32)
    yk = jnp.dot(hb, w_ref[:, nqk:2 * nqk], preferred_element_type=F32)
    zf = zf_ref[...][None]
    zb = zb_ref[...][None]
    for h in range(RET_HEADS):
        for off in (0, half):
            lo = h * dk + off
            sign = -1.0 if off == 0 else 1.0
            other = h * dk + (half - off)
            rq = yq[:, lo:lo + half] * cos + sign * (yq[:, other:other + half] * sin)
            rk = yk[:, lo:lo + half] * cos + sign * (yk[:, other:other + half] * sin)
            q_ref[:, lo:lo + half] = rq.astype(BF16)
            k_ref[:, lo:lo + half] = rk.astype(BF16)
            rk3 = rk.reshape(tm // RET_CHUNK, RET_CHUNK, half)
            kzf_ref[:, lo:lo + half] = (rk3 * zf[:, :, lo:lo + half]).reshape(tm, half).astype(BF16)
            kzb_ref[:, lo:lo + half] = (rk3 * zb[:, :, lo:lo + half]).reshape(tm, half).astype(BF16)
    v_ref[...] = jnp.dot(hb, w_ref[:, 2 * nqk:2 * nqk + nv],
                         preferred_element_type=F32).astype(BF16)
    g = jnp.dot(hb, w_ref[:, 2 * nqk + nv:], preferred_element_type=F32)
    sg_ref[...] = _silu(g)


def _ret_in(h2d, w_in, cos, sin, zf, zb, seq):
    t, d = h2d.shape
    tm = ROW_TILE
    tpb = seq // tm
    nqk = RET_HEADS * (d // RET_HEADS)
    nv = (w_in.shape[1] - 2 * nqk) // 2
    row = lambda width: pl.BlockSpec((tm, width), lambda i: (i, 0))
    pos = pl.BlockSpec((tm, cos.shape[1]), lambda i: (i % tpb, 0))
    return pl.pallas_call(
        _ret_in_kernel,
        grid=(t // tm,),
        in_specs=[row(d), _resident(w_in.shape), pos, pos,
                  _resident(zf.shape), _resident(zb.shape)],
        out_specs=[row(nqk), row(nqk), row(nqk), row(nqk), row(nv), row(nv)],
        out_shape=[jax.ShapeDtypeStruct((t, nqk), BF16)] * 4
        + [jax.ShapeDtypeStruct((t, nv), BF16), jax.ShapeDtypeStruct((t, nv), F32)],
        compiler_params=_params(1),
        name="ret_in_proj",
    )(h2d, w_in, cos, sin, zf, zb)


def _state_step(st_ref, h, q_h, kz_h, v_h, xi, cd):
    st = st_ref[h]
    cross = jnp.dot(q_h, st.astype(BF16), preferred_element_type=F32) * xi
    upd = lax.dot_general(kz_h, v_h, (((0,), (0,)), ((), ())), preferred_element_type=F32)
    st_ref[h] = st * cd + upd
    return cross


def _ret_scan_kernel(cdf_ref, cdb_ref, q_ref, k_ref, kzf_ref, kzb_ref, v_ref, sg_ref,
                     xif_ref, xib_ref, dm_ref, o_ref, st_ref, yb_ref):
    j = pl.program_id(1)
    nc = pl.num_programs(1) // 2

    @pl.when((j == 0) | (j == nc))
    def _():
        st_ref[...] = jnp.zeros(st_ref.shape, F32)

    q = q_ref[0]
    v = v_ref[0]
    dk = q.shape[1] // RET_HEADS
    dv = v.shape[1] // RET_HEADS

    @pl.when(j < nc)
    def _():
        kz = kzb_ref[0]
        c = nc - 1 - j
        for h in range(RET_HEADS):
            yb_ref[c, :, h * dv:(h + 1) * dv] = _state_step(
                st_ref, h, q[:, h * dk:(h + 1) * dk], kz[:, h * dk:(h + 1) * dk],
                v[:, h * dv:(h + 1) * dv], xib_ref[h], cdb_ref[h])

    @pl.when(j >= nc)
    def _():
        k = k_ref[0]
        kz = kzf_ref[0]
        c = j - nc
        for h in range(RET_HEADS):
            q_h = q[:, h * dk:(h + 1) * dk]
            v_h = v[:, h * dv:(h + 1) * dv]
            s = lax.dot_general(q_h, k[:, h * dk:(h + 1) * dk], (((1,), (1,)), ((), ())),
                                preferred_element_type=F32)
            inner = jnp.dot((s * dm_ref[h]).astype(BF16), v_h, preferred_element_type=F32)
            cross = _state_step(st_ref, h, q_h, kz[:, h * dk:(h + 1) * dk], v_h,
                                xif_ref[h], cdf_ref[h])
            y = inner + cross + yb_ref[c, :, h * dv:(h + 1) * dv]
            yn = y * lax.rsqrt(jnp.mean(y * y, axis=-1, keepdims=True) + EPS)
            o_ref[0, :, h * dv:(h + 1) * dv] = (
                sg_ref[0, :, h * dv:(h + 1) * dv] * yn).astype(BF16)


def _retention(q, k, kzf, kzb, v, sg, xi_f, xi_b, dmat, cd_f, cd_b):
    bsz, seq, nqk = q.shape
    nv = v.shape[2]
    L = RET_CHUNK
    nc = seq // L
    dk = nqk // RET_HEADS
    dv = nv // RET_HEADS
    smem = pl.BlockSpec(memory_space=pltpu.SMEM)
    both = lambda j: jnp.where(j < nc, nc - 1 - j, j - nc)
    bwd_only = lambda j: jnp.where(j < nc, nc - 1 - j, 0)
    fwd_only = lambda j: jnp.where(j < nc, 0, j - nc)
    blk = lambda width, chunk: pl.BlockSpec((1, L, width), lambda b, j: (b, chunk(j), 0))
    return pl.pallas_call(
        _ret_scan_kernel,
        grid=(bsz, 2 * nc),
        in_specs=[smem, smem, blk(nqk, both), blk(nqk, fwd_only), blk(nqk, fwd_only),
                  blk(nqk, bwd_only), blk(nv, both), blk(nv, fwd_only),
                  _resident(xi_f.shape), _resident(xi_b.shape), _resident(dmat.shape)],
        out_specs=blk(nv, fwd_only),
        out_shape=jax.ShapeDtypeStruct((bsz, seq, nv), BF16),
        scratch_shapes=[pltpu.VMEM((RET_HEADS, dk, dv), F32), pltpu.VMEM((nc, L, nv), F32)],
        compiler_params=_params(2),
        name="ret_scan",
    )(cd_f, cd_b, q, k, kzf, kzb, v, sg, xi_f, xi_b, dmat)


def _decay_tables(decay_fwd, decay_bwd, dk):
    L = RET_CHUNK
    lg_f = jax.nn.log_sigmoid(decay_fwd.astype(F32))
    lg_b = jax.nn.log_sigmoid(decay_bwd.astype(F32))
    idx = jnp.arange(L, dtype=F32)
    diff = idx[:, None] - idx[None, :]
    dm = jnp.where((diff >= 0)[None],
                   jnp.exp(lg_f[:, None, None] * jnp.maximum(diff, 0.0)[None]),
                   jnp.exp(lg_b[:, None, None] * jnp.maximum(-diff, 0.0)[None]))
    xi_f = jnp.exp(lg_f[:, None] * (idx + 1.0)[None])[:, :, None]
    xi_b = jnp.exp(lg_b[:, None] * (L - idx)[None])[:, :, None]
    zeta_f = jnp.exp(lg_f[:, None] * (L - 1.0 - idx)[None])
    zeta_b = jnp.exp(lg_b[:, None] * idx[None])
    zf = jnp.repeat(zeta_f.T, dk, axis=1)
    zb = jnp.repeat(zeta_b.T, dk, axis=1)
    return dm, xi_f, xi_b, zf, zb, jnp.exp(lg_f * L), jnp.exp(lg_b * L)


def _rope_tables(seq, dk):
    inv = ROPE_BASE ** (-jnp.arange(0, dk, 2, dtype=F32) / dk)
    ang = jnp.arange(seq, dtype=F32)[:, None] * inv[None]
    return jnp.cos(ang), jnp.sin(ang)


def kernel(x, c, rel_bias, att_w_qkv, att_w_o, att_sink, ret_w_in, ret_w_o, ret_decay_fwd,
           ret_decay_bwd, ada_w, ada_b, mix_norm_pre, mix_norm_post, ffn_norm_pre,
           ffn_norm_post, ffn_w_in, ffn_w_out):
    bsz, seq, d = x.shape
    t = bsz * seq
    assert seq % ROW_TILE == 0 and ROW_TILE % RET_CHUNK == 0 and d % RET_HEADS == 0
    assert ada_w.shape[0] == 2 and ffn_w_out.shape[1] % FFN_CHUNK == 0
    rows = -(-bsz // 8) * 8
    vec = lambda a: a.reshape(1, d)

    c_pad = jnp.pad(c, ((0, rows - bsz), (0, 0)))
    mod = _ada_mod(c_pad, ada_w, ada_b)
    modv = mod.reshape(2 * rows * 6, 1, d)

    x2d = x.reshape(t, d)

    nq = Q_HEADS * HEAD_DIM
    nkv = KV_HEADS * HEAD_DIM
    heads = [(2 * p + e) * GROUP + g for p in range(KV_HEADS // 2) for g in range(GROUP)
             for e in range(2)]
    w_q = att_w_qkv[0][:, :nq].reshape(d, Q_HEADS, HEAD_DIM)[:, jnp.array(heads)].reshape(d, nq)
    w_q = (w_q * HEAD_DIM ** -0.5).astype(BF16)
    w_k = att_w_qkv[0][:, nq:nq + nkv].astype(BF16)
    w_vt = att_w_qkv[0][:, nq + nkv:].T.astype(BF16)
    q, k, vt = _qkv_proj(x2d, vec(mix_norm_pre[0]), modv, w_q, w_k, w_vt, seq, rows)
    att, w_ao, w_f0in, w_f0out = _attention(
        q.reshape(bsz, seq, nq), k.reshape(bsz, seq, nkv), vt, rel_bias.astype(F32),
        att_sink[0].astype(F32), [(att_w_o, 0, None), (ffn_w_in, 0, None), (ffn_w_out, 0, None)])

    dk = d // RET_HEADS
    nqk = RET_HEADS * dk
    kscale = jnp.concatenate([jnp.ones((nqk,), F32), jnp.full((nqk,), dk ** -0.5, F32),
                              jnp.ones((ret_w_in.shape[2] - 2 * nqk,), F32)])[None]
    x1, h1, w_rin, w_ro, w_f1in, w_f1out = _post_ffn(
        att.reshape(t, nq), x2d, w_ao, vec(mix_norm_post[0]), vec(ffn_norm_pre[0]), w_f0in,
        w_f0out, vec(ffn_norm_post[0]), modv, 0, seq, rows, g_next=vec(mix_norm_pre[1]),
        cast_jobs=[(ret_w_in, 0, kscale), (ret_w_o, 0, None), (ffn_w_in, 1, None),
                   (ffn_w_out, 1, None)])
    dm, xi_f, xi_b, zf, zb, cd_f, cd_b = _decay_tables(ret_decay_fwd[0], ret_decay_bwd[0], dk)
    cos, sin = _rope_tables(seq, dk)
    q, k, kzf, kzb, v, sg = _ret_in(h1, w_rin, cos, sin, zf, zb, seq)
    r3 = lambda a: a.reshape(bsz, seq, a.shape[1])
    gated = _retention(r3(q), r3(k), r3(kzf), r3(kzb), r3(v), r3(sg), xi_f, xi_b, dm, cd_f, cd_b)
    (x2,) = _post_ffn(gated.reshape(t, -1), x1, w_ro, vec(mix_norm_post[1]),
                      vec(ffn_norm_pre[1]), w_f1in, w_f1out, vec(ffn_norm_post[1]), modv, 1,
                      seq, rows)
    return x2.reshape(bsz, seq, d)
```

```python
import functools
import math

import jax
import jax.numpy as jnp
from jax import lax
from jax.experimental import pallas as pl
from jax.experimental.pallas import tpu as pltpu

F32 = jnp.float32
BF16 = jnp.bfloat16

EPS = 1e-6
NEG = -1e30
LOG2E = math.log2(math.e)

Q_HEADS = 16
KV_HEADS = 4
GROUP = Q_HEADS // KV_HEADS
HEAD_DIM = 64
ATT_BLOCK = 128
ATT_STEP_BLOCKS = 4
REL_BUCKETS = 32
RET_HEADS = 4
RET_CHUNK = 256
RET_STEP_CHUNKS = 2
ROPE_BASE = 10000.0
FFN_CHUNK = 256
ROW_TILE = 512
VMEM_LIMIT = 56 * 1024 * 1024


def _silu(x):
    return x * (1.0 / (1.0 + jnp.exp(-x)))


def _rms(xf, g):
    ms = jnp.mean(xf * xf, axis=-1, keepdims=True)
    return (xf * lax.rsqrt(ms + EPS)) * g


def _resident(shape):
    zeros = (0,) * len(shape)
    return pl.BlockSpec(shape, lambda *_: zeros, pipeline_mode=pl.Buffered(1))


def _params(n_axes, vmem=VMEM_LIMIT):
    return pltpu.CompilerParams(
        dimension_semantics=("arbitrary",) * n_axes, vmem_limit_bytes=vmem)


BF16_SUBLANES = 16


def _cast_plan(rows, n_steps):
    per = BF16_SUBLANES
    while rows % per or rows // per > n_steps:
        per += BF16_SUBLANES
    return per, rows // per


def _cast_io(jobs, n_steps, step_of):
    in_specs, out_specs, out_shapes, args, plan = [], [], [], [], []
    for w, layer, scale in jobs:
        _, rows, cols = w.shape
        per, n_cast = _cast_plan(rows, n_steps)
        blk = lambda *idx, n_cast=n_cast: jnp.minimum(step_of(*idx), n_cast - 1)
        in_specs.append(pl.BlockSpec(
            (1, per, cols), lambda *idx, blk=blk, layer=layer: (layer, blk(*idx), 0)))
        args.append(w)
        if scale is not None:
            in_specs.append(_resident(scale.shape))
            args.append(scale)
        out_specs.append(pl.BlockSpec((per, cols), lambda *idx, blk=blk: (blk(*idx), 0)))
        out_shapes.append(jax.ShapeDtypeStruct((rows, cols), BF16))
        plan.append((n_cast, scale is not None))
    return in_specs, out_specs, out_shapes, args, tuple(plan)


def _run_casts(step, plan, in_refs, out_refs):
    in_refs = list(in_refs)
    for (n_cast, scaled), dst in zip(plan, out_refs):
        src = in_refs.pop(0)
        scale = in_refs.pop(0) if scaled else None

        @pl.when(step < n_cast)
        def _(src=src, scale=scale, dst=dst):
            v = src[0]
            if scale is not None:
                v = v * scale[...]
            dst[...] = v.astype(BF16)


def _n_cast_inputs(plan):
    return sum(2 if scaled else 1 for _, scaled in plan)


def _ada_kernel(c_ref, w_ref, b_ref, o_ref):
    ca = _silu(c_ref[...])
    o_ref[0] = jnp.dot(ca.astype(BF16), w_ref[0].astype(BF16),
                       preferred_element_type=F32) + b_ref[0]


def _ada_mod(c_pad, ada_w, ada_b):
    depth, d, n = ada_w.shape
    rows = c_pad.shape[0]
    tn = 1536
    return pl.pallas_call(
        _ada_kernel,
        grid=(depth, n // tn),
        in_specs=[
            pl.BlockSpec((rows, d), lambda i, j: (0, 0)),
            pl.BlockSpec((1, d, tn), lambda i, j: (i, 0, j)),
            pl.BlockSpec((1, 1, tn), lambda i, j: (i, 0, j)),
        ],
        out_specs=pl.BlockSpec((1, rows, tn), lambda i, j: (i, 0, j)),
        out_shape=jax.ShapeDtypeStruct((depth, rows, n), F32),
        compiler_params=_params(2),
        name="ada_mod",
    )(c_pad, ada_w, ada_b.reshape(depth, 1, n))


def _mod_spec(layer, slot, tiles_per_batch, n_batch_rows):
    base = layer * n_batch_rows * 6 + slot

    def index(t):
        return (base + (t // tiles_per_batch) * 6, 0, 0)
    return index


def _qkv_kernel(x_ref, g_ref, sc_ref, sh_ref, wq_ref, wk_ref, wvt_ref, q_ref, k_ref, vt_ref):
    h = _rms(x_ref[...], g_ref[...]) * (1.0 + sc_ref[0]) + sh_ref[0]
    hb = h.astype(BF16)
    q_ref[...] = jnp.dot(hb, wq_ref[...], preferred_element_type=F32).astype(BF16)
    k_ref[...] = jnp.dot(hb, wk_ref[...], preferred_element_type=F32).astype(BF16)
    vt_ref[0] = lax.dot_general(wvt_ref[...], hb, (((1,), (1,)), ((), ())),
                                preferred_element_type=F32).astype(BF16)


def _qkv_proj(x2d, g_pre, modv, w_q, w_k, w_vt, seq, n_batch_rows):
    t, d = x2d.shape
    nq = w_q.shape[1]
    nkv = w_k.shape[1]
    tm = ROW_TILE
    tpb = seq // tm
    vec = lambda idx: pl.BlockSpec((1, 1, d), idx)
    return pl.pallas_call(
        _qkv_kernel,
        grid=(t // tm,),
        in_specs=[
            pl.BlockSpec((tm, d), lambda i: (i, 0)),
            _resident((1, d)),
            vec(_mod_spec(0, 1, tpb, n_batch_rows)),
            vec(_mod_spec(0, 0, tpb, n_batch_rows)),
            _resident(w_q.shape), _resident(w_k.shape), _resident(w_vt.shape),
        ],
        out_specs=[
            pl.BlockSpec((tm, nq), lambda i: (i, 0)),
            pl.BlockSpec((tm, nkv), lambda i: (i, 0)),
            pl.BlockSpec((1, nkv, tm), lambda i: (i // tpb, 0, i % tpb)),
        ],
        out_shape=[
            jax.ShapeDtypeStruct((t, nq), BF16),
            jax.ShapeDtypeStruct((t, nkv), BF16),
            jax.ShapeDtypeStruct((t // seq, nkv, seq), BF16),
        ],
        compiler_params=_params(1),
        name="qkv_proj",
    )(x2d, g_pre, modv, modv, w_q, w_k, w_vt)


def _att_tables(rb_ref, b2_ref, m2_ref):
    L = ATT_BLOCK
    rows = 64
    for r in range(3 * L // rows):
        j = lax.broadcasted_iota(jnp.int32, (rows, L), 0) + r * rows
        t = lax.broadcasted_iota(jnp.int32, (rows, L), 1)
        rel = j - L - t
        n = jnp.abs(rel)
        large = jnp.full((rows, L), 8, jnp.int32)
        for thr in (12, 16, 23, 32, 46, 64, 91):
            large = large + (n >= thr).astype(jnp.int32)
        bucket = jnp.where(rel > 0, 16, 0) + jnp.where(n < 8, n, large)
        in_win = n <= L
        visible = (in_win & (j >= L), in_win, in_win & (j < 2 * L))
        piece = slice(r * rows, (r + 1) * rows)
        for kind in range(3):
            m2_ref[kind, piece, :] = jnp.where(visible[kind], LOG2E, 0.0).astype(F32)

        def body(hq, carry):
            acc = jnp.zeros((rows, L), F32)
            for b in range(REL_BUCKETS):
                acc = jnp.where(bucket == b, rb_ref[b, hq], acc)
            acc = acc * LOG2E
            for kind in range(3):
                b2_ref[kind, hq, piece, :] = jnp.where(visible[kind], acc, NEG)
            return carry
        lax.fori_loop(0, Q_HEADS, body, 0)


def _att_kernel(*refs, cast_plan):
    rb_ref, sink_ref, q_ref, kp_ref, kc_ref, kn_ref, vp_ref, vc_ref, vn_ref = refs[:9]
    n_cast_in = _n_cast_inputs(cast_plan)
    cast_in = refs[9:9 + n_cast_in]
    o_ref = refs[9 + n_cast_in]
    cast_out = refs[10 + n_cast_in:10 + n_cast_in + len(cast_plan)]
    b2_ref, m2_ref = refs[10 + n_cast_in + len(cast_plan):]
    b = pl.program_id(0)
    i = pl.program_id(1)
    ns = pl.num_programs(1)
    _run_casts(b * ns + i, cast_plan, cast_in, cast_out)

    @pl.when((b == 0) & (i == 0))
    def _():
        _att_tables(rb_ref, b2_ref, m2_ref)

    L = ATT_BLOCK
    dh = HEAD_DIM
    U = ATT_STEP_BLOCKS
    q = q_ref[0]
    kb = jnp.concatenate([kp_ref[0], kc_ref[0], kn_ref[0]], axis=0)
    vt = jnp.concatenate([vp_ref[0], vc_ref[0], vn_ref[0]], axis=1)
    lane = lax.broadcasted_iota(jnp.int32, (L, 2 * dh), 1)
    zero = jnp.zeros((L, 2 * dh), BF16)
    ones = jnp.ones((BF16_SUBLANES, 3 * L), BF16)
    kinds = [1] * U
    kinds[0] = jnp.where(i == 0, 0, 1)
    kinds[U - 1] = jnp.where(i == ns - 1, 2, 1)

    def scores(u, h):
        p, e = divmod(h, 2)
        k_pair = kb[u * L:(u + 3) * L, p * 2 * dh:(p + 1) * 2 * dh]
        mine = (lane < dh) if e == 0 else (lane >= dh)
        qz = jnp.concatenate(
            [jnp.where(mine, q[u * L:(u + 1) * L,
                               (p * GROUP + g) * 2 * dh:(p * GROUP + g + 1) * 2 * dh], zero)
             for g in range(GROUP)], axis=0)
        return lax.dot_general(k_pair, qz, (((1,), (1,)), ((), ())),
                               preferred_element_type=F32)

    units = [(u, h) for u in range(U) for h in range(KV_HEADS)]
    o_rows = []
    st_next = scores(*units[0])
    for idx, (u, h) in enumerate(units):
        st = st_next
        if idx + 1 < len(units):
            st_next = scores(*units[idx + 1])
        m2 = m2_ref[kinds[u]]
        es, sinks = [], []
        for g in range(GROUP):
            hq = h * GROUP + g
            sk2 = jnp.full((1, L), sink_ref[hq], F32) * LOG2E
            l2 = st[:, g * L:(g + 1) * L] * m2 + b2_ref[kinds[u], hq]
            m = jnp.maximum(jnp.max(l2, axis=0, keepdims=True), sk2)
            es.append(jnp.exp2(l2 - m).astype(BF16))
            sinks.append(jnp.exp2(sk2 - m))
        et = jnp.concatenate(es, axis=1)
        va = jnp.concatenate([vt[h * dh:(h + 1) * dh, u * L:(u + 3) * L], ones], axis=0)
        ot = jnp.dot(va, et, preferred_element_type=F32)
        den = ot[dh:dh + 1, :] + jnp.concatenate(sinks, axis=1)
        ot = ot[:dh, :] * (1.0 / den)
        o_rows += [ot[:, g * L:(g + 1) * L] for g in range(GROUP)]
        if h == KV_HEADS - 1:
            ot_all = jnp.concatenate(o_rows, axis=0)
            o_ref[0, u * L:(u + 1) * L, :] = ot_all.T.astype(BF16)
            o_rows = []


def _attention(q, k, vt, rel_bias, sink, cast_jobs):
    bsz, seq, dq = q.shape
    dkv = k.shape[2]
    L = ATT_BLOCK
    nb = seq // L
    U = ATT_STEP_BLOCKS
    ns = nb // U
    smem = pl.BlockSpec(memory_space=pltpu.SMEM)
    prev = lambda i: jnp.maximum(U * i - 1, 0)
    nxt = lambda i: jnp.minimum(U * i + U, nb - 1)
    c_in, c_out, c_shapes, c_args, plan = _cast_io(cast_jobs, bsz * ns, lambda b, i: b * ns + i)
    return pl.pallas_call(
        functools.partial(_att_kernel, cast_plan=plan),
        grid=(bsz, ns),
        in_specs=[
            smem, smem,
            pl.BlockSpec((1, U * L, dq), lambda b, i: (b, i, 0)),
            pl.BlockSpec((1, L, dkv), lambda b, i: (b, prev(i), 0)),
            pl.BlockSpec((1, U * L, dkv), lambda b, i: (b, i, 0)),
            pl.BlockSpec((1, L, dkv), lambda b, i: (b, nxt(i), 0)),
            pl.BlockSpec((1, dkv, L), lambda b, i: (b, 0, prev(i))),
            pl.BlockSpec((1, dkv, U * L), lambda b, i: (b, 0, i)),
            pl.BlockSpec((1, dkv, L), lambda b, i: (b, 0, nxt(i))),
        ] + c_in,
        out_specs=[pl.BlockSpec((1, U * L, dq), lambda b, i: (b, i, 0))] + c_out,
        out_shape=[jax.ShapeDtypeStruct((bsz, seq, dq), BF16)] + c_shapes,
        scratch_shapes=[
            pltpu.VMEM((3, Q_HEADS, 3 * L, L), F32),
            pltpu.VMEM((3, 3 * L, L), F32),
        ],
        compiler_params=_params(2),
        name="swa_attention",
    )(rel_bias, sink, q, k, k, k, vt, vt, vt, *c_args)


def _post_ffn_kernel(*refs, emit_next, cast_plan):
    (a_ref, x_ref, wo_ref, gpost_ref, g1_ref, gpre_ref, sc2_ref, sh2_ref,
     win_ref, wout_ref, gfpost_ref, g2_ref) = refs[:12]
    rest = list(refs[12:])
    if emit_next:
        gn_ref, scn_ref, shn_ref = rest[:3]
        rest = rest[3:]
    n_cast_in = _n_cast_inputs(cast_plan)
    cast_in, rest = rest[:n_cast_in], rest[n_cast_in:]
    xo_ref = rest.pop(0)
    if emit_next:
        ho_ref = rest.pop(0)
    cast_out = rest[:len(cast_plan)]
    hb_ref, x1_ref, acc_ref, y_ref = rest[len(cast_plan):]
    s = pl.program_id(0)
    n = pl.num_programs(0) - 2
    _run_casts(s, cast_plan, cast_in, cast_out)
    hidden = wout_ref.shape[0]

    def advance():
        hb_ref[1] = hb_ref[0]
        acc_ref[1] = acc_ref[0]
        x1_ref[2] = x1_ref[1]
        x1_ref[1] = x1_ref[0]

    n_chunks = hidden // FFN_CHUNK
    n_slices = 8
    rows_of = lambda r: slice(r * (x_ref.shape[0] // n_slices), (r + 1) * (x_ref.shape[0] // n_slices))

    def a_matmul():
        y_ref[...] = jnp.dot(a_ref[...], wo_ref[...], preferred_element_type=F32)

    def after(v, dep):
        if dep is None:
            return v
        zero = (pltpu.bitcast(dep, jnp.uint32) >> 16) >> 16
        return v * pltpu.bitcast(zero | jnp.uint32(0x3F800000), F32)

    def a_norm(r, dep=None):
        rows = rows_of(r)
        x1 = x_ref[rows, :] + g1_ref[0] * _rms(after(y_ref[rows, :], dep), gpost_ref[...])
        h = _rms(x1, gpre_ref[...]) * (1.0 + sc2_ref[0]) + sh2_ref[0]
        hb_ref[0, rows, :] = h.astype(BF16)
        x1_ref[0, rows, :] = x1

    def b_chunk(c, hb, acc):
        lo = c * FFN_CHUNK
        a = jnp.dot(hb, win_ref[:, lo:lo + FFN_CHUNK], preferred_element_type=F32)
        b = jnp.dot(hb, win_ref[:, hidden + lo:hidden + lo + FFN_CHUNK],
                    preferred_element_type=F32)
        act = (_silu(a) * b).astype(BF16)
        return acc + jnp.dot(act, wout_ref[lo:lo + FFN_CHUNK, :], preferred_element_type=F32)

    def c_slice(r, dep=None):
        rows = rows_of(r)
        x2 = x1_ref[2, rows, :] + g2_ref[0] * _rms(after(acc_ref[1, rows, :], dep),
                                                   gfpost_ref[...])
        xo_ref[rows, :] = x2
        if emit_next:
            hn = _rms(x2, gn_ref[...]) * (1.0 + scn_ref[0]) + shn_ref[0]
            ho_ref[rows, :] = hn.astype(BF16)

    @pl.when(s == 0)
    def _():
        x1_ref[...] = jnp.zeros(x1_ref.shape, F32)
        acc_ref[...] = jnp.zeros(acc_ref.shape, F32)
        a_matmul()
        for r in range(n_slices):
            a_norm(r)

    @pl.when((s >= 1) & (s <= n))
    def _():
        advance()
        a_matmul()
        hb = hb_ref[1]
        acc = jnp.zeros((hb.shape[0], wout_ref.shape[1]), F32)
        for c in range(n_chunks):
            acc = b_chunk(c, hb, acc)
            dep = acc[0:1, :]
            if c < n_slices:
                c_slice(c, dep)
            if c >= n_chunks - n_slices - 1 and c < n_chunks - 1:
                a_norm(c - (n_chunks - n_slices - 1), dep)
        acc_ref[0] = acc

    @pl.when(s == n + 1)
    def _():
        advance()
        for r in range(n_slices):
            c_slice(r)


def _post_ffn(a2d, x2d, w_o, g_post, g_pre, w_in, w_out, g_fpost, modv, layer,
              seq, n_batch_rows, g_next=None, cast_jobs=()):
    t, d = x2d.shape
    kin = a2d.shape[1]
    tm = ROW_TILE
    tpb = seq // tm
    n = t // tm
    emit_next = g_next is not None
    tile_a = lambda s: jnp.minimum(s, n - 1)
    tile_c = lambda s: jnp.clip(s - 2, 0, n - 1)

    def vec(lyr, slot, tile_of):
        idx = _mod_spec(lyr, slot, tpb, n_batch_rows)
        return pl.BlockSpec((1, 1, d), lambda s: idx(tile_of(s)))
    row = lambda width, tile_of: pl.BlockSpec((tm, width), lambda s: (tile_of(s), 0))
    in_specs = [
        row(kin, tile_a), row(d, tile_a), _resident(w_o.shape), _resident((1, d)),
        vec(layer, 2, tile_a), _resident((1, d)), vec(layer, 4, tile_a), vec(layer, 3, tile_a),
        _resident(w_in.shape), _resident(w_out.shape), _resident((1, d)), vec(layer, 5, tile_c),
    ]
    args = [a2d, x2d, w_o, g_post, modv, g_pre, modv, modv, w_in, w_out, g_fpost, modv]
    out_specs = [row(d, tile_c)]
    out_shape = [jax.ShapeDtypeStruct((t, d), F32)]
    if emit_next:
        in_specs += [_resident((1, d)), vec(layer + 1, 1, tile_c), vec(layer + 1, 0, tile_c)]
        args += [g_next, modv, modv]
        out_specs.append(row(d, tile_c))
        out_shape.append(jax.ShapeDtypeStruct((t, d), BF16))
    c_in, c_out, c_shapes, c_args, plan = _cast_io(cast_jobs, n, lambda s: s)
    return pl.pallas_call(
        functools.partial(_post_ffn_kernel, emit_next=emit_next, cast_plan=plan),
        grid=(n + 2,),
        in_specs=in_specs + c_in,
        out_specs=out_specs + c_out,
        out_shape=out_shape + c_shapes,
        scratch_shapes=[pltpu.VMEM((2, tm, d), BF16), pltpu.VMEM((3, tm, d), F32),
                        pltpu.VMEM((2, tm, d), F32), pltpu.VMEM((tm, d), F32)],
        compiler_params=_params(1),
        name="post_ffn_next" if emit_next else "post_ffn",
    )(*args, *c_args)


def _ret_in_kernel(h_ref, w_ref, cos_ref, sin_ref, zf_ref, zb_ref,
                   q_ref, k_ref, kzf_ref, kzb_ref, v_ref, sg_ref):
    hb = h_ref[...]
    tm = hb.shape[0]
    cos = cos_ref[...]
    sin = sin_ref[...]
    dk = w_ref.shape[0] // RET_HEADS
    half = dk // 2
    nqk = RET_HEADS * dk
    nv = (w_ref.shape[1] - 2 * nqk) // 2

    yq = jnp.dot(hb, w_ref[:, 0:nqk], preferred_element_type=F32)
    yk = jnp.dot(hb, w_ref[:, nqk:2 * nqk], preferred_element_type=F32)
    zf = zf_ref[...][None]
    zb = zb_ref[...][None]
    for h in range(RET_HEADS):
        for off in (0, half):
            lo = h * dk + off
            sign = -1.0 if off == 0 else 1.0
            other = h * dk + (half - off)
            rq = yq[:, lo:lo + half] * cos + sign * (yq[:, other:other + half] * sin)
            rk = yk[:, lo:lo + half] * cos + sign * (yk[:, other:other + half] * sin)
            q_ref[:, lo:lo + half] = rq.astype(BF16)
            k_ref[:, lo:lo + half] = rk.astype(BF16)
            rk3 = rk.reshape(tm // RET_CHUNK, RET_CHUNK, half)
            kzf_ref[:, lo:lo + half] = (rk3 * zf[:, :, lo:lo + half]).reshape(tm, half).astype(BF16)
            kzb_ref[:, lo:lo + half] = (rk3 * zb[:, :, lo:lo + half]).reshape(tm, half).astype(BF16)
    v_ref[...] = jnp.dot(hb, w_ref[:, 2 * nqk:2 * nqk + nv],
                         preferred_element_type=F32).astype(BF16)
    g = jnp.dot(hb, w_ref[:, 2 * nqk + nv:], preferred_element_type=F32)
    sg_ref[...] = _silu(g)


def _ret_in(h2d, w_in, cos, sin, zf, zb, seq):
    t, d = h2d.shape
    tm = ROW_TILE
    tpb = seq // tm
    nqk = RET_HEADS * (d // RET_HEADS)
    nv = (w_in.shape[1] - 2 * nqk) // 2
    row = lambda width: pl.BlockSpec((tm, width), lambda i: (i, 0))
    pos = pl.BlockSpec((tm, cos.shape[1]), lambda i: (i % tpb, 0))
    return pl.pallas_call(
        _ret_in_kernel,
        grid=(t // tm,),
        in_specs=[row(d), _resident(w_in.shape), pos, pos,
                  _resident(zf.shape), _resident(zb.shape)],
        out_specs=[row(nqk), row(nqk), row(nqk), row(nqk), row(nv), row(nv)],
        out_shape=[jax.ShapeDtypeStruct((t, nqk), BF16)] * 4
        + [jax.ShapeDtypeStruct((t, nv), BF16), jax.ShapeDtypeStruct((t, nv), F32)],
        compiler_params=_params(1),
        name="ret_in_proj",
    )(h2d, w_in, cos, sin, zf, zb)


def _state_update(st_ref, h, st, kz_h, v_h, cd):
    upd = lax.dot_general(kz_h, v_h, (((0,), (0,)), ((), ())), preferred_element_type=F32)
    st_ref[h] = st * cd + upd


def _ret_scan_kernel(cdf_ref, cdb_ref, q_ref, k_ref, kzf_ref, kzb_ref, v_ref, sg_ref,
                     xif_ref, xib_ref, dm_ref, o_ref, st_ref, snap_ref):
    j = pl.program_id(1)
    ns = pl.num_programs(1) // 2
    L = RET_CHUNK
    n_sub = RET_STEP_CHUNKS

    @pl.when((j == 0) | (j == ns))
    def _():
        st_ref[...] = jnp.zeros(st_ref.shape, F32)

    dk = q_ref.shape[2] // RET_HEADS
    dv = v_ref.shape[2] // RET_HEADS

    @pl.when(j < ns)
    def _():
        for ci in reversed(range(n_sub)):
            rows = slice(ci * L, (ci + 1) * L)
            c = (ns - 1 - j) * n_sub + ci
            for h in range(RET_HEADS):
                st = st_ref[h]
                snap_ref[c, h] = st.astype(BF16)
                _state_update(st_ref, h, st, kzb_ref[0, rows, h * dk:(h + 1) * dk],
                              v_ref[0, rows, h * dv:(h + 1) * dv], cdb_ref[h])

    @pl.when(j >= ns)
    def _():
        for ci in range(n_sub):
            rows = slice(ci * L, (ci + 1) * L)
            c = (j - ns) * n_sub + ci
            for h in range(RET_HEADS):
                q_h = q_ref[0, rows, h * dk:(h + 1) * dk]
                v_h = v_ref[0, rows, h * dv:(h + 1) * dv]
                s = lax.dot_general(q_h, k_ref[0, rows, h * dk:(h + 1) * dk],
                                    (((1,), (1,)), ((), ())), preferred_element_type=F32)
                inner = jnp.dot((s * dm_ref[h]).astype(BF16), v_h, preferred_element_type=F32)
                st = st_ref[h]
                cross_f = jnp.dot(q_h, st.astype(BF16), preferred_element_type=F32) * xif_ref[h]
                cross_b = jnp.dot(q_h, snap_ref[c, h], preferred_element_type=F32) * xib_ref[h]
                _state_update(st_ref, h, st, kzf_ref[0, rows, h * dk:(h + 1) * dk], v_h,
                              cdf_ref[h])
                y = inner + cross_f + cross_b
                yn = y * lax.rsqrt(jnp.mean(y * y, axis=-1, keepdims=True) + EPS)
                o_ref[0, rows, h * dv:(h + 1) * dv] = (
                    sg_ref[0, rows, h * dv:(h + 1) * dv] * yn).astype(BF16)


def _retention(q, k, kzf, kzb, v, sg, xi_f, xi_b, dmat, cd_f, cd_b):
    bsz, seq, nqk = q.shape
    nv = v.shape[2]
    L = RET_CHUNK
    rows = RET_STEP_CHUNKS * L
    ns = seq // rows
    dk = nqk // RET_HEADS
    dv = nv // RET_HEADS
    smem = pl.BlockSpec(memory_space=pltpu.SMEM)
    both = lambda j: jnp.where(j < ns, ns - 1 - j, j - ns)
    bwd_only = lambda j: jnp.where(j < ns, ns - 1 - j, 0)
    fwd_only = lambda j: jnp.where(j < ns, 0, j - ns)
    blk = lambda width, at: pl.BlockSpec((1, rows, width), lambda b, j: (b, at(j), 0))
    return pl.pallas_call(
        _ret_scan_kernel,
        grid=(bsz, 2 * ns),
        in_specs=[smem, smem, blk(nqk, fwd_only), blk(nqk, fwd_only), blk(nqk, fwd_only),
                  blk(nqk, bwd_only), blk(nv, both), blk(nv, fwd_only),
                  _resident(xi_f.shape), _resident(xi_b.shape), _resident(dmat.shape)],
        out_specs=blk(nv, fwd_only),
        out_shape=jax.ShapeDtypeStruct((bsz, seq, nv), BF16),
        scratch_shapes=[pltpu.VMEM((RET_HEADS, dk, dv), F32),
                        pltpu.VMEM((seq // L, RET_HEADS, dk, dv), BF16)],
        compiler_params=_params(2),
        name="ret_scan",
    )(cd_f, cd_b, q, k, kzf, kzb, v, sg, xi_f, xi_b, dmat)


def _decay_tables(decay_fwd, decay_bwd, dk):
    L = RET_CHUNK
    lg_f = jax.nn.log_sigmoid(decay_fwd.astype(F32))
    lg_b = jax.nn.log_sigmoid(decay_bwd.astype(F32))
    idx = jnp.arange(L, dtype=F32)
    diff = idx[:, None] - idx[None, :]
    dm = jnp.where((diff >= 0)[None],
                   jnp.exp(lg_f[:, None, None] * jnp.maximum(diff, 0.0)[None]),
                   jnp.exp(lg_b[:, None, None] * jnp.maximum(-diff, 0.0)[None]))
    xi_f = jnp.exp(lg_f[:, None] * (idx + 1.0)[None])[:, :, None]
    xi_b = jnp.exp(lg_b[:, None] * (L - idx)[None])[:, :, None]
    zeta_f = jnp.exp(lg_f[:, None] * (L - 1.0 - idx)[None])
    zeta_b = jnp.exp(lg_b[:, None] * idx[None])
    zf = jnp.repeat(zeta_f.T, dk, axis=1)
    zb = jnp.repeat(zeta_b.T, dk, axis=1)
    return dm, xi_f, xi_b, zf, zb, jnp.exp(lg_f * L), jnp.exp(lg_b * L)


def _rope_tables(seq, dk):
    inv = ROPE_BASE ** (-jnp.arange(0, dk, 2, dtype=F32) / dk)
    ang = jnp.arange(seq, dtype=F32)[:, None] * inv[None]
    return jnp.cos(ang), jnp.sin(ang)


def kernel(x, c, rel_bias, att_w_qkv, att_w_o, att_sink, ret_w_in, ret_w_o, ret_decay_fwd,
           ret_decay_bwd, ada_w, ada_b, mix_norm_pre, mix_norm_post, ffn_norm_pre,
           ffn_norm_post, ffn_w_in, ffn_w_out):
    bsz, seq, d = x.shape
    t = bsz * seq
    assert seq % ROW_TILE == 0 and ROW_TILE % RET_CHUNK == 0 and d % RET_HEADS == 0
    assert ada_w.shape[0] == 2 and ffn_w_out.shape[1] % FFN_CHUNK == 0
    rows = -(-bsz // 8) * 8
    vec = lambda a: a.reshape(1, d)

    c_pad = jnp.pad(c, ((0, rows - bsz), (0, 0)))
    mod = _ada_mod(c_pad, ada_w, ada_b)
    modv = mod.reshape(2 * rows * 6, 1, d)

    x2d = x.reshape(t, d)

    nq = Q_HEADS * HEAD_DIM
    nkv = KV_HEADS * HEAD_DIM
    heads = [(2 * p + e) * GROUP + g for p in range(KV_HEADS // 2) for g in range(GROUP)
             for e in range(2)]
    w_q = att_w_qkv[0][:, :nq].reshape(d, Q_HEADS, HEAD_DIM)[:, jnp.array(heads)].reshape(d, nq)
    w_q = (w_q * HEAD_DIM ** -0.5).astype(BF16)
    w_k = att_w_qkv[0][:, nq:nq + nkv].astype(BF16)
    w_vt = att_w_qkv[0][:, nq + nkv:].T.astype(BF16)
    q, k, vt = _qkv_proj(x2d, vec(mix_norm_pre[0]), modv, w_q, w_k, w_vt, seq, rows)
    att, w_ao, w_f0in, w_f0out = _attention(
        q.reshape(bsz, seq, nq), k.reshape(bsz, seq, nkv), vt, rel_bias.astype(F32),
        att_sink[0].astype(F32), [(att_w_o, 0, None), (ffn_w_in, 0, None), (ffn_w_out, 0, None)])

    dk = d // RET_HEADS
    nqk = RET_HEADS * dk
    kscale = jnp.concatenate([jnp.ones((nqk,), F32), jnp.full((nqk,), dk ** -0.5, F32),
                              jnp.ones((ret_w_in.shape[2] - 2 * nqk,), F32)])[None]
    x1, h1, w_rin, w_ro, w_f1in, w_f1out = _post_ffn(
        att.reshape(t, nq), x2d, w_ao, vec(mix_norm_post[0]), vec(ffn_norm_pre[0]), w_f0in,
        w_f0out, vec(ffn_norm_post[0]), modv, 0, seq, rows, g_next=vec(mix_norm_pre[1]),
        cast_jobs=[(ret_w_in, 0, kscale), (ret_w_o, 0, None), (ffn_w_in, 1, None),
                   (ffn_w_out, 1, None)])
    dm, xi_f, xi_b, zf, zb, cd_f, cd_b = _decay_tables(ret_decay_fwd[0], ret_decay_bwd[0], dk)
    cos, sin = _rope_tables(seq, dk)
    q, k, kzf, kzb, v, sg = _ret_in(h1, w_rin, cos, sin, zf, zb, seq)
    r3 = lambda a: a.reshape(bsz, seq, a.shape[1])
    gated = _retention(r3(q), r3(k), r3(kzf), r3(kzb), r3(v), r3(sg), xi_f, xi_b, dm, cd_f, cd_b)
    (x2,) = _post_ffn(gated.reshape(t, -1), x1, w_ro, vec(mix_norm_post[1]),
                      vec(ffn_norm_pre[1]), w_f1in, w_f1out, vec(ffn_norm_post[1]), modv, 1,
                      seq, rows)
    return x2.reshape(bsz, seq, d)
```

```python
import functools
import math

import jax
import jax.numpy as jnp
from jax import lax
from jax.experimental import pallas as pl
from jax.experimental.pallas import tpu as pltpu

F32 = jnp.float32
BF16 = jnp.bfloat16

EPS = 1e-6
NEG = -1e30
LOG2E = math.log2(math.e)

Q_HEADS = 16
KV_HEADS = 4
GROUP = Q_HEADS // KV_HEADS
HEAD_DIM = 64
ATT_BLOCK = 128
ATT_STEP_BLOCKS = 4
REL_BUCKETS = 32
RET_HEADS = 4
RET_CHUNK = 256
RET_STEP_CHUNKS = 2
ROPE_BASE = 10000.0
FFN_CHUNK = 256
ROW_TILE = 512
VMEM_LIMIT = 56 * 1024 * 1024


def _silu(x):
    return x * (1.0 / (1.0 + jnp.exp(-x)))


def _rms(xf, g):
    ms = jnp.mean(xf * xf, axis=-1, keepdims=True)
    return (xf * lax.rsqrt(ms + EPS)) * g


def _resident(shape):
    zeros = (0,) * len(shape)
    return pl.BlockSpec(shape, lambda *_: zeros, pipeline_mode=pl.Buffered(1))


def _params(n_axes, vmem=VMEM_LIMIT):
    return pltpu.CompilerParams(
        dimension_semantics=("arbitrary",) * n_axes, vmem_limit_bytes=vmem)


BF16_SUBLANES = 16


def _cast_plan(rows, n_steps):
    per = BF16_SUBLANES
    while rows % per or rows // per > n_steps:
        per += BF16_SUBLANES
    return per, rows // per


def _cast_io(jobs, n_steps, step_of):
    in_specs, out_specs, out_shapes, args, plan = [], [], [], [], []
    for w, layer, scale in jobs:
        _, rows, cols = w.shape
        per, n_cast = _cast_plan(rows, n_steps)
        blk = lambda *idx, n_cast=n_cast: jnp.minimum(step_of(*idx), n_cast - 1)
        in_specs.append(pl.BlockSpec(
            (1, per, cols), lambda *idx, blk=blk, layer=layer: (layer, blk(*idx), 0)))
        args.append(w)
        if scale is not None:
            in_specs.append(_resident(scale.shape))
            args.append(scale)
        out_specs.append(pl.BlockSpec((per, cols), lambda *idx, blk=blk: (blk(*idx), 0)))
        out_shapes.append(jax.ShapeDtypeStruct((rows, cols), BF16))
        plan.append((n_cast, scale is not None))
    return in_specs, out_specs, out_shapes, args, tuple(plan)


def _run_casts(step, plan, in_refs, out_refs):
    in_refs = list(in_refs)
    for (n_cast, scaled), dst in zip(plan, out_refs):
        src = in_refs.pop(0)
        scale = in_refs.pop(0) if scaled else None

        @pl.when(step < n_cast)
        def _(src=src, scale=scale, dst=dst):
            v = src[0]
            if scale is not None:
                v = v * scale[...]
            dst[...] = v.astype(BF16)


def _n_cast_inputs(plan):
    return sum(2 if scaled else 1 for _, scaled in plan)


def _ada_kernel(c_ref, w_ref, b_ref, o_ref):
    ca = _silu(c_ref[...])
    o_ref[0] = jnp.dot(ca.astype(BF16), w_ref[0].astype(BF16),
                       preferred_element_type=F32) + b_ref[0]


def _ada_mod(c_pad, ada_w, ada_b):
    depth, d, n = ada_w.shape
    rows = c_pad.shape[0]
    tn = 1536
    return pl.pallas_call(
        _ada_kernel,
        grid=(depth, n // tn),
        in_specs=[
            pl.BlockSpec((rows, d), lambda i, j: (0, 0)),
            pl.BlockSpec((1, d, tn), lambda i, j: (i, 0, j)),
            pl.BlockSpec((1, 1, tn), lambda i, j: (i, 0, j)),
        ],
        out_specs=pl.BlockSpec((1, rows, tn), lambda i, j: (i, 0, j)),
        out_shape=jax.ShapeDtypeStruct((depth, rows, n), F32),
        compiler_params=_params(2),
        name="ada_mod",
    )(c_pad, ada_w, ada_b.reshape(depth, 1, n))


def _mod_spec(layer, slot, tiles_per_batch, n_batch_rows):
    base = layer * n_batch_rows * 6 + slot

    def index(t):
        return (base + (t // tiles_per_batch) * 6, 0, 0)
    return index


def _qkv_kernel(x_ref, g_ref, sc_ref, sh_ref, wq_ref, wk_ref, wvt_ref, q_ref, k_ref, vt_ref):
    h = _rms(x_ref[...], g_ref[...]) * (1.0 + sc_ref[0]) + sh_ref[0]
    hb = h.astype(BF16)
    q_ref[...] = jnp.dot(hb, wq_ref[...], preferred_element_type=F32).astype(BF16)
    k_ref[...] = jnp.dot(hb, wk_ref[...], preferred_element_type=F32).astype(BF16)
    vt_ref[0] = lax.dot_general(wvt_ref[...], hb, (((1,), (1,)), ((), ())),
                                preferred_element_type=F32).astype(BF16)


def _qkv_proj(x2d, g_pre, modv, w_q, w_k, w_vt, seq, n_batch_rows):
    t, d = x2d.shape
    nq = w_q.shape[1]
    nkv = w_k.shape[1]
    tm = ROW_TILE
    tpb = seq // tm
    vec = lambda idx: pl.BlockSpec((1, 1, d), idx)
    return pl.pallas_call(
        _qkv_kernel,
        grid=(t // tm,),
        in_specs=[
            pl.BlockSpec((tm, d), lambda i: (i, 0)),
            _resident((1, d)),
            vec(_mod_spec(0, 1, tpb, n_batch_rows)),
            vec(_mod_spec(0, 0, tpb, n_batch_rows)),
            _resident(w_q.shape), _resident(w_k.shape), _resident(w_vt.shape),
        ],
        out_specs=[
            pl.BlockSpec((tm, nq), lambda i: (i, 0)),
            pl.BlockSpec((tm, nkv), lambda i: (i, 0)),
            pl.BlockSpec((1, nkv, tm), lambda i: (i // tpb, 0, i % tpb)),
        ],
        out_shape=[
            jax.ShapeDtypeStruct((t, nq), BF16),
            jax.ShapeDtypeStruct((t, nkv), BF16),
            jax.ShapeDtypeStruct((t // seq, nkv, seq), BF16),
        ],
        compiler_params=_params(1),
        name="qkv_proj",
    )(x2d, g_pre, modv, modv, w_q, w_k, w_vt)


def _att_tables(rb_ref, b2_ref, m2_ref):
    L = ATT_BLOCK
    rows = 64
    for r in range(3 * L // rows):
        j = lax.broadcasted_iota(jnp.int32, (rows, L), 0) + r * rows
        t = lax.broadcasted_iota(jnp.int32, (rows, L), 1)
        rel = j - L - t
        n = jnp.abs(rel)
        large = jnp.full((rows, L), 8, jnp.int32)
        for thr in (12, 16, 23, 32, 46, 64, 91):
            large = large + (n >= thr).astype(jnp.int32)
        bucket = jnp.where(rel > 0, 16, 0) + jnp.where(n < 8, n, large)
        in_win = n <= L
        visible = (in_win & (j >= L), in_win, in_win & (j < 2 * L))
        piece = slice(r * rows, (r + 1) * rows)
        for kind in range(3):
            m2_ref[kind, piece, :] = jnp.where(visible[kind], LOG2E, 0.0).astype(F32)

        def body(hq, carry):
            acc = jnp.zeros((rows, L), F32)
            for b in range(REL_BUCKETS):
                acc = jnp.where(bucket == b, rb_ref[b, hq], acc)
            acc = acc * LOG2E
            for kind in range(3):
                b2_ref[kind, hq, piece, :] = jnp.where(visible[kind], acc, NEG)
            return carry
        lax.fori_loop(0, Q_HEADS, body, 0)


def _att_kernel(*refs, cast_plan):
    rb_ref, sink_ref, q_ref, kp_ref, kc_ref, kn_ref, vp_ref, vc_ref, vn_ref = refs[:9]
    n_cast_in = _n_cast_inputs(cast_plan)
    cast_in = refs[9:9 + n_cast_in]
    o_ref = refs[9 + n_cast_in]
    cast_out = refs[10 + n_cast_in:10 + n_cast_in + len(cast_plan)]
    b2_ref, m2_ref = refs[10 + n_cast_in + len(cast_plan):]
    b = pl.program_id(0)
    i = pl.program_id(1)
    ns = pl.num_programs(1)
    _run_casts(b * ns + i, cast_plan, cast_in, cast_out)

    @pl.when((b == 0) & (i == 0))
    def _():
        _att_tables(rb_ref, b2_ref, m2_ref)

    L = ATT_BLOCK
    dh = HEAD_DIM
    U = ATT_STEP_BLOCKS
    q = q_ref[0]
    kb = jnp.concatenate([kp_ref[0], kc_ref[0], kn_ref[0]], axis=0)
    vt = jnp.concatenate([vp_ref[0], vc_ref[0], vn_ref[0]], axis=1)
    lane = lax.broadcasted_iota(jnp.int32, (L, 2 * dh), 1)
    zero = jnp.zeros((L, 2 * dh), BF16)
    ones = jnp.ones((BF16_SUBLANES, 3 * L), BF16)
    kinds = [1] * U
    kinds[0] = jnp.where(i == 0, 0, 1)
    kinds[U - 1] = jnp.where(i == ns - 1, 2, 1)

    def scores(u, h):
        p, e = divmod(h, 2)
        k_pair = kb[u * L:(u + 3) * L, p * 2 * dh:(p + 1) * 2 * dh]
        mine = (lane < dh) if e == 0 else (lane >= dh)
        qz = jnp.concatenate(
            [jnp.where(mine, q[u * L:(u + 1) * L,
                               (p * GROUP + g) * 2 * dh:(p * GROUP + g + 1) * 2 * dh], zero)
             for g in range(GROUP)], axis=0)
        return lax.dot_general(k_pair, qz, (((1,), (1,)), ((), ())),
                               preferred_element_type=F32)

    units = [(u, h) for u in range(U) for h in range(KV_HEADS)]
    o_rows = []
    st_next = scores(*units[0])
    for idx, (u, h) in enumerate(units):
        st = st_next
        if idx + 1 < len(units):
            st_next = scores(*units[idx + 1])
        m2 = m2_ref[kinds[u]]
        es, sinks = [], []
        for g in range(GROUP):
            hq = h * GROUP + g
            sk2 = jnp.full((1, L), sink_ref[hq], F32) * LOG2E
            l2 = st[:, g * L:(g + 1) * L] * m2 + b2_ref[kinds[u], hq]
            m = jnp.maximum(jnp.max(l2, axis=0, keepdims=True), sk2)
            es.append(jnp.exp2(l2 - m).astype(BF16))
            sinks.append(jnp.exp2(sk2 - m))
        et = jnp.concatenate(es, axis=1)
        va = jnp.concatenate([vt[h * dh:(h + 1) * dh, u * L:(u + 3) * L], ones], axis=0)
        ot = jnp.dot(va, et, preferred_element_type=F32)
        den = ot[dh:dh + 1, :] + jnp.concatenate(sinks, axis=1)
        ot = ot[:dh, :] * (1.0 / den)
        o_rows += [ot[:, g * L:(g + 1) * L] for g in range(GROUP)]
        if h == KV_HEADS - 1:
            ot_all = jnp.concatenate(o_rows, axis=0)
            o_ref[0, u * L:(u + 1) * L, :] = ot_all.T.astype(BF16)
            o_rows = []


def _attention(q, k, vt, rel_bias, sink, cast_jobs):
    bsz, seq, dq = q.shape
    dkv = k.shape[2]
    L = ATT_BLOCK
    nb = seq // L
    U = ATT_STEP_BLOCKS
    ns = nb // U
    smem = pl.BlockSpec(memory_space=pltpu.SMEM)
    prev = lambda i: jnp.maximum(U * i - 1, 0)
    nxt = lambda i: jnp.minimum(U * i + U, nb - 1)
    c_in, c_out, c_shapes, c_args, plan = _cast_io(cast_jobs, bsz * ns, lambda b, i: b * ns + i)
    return pl.pallas_call(
        functools.partial(_att_kernel, cast_plan=plan),
        grid=(bsz, ns),
        in_specs=[
            smem, smem,
            pl.BlockSpec((1, U * L, dq), lambda b, i: (b, i, 0)),
            pl.BlockSpec((1, L, dkv), lambda b, i: (b, prev(i), 0)),
            pl.BlockSpec((1, U * L, dkv), lambda b, i: (b, i, 0)),
            pl.BlockSpec((1, L, dkv), lambda b, i: (b, nxt(i), 0)),
            pl.BlockSpec((1, dkv, L), lambda b, i: (b, 0, prev(i))),
            pl.BlockSpec((1, dkv, U * L), lambda b, i: (b, 0, i)),
            pl.BlockSpec((1, dkv, L), lambda b, i: (b, 0, nxt(i))),
        ] + c_in,
        out_specs=[pl.BlockSpec((1, U * L, dq), lambda b, i: (b, i, 0))] + c_out,
        out_shape=[jax.ShapeDtypeStruct((bsz, seq, dq), BF16)] + c_shapes,
        scratch_shapes=[
            pltpu.VMEM((3, Q_HEADS, 3 * L, L), F32),
            pltpu.VMEM((3, 3 * L, L), F32),
        ],
        compiler_params=_params(2),
        name="swa_attention",
    )(rel_bias, sink, q, k, k, k, vt, vt, vt, *c_args)


def _post_ffn_kernel(*refs, emit_next, cast_plan):
    (a_ref, x_ref, wo_ref, gpost_ref, g1_ref, gpre_ref, sc2_ref, sh2_ref,
     win_ref, wout_ref, gfpost_ref, g2_ref) = refs[:12]
    rest = list(refs[12:])
    if emit_next:
        gn_ref, scn_ref, shn_ref = rest[:3]
        rest = rest[3:]
    n_cast_in = _n_cast_inputs(cast_plan)
    cast_in, rest = rest[:n_cast_in], rest[n_cast_in:]
    xo_ref = rest.pop(0)
    if emit_next:
        ho_ref = rest.pop(0)
    cast_out = rest[:len(cast_plan)]
    hb_ref, x1_ref, acc_ref, y_ref = rest[len(cast_plan):]
    s = pl.program_id(0)
    n = pl.num_programs(0) - 2
    _run_casts(s, cast_plan, cast_in, cast_out)
    hidden = wout_ref.shape[0]

    def advance():
        hb_ref[1] = hb_ref[0]
        acc_ref[1] = acc_ref[0]
        x1_ref[2] = x1_ref[1]
        x1_ref[1] = x1_ref[0]

    n_chunks = hidden // FFN_CHUNK
    n_slices = 8
    rows_of = lambda r: slice(r * (x_ref.shape[0] // n_slices), (r + 1) * (x_ref.shape[0] // n_slices))

    def a_matmul():
        y_ref[...] = jnp.dot(a_ref[...], wo_ref[...], preferred_element_type=F32)

    def after(v, dep):
        if dep is None:
            return v
        zero = (pltpu.bitcast(dep, jnp.uint32) >> 16) >> 16
        return v * pltpu.bitcast(zero | jnp.uint32(0x3F800000), F32)

    def a_norm(r, dep=None):
        rows = rows_of(r)
        x1 = x_ref[rows, :] + g1_ref[0] * _rms(after(y_ref[rows, :], dep), gpost_ref[...])
        h = _rms(x1, gpre_ref[...]) * (1.0 + sc2_ref[0]) + sh2_ref[0]
        hb_ref[0, rows, :] = h.astype(BF16)
        x1_ref[0, rows, :] = x1

    def b_chunk(c, hb, acc):
        lo = c * FFN_CHUNK
        a = jnp.dot(hb, win_ref[:, lo:lo + FFN_CHUNK], preferred_element_type=F32)
        b = jnp.dot(hb, win_ref[:, hidden + lo:hidden + lo + FFN_CHUNK],
                    preferred_element_type=F32)
        act = (_silu(a) * b).astype(BF16)
        return acc + jnp.dot(act, wout_ref[lo:lo + FFN_CHUNK, :], preferred_element_type=F32)

    def c_slice(r, dep=None):
        rows = rows_of(r)
        x2 = x1_ref[2, rows, :] + g2_ref[0] * _rms(after(acc_ref[1, rows, :], dep),
                                                   gfpost_ref[...])
        xo_ref[rows, :] = x2
        if emit_next:
            hn = _rms(x2, gn_ref[...]) * (1.0 + scn_ref[0]) + shn_ref[0]
            ho_ref[rows, :] = hn.astype(BF16)

    @pl.when(s == 0)
    def _():
        x1_ref[...] = jnp.zeros(x1_ref.shape, F32)
        acc_ref[...] = jnp.zeros(acc_ref.shape, F32)
        a_matmul()
        for r in range(n_slices):
            a_norm(r)

    @pl.when((s >= 1) & (s <= n))
    def _():
        advance()
        a_matmul()
        hb = hb_ref[1]
        acc = jnp.zeros((hb.shape[0], wout_ref.shape[1]), F32)
        for c in range(n_chunks):
            acc = b_chunk(c, hb, acc)
            dep = acc[0:1, :]
            if c < n_slices:
                c_slice(c, dep)
            if c >= n_chunks - n_slices - 1 and c < n_chunks - 1:
                a_norm(c - (n_chunks - n_slices - 1), dep)
        acc_ref[0] = acc

    @pl.when(s == n + 1)
    def _():
        advance()
        for r in range(n_slices):
            c_slice(r)


def _post_ffn(a2d, x2d, w_o, g_post, g_pre, w_in, w_out, g_fpost, modv, layer,
              seq, n_batch_rows, g_next=None, cast_jobs=()):
    t, d = x2d.shape
    kin = a2d.shape[1]
    tm = ROW_TILE
    tpb = seq // tm
    n = t // tm
    emit_next = g_next is not None
    tile_a = lambda s: jnp.minimum(s, n - 1)
    tile_c = lambda s: jnp.clip(s - 2, 0, n - 1)

    def vec(lyr, slot, tile_of):
        idx = _mod_spec(lyr, slot, tpb, n_batch_rows)
        return pl.BlockSpec((1, 1, d), lambda s: idx(tile_of(s)))
    row = lambda width, tile_of: pl.BlockSpec((tm, width), lambda s: (tile_of(s), 0))
    in_specs = [
        row(kin, tile_a), row(d, tile_a), _resident(w_o.shape), _resident((1, d)),
        vec(layer, 2, tile_a), _resident((1, d)), vec(layer, 4, tile_a), vec(layer, 3, tile_a),
        _resident(w_in.shape), _resident(w_out.shape), _resident((1, d)), vec(layer, 5, tile_c),
    ]
    args = [a2d, x2d, w_o, g_post, modv, g_pre, modv, modv, w_in, w_out, g_fpost, modv]
    out_specs = [row(d, tile_c)]
    out_shape = [jax.ShapeDtypeStruct((t, d), F32)]
    if emit_next:
        in_specs += [_resident((1, d)), vec(layer + 1, 1, tile_c), vec(layer + 1, 0, tile_c)]
        args += [g_next, modv, modv]
        out_specs.append(row(d, tile_c))
        out_shape.append(jax.ShapeDtypeStruct((t, d), BF16))
    c_in, c_out, c_shapes, c_args, plan = _cast_io(cast_jobs, n, lambda s: s)
    return pl.pallas_call(
        functools.partial(_post_ffn_kernel, emit_next=emit_next, cast_plan=plan),
        grid=(n + 2,),
        in_specs=in_specs + c_in,
        out_specs=out_specs + c_out,
        out_shape=out_shape + c_shapes,
        scratch_shapes=[pltpu.VMEM((2, tm, d), BF16), pltpu.VMEM((3, tm, d), F32),
                        pltpu.VMEM((2, tm, d), F32), pltpu.VMEM((tm, d), F32)],
        compiler_params=_params(1),
        name="post_ffn_next" if emit_next else "post_ffn",
    )(*args, *c_args)


def _ret_in_kernel(h_ref, w_ref, cos_ref, sin_ref, zf_ref, zb_ref,
                   q_ref, k_ref, kzf_ref, kzb_ref, v_ref, sg_ref):
    hb = h_ref[...]
    tm = hb.shape[0]
    cos = cos_ref[...]
    sin = sin_ref[...]
    dk = w_ref.shape[0] // RET_HEADS
    half = dk // 2
    nqk = RET_HEADS * dk
    nv = (w_ref.shape[1] - 2 * nqk) // 2

    def rotated(y, lo, off):
        other = lo - off + (half - off)
        a = y[:, lo:lo + half] * cos
        b = y[:, other:other + half] * sin
        return a - b if off == 0 else a + b

    step = nqk
    for c0 in range(0, nv, step):
        g = jnp.dot(hb, w_ref[:, 2 * nqk + nv + c0:2 * nqk + nv + c0 + step],
                    preferred_element_type=F32)
        sg_ref[:, c0:c0 + step] = _silu(g)
    yk = jnp.dot(hb, w_ref[:, nqk:2 * nqk], preferred_element_type=F32)
    zf = zf_ref[...][None]
    zb = zb_ref[...][None]
    for h in range(RET_HEADS):
        for off in (0, half):
            lo = h * dk + off
            rk = rotated(yk, lo, off)
            k_ref[:, lo:lo + half] = rk.astype(BF16)
            rk3 = rk.reshape(tm // RET_CHUNK, RET_CHUNK, half)
            kzf_ref[:, lo:lo + half] = (rk3 * zf[:, :, lo:lo + half]).reshape(tm, half).astype(BF16)
            kzb_ref[:, lo:lo + half] = (rk3 * zb[:, :, lo:lo + half]).reshape(tm, half).astype(BF16)
    yq = jnp.dot(hb, w_ref[:, 0:nqk], preferred_element_type=F32)
    for h in range(RET_HEADS):
        for off in (0, half):
            lo = h * dk + off
            q_ref[:, lo:lo + half] = rotated(yq, lo, off).astype(BF16)
    for c0 in range(0, nv, step):
        v_ref[:, c0:c0 + step] = jnp.dot(hb, w_ref[:, 2 * nqk + c0:2 * nqk + c0 + step],
                                         preferred_element_type=F32).astype(BF16)


def _ret_in(h2d, w_in, cos, sin, zf, zb, seq):
    t, d = h2d.shape
    tm = ROW_TILE
    tpb = seq // tm
    nqk = RET_HEADS * (d // RET_HEADS)
    nv = (w_in.shape[1] - 2 * nqk) // 2
    row = lambda width: pl.BlockSpec((tm, width), lambda i: (i, 0))
    pos = pl.BlockSpec((tm, cos.shape[1]), lambda i: (i % tpb, 0))
    return pl.pallas_call(
        _ret_in_kernel,
        grid=(t // tm,),
        in_specs=[row(d), _resident(w_in.shape), pos, pos,
                  _resident(zf.shape), _resident(zb.shape)],
        out_specs=[row(nqk), row(nqk), row(nqk), row(nqk), row(nv), row(nv)],
        out_shape=[jax.ShapeDtypeStruct((t, nqk), BF16)] * 4
        + [jax.ShapeDtypeStruct((t, nv), BF16), jax.ShapeDtypeStruct((t, nv), F32)],
        compiler_params=_params(1),
        name="ret_in_proj",
    )(h2d, w_in, cos, sin, zf, zb)


def _state_update(st_ref, h, kz_h, v_h, cd):
    upd = lax.dot_general(kz_h, v_h, (((0,), (0,)), ((), ())), preferred_element_type=F32)
    st_ref[h] = st_ref[h] * cd + upd


def _ret_scan_kernel(cdf_ref, cdb_ref, q_ref, k_ref, kzf_ref, kzb_ref, v_ref, sg_ref,
                     xif_ref, xib_ref, dm_ref, o_ref, st_ref, snap_ref):
    j = pl.program_id(1)
    ns = pl.num_programs(1) // 2
    L = RET_CHUNK
    n_sub = RET_STEP_CHUNKS

    @pl.when((j == 0) | (j == ns))
    def _():
        st_ref[...] = jnp.zeros(st_ref.shape, F32)

    dk = q_ref.shape[2] // RET_HEADS
    dv = v_ref.shape[2] // RET_HEADS

    @pl.when(j < ns)
    def _():
        for ci in reversed(range(n_sub)):
            rows = slice(ci * L, (ci + 1) * L)
            c = (ns - 1 - j) * n_sub + ci
            for h in range(RET_HEADS):
                snap_ref[c, h] = st_ref[h].astype(BF16)
                _state_update(st_ref, h, kzb_ref[0, rows, h * dk:(h + 1) * dk],
                              v_ref[0, rows, h * dv:(h + 1) * dv], cdb_ref[h])

    @pl.when(j >= ns)
    def _():
        def scores(ci, h):
            rows = slice(ci * L, (ci + 1) * L)
            return lax.dot_general(q_ref[0, rows, h * dk:(h + 1) * dk],
                                   k_ref[0, rows, h * dk:(h + 1) * dk],
                                   (((1,), (1,)), ((), ())), preferred_element_type=F32)

        def finish(ci, h, s):
            rows = slice(ci * L, (ci + 1) * L)
            c = (j - ns) * n_sub + ci
            q_h = q_ref[0, rows, h * dk:(h + 1) * dk]
            v_h = v_ref[0, rows, h * dv:(h + 1) * dv]
            y = jnp.dot((s * dm_ref[h]).astype(BF16), v_h, preferred_element_type=F32)
            y = y + jnp.dot(q_h, st_ref[h].astype(BF16), preferred_element_type=F32) * xif_ref[h]
            y = y + jnp.dot(q_h, snap_ref[c, h], preferred_element_type=F32) * xib_ref[h]
            _state_update(st_ref, h, kzf_ref[0, rows, h * dk:(h + 1) * dk], v_h, cdf_ref[h])
            yn = y * lax.rsqrt(jnp.mean(y * y, axis=-1, keepdims=True) + EPS)
            o_ref[0, rows, h * dv:(h + 1) * dv] = (
                sg_ref[0, rows, h * dv:(h + 1) * dv] * yn).astype(BF16)

        units = [(ci, h) for ci in range(n_sub) for h in range(RET_HEADS)]
        ahead = scores(*units[0])
        for idx, (ci, h) in enumerate(units):
            s = ahead
            if idx + 1 < len(units):
                ahead = scores(*units[idx + 1])
            finish(ci, h, s)


def _retention(q, k, kzf, kzb, v, sg, xi_f, xi_b, dmat, cd_f, cd_b):
    bsz, seq, nqk = q.shape
    nv = v.shape[2]
    L = RET_CHUNK
    rows = RET_STEP_CHUNKS * L
    ns = seq // rows
    dk = nqk // RET_HEADS
    dv = nv // RET_HEADS
    smem = pl.BlockSpec(memory_space=pltpu.SMEM)
    both = lambda j: jnp.where(j < ns, ns - 1 - j, j - ns)
    bwd_only = lambda j: jnp.where(j < ns, ns - 1 - j, 0)
    fwd_only = lambda j: jnp.where(j < ns, 0, j - ns)
    blk = lambda width, at: pl.BlockSpec((1, rows, width), lambda b, j: (b, at(j), 0))
    return pl.pallas_call(
        _ret_scan_kernel,
        grid=(bsz, 2 * ns),
        in_specs=[smem, smem, blk(nqk, fwd_only), blk(nqk, fwd_only), blk(nqk, fwd_only),
                  blk(nqk, bwd_only), blk(nv, both), blk(nv, fwd_only),
                  _resident(xi_f.shape), _resident(xi_b.shape), _resident(dmat.shape)],
        out_specs=blk(nv, fwd_only),
        out_shape=jax.ShapeDtypeStruct((bsz, seq, nv), BF16),
        scratch_shapes=[pltpu.VMEM((RET_HEADS, dk, dv), F32),
                        pltpu.VMEM((seq // L, RET_HEADS, dk, dv), BF16)],
        compiler_params=_params(2),
        name="ret_scan",
    )(cd_f, cd_b, q, k, kzf, kzb, v, sg, xi_f, xi_b, dmat)


def _decay_tables(decay_fwd, decay_bwd, dk):
    L = RET_CHUNK
    lg_f = jax.nn.log_sigmoid(decay_fwd.astype(F32))
    lg_b = jax.nn.log_sigmoid(decay_bwd.astype(F32))
    idx = jnp.arange(L, dtype=F32)
    diff = idx[:, None] - idx[None, :]
    dm = jnp.where((diff >= 0)[None],
                   jnp.exp(lg_f[:, None, None] * jnp.maximum(diff, 0.0)[None]),
                   jnp.exp(lg_b[:, None, None] * jnp.maximum(-diff, 0.0)[None]))
    xi_f = jnp.exp(lg_f[:, None] * (idx + 1.0)[None])[:, :, None]
    xi_b = jnp.exp(lg_b[:, None] * (L - idx)[None])[:, :, None]
    zeta_f = jnp.exp(lg_f[:, None] * (L - 1.0 - idx)[None])
    zeta_b = jnp.exp(lg_b[:, None] * idx[None])
    zf = jnp.repeat(zeta_f.T, dk, axis=1)
    zb = jnp.repeat(zeta_b.T, dk, axis=1)
    return dm, xi_f, xi_b, zf, zb, jnp.exp(lg_f * L), jnp.exp(lg_b * L)


def _rope_tables(seq, dk):
    inv = ROPE_BASE ** (-jnp.arange(0, dk, 2, dtype=F32) / dk)
    ang = jnp.arange(seq, dtype=F32)[:, None] * inv[None]
    return jnp.cos(ang), jnp.sin(ang)


def kernel(x, c, rel_bias, att_w_qkv, att_w_o, att_sink, ret_w_in, ret_w_o, ret_decay_fwd,
           ret_decay_bwd, ada_w, ada_b, mix_norm_pre, mix_norm_post, ffn_norm_pre,
           ffn_norm_post, ffn_w_in, ffn_w_out):
    bsz, seq, d = x.shape
    t = bsz * seq
    assert seq % ROW_TILE == 0 and ROW_TILE % RET_CHUNK == 0 and d % RET_HEADS == 0
    assert ada_w.shape[0] == 2 and ffn_w_out.shape[1] % FFN_CHUNK == 0
    rows = -(-bsz // 8) * 8
    vec = lambda a: a.reshape(1, d)

    c_pad = jnp.pad(c, ((0, rows - bsz), (0, 0)))
    mod = _ada_mod(c_pad, ada_w, ada_b)
    modv = mod.reshape(2 * rows * 6, 1, d)

    x2d = x.reshape(t, d)

    nq = Q_HEADS * HEAD_DIM
    nkv = KV_HEADS * HEAD_DIM
    heads = [(2 * p + e) * GROUP + g for p in range(KV_HEADS // 2) for g in range(GROUP)
             for e in range(2)]
    w_qkv = att_w_qkv[0].astype(BF16)
    w_q = w_qkv[:, :nq].reshape(d, Q_HEADS, HEAD_DIM)[:, jnp.array(heads)].reshape(d, nq)
    w_q = w_q * jnp.asarray(HEAD_DIM ** -0.5, BF16)
    w_k = w_qkv[:, nq:nq + nkv]
    w_vt = w_qkv[:, nq + nkv:].T
    q, k, vt = _qkv_proj(x2d, vec(mix_norm_pre[0]), modv, w_q, w_k, w_vt, seq, rows)
    att, w_ao, w_f0in, w_f0out = _attention(
        q.reshape(bsz, seq, nq), k.reshape(bsz, seq, nkv), vt, rel_bias.astype(F32),
        att_sink[0].astype(F32), [(att_w_o, 0, None), (ffn_w_in, 0, None), (ffn_w_out, 0, None)])

    dk = d // RET_HEADS
    nqk = RET_HEADS * dk
    kscale = jnp.concatenate([jnp.ones((nqk,), F32), jnp.full((nqk,), dk ** -0.5, F32),
                              jnp.ones((ret_w_in.shape[2] - 2 * nqk,), F32)])[None]
    x1, h1, w_rin, w_ro, w_f1in, w_f1out = _post_ffn(
        att.reshape(t, nq), x2d, w_ao, vec(mix_norm_post[0]), vec(ffn_norm_pre[0]), w_f0in,
        w_f0out, vec(ffn_norm_post[0]), modv, 0, seq, rows, g_next=vec(mix_norm_pre[1]),
        cast_jobs=[(ret_w_in, 0, kscale), (ret_w_o, 0, None), (ffn_w_in, 1, None),
                   (ffn_w_out, 1, None)])
    dm, xi_f, xi_b, zf, zb, cd_f, cd_b = _decay_tables(ret_decay_fwd[0], ret_decay_bwd[0], dk)
    cos, sin = _rope_tables(seq, dk)
    q, k, kzf, kzb, v, sg = _ret_in(h1, w_rin, cos, sin, zf, zb, seq)
    r3 = lambda a: a.reshape(bsz, seq, a.shape[1])
    gated = _retention(r3(q), r3(k), r3(kzf), r3(kzb), r3(v), r3(sg), xi_f, xi_b, dm, cd_f, cd_b)
    (x2,) = _post_ffn(gated.reshape(t, -1), x1, w_ro, vec(mix_norm_post[1]),
                      vec(ffn_norm_pre[1]), w_f1in, w_f1out, vec(ffn_norm_post[1]), modv, 1,
                      seq, rows)
    return x2.reshape(bsz, seq, d)
```

```python
import functools
import math

import jax
import jax.numpy as jnp
from jax import lax
from jax.experimental import pallas as pl
from jax.experimental.pallas import tpu as pltpu

F32 = jnp.float32
BF16 = jnp.bfloat16

EPS = 1e-6
NEG = -1e30
LOG2E = math.log2(math.e)

Q_HEADS = 16
KV_HEADS = 4
GROUP = Q_HEADS // KV_HEADS
HEAD_DIM = 64
ATT_BLOCK = 128
ATT_STEP_BLOCKS = 4
REL_BUCKETS = 32
RET_HEADS = 4
RET_CHUNK = 256
RET_STEP_CHUNKS = 2
ROPE_BASE = 10000.0
FFN_CHUNK = 256
ROW_TILE = 512
VMEM_LIMIT = 56 * 1024 * 1024


def _silu(x):
    return x * (1.0 / (1.0 + jnp.exp(-x)))


def _rms(xf, g):
    ms = jnp.mean(xf * xf, axis=-1, keepdims=True)
    return (xf * lax.rsqrt(ms + EPS)) * g


def _resident(shape):
    zeros = (0,) * len(shape)
    return pl.BlockSpec(shape, lambda *_: zeros, pipeline_mode=pl.Buffered(1))


def _params(n_axes, vmem=VMEM_LIMIT):
    return pltpu.CompilerParams(
        dimension_semantics=("arbitrary",) * n_axes, vmem_limit_bytes=vmem)


BF16_SUBLANES = 16


def _cast_plan(rows, n_steps):
    per = BF16_SUBLANES
    while rows % per or rows // per > n_steps:
        per += BF16_SUBLANES
    return per, rows // per


def _cast_io(jobs, n_steps, step_of):
    in_specs, out_specs, out_shapes, args, plan = [], [], [], [], []
    for w, layer, scale in jobs:
        _, rows, cols = w.shape
        per, n_cast = _cast_plan(rows, n_steps)
        blk = lambda *idx, n_cast=n_cast: jnp.minimum(step_of(*idx), n_cast - 1)
        in_specs.append(pl.BlockSpec(
            (1, per, cols), lambda *idx, blk=blk, layer=layer: (layer, blk(*idx), 0)))
        args.append(w)
        if scale is not None:
            in_specs.append(_resident(scale.shape))
            args.append(scale)
        out_specs.append(pl.BlockSpec((per, cols), lambda *idx, blk=blk: (blk(*idx), 0)))
        out_shapes.append(jax.ShapeDtypeStruct((rows, cols), BF16))
        plan.append((n_cast, scale is not None))
    return in_specs, out_specs, out_shapes, args, tuple(plan)


def _run_casts(step, plan, in_refs, out_refs):
    in_refs = list(in_refs)
    for (n_cast, scaled), dst in zip(plan, out_refs):
        src = in_refs.pop(0)
        scale = in_refs.pop(0) if scaled else None

        @pl.when(step < n_cast)
        def _(src=src, scale=scale, dst=dst):
            v = src[0]
            if scale is not None:
                v = v * scale[...]
            dst[...] = v.astype(BF16)


def _n_cast_inputs(plan):
    return sum(2 if scaled else 1 for _, scaled in plan)


def _ada_kernel(c_ref, w_ref, b_ref, o_ref):
    ca = _silu(c_ref[...])
    o_ref[0] = jnp.dot(ca.astype(BF16), w_ref[0].astype(BF16),
                       preferred_element_type=F32) + b_ref[0]


def _ada_mod(c_pad, ada_w, ada_b):
    depth, d, n = ada_w.shape
    rows = c_pad.shape[0]
    tn = 1536
    return pl.pallas_call(
        _ada_kernel,
        grid=(depth, n // tn),
        in_specs=[
            pl.BlockSpec((rows, d), lambda i, j: (0, 0)),
            pl.BlockSpec((1, d, tn), lambda i, j: (i, 0, j)),
            pl.BlockSpec((1, 1, tn), lambda i, j: (i, 0, j)),
        ],
        out_specs=pl.BlockSpec((1, rows, tn), lambda i, j: (i, 0, j)),
        out_shape=jax.ShapeDtypeStruct((depth, rows, n), F32),
        compiler_params=_params(2),
        name="ada_mod",
    )(c_pad, ada_w, ada_b.reshape(depth, 1, n))


def _mod_spec(layer, slot, tiles_per_batch, n_batch_rows):
    base = layer * n_batch_rows * 6 + slot

    def index(t):
        return (base + (t // tiles_per_batch) * 6, 0, 0)
    return index


def _qkv_kernel(x_ref, g_ref, sc_ref, sh_ref, wq_ref, wk_ref, wvt_ref, q_ref, k_ref, vt_ref):
    h = _rms(x_ref[...], g_ref[...]) * (1.0 + sc_ref[0]) + sh_ref[0]
    hb = h.astype(BF16)
    q_ref[...] = jnp.dot(hb, wq_ref[...], preferred_element_type=F32).astype(BF16)
    k_ref[...] = jnp.dot(hb, wk_ref[...], preferred_element_type=F32).astype(BF16)
    vt_ref[0] = lax.dot_general(wvt_ref[...], hb, (((1,), (1,)), ((), ())),
                                preferred_element_type=F32).astype(BF16)


def _qkv_proj(x2d, g_pre, modv, w_q, w_k, w_vt, seq, n_batch_rows):
    t, d = x2d.shape
    nq = w_q.shape[1]
    nkv = w_k.shape[1]
    tm = ROW_TILE
    tpb = seq // tm
    vec = lambda idx: pl.BlockSpec((1, 1, d), idx)
    return pl.pallas_call(
        _qkv_kernel,
        grid=(t // tm,),
        in_specs=[
            pl.BlockSpec((tm, d), lambda i: (i, 0)),
            _resident((1, d)),
            vec(_mod_spec(0, 1, tpb, n_batch_rows)),
            vec(_mod_spec(0, 0, tpb, n_batch_rows)),
            _resident(w_q.shape), _resident(w_k.shape), _resident(w_vt.shape),
        ],
        out_specs=[
            pl.BlockSpec((tm, nq), lambda i: (i, 0)),
            pl.BlockSpec((tm, nkv), lambda i: (i, 0)),
            pl.BlockSpec((1, nkv, tm), lambda i: (i // tpb, 0, i % tpb)),
        ],
        out_shape=[
            jax.ShapeDtypeStruct((t, nq), BF16),
            jax.ShapeDtypeStruct((t, nkv), BF16),
            jax.ShapeDtypeStruct((t // seq, nkv, seq), BF16),
        ],
        compiler_params=_params(1),
        name="qkv_proj",
    )(x2d, g_pre, modv, modv, w_q, w_k, w_vt)


def _att_tables(rb_ref, b2_ref, m2_ref):
    L = ATT_BLOCK
    rows = 64
    for r in range(3 * L // rows):
        j = lax.broadcasted_iota(jnp.int32, (rows, L), 0) + r * rows
        t = lax.broadcasted_iota(jnp.int32, (rows, L), 1)
        rel = j - L - t
        n = jnp.abs(rel)
        large = jnp.full((rows, L), 8, jnp.int32)
        for thr in (12, 16, 23, 32, 46, 64, 91):
            large = large + (n >= thr).astype(jnp.int32)
        bucket = jnp.where(rel > 0, 16, 0) + jnp.where(n < 8, n, large)
        in_win = n <= L
        visible = (in_win & (j >= L), in_win, in_win & (j < 2 * L))
        piece = slice(r * rows, (r + 1) * rows)
        for kind in range(3):
            m2_ref[kind, piece, :] = jnp.where(visible[kind], LOG2E, 0.0).astype(F32)

        def body(hq, carry):
            acc = jnp.zeros((rows, L), F32)
            for b in range(REL_BUCKETS):
                acc = jnp.where(bucket == b, rb_ref[b, hq], acc)
            acc = acc * LOG2E
            for kind in range(3):
                b2_ref[kind, hq, piece, :] = jnp.where(visible[kind], acc, NEG)
            return carry
        lax.fori_loop(0, Q_HEADS, body, 0)


def _att_kernel(*refs, cast_plan):
    rb_ref, sink_ref, q_ref, kp_ref, kc_ref, kn_ref, vp_ref, vc_ref, vn_ref = refs[:9]
    n_cast_in = _n_cast_inputs(cast_plan)
    cast_in = refs[9:9 + n_cast_in]
    o_ref = refs[9 + n_cast_in]
    cast_out = refs[10 + n_cast_in:10 + n_cast_in + len(cast_plan)]
    b2_ref, m2_ref = refs[10 + n_cast_in + len(cast_plan):]
    b = pl.program_id(0)
    i = pl.program_id(1)
    ns = pl.num_programs(1)
    _run_casts(b * ns + i, cast_plan, cast_in, cast_out)

    @pl.when((b == 0) & (i == 0))
    def _():
        _att_tables(rb_ref, b2_ref, m2_ref)

    L = ATT_BLOCK
    dh = HEAD_DIM
    U = ATT_STEP_BLOCKS
    q = q_ref[0]
    kb = jnp.concatenate([kp_ref[0], kc_ref[0], kn_ref[0]], axis=0)
    vt = jnp.concatenate([vp_ref[0], vc_ref[0], vn_ref[0]], axis=1)
    lane = lax.broadcasted_iota(jnp.int32, (L, 2 * dh), 1)
    zero = jnp.zeros((L, 2 * dh), BF16)
    ones = jnp.ones((BF16_SUBLANES, 3 * L), BF16)
    kinds = [1] * U
    kinds[0] = jnp.where(i == 0, 0, 1)
    kinds[U - 1] = jnp.where(i == ns - 1, 2, 1)

    def scores(u, h):
        p, e = divmod(h, 2)
        k_pair = kb[u * L:(u + 3) * L, p * 2 * dh:(p + 1) * 2 * dh]
        mine = (lane < dh) if e == 0 else (lane >= dh)
        qz = jnp.concatenate(
            [jnp.where(mine, q[u * L:(u + 1) * L,
                               (p * GROUP + g) * 2 * dh:(p * GROUP + g + 1) * 2 * dh], zero)
             for g in range(GROUP)], axis=0)
        return lax.dot_general(k_pair, qz, (((1,), (1,)), ((), ())),
                               preferred_element_type=F32)

    units = [(u, h) for u in range(U) for h in range(KV_HEADS)]
    o_rows = []
    st_next = scores(*units[0])
    for idx, (u, h) in enumerate(units):
        st = st_next
        if idx + 1 < len(units):
            st_next = scores(*units[idx + 1])
        m2 = m2_ref[kinds[u]]
        es, sinks = [], []
        for g in range(GROUP):
            hq = h * GROUP + g
            sk2 = jnp.full((1, L), sink_ref[hq], F32) * LOG2E
            l2 = st[:, g * L:(g + 1) * L] * m2 + b2_ref[kinds[u], hq]
            m = jnp.maximum(jnp.max(l2, axis=0, keepdims=True), sk2)
            es.append(jnp.exp2(l2 - m).astype(BF16))
            sinks.append(jnp.exp2(sk2 - m))
        et = jnp.concatenate(es, axis=1)
        va = jnp.concatenate([vt[h * dh:(h + 1) * dh, u * L:(u + 3) * L], ones], axis=0)
        ot = jnp.dot(va, et, preferred_element_type=F32)
        den = ot[dh:dh + 1, :] + jnp.concatenate(sinks, axis=1)
        ot = ot[:dh, :] * (1.0 / den)
        o_rows += [ot[:, g * L:(g + 1) * L] for g in range(GROUP)]
        if h == KV_HEADS - 1:
            ot_all = jnp.concatenate(o_rows, axis=0)
            o_ref[0, u * L:(u + 1) * L, :] = ot_all.T.astype(BF16)
            o_rows = []


def _attention(q, k, vt, rel_bias, sink, cast_jobs):
    bsz, seq, dq = q.shape
    dkv = k.shape[2]
    L = ATT_BLOCK
    nb = seq // L
    U = ATT_STEP_BLOCKS
    ns = nb // U
    smem = pl.BlockSpec(memory_space=pltpu.SMEM)
    prev = lambda i: jnp.maximum(U * i - 1, 0)
    nxt = lambda i: jnp.minimum(U * i + U, nb - 1)
    c_in, c_out, c_shapes, c_args, plan = _cast_io(cast_jobs, bsz * ns, lambda b, i: b * ns + i)
    return pl.pallas_call(
        functools.partial(_att_kernel, cast_plan=plan),
        grid=(bsz, ns),
        in_specs=[
            smem, smem,
            pl.BlockSpec((1, U * L, dq), lambda b, i: (b, i, 0)),
            pl.BlockSpec((1, L, dkv), lambda b, i: (b, prev(i), 0)),
            pl.BlockSpec((1, U * L, dkv), lambda b, i: (b, i, 0)),
            pl.BlockSpec((1, L, dkv), lambda b, i: (b, nxt(i), 0)),
            pl.BlockSpec((1, dkv, L), lambda b, i: (b, 0, prev(i))),
            pl.BlockSpec((1, dkv, U * L), lambda b, i: (b, 0, i)),
            pl.BlockSpec((1, dkv, L), lambda b, i: (b, 0, nxt(i))),
        ] + c_in,
        out_specs=[pl.BlockSpec((1, U * L, dq), lambda b, i: (b, i, 0))] + c_out,
        out_shape=[jax.ShapeDtypeStruct((bsz, seq, dq), BF16)] + c_shapes,
        scratch_shapes=[
            pltpu.VMEM((3, Q_HEADS, 3 * L, L), F32),
            pltpu.VMEM((3, 3 * L, L), F32),
        ],
        compiler_params=_params(2),
        name="swa_attention",
    )(rel_bias, sink, q, k, k, k, vt, vt, vt, *c_args)


def _post_ffn_kernel(*refs, emit_next, cast_plan):
    (a_ref, x_ref, wo_ref, gpost_ref, g1_ref, gpre_ref, sc2_ref, sh2_ref,
     win_ref, wout_ref, gfpost_ref, g2_ref) = refs[:12]
    rest = list(refs[12:])
    if emit_next:
        gn_ref, scn_ref, shn_ref = rest[:3]
        rest = rest[3:]
    n_cast_in = _n_cast_inputs(cast_plan)
    cast_in, rest = rest[:n_cast_in], rest[n_cast_in:]
    xo_ref = rest.pop(0)
    if emit_next:
        ho_ref = rest.pop(0)
    cast_out = rest[:len(cast_plan)]
    hb_ref, x1_ref, acc_ref, y_ref, act_ref = rest[len(cast_plan):]
    s = pl.program_id(0)
    n = pl.num_programs(0) - 2
    _run_casts(s, cast_plan, cast_in, cast_out)
    hidden = wout_ref.shape[0]

    def advance():
        hb_ref[1] = hb_ref[0]
        x1_ref[2] = x1_ref[1]
        x1_ref[1] = x1_ref[0]

    n_chunks = hidden // FFN_CHUNK
    n_slices = 8
    rows_of = lambda r: slice(r * (x_ref.shape[0] // n_slices), (r + 1) * (x_ref.shape[0] // n_slices))

    def a_matmul():
        y_ref[...] = jnp.dot(a_ref[...], wo_ref[...], preferred_element_type=F32)

    def after(v, dep):
        if dep is None:
            return v
        zero = (pltpu.bitcast(dep, jnp.uint32) >> 16) >> 16
        return v * pltpu.bitcast(zero | jnp.uint32(0x3F800000), F32)

    def a_norm(r, dep=None):
        rows = rows_of(r)
        x1 = x_ref[rows, :] + g1_ref[0] * _rms(after(y_ref[rows, :], dep), gpost_ref[...])
        h = _rms(x1, gpre_ref[...]) * (1.0 + sc2_ref[0]) + sh2_ref[0]
        hb_ref[0, rows, :] = h.astype(BF16)
        x1_ref[0, rows, :] = x1

    def b_chunk(c, hb):
        lo = c * FFN_CHUNK
        a = jnp.dot(hb, win_ref[:, lo:lo + FFN_CHUNK], preferred_element_type=F32)
        b = jnp.dot(hb, win_ref[:, hidden + lo:hidden + lo + FFN_CHUNK],
                    preferred_element_type=F32)
        act = _silu(a) * b
        act_ref[:, lo:lo + FFN_CHUNK] = act.astype(BF16)
        return jnp.concatenate([act[0:1, :]] * (wout_ref.shape[1] // FFN_CHUNK), axis=1)

    def c_slice(r, dep=None):
        rows = rows_of(r)
        x2 = x1_ref[2, rows, :] + g2_ref[0] * _rms(after(acc_ref[rows, :], dep),
                                                   gfpost_ref[...])
        xo_ref[rows, :] = x2
        if emit_next:
            hn = _rms(x2, gn_ref[...]) * (1.0 + scn_ref[0]) + shn_ref[0]
            ho_ref[rows, :] = hn.astype(BF16)

    @pl.when(s == 0)
    def _():
        x1_ref[...] = jnp.zeros(x1_ref.shape, F32)
        acc_ref[...] = jnp.zeros(acc_ref.shape, F32)
        a_matmul()
        for r in range(n_slices):
            a_norm(r)

    @pl.when((s >= 1) & (s <= n))
    def _():
        advance()
        a_matmul()
        hb = hb_ref[1]
        for c in range(n_chunks):
            dep = b_chunk(c, hb)
            if c < n_slices:
                c_slice(c, dep)
            if c >= n_chunks - n_slices - 1 and c < n_chunks - 1:
                a_norm(c - (n_chunks - n_slices - 1), dep)
        acc_ref[...] = jnp.dot(act_ref[...], wout_ref[...], preferred_element_type=F32)

    @pl.when(s == n + 1)
    def _():
        advance()
        for r in range(n_slices):
            c_slice(r)


def _post_ffn(a2d, x2d, w_o, g_post, g_pre, w_in, w_out, g_fpost, modv, layer,
              seq, n_batch_rows, g_next=None, cast_jobs=()):
    t, d = x2d.shape
    kin = a2d.shape[1]
    tm = ROW_TILE
    tpb = seq // tm
    n = t // tm
    emit_next = g_next is not None
    tile_a = lambda s: jnp.minimum(s, n - 1)
    tile_c = lambda s: jnp.clip(s - 2, 0, n - 1)

    def vec(lyr, slot, tile_of):
        idx = _mod_spec(lyr, slot, tpb, n_batch_rows)
        return pl.BlockSpec((1, 1, d), lambda s: idx(tile_of(s)))
    row = lambda width, tile_of: pl.BlockSpec((tm, width), lambda s: (tile_of(s), 0))
    in_specs = [
        row(kin, tile_a), row(d, tile_a), _resident(w_o.shape), _resident((1, d)),
        vec(layer, 2, tile_a), _resident((1, d)), vec(layer, 4, tile_a), vec(layer, 3, tile_a),
        _resident(w_in.shape), _resident(w_out.shape), _resident((1, d)), vec(layer, 5, tile_c),
    ]
    args = [a2d, x2d, w_o, g_post, modv, g_pre, modv, modv, w_in, w_out, g_fpost, modv]
    out_specs = [row(d, tile_c)]
    out_shape = [jax.ShapeDtypeStruct((t, d), F32)]
    if emit_next:
        in_specs += [_resident((1, d)), vec(layer + 1, 1, tile_c), vec(layer + 1, 0, tile_c)]
        args += [g_next, modv, modv]
        out_specs.append(row(d, tile_c))
        out_shape.append(jax.ShapeDtypeStruct((t, d), BF16))
    c_in, c_out, c_shapes, c_args, plan = _cast_io(cast_jobs, n, lambda s: s)
    return pl.pallas_call(
        functools.partial(_post_ffn_kernel, emit_next=emit_next, cast_plan=plan),
        grid=(n + 2,),
        in_specs=in_specs + c_in,
        out_specs=out_specs + c_out,
        out_shape=out_shape + c_shapes,
        scratch_shapes=[pltpu.VMEM((2, tm, d), BF16), pltpu.VMEM((3, tm, d), F32),
                        pltpu.VMEM((tm, d), F32), pltpu.VMEM((tm, d), F32),
                        pltpu.VMEM((tm, w_out.shape[0]), BF16)],
        compiler_params=_params(1),
        name="post_ffn_next" if emit_next else "post_ffn",
    )(*args, *c_args)


def _ret_in_kernel(h_ref, w_ref, cos_ref, sin_ref, zf_ref, zb_ref,
                   q_ref, k_ref, kzf_ref, kzb_ref, v_ref, sg_ref):
    hb = h_ref[...]
    tm = hb.shape[0]
    cos = cos_ref[...]
    sin = sin_ref[...]
    dk = w_ref.shape[0] // RET_HEADS
    half = dk // 2
    nqk = RET_HEADS * dk
    nv = (w_ref.shape[1] - 2 * nqk) // 2

    def rotated(y, lo, off):
        other = lo - off + (half - off)
        a = y[:, lo:lo + half] * cos
        b = y[:, other:other + half] * sin
        return a - b if off == 0 else a + b

    step = nqk
    for c0 in range(0, nv, step):
        g = jnp.dot(hb, w_ref[:, 2 * nqk + nv + c0:2 * nqk + nv + c0 + step],
                    preferred_element_type=F32)
        sg_ref[:, c0:c0 + step] = _silu(g)
    yk = jnp.dot(hb, w_ref[:, nqk:2 * nqk], preferred_element_type=F32)
    zf = zf_ref[...][None]
    zb = zb_ref[...][None]
    for h in range(RET_HEADS):
        for off in (0, half):
            lo = h * dk + off
            rk = rotated(yk, lo, off)
            k_ref[:, lo:lo + half] = rk.astype(BF16)
            rk3 = rk.reshape(tm // RET_CHUNK, RET_CHUNK, half)
            kzf_ref[:, lo:lo + half] = (rk3 * zf[:, :, lo:lo + half]).reshape(tm, half).astype(BF16)
            kzb_ref[:, lo:lo + half] = (rk3 * zb[:, :, lo:lo + half]).reshape(tm, half).astype(BF16)
    yq = jnp.dot(hb, w_ref[:, 0:nqk], preferred_element_type=F32)
    for h in range(RET_HEADS):
        for off in (0, half):
            lo = h * dk + off
            q_ref[:, lo:lo + half] = rotated(yq, lo, off).astype(BF16)
    for c0 in range(0, nv, step):
        v_ref[:, c0:c0 + step] = jnp.dot(hb, w_ref[:, 2 * nqk + c0:2 * nqk + c0 + step],
                                         preferred_element_type=F32).astype(BF16)


def _ret_in(h2d, w_in, cos, sin, zf, zb, seq):
    t, d = h2d.shape
    tm = ROW_TILE
    tpb = seq // tm
    nqk = RET_HEADS * (d // RET_HEADS)
    nv = (w_in.shape[1] - 2 * nqk) // 2
    row = lambda width: pl.BlockSpec((tm, width), lambda i: (i, 0))
    pos = pl.BlockSpec((tm, cos.shape[1]), lambda i: (i % tpb, 0))
    return pl.pallas_call(
        _ret_in_kernel,
        grid=(t // tm,),
        in_specs=[row(d), _resident(w_in.shape), pos, pos,
                  _resident(zf.shape), _resident(zb.shape)],
        out_specs=[row(nqk), row(nqk), row(nqk), row(nqk), row(nv), row(nv)],
        out_shape=[jax.ShapeDtypeStruct((t, nqk), BF16)] * 4
        + [jax.ShapeDtypeStruct((t, nv), BF16), jax.ShapeDtypeStruct((t, nv), F32)],
        compiler_params=_params(1),
        name="ret_in_proj",
    )(h2d, w_in, cos, sin, zf, zb)


def _state_update(st_ref, h, kz_h, v_h, cd):
    upd = lax.dot_general(kz_h, v_h, (((0,), (0,)), ((), ())), preferred_element_type=F32)
    st_ref[h] = st_ref[h] * cd + upd


def _ret_scan_kernel(cdf_ref, cdb_ref, q_ref, k_ref, kzf_ref, kzb_ref, v_ref, sg_ref,
                     xif_ref, xib_ref, dm_ref, o_ref, st_ref, snap_ref):
    j = pl.program_id(1)
    ns = pl.num_programs(1) // 2
    L = RET_CHUNK
    n_sub = RET_STEP_CHUNKS

    @pl.when((j == 0) | (j == ns))
    def _():
        st_ref[...] = jnp.zeros(st_ref.shape, F32)

    dk = q_ref.shape[2] // RET_HEADS
    dv = v_ref.shape[2] // RET_HEADS

    @pl.when(j < ns)
    def _():
        for ci in reversed(range(n_sub)):
            rows = slice(ci * L, (ci + 1) * L)
            c = (ns - 1 - j) * n_sub + ci
            for h in range(RET_HEADS):
                snap_ref[c, h] = st_ref[h].astype(BF16)
                _state_update(st_ref, h, kzb_ref[0, rows, h * dk:(h + 1) * dk],
                              v_ref[0, rows, h * dv:(h + 1) * dv], cdb_ref[h])

    @pl.when(j >= ns)
    def _():
        def scores(ci, h):
            rows = slice(ci * L, (ci + 1) * L)
            return lax.dot_general(q_ref[0, rows, h * dk:(h + 1) * dk],
                                   k_ref[0, rows, h * dk:(h + 1) * dk],
                                   (((1,), (1,)), ((), ())), preferred_element_type=F32)

        def finish(ci, h, s):
            rows = slice(ci * L, (ci + 1) * L)
            c = (j - ns) * n_sub + ci
            q_h = q_ref[0, rows, h * dk:(h + 1) * dk]
            v_h = v_ref[0, rows, h * dv:(h + 1) * dv]
            y = jnp.dot((s * dm_ref[h]).astype(BF16), v_h, preferred_element_type=F32)
            y = y + jnp.dot(q_h, st_ref[h].astype(BF16), preferred_element_type=F32) * xif_ref[h]
            y = y + jnp.dot(q_h, snap_ref[c, h], preferred_element_type=F32) * xib_ref[h]
            _state_update(st_ref, h, kzf_ref[0, rows, h * dk:(h + 1) * dk], v_h, cdf_ref[h])
            yn = y * lax.rsqrt(jnp.mean(y * y, axis=-1, keepdims=True) + EPS)
            o_ref[0, rows, h * dv:(h + 1) * dv] = (
                sg_ref[0, rows, h * dv:(h + 1) * dv] * yn).astype(BF16)

        units = [(ci, h) for ci in range(n_sub) for h in range(RET_HEADS)]
        ahead = scores(*units[0])
        for idx, (ci, h) in enumerate(units):
            s = ahead
            if idx + 1 < len(units):
                ahead = scores(*units[idx + 1])
            finish(ci, h, s)


def _retention(q, k, kzf, kzb, v, sg, xi_f, xi_b, dmat, cd_f, cd_b):
    bsz, seq, nqk = q.shape
    nv = v.shape[2]
    L = RET_CHUNK
    rows = RET_STEP_CHUNKS * L
    ns = seq // rows
    dk = nqk // RET_HEADS
    dv = nv // RET_HEADS
    smem = pl.BlockSpec(memory_space=pltpu.SMEM)
    both = lambda j: jnp.where(j < ns, ns - 1 - j, j - ns)
    bwd_only = lambda j: jnp.where(j < ns, ns - 1 - j, 0)
    fwd_only = lambda j: jnp.where(j < ns, 0, j - ns)
    blk = lambda width, at: pl.BlockSpec((1, rows, width), lambda b, j: (b, at(j), 0))
    return pl.pallas_call(
        _ret_scan_kernel,
        grid=(bsz, 2 * ns),
        in_specs=[smem, smem, blk(nqk, fwd_only), blk(nqk, fwd_only), blk(nqk, fwd_only),
                  blk(nqk, bwd_only), blk(nv, both), blk(nv, fwd_only),
                  _resident(xi_f.shape), _resident(xi_b.shape), _resident(dmat.shape)],
        out_specs=blk(nv, fwd_only),
        out_shape=jax.ShapeDtypeStruct((bsz, seq, nv), BF16),
        scratch_shapes=[pltpu.VMEM((RET_HEADS, dk, dv), F32),
                        pltpu.VMEM((seq // L, RET_HEADS, dk, dv), BF16)],
        compiler_params=_params(2),
        name="ret_scan",
    )(cd_f, cd_b, q, k, kzf, kzb, v, sg, xi_f, xi_b, dmat)


def _decay_tables(decay_fwd, decay_bwd, dk):
    L = RET_CHUNK
    lg_f = jax.nn.log_sigmoid(decay_fwd.astype(F32))
    lg_b = jax.nn.log_sigmoid(decay_bwd.astype(F32))
    idx = jnp.arange(L, dtype=F32)
    diff = idx[:, None] - idx[None, :]
    dm = jnp.where((diff >= 0)[None],
                   jnp.exp(lg_f[:, None, None] * jnp.maximum(diff, 0.0)[None]),
                   jnp.exp(lg_b[:, None, None] * jnp.maximum(-diff, 0.0)[None]))
    xi_f = jnp.exp(lg_f[:, None] * (idx + 1.0)[None])[:, :, None]
    xi_b = jnp.exp(lg_b[:, None] * (L - idx)[None])[:, :, None]
    zeta_f = jnp.exp(lg_f[:, None] * (L - 1.0 - idx)[None])
    zeta_b = jnp.exp(lg_b[:, None] * idx[None])
    zf = jnp.repeat(zeta_f.T, dk, axis=1)
    zb = jnp.repeat(zeta_b.T, dk, axis=1)
    return dm, xi_f, xi_b, zf, zb, jnp.exp(lg_f * L), jnp.exp(lg_b * L)


def _rope_tables(seq, dk):
    inv = ROPE_BASE ** (-jnp.arange(0, dk, 2, dtype=F32) / dk)
    ang = jnp.arange(seq, dtype=F32)[:, None] * inv[None]
    return jnp.cos(ang), jnp.sin(ang)


def kernel(x, c, rel_bias, att_w_qkv, att_w_o, att_sink, ret_w_in, ret_w_o, ret_decay_fwd,
           ret_decay_bwd, ada_w, ada_b, mix_norm_pre, mix_norm_post, ffn_norm_pre,
           ffn_norm_post, ffn_w_in, ffn_w_out):
    bsz, seq, d = x.shape
    t = bsz * seq
    assert seq % ROW_TILE == 0 and ROW_TILE % RET_CHUNK == 0 and d % RET_HEADS == 0
    assert ada_w.shape[0] == 2 and ffn_w_out.shape[1] % FFN_CHUNK == 0
    rows = -(-bsz // 8) * 8
    vec = lambda a: a.reshape(1, d)

    c_pad = jnp.pad(c, ((0, rows - bsz), (0, 0)))
    mod = _ada_mod(c_pad, ada_w, ada_b)
    modv = mod.reshape(2 * rows * 6, 1, d)

    x2d = x.reshape(t, d)

    nq = Q_HEADS * HEAD_DIM
    nkv = KV_HEADS * HEAD_DIM
    heads = [(2 * p + e) * GROUP + g for p in range(KV_HEADS // 2) for g in range(GROUP)
             for e in range(2)]
    w_qkv = att_w_qkv[0].astype(BF16)
    w_q = w_qkv[:, :nq].reshape(d, Q_HEADS, HEAD_DIM)[:, jnp.array(heads)].reshape(d, nq)
    w_q = w_q * jnp.asarray(HEAD_DIM ** -0.5, BF16)
    w_k = w_qkv[:, nq:nq + nkv]
    w_vt = w_qkv[:, nq + nkv:].T
    q, k, vt = _qkv_proj(x2d, vec(mix_norm_pre[0]), modv, w_q, w_k, w_vt, seq, rows)
    att, w_ao, w_f0in, w_f0out = _attention(
        q.reshape(bsz, seq, nq), k.reshape(bsz, seq, nkv), vt, rel_bias.astype(F32),
        att_sink[0].astype(F32), [(att_w_o, 0, None), (ffn_w_in, 0, None), (ffn_w_out, 0, None)])

    dk = d // RET_HEADS
    nqk = RET_HEADS * dk
    kscale = jnp.concatenate([jnp.ones((nqk,), F32), jnp.full((nqk,), dk ** -0.5, F32),
                              jnp.ones((ret_w_in.shape[2] - 2 * nqk,), F32)])[None]
    x1, h1, w_rin, w_ro, w_f1in, w_f1out = _post_ffn(
        att.reshape(t, nq), x2d, w_ao, vec(mix_norm_post[0]), vec(ffn_norm_pre[0]), w_f0in,
        w_f0out, vec(ffn_norm_post[0]), modv, 0, seq, rows, g_next=vec(mix_norm_pre[1]),
        cast_jobs=[(ret_w_in, 0, kscale), (ret_w_o, 0, None), (ffn_w_in, 1, None),
                   (ffn_w_out, 1, None)])
    dm, xi_f, xi_b, zf, zb, cd_f, cd_b = _decay_tables(ret_decay_fwd[0], ret_decay_bwd[0], dk)
    cos, sin = _rope_tables(seq, dk)
    q, k, kzf, kzb, v, sg = _ret_in(h1, w_rin, cos, sin, zf, zb, seq)
    r3 = lambda a: a.reshape(bsz, seq, a.shape[1])
    gated = _retention(r3(q), r3(k), r3(kzf), r3(kzb), r3(v), r3(sg), xi_f, xi_b, dm, cd_f, cd_b)
    (x2,) = _post_ffn(gated.reshape(t, -1), x1, w_ro, vec(mix_norm_post[1]),
                      vec(ffn_norm_pre[1]), w_f1in, w_f1out, vec(ffn_norm_post[1]), modv, 1,
                      seq, rows)
    return x2.reshape(bsz, seq, d)
```

```python
import functools
import math

import jax
import jax.numpy as jnp
from jax import lax
from jax.experimental import pallas as pl
from jax.experimental.pallas import tpu as pltpu

F32 = jnp.float32
BF16 = jnp.bfloat16

EPS = 1e-6
NEG = -1e30
LOG2E = math.log2(math.e)

Q_HEADS = 16
KV_HEADS = 4
GROUP = Q_HEADS // KV_HEADS
HEAD_DIM = 64
ATT_BLOCK = 128
ATT_STEP_BLOCKS = 8
REL_BUCKETS = 32
RET_HEADS = 4
RET_CHUNK = 256
RET_STEP_CHUNKS = 2
ROPE_BASE = 10000.0
FFN_CHUNK = 256
ROW_TILE = 512
QKV_ROW_TILE = 1024
VMEM_LIMIT = 56 * 1024 * 1024


def _silu(x):
    return x * (1.0 / (1.0 + jnp.exp(-x)))


def _rms(xf, g):
    ms = jnp.mean(xf * xf, axis=-1, keepdims=True)
    return (xf * lax.rsqrt(ms + EPS)) * g


def _resident(shape):
    zeros = (0,) * len(shape)
    return pl.BlockSpec(shape, lambda *_: zeros, pipeline_mode=pl.Buffered(1))


def _params(n_axes, vmem=VMEM_LIMIT):
    return pltpu.CompilerParams(
        dimension_semantics=("arbitrary",) * n_axes, vmem_limit_bytes=vmem)


BF16_SUBLANES = 16


def _cast_plan(rows, n_steps):
    per = BF16_SUBLANES
    while rows % per or rows // per > n_steps:
        per += BF16_SUBLANES
    return per, rows // per


def _cast_io(jobs, n_steps, step_of):
    in_specs, out_specs, out_shapes, args, plan = [], [], [], [], []
    for w, layer, scale in jobs:
        _, rows, cols = w.shape
        per, n_cast = _cast_plan(rows, n_steps)
        blk = lambda *idx, n_cast=n_cast: jnp.minimum(step_of(*idx), n_cast - 1)
        in_specs.append(pl.BlockSpec(
            (1, per, cols), lambda *idx, blk=blk, layer=layer: (layer, blk(*idx), 0)))
        args.append(w)
        if scale is not None:
            in_specs.append(_resident(scale.shape))
            args.append(scale)
        out_specs.append(pl.BlockSpec((per, cols), lambda *idx, blk=blk: (blk(*idx), 0)))
        out_shapes.append(jax.ShapeDtypeStruct((rows, cols), BF16))
        plan.append((n_cast, scale is not None))
    return in_specs, out_specs, out_shapes, args, tuple(plan)


def _run_casts(step, plan, in_refs, out_refs):
    in_refs = list(in_refs)
    for (n_cast, scaled), dst in zip(plan, out_refs):
        src = in_refs.pop(0)
        scale = in_refs.pop(0) if scaled else None

        @pl.when(step < n_cast)
        def _(src=src, scale=scale, dst=dst):
            v = src[0]
            if scale is not None:
                v = v * scale[...]
            dst[...] = v.astype(BF16)


def _n_cast_inputs(plan):
    return sum(2 if scaled else 1 for _, scaled in plan)


def _ada_kernel(c_ref, w_ref, b_ref, o_ref):
    ca = _silu(c_ref[...])
    o_ref[0] = jnp.dot(ca.astype(BF16), w_ref[0].astype(BF16),
                       preferred_element_type=F32) + b_ref[0]


def _ada_mod(c_pad, ada_w, ada_b):
    depth, d, n = ada_w.shape
    rows = c_pad.shape[0]
    tn = 1536
    return pl.pallas_call(
        _ada_kernel,
        grid=(depth, n // tn),
        in_specs=[
            pl.BlockSpec((rows, d), lambda i, j: (0, 0)),
            pl.BlockSpec((1, d, tn), lambda i, j: (i, 0, j)),
            pl.BlockSpec((1, 1, tn), lambda i, j: (i, 0, j)),
        ],
        out_specs=pl.BlockSpec((1, rows, tn), lambda i, j: (i, 0, j)),
        out_shape=jax.ShapeDtypeStruct((depth, rows, n), F32),
        compiler_params=_params(2),
        name="ada_mod",
    )(c_pad, ada_w, ada_b.reshape(depth, 1, n))


def _mod_spec(layer, slot, tiles_per_batch, n_batch_rows):
    base = layer * n_batch_rows * 6 + slot

    def index(t):
        return (base + (t // tiles_per_batch) * 6, 0, 0)
    return index


def _qkv_kernel(x_ref, g_ref, sc_ref, sh_ref, wq_ref, wk_ref, wvt_ref, q_ref, k_ref, vt_ref):
    h = _rms(x_ref[...], g_ref[...]) * (1.0 + sc_ref[0]) + sh_ref[0]
    hb = h.astype(BF16)
    q_ref[...] = jnp.dot(hb, wq_ref[...], preferred_element_type=F32).astype(BF16)
    k_ref[...] = jnp.dot(hb, wk_ref[...], preferred_element_type=F32).astype(BF16)
    vt_ref[0] = lax.dot_general(wvt_ref[...], hb, (((1,), (1,)), ((), ())),
                                preferred_element_type=F32).astype(BF16)


def _qkv_proj(x2d, g_pre, modv, w_q, w_k, w_vt, seq, n_batch_rows):
    t, d = x2d.shape
    nq = w_q.shape[1]
    nkv = w_k.shape[1]
    tm = QKV_ROW_TILE
    tpb = seq // tm
    vec = lambda idx: pl.BlockSpec((1, 1, d), idx)
    return pl.pallas_call(
        _qkv_kernel,
        grid=(t // tm,),
        in_specs=[
            pl.BlockSpec((tm, d), lambda i: (i, 0)),
            _resident((1, d)),
            vec(_mod_spec(0, 1, tpb, n_batch_rows)),
            vec(_mod_spec(0, 0, tpb, n_batch_rows)),
            _resident(w_q.shape), _resident(w_k.shape), _resident(w_vt.shape),
        ],
        out_specs=[
            pl.BlockSpec((tm, nq), lambda i: (i, 0)),
            pl.BlockSpec((tm, nkv), lambda i: (i, 0)),
            pl.BlockSpec((1, nkv, tm), lambda i: (i // tpb, 0, i % tpb)),
        ],
        out_shape=[
            jax.ShapeDtypeStruct((t, nq), BF16),
            jax.ShapeDtypeStruct((t, nkv), BF16),
            jax.ShapeDtypeStruct((t // seq, nkv, seq), BF16),
        ],
        compiler_params=_params(1),
        name="qkv_proj",
    )(x2d, g_pre, modv, modv, w_q, w_k, w_vt)


def _att_tables(rb_ref, b2_ref, m2_ref):
    L = ATT_BLOCK
    rows = 64
    for r in range(3 * L // rows):
        j = lax.broadcasted_iota(jnp.int32, (rows, L), 0) + r * rows
        t = lax.broadcasted_iota(jnp.int32, (rows, L), 1)
        rel = j - L - t
        n = jnp.abs(rel)
        large = jnp.full((rows, L), 8, jnp.int32)
        for thr in (12, 16, 23, 32, 46, 64, 91):
            large = large + (n >= thr).astype(jnp.int32)
        bucket = jnp.where(rel > 0, 16, 0) + jnp.where(n < 8, n, large)
        in_win = n <= L
        visible = (in_win & (j >= L), in_win, in_win & (j < 2 * L))
        piece = slice(r * rows, (r + 1) * rows)
        for kind in range(3):
            m2_ref[kind, piece, :] = jnp.where(visible[kind], LOG2E, 0.0).astype(F32)

        def body(hq, carry):
            acc = jnp.zeros((rows, L), F32)
            for b in range(REL_BUCKETS):
                acc = jnp.where(bucket == b, rb_ref[b, hq], acc)
            acc = acc * LOG2E
            for kind in range(3):
                b2_ref[kind, hq, piece, :] = jnp.where(visible[kind], acc, NEG)
            return carry
        lax.fori_loop(0, Q_HEADS, body, 0)


def _att_kernel(*refs, cast_plan):
    rb_ref, sink_ref, q_ref, kp_ref, kc_ref, kn_ref, vp_ref, vc_ref, vn_ref = refs[:9]
    n_cast_in = _n_cast_inputs(cast_plan)
    cast_in = refs[9:9 + n_cast_in]
    o_ref = refs[9 + n_cast_in]
    cast_out = refs[10 + n_cast_in:10 + n_cast_in + len(cast_plan)]
    b2_ref, m2_ref = refs[10 + n_cast_in + len(cast_plan):]
    b = pl.program_id(0)
    i = pl.program_id(1)
    ns = pl.num_programs(1)
    _run_casts(b * ns + i, cast_plan, cast_in, cast_out)

    @pl.when((b == 0) & (i == 0))
    def _():
        _att_tables(rb_ref, b2_ref, m2_ref)

    L = ATT_BLOCK
    dh = HEAD_DIM
    U = ATT_STEP_BLOCKS
    q = q_ref[0]
    kb = jnp.concatenate([kp_ref[0], kc_ref[0], kn_ref[0]], axis=0)
    vt = jnp.concatenate([vp_ref[0], vc_ref[0], vn_ref[0]], axis=1)
    lane = lax.broadcasted_iota(jnp.int32, (L, 2 * dh), 1)
    zero = jnp.zeros((L, 2 * dh), BF16)
    ones = jnp.ones((BF16_SUBLANES, 3 * L), BF16)
    kinds = [1] * U
    kinds[0] = jnp.where(i == 0, 0, 1)
    kinds[U - 1] = jnp.where(i == ns - 1, 2, 1)

    def scores(u, h):
        p, e = divmod(h, 2)
        k_pair = kb[u * L:(u + 3) * L, p * 2 * dh:(p + 1) * 2 * dh]
        mine = (lane < dh) if e == 0 else (lane >= dh)
        qz = jnp.concatenate(
            [jnp.where(mine, q[u * L:(u + 1) * L,
                               (p * GROUP + g) * 2 * dh:(p * GROUP + g + 1) * 2 * dh], zero)
             for g in range(GROUP)], axis=0)
        return lax.dot_general(k_pair, qz, (((1,), (1,)), ((), ())),
                               preferred_element_type=F32)

    units = [(u, h) for u in range(U) for h in range(KV_HEADS)]
    o_rows = []
    st_next = scores(*units[0])
    for idx, (u, h) in enumerate(units):
        st = st_next
        if idx + 1 < len(units):
            st_next = scores(*units[idx + 1])
        m2 = m2_ref[kinds[u]]
        es, sinks = [], []
        for g in range(GROUP):
            hq = h * GROUP + g
            sk2 = jnp.full((1, L), sink_ref[hq], F32) * LOG2E
            l2 = st[:, g * L:(g + 1) * L] * m2 + b2_ref[kinds[u], hq]
            m = jnp.maximum(jnp.max(l2, axis=0, keepdims=True), sk2)
            es.append(jnp.exp2(l2 - m).astype(BF16))
            sinks.append(jnp.exp2(sk2 - m))
        et = jnp.concatenate(es, axis=1)
        va = jnp.concatenate([vt[h * dh:(h + 1) * dh, u * L:(u + 3) * L], ones], axis=0)
        ot = jnp.dot(va, et, preferred_element_type=F32)
        den = ot[dh:dh + 1, :] + jnp.concatenate(sinks, axis=1)
        ot = ot[:dh, :] * (1.0 / den)
        o_rows += [ot[:, g * L:(g + 1) * L] for g in range(GROUP)]
        if h == KV_HEADS - 1:
            ot_all = jnp.concatenate(o_rows, axis=0)
            o_ref[0, u * L:(u + 1) * L, :] = ot_all.T.astype(BF16)
            o_rows = []


def _attention(q, k, vt, rel_bias, sink, cast_jobs):
    bsz, seq, dq = q.shape
    dkv = k.shape[2]
    L = ATT_BLOCK
    nb = seq // L
    U = ATT_STEP_BLOCKS
    ns = nb // U
    smem = pl.BlockSpec(memory_space=pltpu.SMEM)
    prev = lambda i: jnp.maximum(U * i - 1, 0)
    nxt = lambda i: jnp.minimum(U * i + U, nb - 1)
    c_in, c_out, c_shapes, c_args, plan = _cast_io(cast_jobs, bsz * ns, lambda b, i: b * ns + i)
    return pl.pallas_call(
        functools.partial(_att_kernel, cast_plan=plan),
        grid=(bsz, ns),
        in_specs=[
            smem, smem,
            pl.BlockSpec((1, U * L, dq), lambda b, i: (b, i, 0)),
            pl.BlockSpec((1, L, dkv), lambda b, i: (b, prev(i), 0)),
            pl.BlockSpec((1, U * L, dkv), lambda b, i: (b, i, 0)),
            pl.BlockSpec((1, L, dkv), lambda b, i: (b, nxt(i), 0)),
            pl.BlockSpec((1, dkv, L), lambda b, i: (b, 0, prev(i))),
            pl.BlockSpec((1, dkv, U * L), lambda b, i: (b, 0, i)),
            pl.BlockSpec((1, dkv, L), lambda b, i: (b, 0, nxt(i))),
        ] + c_in,
        out_specs=[pl.BlockSpec((1, U * L, dq), lambda b, i: (b, i, 0))] + c_out,
        out_shape=[jax.ShapeDtypeStruct((bsz, seq, dq), BF16)] + c_shapes,
        scratch_shapes=[
            pltpu.VMEM((3, Q_HEADS, 3 * L, L), F32),
            pltpu.VMEM((3, 3 * L, L), F32),
        ],
        compiler_params=_params(2),
        name="swa_attention",
    )(rel_bias, sink, q, k, k, k, vt, vt, vt, *c_args)


def _post_ffn_kernel(*refs, emit_next, cast_plan):
    (a_ref, x_ref, wo_ref, gpost_ref, g1_ref, gpre_ref, sc2_ref, sh2_ref,
     win_ref, wout_ref, gfpost_ref, g2_ref) = refs[:12]
    rest = list(refs[12:])
    if emit_next:
        gn_ref, scn_ref, shn_ref = rest[:3]
        rest = rest[3:]
    n_cast_in = _n_cast_inputs(cast_plan)
    cast_in, rest = rest[:n_cast_in], rest[n_cast_in:]
    xo_ref = rest.pop(0)
    if emit_next:
        ho_ref = rest.pop(0)
    cast_out = rest[:len(cast_plan)]
    hb_ref, x1_ref, acc_ref, y_ref = rest[len(cast_plan):]
    s = pl.program_id(0)
    n = pl.num_programs(0) - 2
    _run_casts(s, cast_plan, cast_in, cast_out)
    hidden = wout_ref.shape[0]

    def advance():
        hb_ref[1] = hb_ref[0]
        x1_ref[2] = x1_ref[1]
        x1_ref[1] = x1_ref[0]

    n_chunks = hidden // FFN_CHUNK
    n_slices = 8
    rows_of = lambda r: slice(r * (x_ref.shape[0] // n_slices), (r + 1) * (x_ref.shape[0] // n_slices))

    def a_matmul():
        y_ref[...] = jnp.dot(a_ref[...], wo_ref[...], preferred_element_type=F32)

    def after(v, dep):
        if dep is None:
            return v
        zero = (pltpu.bitcast(dep, jnp.uint32) >> 16) >> 16
        return v * pltpu.bitcast(zero | jnp.uint32(0x3F800000), F32)

    def a_norm(r, dep=None):
        rows = rows_of(r)
        x1 = x_ref[rows, :] + g1_ref[0] * _rms(after(y_ref[rows, :], dep), gpost_ref[...])
        h = _rms(x1, gpre_ref[...]) * (1.0 + sc2_ref[0]) + sh2_ref[0]
        hb_ref[0, rows, :] = h.astype(BF16)
        x1_ref[0, rows, :] = x1

    def b_chunk(c, hb, acc):
        lo = c * FFN_CHUNK
        a = jnp.dot(hb, win_ref[:, lo:lo + FFN_CHUNK], preferred_element_type=F32)
        b = jnp.dot(hb, win_ref[:, hidden + lo:hidden + lo + FFN_CHUNK],
                    preferred_element_type=F32)
        act = (_silu(a) * b).astype(BF16)
        return acc + jnp.dot(act, wout_ref[lo:lo + FFN_CHUNK, :], preferred_element_type=F32)

    def c_slice(r, dep=None):
        rows = rows_of(r)
        x2 = x1_ref[2, rows, :] + g2_ref[0] * _rms(after(acc_ref[rows, :], dep),
                                                   gfpost_ref[...])
        xo_ref[rows, :] = x2
        if emit_next:
            hn = _rms(x2, gn_ref[...]) * (1.0 + scn_ref[0]) + shn_ref[0]
            ho_ref[rows, :] = hn.astype(BF16)

    @pl.when(s == 0)
    def _():
        x1_ref[...] = jnp.zeros(x1_ref.shape, F32)
        acc_ref[...] = jnp.zeros(acc_ref.shape, F32)
        a_matmul()
        for r in range(n_slices):
            a_norm(r)

    @pl.when((s >= 1) & (s <= n))
    def _():
        advance()
        a_matmul()
        hb = hb_ref[1]
        acc = jnp.zeros((hb.shape[0], wout_ref.shape[1]), F32)
        for c in range(n_chunks):
            acc = b_chunk(c, hb, acc)
            dep = acc[0:1, :]
            if c < n_slices:
                c_slice(c, dep)
            if c >= n_chunks - n_slices - 1 and c < n_chunks - 1:
                a_norm(c - (n_chunks - n_slices - 1), dep)
        acc_ref[...] = acc

    @pl.when(s == n + 1)
    def _():
        advance()
        for r in range(n_slices):
            c_slice(r)


def _post_ffn(a2d, x2d, w_o, g_post, g_pre, w_in, w_out, g_fpost, modv, layer,
              seq, n_batch_rows, g_next=None, cast_jobs=()):
    t, d = x2d.shape
    kin = a2d.shape[1]
    tm = ROW_TILE
    tpb = seq // tm
    n = t // tm
    emit_next = g_next is not None
    tile_a = lambda s: jnp.minimum(s, n - 1)
    tile_c = lambda s: jnp.clip(s - 2, 0, n - 1)

    def vec(lyr, slot, tile_of):
        idx = _mod_spec(lyr, slot, tpb, n_batch_rows)
        return pl.BlockSpec((1, 1, d), lambda s: idx(tile_of(s)))
    row = lambda width, tile_of: pl.BlockSpec((tm, width), lambda s: (tile_of(s), 0))
    in_specs = [
        row(kin, tile_a), row(d, tile_a), _resident(w_o.shape), _resident((1, d)),
        vec(layer, 2, tile_a), _resident((1, d)), vec(layer, 4, tile_a), vec(layer, 3, tile_a),
        _resident(w_in.shape), _resident(w_out.shape), _resident((1, d)), vec(layer, 5, tile_c),
    ]
    args = [a2d, x2d, w_o, g_post, modv, g_pre, modv, modv, w_in, w_out, g_fpost, modv]
    out_specs = [row(d, tile_c)]
    out_shape = [jax.ShapeDtypeStruct((t, d), F32)]
    if emit_next:
        in_specs += [_resident((1, d)), vec(layer + 1, 1, tile_c), vec(layer + 1, 0, tile_c)]
        args += [g_next, modv, modv]
        out_specs.append(row(d, tile_c))
        out_shape.append(jax.ShapeDtypeStruct((t, d), BF16))
    c_in, c_out, c_shapes, c_args, plan = _cast_io(cast_jobs, n, lambda s: s)
    return pl.pallas_call(
        functools.partial(_post_ffn_kernel, emit_next=emit_next, cast_plan=plan),
        grid=(n + 2,),
        in_specs=in_specs + c_in,
        out_specs=out_specs + c_out,
        out_shape=out_shape + c_shapes,
        scratch_shapes=[pltpu.VMEM((2, tm, d), BF16), pltpu.VMEM((3, tm, d), F32),
                        pltpu.VMEM((tm, d), F32), pltpu.VMEM((tm, d), F32)],
        compiler_params=_params(1),
        name="post_ffn_next" if emit_next else "post_ffn",
    )(*args, *c_args)


def _ret_in_kernel(h_ref, w_ref, cos_ref, sin_ref, zf_ref, zb_ref,
                   q_ref, k_ref, kzf_ref, kzb_ref, v_ref, sg_ref):
    hb = h_ref[...]
    tm = hb.shape[0]
    cos = cos_ref[...]
    sin = sin_ref[...]
    dk = w_ref.shape[0] // RET_HEADS
    half = dk // 2
    nqk = RET_HEADS * dk
    nv = (w_ref.shape[1] - 2 * nqk) // 2

    def rotated(y, lo, off):
        other = lo - off + (half - off)
        a = y[:, lo:lo + half] * cos
        b = y[:, other:other + half] * sin
        return a - b if off == 0 else a + b

    step = nqk
    for c0 in range(0, nv, step):
        g = jnp.dot(hb, w_ref[:, 2 * nqk + nv + c0:2 * nqk + nv + c0 + step],
                    preferred_element_type=F32)
        sg_ref[:, c0:c0 + step] = _silu(g)
    yk = jnp.dot(hb, w_ref[:, nqk:2 * nqk], preferred_element_type=F32)
    zf = zf_ref[...][None]
    zb = zb_ref[...][None]
    for h in range(RET_HEADS):
        for off in (0, half):
            lo = h * dk + off
            rk = rotated(yk, lo, off)
            k_ref[:, lo:lo + half] = rk.astype(BF16)
            rk3 = rk.reshape(tm // RET_CHUNK, RET_CHUNK, half)
            kzf_ref[:, lo:lo + half] = (rk3 * zf[:, :, lo:lo + half]).reshape(tm, half).astype(BF16)
            kzb_ref[:, lo:lo + half] = (rk3 * zb[:, :, lo:lo + half]).reshape(tm, half).astype(BF16)
    yq = jnp.dot(hb, w_ref[:, 0:nqk], preferred_element_type=F32)
    for h in range(RET_HEADS):
        for off in (0, half):
            lo = h * dk + off
            q_ref[:, lo:lo + half] = rotated(yq, lo, off).astype(BF16)
    for c0 in range(0, nv, step):
        v_ref[:, c0:c0 + step] = jnp.dot(hb, w_ref[:, 2 * nqk + c0:2 * nqk + c0 + step],
                                         preferred_element_type=F32).astype(BF16)


def _ret_in(h2d, w_in, cos, sin, zf, zb, seq):
    t, d = h2d.shape
    tm = ROW_TILE
    tpb = seq // tm
    nqk = RET_HEADS * (d // RET_HEADS)
    nv = (w_in.shape[1] - 2 * nqk) // 2
    row = lambda width: pl.BlockSpec((tm, width), lambda i: (i, 0))
    pos = pl.BlockSpec((tm, cos.shape[1]), lambda i: (i % tpb, 0))
    return pl.pallas_call(
        _ret_in_kernel,
        grid=(t // tm,),
        in_specs=[row(d), _resident(w_in.shape), pos, pos,
                  _resident(zf.shape), _resident(zb.shape)],
        out_specs=[row(nqk), row(nqk), row(nqk), row(nqk), row(nv), row(nv)],
        out_shape=[jax.ShapeDtypeStruct((t, nqk), BF16)] * 4
        + [jax.ShapeDtypeStruct((t, nv), BF16), jax.ShapeDtypeStruct((t, nv), F32)],
        compiler_params=_params(1),
        name="ret_in_proj",
    )(h2d, w_in, cos, sin, zf, zb)


def _state_update(st_ref, h, kz_h, v_h, cd):
    upd = lax.dot_general(kz_h, v_h, (((0,), (0,)), ((), ())), preferred_element_type=F32)
    st_ref[h] = st_ref[h] * cd + upd


def _ret_scan_kernel(cdf_ref, cdb_ref, q_ref, k_ref, kzf_ref, kzb_ref, v_ref, sg_ref,
                     xif_ref, xib_ref, dm_ref, o_ref, st_ref, snap_ref):
    j = pl.program_id(1)
    ns = pl.num_programs(1) // 2
    L = RET_CHUNK
    n_sub = RET_STEP_CHUNKS

    @pl.when((j == 0) | (j == ns))
    def _():
        st_ref[...] = jnp.zeros(st_ref.shape, F32)

    dk = q_ref.shape[2] // RET_HEADS
    dv = v_ref.shape[2] // RET_HEADS

    @pl.when(j < ns)
    def _():
        for ci in reversed(range(n_sub)):
            rows = slice(ci * L, (ci + 1) * L)
            c = (ns - 1 - j) * n_sub + ci
            for h in range(RET_HEADS):
                snap_ref[c, h] = st_ref[h].astype(BF16)
                _state_update(st_ref, h, kzb_ref[0, rows, h * dk:(h + 1) * dk],
                              v_ref[0, rows, h * dv:(h + 1) * dv], cdb_ref[h])

    @pl.when(j >= ns)
    def _():
        def scores(ci, h):
            rows = slice(ci * L, (ci + 1) * L)
            return lax.dot_general(q_ref[0, rows, h * dk:(h + 1) * dk],
                                   k_ref[0, rows, h * dk:(h + 1) * dk],
                                   (((1,), (1,)), ((), ())), preferred_element_type=F32)

        def finish(ci, h, s):
            rows = slice(ci * L, (ci + 1) * L)
            c = (j - ns) * n_sub + ci
            q_h = q_ref[0, rows, h * dk:(h + 1) * dk]
            v_h = v_ref[0, rows, h * dv:(h + 1) * dv]
            y = jnp.dot((s * dm_ref[h]).astype(BF16), v_h, preferred_element_type=F32)
            y = y + jnp.dot(q_h, st_ref[h].astype(BF16), preferred_element_type=F32) * xif_ref[h]
            y = y + jnp.dot(q_h, snap_ref[c, h], preferred_element_type=F32) * xib_ref[h]
            _state_update(st_ref, h, kzf_ref[0, rows, h * dk:(h + 1) * dk], v_h, cdf_ref[h])
            yn = y * lax.rsqrt(jnp.mean(y * y, axis=-1, keepdims=True) + EPS)
            o_ref[0, rows, h * dv:(h + 1) * dv] = (
                sg_ref[0, rows, h * dv:(h + 1) * dv] * yn).astype(BF16)

        units = [(ci, h) for ci in range(n_sub) for h in range(RET_HEADS)]
        ahead = scores(*units[0])
        for idx, (ci, h) in enumerate(units):
            s = ahead
            if idx + 1 < len(units):
                ahead = scores(*units[idx + 1])
            finish(ci, h, s)


def _retention(q, k, kzf, kzb, v, sg, xi_f, xi_b, dmat, cd_f, cd_b):
    bsz, seq, nqk = q.shape
    nv = v.shape[2]
    L = RET_CHUNK
    rows = RET_STEP_CHUNKS * L
    ns = seq // rows
    dk = nqk // RET_HEADS
    dv = nv // RET_HEADS
    smem = pl.BlockSpec(memory_space=pltpu.SMEM)
    both = lambda j: jnp.where(j < ns, ns - 1 - j, j - ns)
    bwd_only = lambda j: jnp.where(j < ns, ns - 1 - j, 0)
    fwd_only = lambda j: jnp.where(j < ns, 0, j - ns)
    blk = lambda width, at: pl.BlockSpec((1, rows, width), lambda b, j: (b, at(j), 0))
    return pl.pallas_call(
        _ret_scan_kernel,
        grid=(bsz, 2 * ns),
        in_specs=[smem, smem, blk(nqk, fwd_only), blk(nqk, fwd_only), blk(nqk, fwd_only),
                  blk(nqk, bwd_only), blk(nv, both), blk(nv, fwd_only),
                  _resident(xi_f.shape), _resident(xi_b.shape), _resident(dmat.shape)],
        out_specs=blk(nv, fwd_only),
        out_shape=jax.ShapeDtypeStruct((bsz, seq, nv), BF16),
        scratch_shapes=[pltpu.VMEM((RET_HEADS, dk, dv), F32),
                        pltpu.VMEM((seq // L, RET_HEADS, dk, dv), BF16)],
        compiler_params=_params(2),
        name="ret_scan",
    )(cd_f, cd_b, q, k, kzf, kzb, v, sg, xi_f, xi_b, dmat)


def _decay_tables(decay_fwd, decay_bwd, dk):
    L = RET_CHUNK
    lg_f = jax.nn.log_sigmoid(decay_fwd.astype(F32))
    lg_b = jax.nn.log_sigmoid(decay_bwd.astype(F32))
    idx = jnp.arange(L, dtype=F32)
    diff = idx[:, None] - idx[None, :]
    dm = jnp.where((diff >= 0)[None],
                   jnp.exp(lg_f[:, None, None] * jnp.maximum(diff, 0.0)[None]),
                   jnp.exp(lg_b[:, None, None] * jnp.maximum(-diff, 0.0)[None]))
    xi_f = jnp.exp(lg_f[:, None] * (idx + 1.0)[None])[:, :, None]
    xi_b = jnp.exp(lg_b[:, None] * (L - idx)[None])[:, :, None]
    zeta_f = jnp.exp(lg_f[:, None] * (L - 1.0 - idx)[None])
    zeta_b = jnp.exp(lg_b[:, None] * idx[None])
    zf = jnp.repeat(zeta_f.T, dk, axis=1)
    zb = jnp.repeat(zeta_b.T, dk, axis=1)
    return dm, xi_f, xi_b, zf, zb, jnp.exp(lg_f * L), jnp.exp(lg_b * L)


def _rope_tables(seq, dk):
    inv = ROPE_BASE ** (-jnp.arange(0, dk, 2, dtype=F32) / dk)
    ang = jnp.arange(seq, dtype=F32)[:, None] * inv[None]
    return jnp.cos(ang), jnp.sin(ang)


def kernel(x, c, rel_bias, att_w_qkv, att_w_o, att_sink, ret_w_in, ret_w_o, ret_decay_fwd,
           ret_decay_bwd, ada_w, ada_b, mix_norm_pre, mix_norm_post, ffn_norm_pre,
           ffn_norm_post, ffn_w_in, ffn_w_out):
    bsz, seq, d = x.shape
    t = bsz * seq
    assert seq % ROW_TILE == 0 and ROW_TILE % RET_CHUNK == 0 and d % RET_HEADS == 0
    assert seq % QKV_ROW_TILE == 0
    assert ada_w.shape[0] == 2 and ffn_w_out.shape[1] % FFN_CHUNK == 0
    rows = -(-bsz // 8) * 8
    vec = lambda a: a.reshape(1, d)

    c_pad = jnp.pad(c, ((0, rows - bsz), (0, 0)))
    mod = _ada_mod(c_pad, ada_w, ada_b)
    modv = mod.reshape(2 * rows * 6, 1, d)

    x2d = x.reshape(t, d)

    nq = Q_HEADS * HEAD_DIM
    nkv = KV_HEADS * HEAD_DIM
    heads = [(2 * p + e) * GROUP + g for p in range(KV_HEADS // 2) for g in range(GROUP)
             for e in range(2)]
    w_qkv = att_w_qkv[0].astype(BF16)
    w_q = w_qkv[:, :nq].reshape(d, Q_HEADS, HEAD_DIM)[:, jnp.array(heads)].reshape(d, nq)
    w_q = w_q * jnp.asarray(HEAD_DIM ** -0.5, BF16)
    w_k = w_qkv[:, nq:nq + nkv]
    w_vt = w_qkv[:, nq + nkv:].T
    q, k, vt = _qkv_proj(x2d, vec(mix_norm_pre[0]), modv, w_q, w_k, w_vt, seq, rows)
    att, w_ao, w_f0in, w_f0out = _attention(
        q.reshape(bsz, seq, nq), k.reshape(bsz, seq, nkv), vt, rel_bias.astype(F32),
        att_sink[0].astype(F32), [(att_w_o, 0, None), (ffn_w_in, 0, None), (ffn_w_out, 0, None)])

    dk = d // RET_HEADS
    nqk = RET_HEADS * dk
    kscale = jnp.concatenate([jnp.ones((nqk,), F32), jnp.full((nqk,), dk ** -0.5, F32),
                              jnp.ones((ret_w_in.shape[2] - 2 * nqk,), F32)])[None]
    x1, h1, w_rin, w_ro, w_f1in, w_f1out = _post_ffn(
        att.reshape(t, nq), x2d, w_ao, vec(mix_norm_post[0]), vec(ffn_norm_pre[0]), w_f0in,
        w_f0out, vec(ffn_norm_post[0]), modv, 0, seq, rows, g_next=vec(mix_norm_pre[1]),
        cast_jobs=[(ret_w_in, 0, kscale), (ret_w_o, 0, None), (ffn_w_in, 1, None),
                   (ffn_w_out, 1, None)])
    dm, xi_f, xi_b, zf, zb, cd_f, cd_b = _decay_tables(ret_decay_fwd[0], ret_decay_bwd[0], dk)
    cos, sin = _rope_tables(seq, dk)
    q, k, kzf, kzb, v, sg = _ret_in(h1, w_rin, cos, sin, zf, zb, seq)
    r3 = lambda a: a.reshape(bsz, seq, a.shape[1])
    gated = _retention(r3(q), r3(k), r3(kzf), r3(kzb), r3(v), r3(sg), xi_f, xi_b, dm, cd_f, cd_b)
    (x2,) = _post_ffn(gated.reshape(t, -1), x1, w_ro, vec(mix_norm_post[1]),
                      vec(ffn_norm_pre[1]), w_f1in, w_f1out, vec(ffn_norm_post[1]), modv, 1,
                      seq, rows)
    return x2.reshape(bsz, seq, d)
```

```python
import functools
import math

import jax
import jax.numpy as jnp
from jax import lax
from jax.experimental import pallas as pl
from jax.experimental.pallas import tpu as pltpu

F32 = jnp.float32
BF16 = jnp.bfloat16

EPS = 1e-6
NEG = -1e30
LOG2E = math.log2(math.e)

Q_HEADS = 16
KV_HEADS = 4
GROUP = Q_HEADS // KV_HEADS
HEAD_DIM = 64
ATT_BLOCK = 128
ATT_STEP_BLOCKS = 8
REL_BUCKETS = 32
RET_HEADS = 4
RET_CHUNK = 256
RET_STEP_CHUNKS = 2
ROPE_BASE = 10000.0
FFN_CHUNK = 256
ROW_TILE = 512
QKV_ROW_TILE = 1024
VMEM_LIMIT = 56 * 1024 * 1024


def _silu(x):
    hx = 0.5 * x
    return hx * jnp.tanh(hx) + hx


def _rms(xf, g):
    ms = jnp.mean(xf * xf, axis=-1, keepdims=True)
    return (xf * lax.rsqrt(ms + EPS)) * g


def _resident(shape):
    zeros = (0,) * len(shape)
    return pl.BlockSpec(shape, lambda *_: zeros, pipeline_mode=pl.Buffered(1))


def _params(n_axes, vmem=VMEM_LIMIT):
    return pltpu.CompilerParams(
        dimension_semantics=("arbitrary",) * n_axes, vmem_limit_bytes=vmem)


BF16_SUBLANES = 16


def _cast_plan(rows, n_steps):
    per = BF16_SUBLANES
    while rows % per or rows // per > n_steps:
        per += BF16_SUBLANES
    return per, rows // per


def _cast_io(jobs, n_steps, step_of):
    in_specs, out_specs, out_shapes, args, plan = [], [], [], [], []
    for w, layer, scale in jobs:
        _, rows, cols = w.shape
        per, n_cast = _cast_plan(rows, n_steps)
        blk = lambda *idx, n_cast=n_cast: jnp.minimum(step_of(*idx), n_cast - 1)
        in_specs.append(pl.BlockSpec(
            (1, per, cols), lambda *idx, blk=blk, layer=layer: (layer, blk(*idx), 0)))
        args.append(w)
        if scale is not None:
            in_specs.append(_resident(scale.shape))
            args.append(scale)
        out_specs.append(pl.BlockSpec((per, cols), lambda *idx, blk=blk: (blk(*idx), 0)))
        out_shapes.append(jax.ShapeDtypeStruct((rows, cols), BF16))
        plan.append((n_cast, scale is not None))
    return in_specs, out_specs, out_shapes, args, tuple(plan)


def _run_casts(step, plan, in_refs, out_refs):
    in_refs = list(in_refs)
    for (n_cast, scaled), dst in zip(plan, out_refs):
        src = in_refs.pop(0)
        scale = in_refs.pop(0) if scaled else None

        @pl.when(step < n_cast)
        def _(src=src, scale=scale, dst=dst):
            v = src[0]
            if scale is not None:
                v = v * scale[...]
            dst[...] = v.astype(BF16)


def _n_cast_inputs(plan):
    return sum(2 if scaled else 1 for _, scaled in plan)


def _ada_kernel(c_ref, w_ref, b_ref, o_ref):
    ca = _silu(c_ref[...])
    o_ref[0] = jnp.dot(ca.astype(BF16), w_ref[0].astype(BF16),
                       preferred_element_type=F32) + b_ref[0]


def _ada_mod(c_pad, ada_w, ada_b):
    depth, d, n = ada_w.shape
    rows = c_pad.shape[0]
    tn = 1536
    return pl.pallas_call(
        _ada_kernel,
        grid=(depth, n // tn),
        in_specs=[
            pl.BlockSpec((rows, d), lambda i, j: (0, 0)),
            pl.BlockSpec((1, d, tn), lambda i, j: (i, 0, j)),
            pl.BlockSpec((1, 1, tn), lambda i, j: (i, 0, j)),
        ],
        out_specs=pl.BlockSpec((1, rows, tn), lambda i, j: (i, 0, j)),
        out_shape=jax.ShapeDtypeStruct((depth, rows, n), F32),
        compiler_params=_params(2),
        name="ada_mod",
    )(c_pad, ada_w, ada_b.reshape(depth, 1, n))


def _mod_spec(layer, slot, tiles_per_batch, n_batch_rows):
    base = layer * n_batch_rows * 6 + slot

    def index(t):
        return (base + (t // tiles_per_batch) * 6, 0, 0)
    return index


def _qkv_kernel(x_ref, g_ref, sc_ref, sh_ref, wq_ref, wk_ref, wvt_ref, q_ref, k_ref, vt_ref):
    h = _rms(x_ref[...], g_ref[...]) * (1.0 + sc_ref[0]) + sh_ref[0]
    hb = h.astype(BF16)
    q_ref[...] = jnp.dot(hb, wq_ref[...], preferred_element_type=F32).astype(BF16)
    k_ref[...] = jnp.dot(hb, wk_ref[...], preferred_element_type=F32).astype(BF16)
    vt_ref[0] = lax.dot_general(wvt_ref[...], hb, (((1,), (1,)), ((), ())),
                                preferred_element_type=F32).astype(BF16)


def _qkv_proj(x2d, g_pre, modv, w_q, w_k, w_vt, seq, n_batch_rows):
    t, d = x2d.shape
    nq = w_q.shape[1]
    nkv = w_k.shape[1]
    tm = QKV_ROW_TILE
    tpb = seq // tm
    vec = lambda idx: pl.BlockSpec((1, 1, d), idx)
    return pl.pallas_call(
        _qkv_kernel,
        grid=(t // tm,),
        in_specs=[
            pl.BlockSpec((tm, d), lambda i: (i, 0)),
            _resident((1, d)),
            vec(_mod_spec(0, 1, tpb, n_batch_rows)),
            vec(_mod_spec(0, 0, tpb, n_batch_rows)),
            _resident(w_q.shape), _resident(w_k.shape), _resident(w_vt.shape),
        ],
        out_specs=[
            pl.BlockSpec((tm, nq), lambda i: (i, 0)),
            pl.BlockSpec((tm, nkv), lambda i: (i, 0)),
            pl.BlockSpec((1, nkv, tm), lambda i: (i // tpb, 0, i % tpb)),
        ],
        out_shape=[
            jax.ShapeDtypeStruct((t, nq), BF16),
            jax.ShapeDtypeStruct((t, nkv), BF16),
            jax.ShapeDtypeStruct((t // seq, nkv, seq), BF16),
        ],
        compiler_params=_params(1),
        name="qkv_proj",
    )(x2d, g_pre, modv, modv, w_q, w_k, w_vt)


def _att_tables(rb_ref, b2_ref, m2_ref):
    L = ATT_BLOCK
    rows = 64
    for r in range(3 * L // rows):
        j = lax.broadcasted_iota(jnp.int32, (rows, L), 0) + r * rows
        t = lax.broadcasted_iota(jnp.int32, (rows, L), 1)
        rel = j - L - t
        n = jnp.abs(rel)
        large = jnp.full((rows, L), 8, jnp.int32)
        for thr in (12, 16, 23, 32, 46, 64, 91):
            large = large + (n >= thr).astype(jnp.int32)
        bucket = jnp.where(rel > 0, 16, 0) + jnp.where(n < 8, n, large)
        in_win = n <= L
        visible = (in_win & (j >= L), in_win, in_win & (j < 2 * L))
        piece = slice(r * rows, (r + 1) * rows)
        for kind in range(3):
            m2_ref[kind, piece, :] = jnp.where(visible[kind], LOG2E, 0.0).astype(F32)

        def body(hq, carry):
            acc = jnp.zeros((rows, L), F32)
            for b in range(REL_BUCKETS):
                acc = jnp.where(bucket == b, rb_ref[b, hq], acc)
            acc = acc * LOG2E
            for kind in range(3):
                b2_ref[kind, hq, piece, :] = jnp.where(visible[kind], acc, NEG)
            return carry
        lax.fori_loop(0, Q_HEADS, body, 0)


def _att_kernel(*refs, cast_plan):
    rb_ref, sink_ref, q_ref, kp_ref, kc_ref, kn_ref, vp_ref, vc_ref, vn_ref = refs[:9]
    n_cast_in = _n_cast_inputs(cast_plan)
    cast_in = refs[9:9 + n_cast_in]
    o_ref = refs[9 + n_cast_in]
    cast_out = refs[10 + n_cast_in:10 + n_cast_in + len(cast_plan)]
    b2_ref, m2_ref = refs[10 + n_cast_in + len(cast_plan):]
    b = pl.program_id(0)
    i = pl.program_id(1)
    ns = pl.num_programs(1)
    _run_casts(b * ns + i, cast_plan, cast_in, cast_out)

    @pl.when((b == 0) & (i == 0))
    def _():
        _att_tables(rb_ref, b2_ref, m2_ref)

    L = ATT_BLOCK
    dh = HEAD_DIM
    U = ATT_STEP_BLOCKS
    q = q_ref[0]
    kb = jnp.concatenate([kp_ref[0], kc_ref[0], kn_ref[0]], axis=0)
    vt = jnp.concatenate([vp_ref[0], vc_ref[0], vn_ref[0]], axis=1)
    lane = lax.broadcasted_iota(jnp.int32, (L, 2 * dh), 1)
    zero = jnp.zeros((L, 2 * dh), BF16)
    ones = jnp.ones((BF16_SUBLANES, 3 * L), BF16)
    kinds = [1] * U
    kinds[0] = jnp.where(i == 0, 0, 1)
    kinds[U - 1] = jnp.where(i == ns - 1, 2, 1)

    def scores(u, h):
        p, e = divmod(h, 2)
        k_pair = kb[u * L:(u + 3) * L, p * 2 * dh:(p + 1) * 2 * dh]
        mine = (lane < dh) if e == 0 else (lane >= dh)
        qz = jnp.concatenate(
            [jnp.where(mine, q[u * L:(u + 1) * L,
                               (p * GROUP + g) * 2 * dh:(p * GROUP + g + 1) * 2 * dh], zero)
             for g in range(GROUP)], axis=0)
        return lax.dot_general(k_pair, qz, (((1,), (1,)), ((), ())),
                               preferred_element_type=F32)

    units = [(u, h) for u in range(U) for h in range(KV_HEADS)]
    o_rows = []
    st_next = scores(*units[0])
    for idx, (u, h) in enumerate(units):
        st = st_next
        if idx + 1 < len(units):
            st_next = scores(*units[idx + 1])
        m2 = m2_ref[kinds[u]]
        es, sinks = [], []
        for g in range(GROUP):
            hq = h * GROUP + g
            sk2 = jnp.full((1, L), sink_ref[hq], F32) * LOG2E
            l2 = st[:, g * L:(g + 1) * L] * m2 + b2_ref[kinds[u], hq]
            m = jnp.maximum(jnp.max(l2, axis=0, keepdims=True), sk2)
            es.append(jnp.exp2(l2 - m).astype(BF16))
            sinks.append(jnp.exp2(sk2 - m))
        et = jnp.concatenate(es, axis=1)
        va = jnp.concatenate([vt[h * dh:(h + 1) * dh, u * L:(u + 3) * L], ones], axis=0)
        ot = jnp.dot(va, et, preferred_element_type=F32)
        den = ot[dh:dh + 1, :] + jnp.concatenate(sinks, axis=1)
        ot = ot[:dh, :] * (1.0 / den)
        o_rows += [ot[:, g * L:(g + 1) * L] for g in range(GROUP)]
        if h == KV_HEADS - 1:
            ot_all = jnp.concatenate(o_rows, axis=0)
            o_ref[0, u * L:(u + 1) * L, :] = ot_all.T.astype(BF16)
            o_rows = []


def _attention(q, k, vt, rel_bias, sink, cast_jobs):
    bsz, seq, dq = q.shape
    dkv = k.shape[2]
    L = ATT_BLOCK
    nb = seq // L
    U = ATT_STEP_BLOCKS
    ns = nb // U
    smem = pl.BlockSpec(memory_space=pltpu.SMEM)
    prev = lambda i: jnp.maximum(U * i - 1, 0)
    nxt = lambda i: jnp.minimum(U * i + U, nb - 1)
    c_in, c_out, c_shapes, c_args, plan = _cast_io(cast_jobs, bsz * ns, lambda b, i: b * ns + i)
    return pl.pallas_call(
        functools.partial(_att_kernel, cast_plan=plan),
        grid=(bsz, ns),
        in_specs=[
            smem, smem,
            pl.BlockSpec((1, U * L, dq), lambda b, i: (b, i, 0)),
            pl.BlockSpec((1, L, dkv), lambda b, i: (b, prev(i), 0)),
            pl.BlockSpec((1, U * L, dkv), lambda b, i: (b, i, 0)),
            pl.BlockSpec((1, L, dkv), lambda b, i: (b, nxt(i), 0)),
            pl.BlockSpec((1, dkv, L), lambda b, i: (b, 0, prev(i))),
            pl.BlockSpec((1, dkv, U * L), lambda b, i: (b, 0, i)),
            pl.BlockSpec((1, dkv, L), lambda b, i: (b, 0, nxt(i))),
        ] + c_in,
        out_specs=[pl.BlockSpec((1, U * L, dq), lambda b, i: (b, i, 0))] + c_out,
        out_shape=[jax.ShapeDtypeStruct((bsz, seq, dq), BF16)] + c_shapes,
        scratch_shapes=[
            pltpu.VMEM((3, Q_HEADS, 3 * L, L), F32),
            pltpu.VMEM((3, 3 * L, L), F32),
        ],
        compiler_params=_params(2),
        name="swa_attention",
    )(rel_bias, sink, q, k, k, k, vt, vt, vt, *c_args)


def _post_ffn_kernel(*refs, emit_next, cast_plan):
    (a_ref, x_ref, wo_ref, gpost_ref, g1_ref, gpre_ref, sc2_ref, sh2_ref,
     win_ref, wout_ref, gfpost_ref, g2_ref) = refs[:12]
    rest = list(refs[12:])
    if emit_next:
        gn_ref, scn_ref, shn_ref = rest[:3]
        rest = rest[3:]
    n_cast_in = _n_cast_inputs(cast_plan)
    cast_in, rest = rest[:n_cast_in], rest[n_cast_in:]
    xo_ref = rest.pop(0)
    if emit_next:
        ho_ref = rest.pop(0)
    cast_out = rest[:len(cast_plan)]
    hb_ref, x1_ref, acc_ref, y_ref = rest[len(cast_plan):]
    s = pl.program_id(0)
    n = pl.num_programs(0) - 2
    _run_casts(s, cast_plan, cast_in, cast_out)
    hidden = wout_ref.shape[0]

    def advance():
        hb_ref[1] = hb_ref[0]
        x1_ref[2] = x1_ref[1]
        x1_ref[1] = x1_ref[0]

    n_chunks = hidden // FFN_CHUNK
    n_slices = 8
    rows_of = lambda r: slice(r * (x_ref.shape[0] // n_slices), (r + 1) * (x_ref.shape[0] // n_slices))

    def a_matmul():
        y_ref[...] = jnp.dot(a_ref[...], wo_ref[...], preferred_element_type=F32)

    def after(v, dep):
        if dep is None:
            return v
        zero = (pltpu.bitcast(dep, jnp.uint32) >> 16) >> 16
        return v * pltpu.bitcast(zero | jnp.uint32(0x3F800000), F32)

    def a_norm(r, dep=None):
        rows = rows_of(r)
        x1 = x_ref[rows, :] + g1_ref[0] * _rms(after(y_ref[rows, :], dep), gpost_ref[...])
        h = _rms(x1, gpre_ref[...]) * (1.0 + sc2_ref[0]) + sh2_ref[0]
        hb_ref[0, rows, :] = h.astype(BF16)
        x1_ref[0, rows, :] = x1

    def b_chunk(c, hb, acc):
        lo = c * FFN_CHUNK
        a = jnp.dot(hb, win_ref[:, lo:lo + FFN_CHUNK], preferred_element_type=F32)
        b = jnp.dot(hb, win_ref[:, hidden + lo:hidden + lo + FFN_CHUNK],
                    preferred_element_type=F32)
        act = (_silu(a) * b).astype(BF16)
        return acc + jnp.dot(act, wout_ref[lo:lo + FFN_CHUNK, :], preferred_element_type=F32)

    def c_slice(r, dep=None):
        rows = rows_of(r)
        x2 = x1_ref[2, rows, :] + g2_ref[0] * _rms(after(acc_ref[rows, :], dep),
                                                   gfpost_ref[...])
        xo_ref[rows, :] = x2
        if emit_next:
            hn = _rms(x2, gn_ref[...]) * (1.0 + scn_ref[0]) + shn_ref[0]
            ho_ref[rows, :] = hn.astype(BF16)

    @pl.when(s == 0)
    def _():
        x1_ref[...] = jnp.zeros(x1_ref.shape, F32)
        acc_ref[...] = jnp.zeros(acc_ref.shape, F32)
        a_matmul()
        for r in range(n_slices):
            a_norm(r)

    @pl.when((s >= 1) & (s <= n))
    def _():
        advance()
        a_matmul()
        hb = hb_ref[1]
        acc = jnp.zeros((hb.shape[0], wout_ref.shape[1]), F32)
        for c in range(n_chunks):
            acc = b_chunk(c, hb, acc)
            dep = acc[0:1, :]
            if c < n_slices:
                c_slice(c, dep)
            if c >= n_chunks - n_slices - 1 and c < n_chunks - 1:
                a_norm(c - (n_chunks - n_slices - 1), dep)
        acc_ref[...] = acc

    @pl.when(s == n + 1)
    def _():
        advance()
        for r in range(n_slices):
            c_slice(r)


def _post_ffn(a2d, x2d, w_o, g_post, g_pre, w_in, w_out, g_fpost, modv, layer,
              seq, n_batch_rows, g_next=None, cast_jobs=()):
    t, d = x2d.shape
    kin = a2d.shape[1]
    tm = ROW_TILE
    tpb = seq // tm
    n = t // tm
    emit_next = g_next is not None
    tile_a = lambda s: jnp.minimum(s, n - 1)
    tile_c = lambda s: jnp.clip(s - 2, 0, n - 1)

    def vec(lyr, slot, tile_of):
        idx = _mod_spec(lyr, slot, tpb, n_batch_rows)
        return pl.BlockSpec((1, 1, d), lambda s: idx(tile_of(s)))
    row = lambda width, tile_of: pl.BlockSpec((tm, width), lambda s: (tile_of(s), 0))
    in_specs = [
        row(kin, tile_a), row(d, tile_a), _resident(w_o.shape), _resident((1, d)),
        vec(layer, 2, tile_a), _resident((1, d)), vec(layer, 4, tile_a), vec(layer, 3, tile_a),
        _resident(w_in.shape), _resident(w_out.shape), _resident((1, d)), vec(layer, 5, tile_c),
    ]
    args = [a2d, x2d, w_o, g_post, modv, g_pre, modv, modv, w_in, w_out, g_fpost, modv]
    out_specs = [row(d, tile_c)]
    out_shape = [jax.ShapeDtypeStruct((t, d), F32)]
    if emit_next:
        in_specs += [_resident((1, d)), vec(layer + 1, 1, tile_c), vec(layer + 1, 0, tile_c)]
        args += [g_next, modv, modv]
        out_specs.append(row(d, tile_c))
        out_shape.append(jax.ShapeDtypeStruct((t, d), BF16))
    c_in, c_out, c_shapes, c_args, plan = _cast_io(cast_jobs, n, lambda s: s)
    return pl.pallas_call(
        functools.partial(_post_ffn_kernel, emit_next=emit_next, cast_plan=plan),
        grid=(n + 2,),
        in_specs=in_specs + c_in,
        out_specs=out_specs + c_out,
        out_shape=out_shape + c_shapes,
        scratch_shapes=[pltpu.VMEM((2, tm, d), BF16), pltpu.VMEM((3, tm, d), F32),
                        pltpu.VMEM((tm, d), F32), pltpu.VMEM((tm, d), F32)],
        compiler_params=_params(1),
        name="post_ffn_next" if emit_next else "post_ffn",
    )(*args, *c_args)


def _ret_in_kernel(h_ref, w_ref, cos_ref, sin_ref, zf_ref, zb_ref,
                   q_ref, k_ref, kzf_ref, kzb_ref, v_ref, sg_ref):
    hb = h_ref[...]
    tm = hb.shape[0]
    cos = cos_ref[...]
    sin = sin_ref[...]
    dk = w_ref.shape[0] // RET_HEADS
    half = dk // 2
    nqk = RET_HEADS * dk
    nv = (w_ref.shape[1] - 2 * nqk) // 2

    def project(col):
        return jnp.dot(hb, w_ref[:, col:col + dk], preferred_element_type=F32)

    def rotated(y, off):
        a = y[:, off:off + half] * cos
        b = y[:, half - off:dk - off] * sin
        return a - b if off == 0 else a + b

    for c0 in range(0, nv, dk):
        sg_ref[:, c0:c0 + dk] = _silu(project(2 * nqk + nv + c0))
    zf = zf_ref[...][None]
    zb = zb_ref[...][None]
    for h in range(RET_HEADS):
        yk = project(nqk + h * dk)
        for off in (0, half):
            lo = h * dk + off
            rk = rotated(yk, off)
            k_ref[:, lo:lo + half] = rk.astype(BF16)
            rk3 = rk.reshape(tm // RET_CHUNK, RET_CHUNK, half)
            kzf_ref[:, lo:lo + half] = (rk3 * zf[:, :, lo:lo + half]).reshape(tm, half).astype(BF16)
            kzb_ref[:, lo:lo + half] = (rk3 * zb[:, :, lo:lo + half]).reshape(tm, half).astype(BF16)
    for h in range(RET_HEADS):
        yq = project(h * dk)
        for off in (0, half):
            lo = h * dk + off
            q_ref[:, lo:lo + half] = rotated(yq, off).astype(BF16)
    for c0 in range(0, nv, dk):
        v_ref[:, c0:c0 + dk] = project(2 * nqk + c0).astype(BF16)


def _ret_in(h2d, w_in, cos, sin, zf, zb, seq):
    t, d = h2d.shape
    tm = ROW_TILE
    tpb = seq // tm
    nqk = RET_HEADS * (d // RET_HEADS)
    nv = (w_in.shape[1] - 2 * nqk) // 2
    row = lambda width: pl.BlockSpec((tm, width), lambda i: (i, 0))
    pos = pl.BlockSpec((tm, cos.shape[1]), lambda i: (i % tpb, 0))
    return pl.pallas_call(
        _ret_in_kernel,
        grid=(t // tm,),
        in_specs=[row(d), _resident(w_in.shape), pos, pos,
                  _resident(zf.shape), _resident(zb.shape)],
        out_specs=[row(nqk), row(nqk), row(nqk), row(nqk), row(nv), row(nv)],
        out_shape=[jax.ShapeDtypeStruct((t, nqk), BF16)] * 4
        + [jax.ShapeDtypeStruct((t, nv), BF16), jax.ShapeDtypeStruct((t, nv), F32)],
        compiler_params=_params(1),
        name="ret_in_proj",
    )(h2d, w_in, cos, sin, zf, zb)


def _state_update(st_ref, h, kz_h, v_h, cd):
    upd = lax.dot_general(kz_h, v_h, (((0,), (0,)), ((), ())), preferred_element_type=F32)
    st_ref[h] = st_ref[h] * cd + upd


def _ret_scan_kernel(cdf_ref, cdb_ref, q_ref, k_ref, kzf_ref, kzb_ref, v_ref, sg_ref,
                     xif_ref, xib_ref, dm_ref, o_ref, st_ref, snap_ref):
    j = pl.program_id(1)
    ns = pl.num_programs(1) // 2
    L = RET_CHUNK
    n_sub = RET_STEP_CHUNKS

    @pl.when((j == 0) | (j == ns))
    def _():
        st_ref[...] = jnp.zeros(st_ref.shape, F32)

    dk = q_ref.shape[2] // RET_HEADS
    dv = v_ref.shape[2] // RET_HEADS

    @pl.when(j < ns)
    def _():
        for ci in reversed(range(n_sub)):
            rows = slice(ci * L, (ci + 1) * L)
            c = (ns - 1 - j) * n_sub + ci
            for h in range(RET_HEADS):
                snap_ref[c, h] = st_ref[h].astype(BF16)
                _state_update(st_ref, h, kzb_ref[0, rows, h * dk:(h + 1) * dk],
                              v_ref[0, rows, h * dv:(h + 1) * dv], cdb_ref[h])

    @pl.when(j >= ns)
    def _():
        def scores(ci, h):
            rows = slice(ci * L, (ci + 1) * L)
            return lax.dot_general(q_ref[0, rows, h * dk:(h + 1) * dk],
                                   k_ref[0, rows, h * dk:(h + 1) * dk],
                                   (((1,), (1,)), ((), ())), preferred_element_type=F32)

        def finish(ci, h, s):
            rows = slice(ci * L, (ci + 1) * L)
            c = (j - ns) * n_sub + ci
            q_h = q_ref[0, rows, h * dk:(h + 1) * dk]
            v_h = v_ref[0, rows, h * dv:(h + 1) * dv]
            y = jnp.dot((s * dm_ref[h]).astype(BF16), v_h, preferred_element_type=F32)
            y = y + jnp.dot(q_h, st_ref[h].astype(BF16), preferred_element_type=F32) * xif_ref[h]
            y = y + jnp.dot(q_h, snap_ref[c, h], preferred_element_type=F32) * xib_ref[h]
            _state_update(st_ref, h, kzf_ref[0, rows, h * dk:(h + 1) * dk], v_h, cdf_ref[h])
            yn = y * lax.rsqrt(jnp.mean(y * y, axis=-1, keepdims=True) + EPS)
            o_ref[0, rows, h * dv:(h + 1) * dv] = (
                sg_ref[0, rows, h * dv:(h + 1) * dv] * yn).astype(BF16)

        units = [(ci, h) for ci in range(n_sub) for h in range(RET_HEADS)]
        ahead = scores(*units[0])
        for idx, (ci, h) in enumerate(units):
            s = ahead
            if idx + 1 < len(units):
                ahead = scores(*units[idx + 1])
            finish(ci, h, s)


def _retention(q, k, kzf, kzb, v, sg, xi_f, xi_b, dmat, cd_f, cd_b):
    bsz, seq, nqk = q.shape
    nv = v.shape[2]
    L = RET_CHUNK
    rows = RET_STEP_CHUNKS * L
    ns = seq // rows
    dk = nqk // RET_HEADS
    dv = nv // RET_HEADS
    smem = pl.BlockSpec(memory_space=pltpu.SMEM)
    both = lambda j: jnp.where(j < ns, ns - 1 - j, j - ns)
    bwd_only = lambda j: jnp.where(j < ns, ns - 1 - j, 0)
    fwd_only = lambda j: jnp.where(j < ns, 0, j - ns)
    blk = lambda width, at: pl.BlockSpec((1, rows, width), lambda b, j: (b, at(j), 0))
    return pl.pallas_call(
        _ret_scan_kernel,
        grid=(bsz, 2 * ns),
        in_specs=[smem, smem, blk(nqk, fwd_only), blk(nqk, fwd_only), blk(nqk, fwd_only),
                  blk(nqk, bwd_only), blk(nv, both), blk(nv, fwd_only),
                  _resident(xi_f.shape), _resident(xi_b.shape), _resident(dmat.shape)],
        out_specs=blk(nv, fwd_only),
        out_shape=jax.ShapeDtypeStruct((bsz, seq, nv), BF16),
        scratch_shapes=[pltpu.VMEM((RET_HEADS, dk, dv), F32),
                        pltpu.VMEM((seq // L, RET_HEADS, dk, dv), BF16)],
        compiler_params=_params(2),
        name="ret_scan",
    )(cd_f, cd_b, q, k, kzf, kzb, v, sg, xi_f, xi_b, dmat)


def _decay_tables(decay_fwd, decay_bwd, dk):
    L = RET_CHUNK
    lg_f = jax.nn.log_sigmoid(decay_fwd.astype(F32))
    lg_b = jax.nn.log_sigmoid(decay_bwd.astype(F32))
    idx = jnp.arange(L, dtype=F32)
    diff = idx[:, None] - idx[None, :]
    dm = jnp.where((diff >= 0)[None],
                   jnp.exp(lg_f[:, None, None] * jnp.maximum(diff, 0.0)[None]),
                   jnp.exp(lg_b[:, None, None] * jnp.maximum(-diff, 0.0)[None]))
    xi_f = jnp.exp(lg_f[:, None] * (idx + 1.0)[None])[:, :, None]
    xi_b = jnp.exp(lg_b[:, None] * (L - idx)[None])[:, :, None]
    zeta_f = jnp.exp(lg_f[:, None] * (L - 1.0 - idx)[None])
    zeta_b = jnp.exp(lg_b[:, None] * idx[None])
    zf = jnp.repeat(zeta_f.T, dk, axis=1)
    zb = jnp.repeat(zeta_b.T, dk, axis=1)
    return dm, xi_f, xi_b, zf, zb, jnp.exp(lg_f * L), jnp.exp(lg_b * L)


def _rope_tables(seq, dk):
    inv = ROPE_BASE ** (-jnp.arange(0, dk, 2, dtype=F32) / dk)
    ang = jnp.arange(seq, dtype=F32)[:, None] * inv[None]
    return jnp.cos(ang), jnp.sin(ang)


def kernel(x, c, rel_bias, att_w_qkv, att_w_o, att_sink, ret_w_in, ret_w_o, ret_decay_fwd,
           ret_decay_bwd, ada_w, ada_b, mix_norm_pre, mix_norm_post, ffn_norm_pre,
           ffn_norm_post, ffn_w_in, ffn_w_out):
    bsz, seq, d = x.shape
    t = bsz * seq
    assert seq % ROW_TILE == 0 and ROW_TILE % RET_CHUNK == 0 and d % RET_HEADS == 0
    assert seq % QKV_ROW_TILE == 0
    assert ada_w.shape[0] == 2 and ffn_w_out.shape[1] % FFN_CHUNK == 0
    rows = -(-bsz // 8) * 8
    vec = lambda a: a.reshape(1, d)

    c_pad = jnp.pad(c, ((0, rows - bsz), (0, 0)))
    mod = _ada_mod(c_pad, ada_w, ada_b)
    modv = mod.reshape(2 * rows * 6, 1, d)

    x2d = x.reshape(t, d)

    nq = Q_HEADS * HEAD_DIM
    nkv = KV_HEADS * HEAD_DIM
    heads = [(2 * p + e) * GROUP + g for p in range(KV_HEADS // 2) for g in range(GROUP)
             for e in range(2)]
    w_qkv = att_w_qkv[0].astype(BF16)
    w_q = w_qkv[:, :nq].reshape(d, Q_HEADS, HEAD_DIM)[:, jnp.array(heads)].reshape(d, nq)
    w_q = w_q * jnp.asarray(HEAD_DIM ** -0.5, BF16)
    w_k = w_qkv[:, nq:nq + nkv]
    w_vt = w_qkv[:, nq + nkv:].T
    q, k, vt = _qkv_proj(x2d, vec(mix_norm_pre[0]), modv, w_q, w_k, w_vt, seq, rows)
    att, w_ao, w_f0in, w_f0out = _attention(
        q.reshape(bsz, seq, nq), k.reshape(bsz, seq, nkv), vt, rel_bias.astype(F32),
        att_sink[0].astype(F32), [(att_w_o, 0, None), (ffn_w_in, 0, None), (ffn_w_out, 0, None)])

    dk = d // RET_HEADS
    nqk = RET_HEADS * dk
    kscale = jnp.concatenate([jnp.ones((nqk,), F32), jnp.full((nqk,), dk ** -0.5, F32),
                              jnp.ones((ret_w_in.shape[2] - 2 * nqk,), F32)])[None]
    x1, h1, w_rin, w_ro, w_f1in, w_f1out = _post_ffn(
        att.reshape(t, nq), x2d, w_ao, vec(mix_norm_post[0]), vec(ffn_norm_pre[0]), w_f0in,
        w_f0out, vec(ffn_norm_post[0]), modv, 0, seq, rows, g_next=vec(mix_norm_pre[1]),
        cast_jobs=[(ret_w_in, 0, kscale), (ret_w_o, 0, None), (ffn_w_in, 1, None),
                   (ffn_w_out, 1, None)])
    dm, xi_f, xi_b, zf, zb, cd_f, cd_b = _decay_tables(ret_decay_fwd[0], ret_decay_bwd[0], dk)
    cos, sin = _rope_tables(seq, dk)
    q, k, kzf, kzb, v, sg = _ret_in(h1, w_rin, cos, sin, zf, zb, seq)
    r3 = lambda a: a.reshape(bsz, seq, a.shape[1])
    gated = _retention(r3(q), r3(k), r3(kzf), r3(kzb), r3(v), r3(sg), xi_f, xi_b, dm, cd_f, cd_b)
    (x2,) = _post_ffn(gated.reshape(t, -1), x1, w_ro, vec(mix_norm_post[1]),
                      vec(ffn_norm_pre[1]), w_f1in, w_f1out, vec(ffn_norm_post[1]), modv, 1,
                      seq, rows)
    return x2.reshape(bsz, seq, d)
```

```python
import functools
import math

import jax
import jax.numpy as jnp
from jax import lax
from jax.experimental import pallas as pl
from jax.experimental.pallas import tpu as pltpu

F32 = jnp.float32
BF16 = jnp.bfloat16

EPS = 1e-6
NEG = -1e30
LOG2E = math.log2(math.e)

Q_HEADS = 16
KV_HEADS = 4
GROUP = Q_HEADS // KV_HEADS
HEAD_DIM = 64
ATT_BLOCK = 128
ATT_STEP_BLOCKS = 8
REL_BUCKETS = 32
RET_HEADS = 4
RET_CHUNK = 256
RET_STEP_CHUNKS = 2
ROPE_BASE = 10000.0
FFN_CHUNK = 256
ROW_TILE = 512
QKV_ROW_TILE = 1024
VMEM_LIMIT = 56 * 1024 * 1024


def _silu(x):
    hx = 0.5 * x
    return hx * jnp.tanh(hx) + hx


def _rms(xf, g):
    ms = jnp.mean(xf * xf, axis=-1, keepdims=True)
    return (xf * lax.rsqrt(ms + EPS)) * g


def _resident(shape):
    zeros = (0,) * len(shape)
    return pl.BlockSpec(shape, lambda *_: zeros, pipeline_mode=pl.Buffered(1))


def _params(n_axes, vmem=VMEM_LIMIT):
    return pltpu.CompilerParams(
        dimension_semantics=("arbitrary",) * n_axes, vmem_limit_bytes=vmem)


BF16_SUBLANES = 16


def _cast_plan(rows, n_steps):
    per = BF16_SUBLANES
    while rows % per or rows // per > n_steps:
        per += BF16_SUBLANES
    return per, rows // per


def _cast_io(jobs, n_steps, step_of):
    in_specs, out_specs, out_shapes, args, plan = [], [], [], [], []
    for w, layer, scale in jobs:
        _, rows, cols = w.shape
        per, n_cast = _cast_plan(rows, n_steps)
        blk = lambda *idx, n_cast=n_cast: jnp.minimum(step_of(*idx), n_cast - 1)
        in_specs.append(pl.BlockSpec(
            (1, per, cols), lambda *idx, blk=blk, layer=layer: (layer, blk(*idx), 0)))
        args.append(w)
        if scale is not None:
            in_specs.append(_resident(scale.shape))
            args.append(scale)
        out_specs.append(pl.BlockSpec((per, cols), lambda *idx, blk=blk: (blk(*idx), 0)))
        out_shapes.append(jax.ShapeDtypeStruct((rows, cols), BF16))
        plan.append((n_cast, scale is not None))
    return in_specs, out_specs, out_shapes, args, tuple(plan)


def _run_casts(step, plan, in_refs, out_refs):
    in_refs = list(in_refs)
    for (n_cast, scaled), dst in zip(plan, out_refs):
        src = in_refs.pop(0)
        scale = in_refs.pop(0) if scaled else None

        @pl.when(step < n_cast)
        def _(src=src, scale=scale, dst=dst):
            v = src[0]
            if scale is not None:
                v = v * scale[...]
            dst[...] = v.astype(BF16)


def _n_cast_inputs(plan):
    return sum(2 if scaled else 1 for _, scaled in plan)


def _ada_kernel(c_ref, w_ref, b_ref, o_ref):
    ca = _silu(c_ref[...])
    o_ref[0] = jnp.dot(ca.astype(BF16), w_ref[0].astype(BF16),
                       preferred_element_type=F32) + b_ref[0]


def _ada_mod(c_pad, ada_w, ada_b):
    depth, d, n = ada_w.shape
    rows = c_pad.shape[0]
    tn = 1536
    return pl.pallas_call(
        _ada_kernel,
        grid=(depth, n // tn),
        in_specs=[
            pl.BlockSpec((rows, d), lambda i, j: (0, 0)),
            pl.BlockSpec((1, d, tn), lambda i, j: (i, 0, j)),
            pl.BlockSpec((1, 1, tn), lambda i, j: (i, 0, j)),
        ],
        out_specs=pl.BlockSpec((1, rows, tn), lambda i, j: (i, 0, j)),
        out_shape=jax.ShapeDtypeStruct((depth, rows, n), F32),
        compiler_params=_params(2),
        name="ada_mod",
    )(c_pad, ada_w, ada_b.reshape(depth, 1, n))


def _mod_spec(layer, slot, tiles_per_batch, n_batch_rows):
    base = layer * n_batch_rows * 6 + slot

    def index(t):
        return (base + (t // tiles_per_batch) * 6, 0, 0)
    return index


def _qkv_kernel(x_ref, g_ref, sc_ref, sh_ref, wq_ref, wk_ref, wvt_ref, q_ref, k_ref, vt_ref):
    h = _rms(x_ref[...], g_ref[...]) * (1.0 + sc_ref[0]) + sh_ref[0]
    hb = h.astype(BF16)
    q_ref[...] = jnp.dot(hb, wq_ref[...], preferred_element_type=F32).astype(BF16)
    k_ref[...] = jnp.dot(hb, wk_ref[...], preferred_element_type=F32).astype(BF16)
    vt_ref[0] = lax.dot_general(wvt_ref[...], hb, (((1,), (1,)), ((), ())),
                                preferred_element_type=F32).astype(BF16)


def _qkv_proj(x2d, g_pre, modv, w_q, w_k, w_vt, seq, n_batch_rows):
    t, d = x2d.shape
    nq = w_q.shape[1]
    nkv = w_k.shape[1]
    tm = QKV_ROW_TILE
    tpb = seq // tm
    vec = lambda idx: pl.BlockSpec((1, 1, d), idx)
    return pl.pallas_call(
        _qkv_kernel,
        grid=(t // tm,),
        in_specs=[
            pl.BlockSpec((tm, d), lambda i: (i, 0)),
            _resident((1, d)),
            vec(_mod_spec(0, 1, tpb, n_batch_rows)),
            vec(_mod_spec(0, 0, tpb, n_batch_rows)),
            _resident(w_q.shape), _resident(w_k.shape), _resident(w_vt.shape),
        ],
        out_specs=[
            pl.BlockSpec((tm, nq), lambda i: (i, 0)),
            pl.BlockSpec((tm, nkv), lambda i: (i, 0)),
            pl.BlockSpec((1, nkv, tm), lambda i: (i // tpb, 0, i % tpb)),
        ],
        out_shape=[
            jax.ShapeDtypeStruct((t, nq), BF16),
            jax.ShapeDtypeStruct((t, nkv), BF16),
            jax.ShapeDtypeStruct((t // seq, nkv, seq), BF16),
        ],
        compiler_params=_params(1),
        name="qkv_proj",
    )(x2d, g_pre, modv, modv, w_q, w_k, w_vt)


def _att_tables(rb_ref, b2_ref, m2_ref):
    L = ATT_BLOCK
    rows = 64
    for r in range(3 * L // rows):
        j = lax.broadcasted_iota(jnp.int32, (rows, L), 0) + r * rows
        t = lax.broadcasted_iota(jnp.int32, (rows, L), 1)
        rel = j - L - t
        n = jnp.abs(rel)
        large = jnp.full((rows, L), 8, jnp.int32)
        for thr in (12, 16, 23, 32, 46, 64, 91):
            large = large + (n >= thr).astype(jnp.int32)
        bucket = jnp.where(rel > 0, 16, 0) + jnp.where(n < 8, n, large)
        in_win = n <= L
        visible = (in_win & (j >= L), in_win, in_win & (j < 2 * L))
        piece = slice(r * rows, (r + 1) * rows)
        for kind in range(3):
            m2_ref[kind, piece, :] = jnp.where(visible[kind], LOG2E, 0.0).astype(F32)

        def body(hq, carry):
            acc = jnp.zeros((rows, L), F32)
            for b in range(REL_BUCKETS):
                acc = jnp.where(bucket == b, rb_ref[b, hq], acc)
            acc = acc * LOG2E
            for kind in range(3):
                b2_ref[kind, hq, piece, :] = jnp.where(visible[kind], acc, NEG)
            return carry
        lax.fori_loop(0, Q_HEADS, body, 0)


def _att_kernel(*refs, cast_plan):
    rb_ref, sink_ref, q_ref, kp_ref, kc_ref, kn_ref, vp_ref, vc_ref, vn_ref = refs[:9]
    n_cast_in = _n_cast_inputs(cast_plan)
    cast_in = refs[9:9 + n_cast_in]
    o_ref = refs[9 + n_cast_in]
    cast_out = refs[10 + n_cast_in:10 + n_cast_in + len(cast_plan)]
    b2_ref, m2_ref = refs[10 + n_cast_in + len(cast_plan):]
    b = pl.program_id(0)
    i = pl.program_id(1)
    ns = pl.num_programs(1)
    _run_casts(b * ns + i, cast_plan, cast_in, cast_out)

    @pl.when((b == 0) & (i == 0))
    def _():
        _att_tables(rb_ref, b2_ref, m2_ref)

    L = ATT_BLOCK
    dh = HEAD_DIM
    U = ATT_STEP_BLOCKS
    q = q_ref[0]
    kb = jnp.concatenate([kp_ref[0], kc_ref[0], kn_ref[0]], axis=0)
    vt = jnp.concatenate([vp_ref[0], vc_ref[0], vn_ref[0]], axis=1)
    lane = lax.broadcasted_iota(jnp.int32, (L, 2 * dh), 1)
    zero = jnp.zeros((L, 2 * dh), BF16)
    ones = jnp.ones((BF16_SUBLANES, 3 * L), BF16)
    kinds = [1] * U
    kinds[0] = jnp.where(i == 0, 0, 1)
    kinds[U - 1] = jnp.where(i == ns - 1, 2, 1)

    def scores(u, h):
        p, e = divmod(h, 2)
        k_pair = kb[u * L:(u + 3) * L, p * 2 * dh:(p + 1) * 2 * dh]
        mine = (lane < dh) if e == 0 else (lane >= dh)
        qz = jnp.concatenate(
            [jnp.where(mine, q[u * L:(u + 1) * L,
                               (p * GROUP + g) * 2 * dh:(p * GROUP + g + 1) * 2 * dh], zero)
             for g in range(GROUP)], axis=0)
        return lax.dot_general(k_pair, qz, (((1,), (1,)), ((), ())),
                               preferred_element_type=F32)

    units = [(u, h) for u in range(U) for h in range(KV_HEADS)]
    o_rows = []
    st_next = scores(*units[0])
    for idx, (u, h) in enumerate(units):
        st = st_next
        if idx + 1 < len(units):
            st_next = scores(*units[idx + 1])
        m2 = m2_ref[kinds[u]]
        es, sinks = [], []
        for g in range(GROUP):
            hq = h * GROUP + g
            sk2 = jnp.full((1, L), sink_ref[hq], F32) * LOG2E
            l2 = st[:, g * L:(g + 1) * L] * m2 + b2_ref[kinds[u], hq]
            m = jnp.maximum(jnp.max(l2, axis=0, keepdims=True), sk2)
            es.append(jnp.exp2(l2 - m).astype(BF16))
            sinks.append(jnp.exp2(sk2 - m))
        et = jnp.concatenate(es, axis=1)
        va = jnp.concatenate([vt[h * dh:(h + 1) * dh, u * L:(u + 3) * L], ones], axis=0)
        ot = jnp.dot(va, et, preferred_element_type=F32)
        den = ot[dh:dh + 1, :] + jnp.concatenate(sinks, axis=1)
        ot = ot[:dh, :] * (1.0 / den)
        o_rows += [ot[:, g * L:(g + 1) * L] for g in range(GROUP)]
        if h == KV_HEADS - 1:
            ot_all = jnp.concatenate(o_rows, axis=0)
            o_ref[0, u * L:(u + 1) * L, :] = ot_all.T.astype(BF16)
            o_rows = []


def _attention(q, k, vt, rel_bias, sink, cast_jobs):
    bsz, seq, dq = q.shape
    dkv = k.shape[2]
    L = ATT_BLOCK
    nb = seq // L
    U = ATT_STEP_BLOCKS
    ns = nb // U
    smem = pl.BlockSpec(memory_space=pltpu.SMEM)
    prev = lambda i: jnp.maximum(U * i - 1, 0)
    nxt = lambda i: jnp.minimum(U * i + U, nb - 1)
    c_in, c_out, c_shapes, c_args, plan = _cast_io(cast_jobs, bsz * ns, lambda b, i: b * ns + i)
    return pl.pallas_call(
        functools.partial(_att_kernel, cast_plan=plan),
        grid=(bsz, ns),
        in_specs=[
            smem, smem,
            pl.BlockSpec((1, U * L, dq), lambda b, i: (b, i, 0)),
            pl.BlockSpec((1, L, dkv), lambda b, i: (b, prev(i), 0)),
            pl.BlockSpec((1, U * L, dkv), lambda b, i: (b, i, 0)),
            pl.BlockSpec((1, L, dkv), lambda b, i: (b, nxt(i), 0)),
            pl.BlockSpec((1, dkv, L), lambda b, i: (b, 0, prev(i))),
            pl.BlockSpec((1, dkv, U * L), lambda b, i: (b, 0, i)),
            pl.BlockSpec((1, dkv, L), lambda b, i: (b, 0, nxt(i))),
        ] + c_in,
        out_specs=[pl.BlockSpec((1, U * L, dq), lambda b, i: (b, i, 0))] + c_out,
        out_shape=[jax.ShapeDtypeStruct((bsz, seq, dq), BF16)] + c_shapes,
        scratch_shapes=[
            pltpu.VMEM((3, Q_HEADS, 3 * L, L), F32),
            pltpu.VMEM((3, 3 * L, L), F32),
        ],
        compiler_params=_params(2),
        name="swa_attention",
    )(rel_bias, sink, q, k, k, k, vt, vt, vt, *c_args)


def _post_ffn_kernel(*refs, emit_next, cast_plan):
    (a_ref, x_ref, wo_ref, gpost_ref, g1_ref, gpre_ref, sc2_ref, sh2_ref,
     win_ref, wout_ref, gfpost_ref, g2_ref) = refs[:12]
    rest = list(refs[12:])
    if emit_next:
        gn_ref, scn_ref, shn_ref = rest[:3]
        rest = rest[3:]
    n_cast_in = _n_cast_inputs(cast_plan)
    cast_in, rest = rest[:n_cast_in], rest[n_cast_in:]
    xo_ref = rest.pop(0)
    if emit_next:
        ho_ref = rest.pop(0)
    cast_out = rest[:len(cast_plan)]
    hb_ref, x1_ref, acc_ref, y_ref = rest[len(cast_plan):]
    s = pl.program_id(0)
    n = pl.num_programs(0) - 2
    _run_casts(s, cast_plan, cast_in, cast_out)
    hidden = wout_ref.shape[0]

    def advance():
        hb_ref[1] = hb_ref[0]
        x1_ref[2] = x1_ref[1]
        x1_ref[1] = x1_ref[0]

    n_chunks = hidden // FFN_CHUNK
    n_slices = 8
    rows_of = lambda r: slice(r * (x_ref.shape[0] // n_slices), (r + 1) * (x_ref.shape[0] // n_slices))

    def a_matmul():
        y_ref[...] = jnp.dot(a_ref[...], wo_ref[...], preferred_element_type=F32)

    def after(v, dep):
        if dep is None:
            return v
        zero = (pltpu.bitcast(dep, jnp.uint32) >> 16) >> 16
        return v * pltpu.bitcast(zero | jnp.uint32(0x3F800000), F32)

    def a_norm(r, dep=None):
        rows = rows_of(r)
        x1 = x_ref[rows, :] + g1_ref[0] * _rms(after(y_ref[rows, :], dep), gpost_ref[...])
        h = _rms(x1, gpre_ref[...]) * (1.0 + sc2_ref[0]) + sh2_ref[0]
        hb_ref[0, rows, :] = h.astype(BF16)
        x1_ref[0, rows, :] = x1

    def b_chunk(c, hb, acc):
        lo = c * FFN_CHUNK
        a = jnp.dot(hb, win_ref[:, lo:lo + FFN_CHUNK], preferred_element_type=F32)
        b = jnp.dot(hb, win_ref[:, hidden + lo:hidden + lo + FFN_CHUNK],
                    preferred_element_type=F32)
        act = (_silu(a) * b).astype(BF16)
        return acc + jnp.dot(act, wout_ref[lo:lo + FFN_CHUNK, :], preferred_element_type=F32)

    def c_slice(r, dep=None):
        rows = rows_of(r)
        x2 = x1_ref[2, rows, :] + g2_ref[0] * _rms(after(acc_ref[rows, :], dep),
                                                   gfpost_ref[...])
        xo_ref[rows, :] = x2
        if emit_next:
            hn = _rms(x2, gn_ref[...]) * (1.0 + scn_ref[0]) + shn_ref[0]
            ho_ref[rows, :] = hn.astype(BF16)

    @pl.when(s == 0)
    def _():
        x1_ref[...] = jnp.zeros(x1_ref.shape, F32)
        acc_ref[...] = jnp.zeros(acc_ref.shape, F32)
        a_matmul()
        for r in range(n_slices):
            a_norm(r)

    @pl.when((s >= 1) & (s <= n))
    def _():
        advance()
        a_matmul()
        hb = hb_ref[1]
        acc = jnp.zeros((hb.shape[0], wout_ref.shape[1]), F32)
        for c in range(n_chunks):
            acc = b_chunk(c, hb, acc)
            dep = acc[0:1, :]
            if c < n_slices:
                c_slice(c, dep)
            if c >= n_chunks - n_slices - 1 and c < n_chunks - 1:
                a_norm(c - (n_chunks - n_slices - 1), dep)
        acc_ref[...] = acc

    @pl.when(s == n + 1)
    def _():
        advance()
        for r in range(n_slices):
            c_slice(r)


def _post_ffn(a2d, x2d, w_o, g_post, g_pre, w_in, w_out, g_fpost, modv, layer,
              seq, n_batch_rows, g_next=None, cast_jobs=()):
    t, d = x2d.shape
    kin = a2d.shape[1]
    tm = ROW_TILE
    tpb = seq // tm
    n = t // tm
    emit_next = g_next is not None
    tile_a = lambda s: jnp.minimum(s, n - 1)
    tile_c = lambda s: jnp.clip(s - 2, 0, n - 1)

    def vec(lyr, slot, tile_of):
        idx = _mod_spec(lyr, slot, tpb, n_batch_rows)
        return pl.BlockSpec((1, 1, d), lambda s: idx(tile_of(s)))
    row = lambda width, tile_of: pl.BlockSpec((tm, width), lambda s: (tile_of(s), 0))
    in_specs = [
        row(kin, tile_a), row(d, tile_a), _resident(w_o.shape), _resident((1, d)),
        vec(layer, 2, tile_a), _resident((1, d)), vec(layer, 4, tile_a), vec(layer, 3, tile_a),
        _resident(w_in.shape), _resident(w_out.shape), _resident((1, d)), vec(layer, 5, tile_c),
    ]
    args = [a2d, x2d, w_o, g_post, modv, g_pre, modv, modv, w_in, w_out, g_fpost, modv]
    out_specs = [row(d, tile_c)]
    out_shape = [jax.ShapeDtypeStruct((t, d), F32)]
    if emit_next:
        in_specs += [_resident((1, d)), vec(layer + 1, 1, tile_c), vec(layer + 1, 0, tile_c)]
        args += [g_next, modv, modv]
        out_specs.append(row(d, tile_c))
        out_shape.append(jax.ShapeDtypeStruct((t, d), BF16))
    c_in, c_out, c_shapes, c_args, plan = _cast_io(cast_jobs, n, lambda s: s)
    return pl.pallas_call(
        functools.partial(_post_ffn_kernel, emit_next=emit_next, cast_plan=plan),
        grid=(n + 2,),
        in_specs=in_specs + c_in,
        out_specs=out_specs + c_out,
        out_shape=out_shape + c_shapes,
        scratch_shapes=[pltpu.VMEM((2, tm, d), BF16), pltpu.VMEM((3, tm, d), F32),
                        pltpu.VMEM((tm, d), F32), pltpu.VMEM((tm, d), F32)],
        compiler_params=_params(1),
        name="post_ffn_next" if emit_next else "post_ffn",
    )(*args, *c_args)


def _ret_in_kernel(h_ref, w_ref, cos_ref, sin_ref, zf_ref, zb_ref,
                   q_ref, k_ref, kzf_ref, kzb_ref, v_ref):
    hb = h_ref[...]
    tm = hb.shape[0]
    cos = cos_ref[...]
    sin = sin_ref[...]
    dk = w_ref.shape[0] // RET_HEADS
    half = dk // 2
    nqk = RET_HEADS * dk
    nv = w_ref.shape[1] - 2 * nqk

    def project(col):
        return jnp.dot(hb, w_ref[:, col:col + dk], preferred_element_type=F32)

    def rotated(y, off):
        a = y[:, off:off + half] * cos
        b = y[:, half - off:dk - off] * sin
        return a - b if off == 0 else a + b

    zf = zf_ref[...][None]
    zb = zb_ref[...][None]
    for h in range(RET_HEADS):
        yk = project(nqk + h * dk)
        for off in (0, half):
            lo = h * dk + off
            rk = rotated(yk, off)
            k_ref[:, lo:lo + half] = rk.astype(BF16)
            rk3 = rk.reshape(tm // RET_CHUNK, RET_CHUNK, half)
            kzf_ref[:, lo:lo + half] = (rk3 * zf[:, :, lo:lo + half]).reshape(tm, half).astype(BF16)
            kzb_ref[:, lo:lo + half] = (rk3 * zb[:, :, lo:lo + half]).reshape(tm, half).astype(BF16)
    for h in range(RET_HEADS):
        yq = project(h * dk)
        for off in (0, half):
            lo = h * dk + off
            q_ref[:, lo:lo + half] = rotated(yq, off).astype(BF16)
    for c0 in range(0, nv, dk):
        v_ref[:, c0:c0 + dk] = project(2 * nqk + c0).astype(BF16)


def _ret_in(h2d, w_in, cos, sin, zf, zb, seq):
    t, d = h2d.shape
    tm = ROW_TILE
    tpb = seq // tm
    nqk = RET_HEADS * (d // RET_HEADS)
    nv = (w_in.shape[1] - 2 * nqk) // 2
    row = lambda width: pl.BlockSpec((tm, width), lambda i: (i, 0))
    pos = pl.BlockSpec((tm, cos.shape[1]), lambda i: (i % tpb, 0))
    return pl.pallas_call(
        _ret_in_kernel,
        grid=(t // tm,),
        in_specs=[row(d), _resident((d, 2 * nqk + nv)), pos, pos,
                  _resident(zf.shape), _resident(zb.shape)],
        out_specs=[row(nqk), row(nqk), row(nqk), row(nqk), row(nv)],
        out_shape=[jax.ShapeDtypeStruct((t, nqk), BF16)] * 4
        + [jax.ShapeDtypeStruct((t, nv), BF16)],
        compiler_params=_params(1),
        name="ret_in_proj",
    )(h2d, w_in, cos, sin, zf, zb)


def _state_update(st_ref, h, kz_h, v_h, cd):
    upd = lax.dot_general(kz_h, v_h, (((0,), (0,)), ((), ())), preferred_element_type=F32)
    st_ref[h] = st_ref[h] * cd + upd


def _ret_scan_kernel(cdf_ref, cdb_ref, q_ref, k_ref, kzf_ref, kzb_ref, v_ref, h_ref, wg_ref,
                     xif_ref, xib_ref, dm_ref, o_ref, st_ref, snap_ref, sg_ref):
    j = pl.program_id(1)
    ns = pl.num_programs(1) // 2
    L = RET_CHUNK
    n_sub = RET_STEP_CHUNKS

    @pl.when((j == 0) | (j == ns))
    def _():
        st_ref[...] = jnp.zeros(st_ref.shape, F32)

    dk = q_ref.shape[2] // RET_HEADS
    dv = v_ref.shape[2] // RET_HEADS

    @pl.when(j < ns)
    def _():
        for ci in reversed(range(n_sub)):
            rows = slice(ci * L, (ci + 1) * L)
            c = (ns - 1 - j) * n_sub + ci
            for h in range(RET_HEADS):
                snap_ref[c, h] = st_ref[h].astype(BF16)
                _state_update(st_ref, h, kzb_ref[0, rows, h * dk:(h + 1) * dk],
                              v_ref[0, rows, h * dv:(h + 1) * dv], cdb_ref[h])

    @pl.when(j >= ns)
    def _():
        sg_ref[...] = _silu(jnp.dot(h_ref[0], wg_ref[...], preferred_element_type=F32))

        def scores(ci, h):
            rows = slice(ci * L, (ci + 1) * L)
            return lax.dot_general(q_ref[0, rows, h * dk:(h + 1) * dk],
                                   k_ref[0, rows, h * dk:(h + 1) * dk],
                                   (((1,), (1,)), ((), ())), preferred_element_type=F32)

        def finish(ci, h, s):
            rows = slice(ci * L, (ci + 1) * L)
            c = (j - ns) * n_sub + ci
            q_h = q_ref[0, rows, h * dk:(h + 1) * dk]
            v_h = v_ref[0, rows, h * dv:(h + 1) * dv]
            y = jnp.dot((s * dm_ref[h]).astype(BF16), v_h, preferred_element_type=F32)
            y = y + jnp.dot(q_h, st_ref[h].astype(BF16), preferred_element_type=F32) * xif_ref[h]
            y = y + jnp.dot(q_h, snap_ref[c, h], preferred_element_type=F32) * xib_ref[h]
            _state_update(st_ref, h, kzf_ref[0, rows, h * dk:(h + 1) * dk], v_h, cdf_ref[h])
            yn = y * lax.rsqrt(jnp.mean(y * y, axis=-1, keepdims=True) + EPS)
            o_ref[0, rows, h * dv:(h + 1) * dv] = (
                sg_ref[rows, h * dv:(h + 1) * dv] * yn).astype(BF16)

        units = [(ci, h) for ci in range(n_sub) for h in range(RET_HEADS)]
        ahead = scores(*units[0])
        for idx, (ci, h) in enumerate(units):
            s = ahead
            if idx + 1 < len(units):
                ahead = scores(*units[idx + 1])
            finish(ci, h, s)


def _retention(q, k, kzf, kzb, v, h, w_in, xi_f, xi_b, dmat, cd_f, cd_b):
    bsz, seq, nqk = q.shape
    nv = v.shape[2]
    d = h.shape[2]
    gate_col = (2 * nqk + nv) // nv
    L = RET_CHUNK
    rows = RET_STEP_CHUNKS * L
    ns = seq // rows
    dk = nqk // RET_HEADS
    dv = nv // RET_HEADS
    smem = pl.BlockSpec(memory_space=pltpu.SMEM)
    both = lambda j: jnp.where(j < ns, ns - 1 - j, j - ns)
    bwd_only = lambda j: jnp.where(j < ns, ns - 1 - j, 0)
    fwd_only = lambda j: jnp.where(j < ns, 0, j - ns)
    blk = lambda width, at: pl.BlockSpec((1, rows, width), lambda b, j: (b, at(j), 0))
    return pl.pallas_call(
        _ret_scan_kernel,
        grid=(bsz, 2 * ns),
        in_specs=[smem, smem, blk(nqk, fwd_only), blk(nqk, fwd_only), blk(nqk, fwd_only),
                  blk(nqk, bwd_only), blk(nv, both), blk(d, fwd_only),
                  pl.BlockSpec((d, nv), lambda b, j: (0, gate_col),
                               pipeline_mode=pl.Buffered(1)),
                  _resident(xi_f.shape), _resident(xi_b.shape), _resident(dmat.shape)],
        out_specs=blk(nv, fwd_only),
        out_shape=jax.ShapeDtypeStruct((bsz, seq, nv), BF16),
        scratch_shapes=[pltpu.VMEM((RET_HEADS, dk, dv), F32),
                        pltpu.VMEM((seq // L, RET_HEADS, dk, dv), BF16),
                        pltpu.VMEM((rows, nv), F32)],
        compiler_params=_params(2),
        name="ret_scan",
    )(cd_f, cd_b, q, k, kzf, kzb, v, h, w_in, xi_f, xi_b, dmat)


def _decay_tables(decay_fwd, decay_bwd, dk):
    L = RET_CHUNK
    lg_f = jax.nn.log_sigmoid(decay_fwd.astype(F32))
    lg_b = jax.nn.log_sigmoid(decay_bwd.astype(F32))
    idx = jnp.arange(L, dtype=F32)
    diff = idx[:, None] - idx[None, :]
    dm = jnp.where((diff >= 0)[None],
                   jnp.exp(lg_f[:, None, None] * jnp.maximum(diff, 0.0)[None]),
                   jnp.exp(lg_b[:, None, None] * jnp.maximum(-diff, 0.0)[None]))
    xi_f = jnp.exp(lg_f[:, None] * (idx + 1.0)[None])[:, :, None]
    xi_b = jnp.exp(lg_b[:, None] * (L - idx)[None])[:, :, None]
    zeta_f = jnp.exp(lg_f[:, None] * (L - 1.0 - idx)[None])
    zeta_b = jnp.exp(lg_b[:, None] * idx[None])
    zf = jnp.repeat(zeta_f.T, dk, axis=1)
    zb = jnp.repeat(zeta_b.T, dk, axis=1)
    return dm, xi_f, xi_b, zf, zb, jnp.exp(lg_f * L), jnp.exp(lg_b * L)


def _rope_tables(seq, dk):
    inv = ROPE_BASE ** (-jnp.arange(0, dk, 2, dtype=F32) / dk)
    ang = jnp.arange(seq, dtype=F32)[:, None] * inv[None]
    return jnp.cos(ang), jnp.sin(ang)


def kernel(x, c, rel_bias, att_w_qkv, att_w_o, att_sink, ret_w_in, ret_w_o, ret_decay_fwd,
           ret_decay_bwd, ada_w, ada_b, mix_norm_pre, mix_norm_post, ffn_norm_pre,
           ffn_norm_post, ffn_w_in, ffn_w_out):
    bsz, seq, d = x.shape
    t = bsz * seq
    assert seq % ROW_TILE == 0 and ROW_TILE % RET_CHUNK == 0 and d % RET_HEADS == 0
    assert seq % QKV_ROW_TILE == 0
    assert ada_w.shape[0] == 2 and ffn_w_out.shape[1] % FFN_CHUNK == 0
    rows = -(-bsz // 8) * 8
    vec = lambda a: a.reshape(1, d)

    c_pad = jnp.pad(c, ((0, rows - bsz), (0, 0)))
    mod = _ada_mod(c_pad, ada_w, ada_b)
    modv = mod.reshape(2 * rows * 6, 1, d)

    x2d = x.reshape(t, d)

    nq = Q_HEADS * HEAD_DIM
    nkv = KV_HEADS * HEAD_DIM
    heads = [(2 * p + e) * GROUP + g for p in range(KV_HEADS // 2) for g in range(GROUP)
             for e in range(2)]
    w_qkv = att_w_qkv[0].astype(BF16)
    w_q = w_qkv[:, :nq].reshape(d, Q_HEADS, HEAD_DIM)[:, jnp.array(heads)].reshape(d, nq)
    w_q = w_q * jnp.asarray(HEAD_DIM ** -0.5, BF16)
    w_k = w_qkv[:, nq:nq + nkv]
    w_vt = w_qkv[:, nq + nkv:].T
    q, k, vt = _qkv_proj(x2d, vec(mix_norm_pre[0]), modv, w_q, w_k, w_vt, seq, rows)
    att, w_ao, w_f0in, w_f0out = _attention(
        q.reshape(bsz, seq, nq), k.reshape(bsz, seq, nkv), vt, rel_bias.astype(F32),
        att_sink[0].astype(F32), [(att_w_o, 0, None), (ffn_w_in, 0, None), (ffn_w_out, 0, None)])

    dk = d // RET_HEADS
    nqk = RET_HEADS * dk
    kscale = jnp.concatenate([jnp.ones((nqk,), F32), jnp.full((nqk,), dk ** -0.5, F32),
                              jnp.ones((ret_w_in.shape[2] - 2 * nqk,), F32)])[None]
    x1, h1, w_rin, w_ro, w_f1in, w_f1out = _post_ffn(
        att.reshape(t, nq), x2d, w_ao, vec(mix_norm_post[0]), vec(ffn_norm_pre[0]), w_f0in,
        w_f0out, vec(ffn_norm_post[0]), modv, 0, seq, rows, g_next=vec(mix_norm_pre[1]),
        cast_jobs=[(ret_w_in, 0, kscale), (ret_w_o, 0, None), (ffn_w_in, 1, None),
                   (ffn_w_out, 1, None)])
    dm, xi_f, xi_b, zf, zb, cd_f, cd_b = _decay_tables(ret_decay_fwd[0], ret_decay_bwd[0], dk)
    cos, sin = _rope_tables(seq, dk)
    q, k, kzf, kzb, v = _ret_in(h1, w_rin, cos, sin, zf, zb, seq)
    r3 = lambda a: a.reshape(bsz, seq, a.shape[1])
    gated = _retention(r3(q), r3(k), r3(kzf), r3(kzb), r3(v), r3(h1), w_rin, xi_f, xi_b, dm,
                       cd_f, cd_b)
    (x2,) = _post_ffn(gated.reshape(t, -1), x1, w_ro, vec(mix_norm_post[1]),
                      vec(ffn_norm_pre[1]), w_f1in, w_f1out, vec(ffn_norm_post[1]), modv, 1,
                      seq, rows)
    return x2.reshape(bsz, seq, d)
```

```python
import functools
import math

import jax
import jax.numpy as jnp
from jax import lax
from jax.experimental import pallas as pl
from jax.experimental.pallas import tpu as pltpu

F32 = jnp.float32
BF16 = jnp.bfloat16

EPS = 1e-6
NEG = -1e30
LOG2E = math.log2(math.e)

Q_HEADS = 16
KV_HEADS = 4
GROUP = Q_HEADS // KV_HEADS
HEAD_DIM = 64
ATT_BLOCK = 128
ATT_STEP_BLOCKS = 8
REL_BUCKETS = 32
RET_HEADS = 4
RET_CHUNK = 256
RET_STEP_CHUNKS = 2
ROPE_BASE = 10000.0
FFN_CHUNK = 256
ROW_TILE = 512
PROJ_ROW_TILE = 1024
VMEM_LIMIT = 56 * 1024 * 1024


def _silu(x):
    hx = 0.5 * x
    return hx * jnp.tanh(hx) + hx


def _rms(xf, g):
    ms = jnp.mean(xf * xf, axis=-1, keepdims=True)
    return (xf * lax.rsqrt(ms + EPS)) * g


def _resident(shape):
    zeros = (0,) * len(shape)
    return pl.BlockSpec(shape, lambda *_: zeros, pipeline_mode=pl.Buffered(1))


def _params(n_axes, vmem=VMEM_LIMIT):
    return pltpu.CompilerParams(
        dimension_semantics=("arbitrary",) * n_axes, vmem_limit_bytes=vmem)


BF16_SUBLANES = 16


def _cast_plan(rows, n_steps):
    per = BF16_SUBLANES
    while rows % per or rows // per > n_steps:
        per += BF16_SUBLANES
    return per, rows // per


def _cast_io(jobs, n_steps, step_of):
    in_specs, out_specs, out_shapes, args, plan = [], [], [], [], []
    for w, layer, scale in jobs:
        _, rows, cols = w.shape
        per, n_cast = _cast_plan(rows, n_steps)
        blk = lambda *idx, n_cast=n_cast: jnp.minimum(step_of(*idx), n_cast - 1)
        in_specs.append(pl.BlockSpec(
            (1, per, cols), lambda *idx, blk=blk, layer=layer: (layer, blk(*idx), 0)))
        args.append(w)
        if scale is not None:
            in_specs.append(_resident(scale.shape))
            args.append(scale)
        out_specs.append(pl.BlockSpec((per, cols), lambda *idx, blk=blk: (blk(*idx), 0)))
        out_shapes.append(jax.ShapeDtypeStruct((rows, cols), BF16))
        plan.append((n_cast, scale is not None))
    return in_specs, out_specs, out_shapes, args, tuple(plan)


def _run_casts(step, plan, in_refs, out_refs):
    in_refs = list(in_refs)
    for (n_cast, scaled), dst in zip(plan, out_refs):
        src = in_refs.pop(0)
        scale = in_refs.pop(0) if scaled else None

        @pl.when(step < n_cast)
        def _(src=src, scale=scale, dst=dst):
            v = src[0]
            if scale is not None:
                v = v * scale[...]
            dst[...] = v.astype(BF16)


def _n_cast_inputs(plan):
    return sum(2 if scaled else 1 for _, scaled in plan)


def _ada_kernel(c_ref, w_ref, b_ref, o_ref):
    ca = _silu(c_ref[...])
    o_ref[0] = jnp.dot(ca.astype(BF16), w_ref[0].astype(BF16),
                       preferred_element_type=F32) + b_ref[0]


def _ada_mod(c_pad, ada_w, ada_b):
    depth, d, n = ada_w.shape
    rows = c_pad.shape[0]
    tn = 1536
    return pl.pallas_call(
        _ada_kernel,
        grid=(depth, n // tn),
        in_specs=[
            pl.BlockSpec((rows, d), lambda i, j: (0, 0)),
            pl.BlockSpec((1, d, tn), lambda i, j: (i, 0, j)),
            pl.BlockSpec((1, 1, tn), lambda i, j: (i, 0, j)),
        ],
        out_specs=pl.BlockSpec((1, rows, tn), lambda i, j: (i, 0, j)),
        out_shape=jax.ShapeDtypeStruct((depth, rows, n), F32),
        compiler_params=_params(2),
        name="ada_mod",
    )(c_pad, ada_w, ada_b.reshape(depth, 1, n))


def _mod_spec(layer, slot, tiles_per_batch, n_batch_rows):
    base = layer * n_batch_rows * 6 + slot

    def index(t):
        return (base + (t // tiles_per_batch) * 6, 0, 0)
    return index


def _qkv_kernel(x_ref, g_ref, sc_ref, sh_ref, wq_ref, wk_ref, wvt_ref, q_ref, k_ref, vt_ref):
    h = _rms(x_ref[...], g_ref[...]) * (1.0 + sc_ref[0]) + sh_ref[0]
    hb = h.astype(BF16)
    q_ref[...] = jnp.dot(hb, wq_ref[...], preferred_element_type=F32).astype(BF16)
    k_ref[...] = jnp.dot(hb, wk_ref[...], preferred_element_type=F32).astype(BF16)
    vt_ref[0] = lax.dot_general(wvt_ref[...], hb, (((1,), (1,)), ((), ())),
                                preferred_element_type=F32).astype(BF16)


def _qkv_proj(x2d, g_pre, modv, w_q, w_k, w_vt, seq, n_batch_rows):
    t, d = x2d.shape
    nq = w_q.shape[1]
    nkv = w_k.shape[1]
    tm = PROJ_ROW_TILE
    tpb = seq // tm
    vec = lambda idx: pl.BlockSpec((1, 1, d), idx)
    return pl.pallas_call(
        _qkv_kernel,
        grid=(t // tm,),
        in_specs=[
            pl.BlockSpec((tm, d), lambda i: (i, 0)),
            _resident((1, d)),
            vec(_mod_spec(0, 1, tpb, n_batch_rows)),
            vec(_mod_spec(0, 0, tpb, n_batch_rows)),
            _resident(w_q.shape), _resident(w_k.shape), _resident(w_vt.shape),
        ],
        out_specs=[
            pl.BlockSpec((tm, nq), lambda i: (i, 0)),
            pl.BlockSpec((tm, nkv), lambda i: (i, 0)),
            pl.BlockSpec((1, nkv, tm), lambda i: (i // tpb, 0, i % tpb)),
        ],
        out_shape=[
            jax.ShapeDtypeStruct((t, nq), BF16),
            jax.ShapeDtypeStruct((t, nkv), BF16),
            jax.ShapeDtypeStruct((t // seq, nkv, seq), BF16),
        ],
        compiler_params=_params(1),
        name="qkv_proj",
    )(x2d, g_pre, modv, modv, w_q, w_k, w_vt)


def _att_tables(rb_ref, b2_ref, m2_ref):
    L = ATT_BLOCK
    rows = 64
    for r in range(3 * L // rows):
        j = lax.broadcasted_iota(jnp.int32, (rows, L), 0) + r * rows
        t = lax.broadcasted_iota(jnp.int32, (rows, L), 1)
        rel = j - L - t
        n = jnp.abs(rel)
        large = jnp.full((rows, L), 8, jnp.int32)
        for thr in (12, 16, 23, 32, 46, 64, 91):
            large = large + (n >= thr).astype(jnp.int32)
        bucket = jnp.where(rel > 0, 16, 0) + jnp.where(n < 8, n, large)
        in_win = n <= L
        visible = (in_win & (j >= L), in_win, in_win & (j < 2 * L))
        piece = slice(r * rows, (r + 1) * rows)
        for kind in range(3):
            m2_ref[kind, piece, :] = jnp.where(visible[kind], LOG2E, 0.0).astype(F32)

        def body(hq, carry):
            acc = jnp.zeros((rows, L), F32)
            for b in range(REL_BUCKETS):
                acc = jnp.where(bucket == b, rb_ref[b, hq], acc)
            acc = acc * LOG2E
            for kind in range(3):
                b2_ref[kind, hq, piece, :] = jnp.where(visible[kind], acc, NEG)
            return carry
        lax.fori_loop(0, Q_HEADS, body, 0)


def _att_kernel(*refs, cast_plan):
    rb_ref, sink_ref, q_ref, kp_ref, kc_ref, kn_ref, vp_ref, vc_ref, vn_ref = refs[:9]
    n_cast_in = _n_cast_inputs(cast_plan)
    cast_in = refs[9:9 + n_cast_in]
    o_ref = refs[9 + n_cast_in]
    cast_out = refs[10 + n_cast_in:10 + n_cast_in + len(cast_plan)]
    b2_ref, m2_ref = refs[10 + n_cast_in + len(cast_plan):]
    b = pl.program_id(0)
    i = pl.program_id(1)
    ns = pl.num_programs(1)
    _run_casts(b * ns + i, cast_plan, cast_in, cast_out)

    @pl.when((b == 0) & (i == 0))
    def _():
        _att_tables(rb_ref, b2_ref, m2_ref)

    L = ATT_BLOCK
    dh = HEAD_DIM
    U = ATT_STEP_BLOCKS
    q = q_ref[0]
    kb = jnp.concatenate([kp_ref[0], kc_ref[0], kn_ref[0]], axis=0)
    vt = jnp.concatenate([vp_ref[0], vc_ref[0], vn_ref[0]], axis=1)
    lane = lax.broadcasted_iota(jnp.int32, (L, 2 * dh), 1)
    zero = jnp.zeros((L, 2 * dh), BF16)
    ones = jnp.ones((BF16_SUBLANES, 3 * L), BF16)
    kinds = [1] * U
    kinds[0] = jnp.where(i == 0, 0, 1)
    kinds[U - 1] = jnp.where(i == ns - 1, 2, 1)

    def scores(u, h):
        p, e = divmod(h, 2)
        k_pair = kb[u * L:(u + 3) * L, p * 2 * dh:(p + 1) * 2 * dh]
        mine = (lane < dh) if e == 0 else (lane >= dh)
        qz = jnp.concatenate(
            [jnp.where(mine, q[u * L:(u + 1) * L,
                               (p * GROUP + g) * 2 * dh:(p * GROUP + g + 1) * 2 * dh], zero)
             for g in range(GROUP)], axis=0)
        return lax.dot_general(k_pair, qz, (((1,), (1,)), ((), ())),
                               preferred_element_type=F32)

    units = [(u, h) for u in range(U) for h in range(KV_HEADS)]
    o_rows = []
    st_next = scores(*units[0])
    for idx, (u, h) in enumerate(units):
        st = st_next
        if idx + 1 < len(units):
            st_next = scores(*units[idx + 1])
        m2 = m2_ref[kinds[u]]
        es, sinks = [], []
        for g in range(GROUP):
            hq = h * GROUP + g
            sk2 = jnp.full((1, L), sink_ref[hq], F32) * LOG2E
            l2 = st[:, g * L:(g + 1) * L] * m2 + b2_ref[kinds[u], hq]
            m = jnp.maximum(jnp.max(l2, axis=0, keepdims=True), sk2)
            es.append(jnp.exp2(l2 - m).astype(BF16))
            sinks.append(jnp.exp2(sk2 - m))
        et = jnp.concatenate(es, axis=1)
        va = jnp.concatenate([vt[h * dh:(h + 1) * dh, u * L:(u + 3) * L], ones], axis=0)
        ot = jnp.dot(va, et, preferred_element_type=F32)
        den = ot[dh:dh + 1, :] + jnp.concatenate(sinks, axis=1)
        ot = ot[:dh, :] * (1.0 / den)
        o_rows += [ot[:, g * L:(g + 1) * L] for g in range(GROUP)]
        if h == KV_HEADS - 1:
            ot_all = jnp.concatenate(o_rows, axis=0)
            o_ref[0, u * L:(u + 1) * L, :] = ot_all.T.astype(BF16)
            o_rows = []


def _attention(q, k, vt, rel_bias, sink, cast_jobs):
    bsz, seq, dq = q.shape
    dkv = k.shape[2]
    L = ATT_BLOCK
    nb = seq // L
    U = ATT_STEP_BLOCKS
    ns = nb // U
    smem = pl.BlockSpec(memory_space=pltpu.SMEM)
    prev = lambda i: jnp.maximum(U * i - 1, 0)
    nxt = lambda i: jnp.minimum(U * i + U, nb - 1)
    c_in, c_out, c_shapes, c_args, plan = _cast_io(cast_jobs, bsz * ns, lambda b, i: b * ns + i)
    return pl.pallas_call(
        functools.partial(_att_kernel, cast_plan=plan),
        grid=(bsz, ns),
        in_specs=[
            smem, smem,
            pl.BlockSpec((1, U * L, dq), lambda b, i: (b, i, 0)),
            pl.BlockSpec((1, L, dkv), lambda b, i: (b, prev(i), 0)),
            pl.BlockSpec((1, U * L, dkv), lambda b, i: (b, i, 0)),
            pl.BlockSpec((1, L, dkv), lambda b, i: (b, nxt(i), 0)),
            pl.BlockSpec((1, dkv, L), lambda b, i: (b, 0, prev(i))),
            pl.BlockSpec((1, dkv, U * L), lambda b, i: (b, 0, i)),
            pl.BlockSpec((1, dkv, L), lambda b, i: (b, 0, nxt(i))),
        ] + c_in,
        out_specs=[pl.BlockSpec((1, U * L, dq), lambda b, i: (b, i, 0))] + c_out,
        out_shape=[jax.ShapeDtypeStruct((bsz, seq, dq), BF16)] + c_shapes,
        scratch_shapes=[
            pltpu.VMEM((3, Q_HEADS, 3 * L, L), F32),
            pltpu.VMEM((3, 3 * L, L), F32),
        ],
        compiler_params=_params(2),
        name="swa_attention",
    )(rel_bias, sink, q, k, k, k, vt, vt, vt, *c_args)


def _post_ffn_kernel(*refs, emit_next, cast_plan):
    (a_ref, x_ref, wo_ref, gpost_ref, g1_ref, gpre_ref, sc2_ref, sh2_ref,
     win_ref, wout_ref, gfpost_ref, g2_ref) = refs[:12]
    rest = list(refs[12:])
    if emit_next:
        gn_ref, scn_ref, shn_ref = rest[:3]
        rest = rest[3:]
    n_cast_in = _n_cast_inputs(cast_plan)
    cast_in, rest = rest[:n_cast_in], rest[n_cast_in:]
    xo_ref = rest.pop(0)
    if emit_next:
        ho_ref = rest.pop(0)
    cast_out = rest[:len(cast_plan)]
    hb_ref, x1_ref, acc_ref, y_ref = rest[len(cast_plan):]
    s = pl.program_id(0)
    n = pl.num_programs(0) - 2
    _run_casts(s, cast_plan, cast_in, cast_out)
    hidden = wout_ref.shape[0]

    def advance():
        hb_ref[1] = hb_ref[0]
        x1_ref[2] = x1_ref[1]
        x1_ref[1] = x1_ref[0]

    n_chunks = hidden // FFN_CHUNK
    n_slices = 8
    rows_of = lambda r: slice(r * (x_ref.shape[0] // n_slices), (r + 1) * (x_ref.shape[0] // n_slices))

    def a_matmul():
        y_ref[...] = jnp.dot(a_ref[...], wo_ref[...], preferred_element_type=F32)

    def after(v, dep):
        if dep is None:
            return v
        zero = (pltpu.bitcast(dep, jnp.uint32) >> 16) >> 16
        return v * pltpu.bitcast(zero | jnp.uint32(0x3F800000), F32)

    def a_norm(r, dep=None):
        rows = rows_of(r)
        x1 = x_ref[rows, :] + g1_ref[0] * _rms(after(y_ref[rows, :], dep), gpost_ref[...])
        h = _rms(x1, gpre_ref[...]) * (1.0 + sc2_ref[0]) + sh2_ref[0]
        hb_ref[0, rows, :] = h.astype(BF16)
        x1_ref[0, rows, :] = x1

    def b_chunk(c, hb, acc):
        lo = c * FFN_CHUNK
        a = jnp.dot(hb, win_ref[:, lo:lo + FFN_CHUNK], preferred_element_type=F32)
        b = jnp.dot(hb, win_ref[:, hidden + lo:hidden + lo + FFN_CHUNK],
                    preferred_element_type=F32)
        act = (_silu(a) * b).astype(BF16)
        return acc + jnp.dot(act, wout_ref[lo:lo + FFN_CHUNK, :], preferred_element_type=F32)

    def c_slice(r, dep=None):
        rows = rows_of(r)
        x2 = x1_ref[2, rows, :] + g2_ref[0] * _rms(after(acc_ref[rows, :], dep),
                                                   gfpost_ref[...])
        xo_ref[rows, :] = x2
        if emit_next:
            hn = _rms(x2, gn_ref[...]) * (1.0 + scn_ref[0]) + shn_ref[0]
            ho_ref[rows, :] = hn.astype(BF16)

    @pl.when(s == 0)
    def _():
        x1_ref[...] = jnp.zeros(x1_ref.shape, F32)
        acc_ref[...] = jnp.zeros(acc_ref.shape, F32)
        a_matmul()
        for r in range(n_slices):
            a_norm(r)

    @pl.when((s >= 1) & (s <= n))
    def _():
        advance()
        a_matmul()
        hb = hb_ref[1]
        acc = jnp.zeros((hb.shape[0], wout_ref.shape[1]), F32)
        for c in range(n_chunks):
            acc = b_chunk(c, hb, acc)
            dep = acc[0:1, :]
            if c < n_slices:
                c_slice(c, dep)
            if c >= n_chunks - n_slices - 1 and c < n_chunks - 1:
                a_norm(c - (n_chunks - n_slices - 1), dep)
        acc_ref[...] = acc

    @pl.when(s == n + 1)
    def _():
        advance()
        for r in range(n_slices):
            c_slice(r)


def _post_ffn(a2d, x2d, w_o, g_post, g_pre, w_in, w_out, g_fpost, modv, layer,
              seq, n_batch_rows, g_next=None, cast_jobs=()):
    t, d = x2d.shape
    kin = a2d.shape[1]
    tm = ROW_TILE
    tpb = seq // tm
    n = t // tm
    emit_next = g_next is not None
    tile_a = lambda s: jnp.minimum(s, n - 1)
    tile_c = lambda s: jnp.clip(s - 2, 0, n - 1)

    def vec(lyr, slot, tile_of):
        idx = _mod_spec(lyr, slot, tpb, n_batch_rows)
        return pl.BlockSpec((1, 1, d), lambda s: idx(tile_of(s)))
    row = lambda width, tile_of: pl.BlockSpec((tm, width), lambda s: (tile_of(s), 0))
    in_specs = [
        row(kin, tile_a), row(d, tile_a), _resident(w_o.shape), _resident((1, d)),
        vec(layer, 2, tile_a), _resident((1, d)), vec(layer, 4, tile_a), vec(layer, 3, tile_a),
        _resident(w_in.shape), _resident(w_out.shape), _resident((1, d)), vec(layer, 5, tile_c),
    ]
    args = [a2d, x2d, w_o, g_post, modv, g_pre, modv, modv, w_in, w_out, g_fpost, modv]
    out_specs = [row(d, tile_c)]
    out_shape = [jax.ShapeDtypeStruct((t, d), F32)]
    if emit_next:
        in_specs += [_resident((1, d)), vec(layer + 1, 1, tile_c), vec(layer + 1, 0, tile_c)]
        args += [g_next, modv, modv]
        out_specs.append(row(d, tile_c))
        out_shape.append(jax.ShapeDtypeStruct((t, d), BF16))
    c_in, c_out, c_shapes, c_args, plan = _cast_io(cast_jobs, n, lambda s: s)
    return pl.pallas_call(
        functools.partial(_post_ffn_kernel, emit_next=emit_next, cast_plan=plan),
        grid=(n + 2,),
        in_specs=in_specs + c_in,
        out_specs=out_specs + c_out,
        out_shape=out_shape + c_shapes,
        scratch_shapes=[pltpu.VMEM((2, tm, d), BF16), pltpu.VMEM((3, tm, d), F32),
                        pltpu.VMEM((tm, d), F32), pltpu.VMEM((tm, d), F32)],
        compiler_params=_params(1),
        name="post_ffn_next" if emit_next else "post_ffn",
    )(*args, *c_args)


def _ret_in_kernel(h_ref, w_ref, cos_ref, sin_ref, zf_ref, zb_ref,
                   q_ref, k_ref, kzf_ref, kzb_ref, v_ref):
    hb = h_ref[...]
    tm = hb.shape[0]
    cos = cos_ref[...]
    sin = sin_ref[...]
    dk = w_ref.shape[0] // RET_HEADS
    half = dk // 2
    nqk = RET_HEADS * dk
    nv = w_ref.shape[1] - 2 * nqk

    def project(col):
        return jnp.dot(hb, w_ref[:, col:col + dk], preferred_element_type=F32)

    def rotated(y, off):
        a = y[:, off:off + half] * cos
        b = y[:, half - off:dk - off] * sin
        return a - b if off == 0 else a + b

    zf = zf_ref[...][None]
    zb = zb_ref[...][None]
    for h in range(RET_HEADS):
        yk = project(nqk + h * dk)
        for off in (0, half):
            lo = h * dk + off
            rk = rotated(yk, off)
            k_ref[:, lo:lo + half] = rk.astype(BF16)
            rk3 = rk.reshape(tm // RET_CHUNK, RET_CHUNK, half)
            kzf_ref[:, lo:lo + half] = (rk3 * zf[:, :, lo:lo + half]).reshape(tm, half).astype(BF16)
            kzb_ref[:, lo:lo + half] = (rk3 * zb[:, :, lo:lo + half]).reshape(tm, half).astype(BF16)
    for h in range(RET_HEADS):
        yq = project(h * dk)
        for off in (0, half):
            lo = h * dk + off
            q_ref[:, lo:lo + half] = rotated(yq, off).astype(BF16)
    for c0 in range(0, nv, dk):
        v_ref[:, c0:c0 + dk] = project(2 * nqk + c0).astype(BF16)


def _ret_in(h2d, w_in, cos, sin, zf, zb, seq):
    t, d = h2d.shape
    tm = PROJ_ROW_TILE
    tpb = seq // tm
    nqk = RET_HEADS * (d // RET_HEADS)
    nv = (w_in.shape[1] - 2 * nqk) // 2
    row = lambda width: pl.BlockSpec((tm, width), lambda i: (i, 0))
    pos = pl.BlockSpec((tm, cos.shape[1]), lambda i: (i % tpb, 0))
    return pl.pallas_call(
        _ret_in_kernel,
        grid=(t // tm,),
        in_specs=[row(d), _resident((d, 2 * nqk + nv)), pos, pos,
                  _resident(zf.shape), _resident(zb.shape)],
        out_specs=[row(nqk), row(nqk), row(nqk), row(nqk), row(nv)],
        out_shape=[jax.ShapeDtypeStruct((t, nqk), BF16)] * 4
        + [jax.ShapeDtypeStruct((t, nv), BF16)],
        compiler_params=_params(1),
        name="ret_in_proj",
    )(h2d, w_in, cos, sin, zf, zb)


def _state_update(st_ref, h, kz_h, v_h, cd):
    upd = lax.dot_general(kz_h, v_h, (((0,), (0,)), ((), ())), preferred_element_type=F32)
    st_ref[h] = st_ref[h] * cd + upd


def _ret_scan_kernel(cdf_ref, cdb_ref, q_ref, k_ref, kzf_ref, kzb_ref, v_ref, h_ref, wg_ref,
                     xif_ref, xib_ref, dm_ref, o_ref, st_ref, snap_ref, sg_ref):
    j = pl.program_id(1)
    ns = pl.num_programs(1) // 2
    L = RET_CHUNK
    n_sub = RET_STEP_CHUNKS

    @pl.when((j == 0) | (j == ns))
    def _():
        st_ref[...] = jnp.zeros(st_ref.shape, F32)

    dk = q_ref.shape[2] // RET_HEADS
    dv = v_ref.shape[2] // RET_HEADS

    @pl.when(j < ns)
    def _():
        for ci in reversed(range(n_sub)):
            rows = slice(ci * L, (ci + 1) * L)
            c = (ns - 1 - j) * n_sub + ci
            for h in range(RET_HEADS):
                snap_ref[c, h] = st_ref[h].astype(BF16)
                _state_update(st_ref, h, kzb_ref[0, rows, h * dk:(h + 1) * dk],
                              v_ref[0, rows, h * dv:(h + 1) * dv], cdb_ref[h])

    @pl.when(j >= ns)
    def _():
        sg_ref[...] = _silu(jnp.dot(h_ref[0], wg_ref[...], preferred_element_type=F32))

        def scores(ci, h):
            rows = slice(ci * L, (ci + 1) * L)
            return lax.dot_general(q_ref[0, rows, h * dk:(h + 1) * dk],
                                   k_ref[0, rows, h * dk:(h + 1) * dk],
                                   (((1,), (1,)), ((), ())), preferred_element_type=F32)

        def finish(ci, h, s):
            rows = slice(ci * L, (ci + 1) * L)
            c = (j - ns) * n_sub + ci
            q_h = q_ref[0, rows, h * dk:(h + 1) * dk]
            v_h = v_ref[0, rows, h * dv:(h + 1) * dv]
            y = jnp.dot((s * dm_ref[h]).astype(BF16), v_h, preferred_element_type=F32)
            y = y + jnp.dot(q_h, st_ref[h].astype(BF16), preferred_element_type=F32) * xif_ref[h]
            y = y + jnp.dot(q_h, snap_ref[c, h], preferred_element_type=F32) * xib_ref[h]
            _state_update(st_ref, h, kzf_ref[0, rows, h * dk:(h + 1) * dk], v_h, cdf_ref[h])
            yn = y * lax.rsqrt(jnp.mean(y * y, axis=-1, keepdims=True) + EPS)
            o_ref[0, rows, h * dv:(h + 1) * dv] = (
                sg_ref[rows, h * dv:(h + 1) * dv] * yn).astype(BF16)

        units = [(ci, h) for ci in range(n_sub) for h in range(RET_HEADS)]
        ahead = scores(*units[0])
        for idx, (ci, h) in enumerate(units):
            s = ahead
            if idx + 1 < len(units):
                ahead = scores(*units[idx + 1])
            finish(ci, h, s)


def _retention(q, k, kzf, kzb, v, h, w_in, xi_f, xi_b, dmat, cd_f, cd_b):
    bsz, seq, nqk = q.shape
    nv = v.shape[2]
    d = h.shape[2]
    gate_col = (2 * nqk + nv) // nv
    L = RET_CHUNK
    rows = RET_STEP_CHUNKS * L
    ns = seq // rows
    dk = nqk // RET_HEADS
    dv = nv // RET_HEADS
    smem = pl.BlockSpec(memory_space=pltpu.SMEM)
    both = lambda j: jnp.where(j < ns, ns - 1 - j, j - ns)
    bwd_only = lambda j: jnp.where(j < ns, ns - 1 - j, 0)
    fwd_only = lambda j: jnp.where(j < ns, 0, j - ns)
    blk = lambda width, at: pl.BlockSpec((1, rows, width), lambda b, j: (b, at(j), 0))
    return pl.pallas_call(
        _ret_scan_kernel,
        grid=(bsz, 2 * ns),
        in_specs=[smem, smem, blk(nqk, fwd_only), blk(nqk, fwd_only), blk(nqk, fwd_only),
                  blk(nqk, bwd_only), blk(nv, both), blk(d, fwd_only),
                  pl.BlockSpec((d, nv), lambda b, j: (0, gate_col),
                               pipeline_mode=pl.Buffered(1)),
                  _resident(xi_f.shape), _resident(xi_b.shape), _resident(dmat.shape)],
        out_specs=blk(nv, fwd_only),
        out_shape=jax.ShapeDtypeStruct((bsz, seq, nv), BF16),
        scratch_shapes=[pltpu.VMEM((RET_HEADS, dk, dv), F32),
                        pltpu.VMEM((seq // L, RET_HEADS, dk, dv), BF16),
                        pltpu.VMEM((rows, nv), F32)],
        compiler_params=_params(2),
        name="ret_scan",
    )(cd_f, cd_b, q, k, kzf, kzb, v, h, w_in, xi_f, xi_b, dmat)


def _decay_tables(decay_fwd, decay_bwd, dk):
    L = RET_CHUNK
    lg_f = jax.nn.log_sigmoid(decay_fwd.astype(F32))
    lg_b = jax.nn.log_sigmoid(decay_bwd.astype(F32))
    idx = jnp.arange(L, dtype=F32)
    diff = idx[:, None] - idx[None, :]
    dm = jnp.where((diff >= 0)[None],
                   jnp.exp(lg_f[:, None, None] * jnp.maximum(diff, 0.0)[None]),
                   jnp.exp(lg_b[:, None, None] * jnp.maximum(-diff, 0.0)[None]))
    xi_f = jnp.exp(lg_f[:, None] * (idx + 1.0)[None])[:, :, None]
    xi_b = jnp.exp(lg_b[:, None] * (L - idx)[None])[:, :, None]
    zeta_f = jnp.exp(lg_f[:, None] * (L - 1.0 - idx)[None])
    zeta_b = jnp.exp(lg_b[:, None] * idx[None])
    zf = jnp.repeat(zeta_f.T, dk, axis=1)
    zb = jnp.repeat(zeta_b.T, dk, axis=1)
    return dm, xi_f, xi_b, zf, zb, jnp.exp(lg_f * L), jnp.exp(lg_b * L)


def _rope_tables(seq, dk):
    inv = ROPE_BASE ** (-jnp.arange(0, dk, 2, dtype=F32) / dk)
    ang = jnp.arange(seq, dtype=F32)[:, None] * inv[None]
    return jnp.cos(ang), jnp.sin(ang)


def kernel(x, c, rel_bias, att_w_qkv, att_w_o, att_sink, ret_w_in, ret_w_o, ret_decay_fwd,
           ret_decay_bwd, ada_w, ada_b, mix_norm_pre, mix_norm_post, ffn_norm_pre,
           ffn_norm_post, ffn_w_in, ffn_w_out):
    bsz, seq, d = x.shape
    t = bsz * seq
    assert seq % ROW_TILE == 0 and ROW_TILE % RET_CHUNK == 0 and d % RET_HEADS == 0
    assert seq % PROJ_ROW_TILE == 0 and PROJ_ROW_TILE % RET_CHUNK == 0
    assert ada_w.shape[0] == 2 and ffn_w_out.shape[1] % FFN_CHUNK == 0
    rows = -(-bsz // 8) * 8
    vec = lambda a: a.reshape(1, d)

    c_pad = jnp.pad(c, ((0, rows - bsz), (0, 0)))
    mod = _ada_mod(c_pad, ada_w, ada_b)
    modv = mod.reshape(2 * rows * 6, 1, d)

    x2d = x.reshape(t, d)

    nq = Q_HEADS * HEAD_DIM
    nkv = KV_HEADS * HEAD_DIM
    heads = [(2 * p + e) * GROUP + g for p in range(KV_HEADS // 2) for g in range(GROUP)
             for e in range(2)]
    w_qkv = att_w_qkv[0].astype(BF16)
    w_q = w_qkv[:, :nq].reshape(d, Q_HEADS, HEAD_DIM)[:, jnp.array(heads)].reshape(d, nq)
    w_q = w_q * jnp.asarray(HEAD_DIM ** -0.5, BF16)
    w_k = w_qkv[:, nq:nq + nkv]
    w_vt = w_qkv[:, nq + nkv:].T
    q, k, vt = _qkv_proj(x2d, vec(mix_norm_pre[0]), modv, w_q, w_k, w_vt, seq, rows)
    att, w_ao, w_f0in, w_f0out = _attention(
        q.reshape(bsz, seq, nq), k.reshape(bsz, seq, nkv), vt, rel_bias.astype(F32),
        att_sink[0].astype(F32), [(att_w_o, 0, None), (ffn_w_in, 0, None), (ffn_w_out, 0, None)])

    dk = d // RET_HEADS
    nqk = RET_HEADS * dk
    kscale = jnp.concatenate([jnp.ones((nqk,), F32), jnp.full((nqk,), dk ** -0.5, F32),
                              jnp.ones((ret_w_in.shape[2] - 2 * nqk,), F32)])[None]
    x1, h1, w_rin, w_ro, w_f1in, w_f1out = _post_ffn(
        att.reshape(t, nq), x2d, w_ao, vec(mix_norm_post[0]), vec(ffn_norm_pre[0]), w_f0in,
        w_f0out, vec(ffn_norm_post[0]), modv, 0, seq, rows, g_next=vec(mix_norm_pre[1]),
        cast_jobs=[(ret_w_in, 0, kscale), (ret_w_o, 0, None), (ffn_w_in, 1, None),
                   (ffn_w_out, 1, None)])
    dm, xi_f, xi_b, zf, zb, cd_f, cd_b = _decay_tables(ret_decay_fwd[0], ret_decay_bwd[0], dk)
    cos, sin = _rope_tables(seq, dk)
    q, k, kzf, kzb, v = _ret_in(h1, w_rin, cos, sin, zf, zb, seq)
    r3 = lambda a: a.reshape(bsz, seq, a.shape[1])
    gated = _retention(r3(q), r3(k), r3(kzf), r3(kzb), r3(v), r3(h1), w_rin, xi_f, xi_b, dm,
                       cd_f, cd_b)
    (x2,) = _post_ffn(gated.reshape(t, -1), x1, w_ro, vec(mix_norm_post[1]),
                      vec(ffn_norm_pre[1]), w_f1in, w_f1out, vec(ffn_norm_post[1]), modv, 1,
                      seq, rows)
    return x2.reshape(bsz, seq, d)
```

```python
import functools
import math

import jax
import jax.numpy as jnp
from jax import lax
from jax.experimental import pallas as pl
from jax.experimental.pallas import tpu as pltpu

F32 = jnp.float32
BF16 = jnp.bfloat16

EPS = 1e-6
NEG = -1e30
LOG2E = math.log2(math.e)

Q_HEADS = 16
KV_HEADS = 4
GROUP = Q_HEADS // KV_HEADS
HEAD_DIM = 64
ATT_BLOCK = 128
ATT_STEP_BLOCKS = 16
REL_BUCKETS = 32
RET_HEADS = 4
RET_CHUNK = 256
RET_STEP_CHUNKS = 2
ROPE_BASE = 10000.0
FFN_CHUNK = 256
ROW_TILE = 512
PROJ_ROW_TILE = 1024
VMEM_LIMIT = 56 * 1024 * 1024


def _silu(x):
    hx = 0.5 * x
    return hx * jnp.tanh(hx) + hx


def _rms(xf, g):
    ms = jnp.mean(xf * xf, axis=-1, keepdims=True)
    return (xf * lax.rsqrt(ms + EPS)) * g


def _resident(shape):
    zeros = (0,) * len(shape)
    return pl.BlockSpec(shape, lambda *_: zeros, pipeline_mode=pl.Buffered(1))


def _params(n_axes, vmem=VMEM_LIMIT):
    return pltpu.CompilerParams(
        dimension_semantics=("arbitrary",) * n_axes, vmem_limit_bytes=vmem)


BF16_SUBLANES = 16


def _cast_plan(rows, n_steps):
    per = BF16_SUBLANES
    while rows % per or rows // per > n_steps:
        per += BF16_SUBLANES
    return per, rows // per


def _cast_io(jobs, n_steps, step_of):
    in_specs, out_specs, out_shapes, args, plan = [], [], [], [], []
    for w, layer, scale in jobs:
        _, rows, cols = w.shape
        per, n_cast = _cast_plan(rows, n_steps)
        blk = lambda *idx, n_cast=n_cast: jnp.minimum(step_of(*idx), n_cast - 1)
        in_specs.append(pl.BlockSpec(
            (1, per, cols), lambda *idx, blk=blk, layer=layer: (layer, blk(*idx), 0)))
        args.append(w)
        if scale is not None:
            in_specs.append(_resident(scale.shape))
            args.append(scale)
        out_specs.append(pl.BlockSpec((per, cols), lambda *idx, blk=blk: (blk(*idx), 0)))
        out_shapes.append(jax.ShapeDtypeStruct((rows, cols), BF16))
        plan.append((n_cast, scale is not None))
    return in_specs, out_specs, out_shapes, args, tuple(plan)


def _run_casts(step, plan, in_refs, out_refs):
    in_refs = list(in_refs)
    for (n_cast, scaled), dst in zip(plan, out_refs):
        src = in_refs.pop(0)
        scale = in_refs.pop(0) if scaled else None

        @pl.when(step < n_cast)
        def _(src=src, scale=scale, dst=dst):
            v = src[0]
            if scale is not None:
                v = v * scale[...]
            dst[...] = v.astype(BF16)


def _n_cast_inputs(plan):
    return sum(2 if scaled else 1 for _, scaled in plan)


def _ada_kernel(c_ref, w_ref, b_ref, o_ref):
    ca = _silu(c_ref[...])
    o_ref[0] = jnp.dot(ca.astype(BF16), w_ref[0].astype(BF16),
                       preferred_element_type=F32) + b_ref[0]


def _ada_mod(c_pad, ada_w, ada_b):
    depth, d, n = ada_w.shape
    rows = c_pad.shape[0]
    tn = 1536
    return pl.pallas_call(
        _ada_kernel,
        grid=(depth, n // tn),
        in_specs=[
            pl.BlockSpec((rows, d), lambda i, j: (0, 0)),
            pl.BlockSpec((1, d, tn), lambda i, j: (i, 0, j)),
            pl.BlockSpec((1, 1, tn), lambda i, j: (i, 0, j)),
        ],
        out_specs=pl.BlockSpec((1, rows, tn), lambda i, j: (i, 0, j)),
        out_shape=jax.ShapeDtypeStruct((depth, rows, n), F32),
        compiler_params=_params(2),
        name="ada_mod",
    )(c_pad, ada_w, ada_b.reshape(depth, 1, n))


def _mod_spec(layer, slot, tiles_per_batch, n_batch_rows):
    base = layer * n_batch_rows * 6 + slot

    def index(t):
        return (base + (t // tiles_per_batch) * 6, 0, 0)
    return index


def _qkv_kernel(x_ref, g_ref, sc_ref, sh_ref, wq_ref, wk_ref, wvt_ref, q_ref, k_ref, vt_ref):
    h = _rms(x_ref[...], g_ref[...]) * (1.0 + sc_ref[0]) + sh_ref[0]
    hb = h.astype(BF16)
    q_ref[...] = jnp.dot(hb, wq_ref[...], preferred_element_type=F32).astype(BF16)
    k_ref[...] = jnp.dot(hb, wk_ref[...], preferred_element_type=F32).astype(BF16)
    vt_ref[0] = lax.dot_general(wvt_ref[...], hb, (((1,), (1,)), ((), ())),
                                preferred_element_type=F32).astype(BF16)


def _qkv_proj(x2d, g_pre, modv, w_q, w_k, w_vt, seq, n_batch_rows):
    t, d = x2d.shape
    nq = w_q.shape[1]
    nkv = w_k.shape[1]
    tm = PROJ_ROW_TILE
    tpb = seq // tm
    vec = lambda idx: pl.BlockSpec((1, 1, d), idx)
    return pl.pallas_call(
        _qkv_kernel,
        grid=(t // tm,),
        in_specs=[
            pl.BlockSpec((tm, d), lambda i: (i, 0)),
            _resident((1, d)),
            vec(_mod_spec(0, 1, tpb, n_batch_rows)),
            vec(_mod_spec(0, 0, tpb, n_batch_rows)),
            _resident(w_q.shape), _resident(w_k.shape), _resident(w_vt.shape),
        ],
        out_specs=[
            pl.BlockSpec((tm, nq), lambda i: (i, 0)),
            pl.BlockSpec((tm, nkv), lambda i: (i, 0)),
            pl.BlockSpec((1, nkv, tm), lambda i: (i // tpb, 0, i % tpb)),
        ],
        out_shape=[
            jax.ShapeDtypeStruct((t, nq), BF16),
            jax.ShapeDtypeStruct((t, nkv), BF16),
            jax.ShapeDtypeStruct((t // seq, nkv, seq), BF16),
        ],
        compiler_params=_params(1),
        name="qkv_proj",
    )(x2d, g_pre, modv, modv, w_q, w_k, w_vt)


def _att_tables(rb_ref, b2_ref, m2_ref):
    L = ATT_BLOCK
    rows = 64
    for r in range(3 * L // rows):
        j = lax.broadcasted_iota(jnp.int32, (rows, L), 0) + r * rows
        t = lax.broadcasted_iota(jnp.int32, (rows, L), 1)
        rel = j - L - t
        n = jnp.abs(rel)
        large = jnp.full((rows, L), 8, jnp.int32)
        for thr in (12, 16, 23, 32, 46, 64, 91):
            large = large + (n >= thr).astype(jnp.int32)
        bucket = jnp.where(rel > 0, 16, 0) + jnp.where(n < 8, n, large)
        in_win = n <= L
        visible = (in_win & (j >= L), in_win, in_win & (j < 2 * L))
        piece = slice(r * rows, (r + 1) * rows)
        for kind in range(3):
            m2_ref[kind, piece, :] = jnp.where(visible[kind], LOG2E, 0.0).astype(F32)

        def body(hq, carry):
            acc = jnp.zeros((rows, L), F32)
            for b in range(REL_BUCKETS):
                acc = jnp.where(bucket == b, rb_ref[b, hq], acc)
            acc = acc * LOG2E
            for kind in range(3):
                b2_ref[kind, hq, piece, :] = jnp.where(visible[kind], acc, NEG)
            return carry
        lax.fori_loop(0, Q_HEADS, body, 0)


def _att_kernel(*refs, cast_plan):
    rb_ref, sink_ref, q_ref, kp_ref, kc_ref, kn_ref, vp_ref, vc_ref, vn_ref = refs[:9]
    n_cast_in = _n_cast_inputs(cast_plan)
    cast_in = refs[9:9 + n_cast_in]
    o_ref = refs[9 + n_cast_in]
    cast_out = refs[10 + n_cast_in:10 + n_cast_in + len(cast_plan)]
    b2_ref, m2_ref = refs[10 + n_cast_in + len(cast_plan):]
    b = pl.program_id(0)
    i = pl.program_id(1)
    ns = pl.num_programs(1)
    _run_casts(b * ns + i, cast_plan, cast_in, cast_out)

    @pl.when((b == 0) & (i == 0))
    def _():
        _att_tables(rb_ref, b2_ref, m2_ref)

    L = ATT_BLOCK
    dh = HEAD_DIM
    U = ATT_STEP_BLOCKS
    q = q_ref[0]
    kb = jnp.concatenate([kp_ref[0], kc_ref[0], kn_ref[0]], axis=0)
    vt = jnp.concatenate([vp_ref[0], vc_ref[0], vn_ref[0]], axis=1)
    lane = lax.broadcasted_iota(jnp.int32, (L, 2 * dh), 1)
    zero = jnp.zeros((L, 2 * dh), BF16)
    ones = jnp.ones((BF16_SUBLANES, 3 * L), BF16)
    kinds = [1] * U
    kinds[0] = jnp.where(i == 0, 0, 1)
    kinds[U - 1] = jnp.where(i == ns - 1, 2, 1)

    def scores(u, h):
        p, e = divmod(h, 2)
        k_pair = kb[u * L:(u + 3) * L, p * 2 * dh:(p + 1) * 2 * dh]
        mine = (lane < dh) if e == 0 else (lane >= dh)
        qz = jnp.concatenate(
            [jnp.where(mine, q[u * L:(u + 1) * L,
                               (p * GROUP + g) * 2 * dh:(p * GROUP + g + 1) * 2 * dh], zero)
             for g in range(GROUP)], axis=0)
        return lax.dot_general(k_pair, qz, (((1,), (1,)), ((), ())),
                               preferred_element_type=F32)

    units = [(u, h) for u in range(U) for h in range(KV_HEADS)]
    o_rows = []
    st_next = scores(*units[0])
    for idx, (u, h) in enumerate(units):
        st = st_next
        if idx + 1 < len(units):
            st_next = scores(*units[idx + 1])
        m2 = m2_ref[kinds[u]]
        es, sinks = [], []
        for g in range(GROUP):
            hq = h * GROUP + g
            sk2 = jnp.full((1, L), sink_ref[hq], F32) * LOG2E
            l2 = st[:, g * L:(g + 1) * L] * m2 + b2_ref[kinds[u], hq]
            m = jnp.maximum(jnp.max(l2, axis=0, keepdims=True), sk2)
            es.append(jnp.exp2(l2 - m).astype(BF16))
            sinks.append(jnp.exp2(sk2 - m))
        et = jnp.concatenate(es, axis=1)
        va = jnp.concatenate([vt[h * dh:(h + 1) * dh, u * L:(u + 3) * L], ones], axis=0)
        ot = jnp.dot(va, et, preferred_element_type=F32)
        den = ot[dh:dh + 1, :] + jnp.concatenate(sinks, axis=1)
        ot = ot[:dh, :] * (1.0 / den)
        o_rows += [ot[:, g * L:(g + 1) * L] for g in range(GROUP)]
        if h == KV_HEADS - 1:
            ot_all = jnp.concatenate(o_rows, axis=0)
            o_ref[0, u * L:(u + 1) * L, :] = ot_all.T.astype(BF16)
            o_rows = []


def _attention(q, k, vt, rel_bias, sink, cast_jobs):
    bsz, seq, dq = q.shape
    dkv = k.shape[2]
    L = ATT_BLOCK
    nb = seq // L
    U = ATT_STEP_BLOCKS
    assert U >= 2 and nb % U == 0
    ns = nb // U
    smem = pl.BlockSpec(memory_space=pltpu.SMEM)
    prev = lambda i: jnp.maximum(U * i - 1, 0)
    nxt = lambda i: jnp.minimum(U * i + U, nb - 1)
    c_in, c_out, c_shapes, c_args, plan = _cast_io(cast_jobs, bsz * ns, lambda b, i: b * ns + i)
    return pl.pallas_call(
        functools.partial(_att_kernel, cast_plan=plan),
        grid=(bsz, ns),
        in_specs=[
            smem, smem,
            pl.BlockSpec((1, U * L, dq), lambda b, i: (b, i, 0)),
            pl.BlockSpec((1, L, dkv), lambda b, i: (b, prev(i), 0)),
            pl.BlockSpec((1, U * L, dkv), lambda b, i: (b, i, 0)),
            pl.BlockSpec((1, L, dkv), lambda b, i: (b, nxt(i), 0)),
            pl.BlockSpec((1, dkv, L), lambda b, i: (b, 0, prev(i))),
            pl.BlockSpec((1, dkv, U * L), lambda b, i: (b, 0, i)),
            pl.BlockSpec((1, dkv, L), lambda b, i: (b, 0, nxt(i))),
        ] + c_in,
        out_specs=[pl.BlockSpec((1, U * L, dq), lambda b, i: (b, i, 0))] + c_out,
        out_shape=[jax.ShapeDtypeStruct((bsz, seq, dq), BF16)] + c_shapes,
        scratch_shapes=[
            pltpu.VMEM((3, Q_HEADS, 3 * L, L), F32),
            pltpu.VMEM((3, 3 * L, L), F32),
        ],
        compiler_params=_params(2),
        name="swa_attention",
    )(rel_bias, sink, q, k, k, k, vt, vt, vt, *c_args)


def _post_ffn_kernel(*refs, emit_next, cast_plan):
    (a_ref, x_ref, wo_ref, gpost_ref, g1_ref, gpre_ref, sc2_ref, sh2_ref,
     win_ref, wout_ref, gfpost_ref, g2_ref) = refs[:12]
    rest = list(refs[12:])
    if emit_next:
        gn_ref, scn_ref, shn_ref = rest[:3]
        rest = rest[3:]
    n_cast_in = _n_cast_inputs(cast_plan)
    cast_in, rest = rest[:n_cast_in], rest[n_cast_in:]
    xo_ref = rest.pop(0)
    if emit_next:
        ho_ref = rest.pop(0)
    cast_out = rest[:len(cast_plan)]
    hb_ref, x1_ref, acc_ref, y_ref = rest[len(cast_plan):]
    s = pl.program_id(0)
    n = pl.num_programs(0) - 2
    _run_casts(s, cast_plan, cast_in, cast_out)
    hidden = wout_ref.shape[0]

    def advance():
        hb_ref[1] = hb_ref[0]
        x1_ref[2] = x1_ref[1]
        x1_ref[1] = x1_ref[0]

    n_chunks = hidden // FFN_CHUNK
    n_slices = 8
    rows_of = lambda r: slice(r * (x_ref.shape[0] // n_slices), (r + 1) * (x_ref.shape[0] // n_slices))

    def a_matmul():
        y_ref[...] = jnp.dot(a_ref[...], wo_ref[...], preferred_element_type=F32)

    def after(v, dep):
        if dep is None:
            return v
        zero = (pltpu.bitcast(dep, jnp.uint32) >> 16) >> 16
        return v * pltpu.bitcast(zero | jnp.uint32(0x3F800000), F32)

    def a_norm(r, dep=None):
        rows = rows_of(r)
        x1 = x_ref[rows, :] + g1_ref[0] * _rms(after(y_ref[rows, :], dep), gpost_ref[...])
        h = _rms(x1, gpre_ref[...]) * (1.0 + sc2_ref[0]) + sh2_ref[0]
        hb_ref[0, rows, :] = h.astype(BF16)
        x1_ref[0, rows, :] = x1

    def b_chunk(c, hb, acc):
        lo = c * FFN_CHUNK
        a = jnp.dot(hb, win_ref[:, lo:lo + FFN_CHUNK], preferred_element_type=F32)
        b = jnp.dot(hb, win_ref[:, hidden + lo:hidden + lo + FFN_CHUNK],
                    preferred_element_type=F32)
        act = (_silu(a) * b).astype(BF16)
        return acc + jnp.dot(act, wout_ref[lo:lo + FFN_CHUNK, :], preferred_element_type=F32)

    def c_slice(r, dep=None):
        rows = rows_of(r)
        x2 = x1_ref[2, rows, :] + g2_ref[0] * _rms(after(acc_ref[rows, :], dep),
                                                   gfpost_ref[...])
        xo_ref[rows, :] = x2
        if emit_next:
            hn = _rms(x2, gn_ref[...]) * (1.0 + scn_ref[0]) + shn_ref[0]
            ho_ref[rows, :] = hn.astype(BF16)

    @pl.when(s == 0)
    def _():
        x1_ref[...] = jnp.zeros(x1_ref.shape, F32)
        acc_ref[...] = jnp.zeros(acc_ref.shape, F32)
        a_matmul()
        for r in range(n_slices):
            a_norm(r)

    @pl.when((s >= 1) & (s <= n))
    def _():
        advance()
        a_matmul()
        hb = hb_ref[1]
        acc = jnp.zeros((hb.shape[0], wout_ref.shape[1]), F32)
        for c in range(n_chunks):
            acc = b_chunk(c, hb, acc)
            dep = acc[0:1, :]
            if c < n_slices:
                c_slice(c, dep)
            if c >= n_chunks - n_slices - 1 and c < n_chunks - 1:
                a_norm(c - (n_chunks - n_slices - 1), dep)
        acc_ref[...] = acc

    @pl.when(s == n + 1)
    def _():
        advance()
        for r in range(n_slices):
            c_slice(r)


def _post_ffn(a2d, x2d, w_o, g_post, g_pre, w_in, w_out, g_fpost, modv, layer,
              seq, n_batch_rows, g_next=None, cast_jobs=()):
    t, d = x2d.shape
    kin = a2d.shape[1]
    tm = ROW_TILE
    tpb = seq // tm
    n = t // tm
    emit_next = g_next is not None
    tile_a = lambda s: jnp.minimum(s, n - 1)
    tile_c = lambda s: jnp.clip(s - 2, 0, n - 1)

    def vec(lyr, slot, tile_of):
        idx = _mod_spec(lyr, slot, tpb, n_batch_rows)
        return pl.BlockSpec((1, 1, d), lambda s: idx(tile_of(s)))
    row = lambda width, tile_of: pl.BlockSpec((tm, width), lambda s: (tile_of(s), 0))
    in_specs = [
        row(kin, tile_a), row(d, tile_a), _resident(w_o.shape), _resident((1, d)),
        vec(layer, 2, tile_a), _resident((1, d)), vec(layer, 4, tile_a), vec(layer, 3, tile_a),
        _resident(w_in.shape), _resident(w_out.shape), _resident((1, d)), vec(layer, 5, tile_c),
    ]
    args = [a2d, x2d, w_o, g_post, modv, g_pre, modv, modv, w_in, w_out, g_fpost, modv]
    out_specs = [row(d, tile_c)]
    out_shape = [jax.ShapeDtypeStruct((t, d), F32)]
    if emit_next:
        in_specs += [_resident((1, d)), vec(layer + 1, 1, tile_c), vec(layer + 1, 0, tile_c)]
        args += [g_next, modv, modv]
        out_specs.append(row(d, tile_c))
        out_shape.append(jax.ShapeDtypeStruct((t, d), BF16))
    c_in, c_out, c_shapes, c_args, plan = _cast_io(cast_jobs, n, lambda s: s)
    return pl.pallas_call(
        functools.partial(_post_ffn_kernel, emit_next=emit_next, cast_plan=plan),
        grid=(n + 2,),
        in_specs=in_specs + c_in,
        out_specs=out_specs + c_out,
        out_shape=out_shape + c_shapes,
        scratch_shapes=[pltpu.VMEM((2, tm, d), BF16), pltpu.VMEM((3, tm, d), F32),
                        pltpu.VMEM((tm, d), F32), pltpu.VMEM((tm, d), F32)],
        compiler_params=_params(1),
        name="post_ffn_next" if emit_next else "post_ffn",
    )(*args, *c_args)


def _ret_in_kernel(h_ref, w_ref, cos_ref, sin_ref, zf_ref, zb_ref,
                   q_ref, k_ref, kzf_ref, kzb_ref, v_ref):
    hb = h_ref[...]
    tm = hb.shape[0]
    cos = cos_ref[...]
    sin = sin_ref[...]
    dk = w_ref.shape[0] // RET_HEADS
    half = dk // 2
    nqk = RET_HEADS * dk
    nv = w_ref.shape[1] - 2 * nqk

    def project(col):
        return jnp.dot(hb, w_ref[:, col:col + dk], preferred_element_type=F32)

    def rotated(y, off):
        a = y[:, off:off + half] * cos
        b = y[:, half - off:dk - off] * sin
        return a - b if off == 0 else a + b

    zf = zf_ref[...][None]
    zb = zb_ref[...][None]
    for h in range(RET_HEADS):
        yk = project(nqk + h * dk)
        for off in (0, half):
            lo = h * dk + off
            rk = rotated(yk, off)
            k_ref[:, lo:lo + half] = rk.astype(BF16)
            rk3 = rk.reshape(tm // RET_CHUNK, RET_CHUNK, half)
            kzf_ref[:, lo:lo + half] = (rk3 * zf[:, :, lo:lo + half]).reshape(tm, half).astype(BF16)
            kzb_ref[:, lo:lo + half] = (rk3 * zb[:, :, lo:lo + half]).reshape(tm, half).astype(BF16)
    for h in range(RET_HEADS):
        yq = project(h * dk)
        for off in (0, half):
            lo = h * dk + off
            q_ref[:, lo:lo + half] = rotated(yq, off).astype(BF16)
    for c0 in range(0, nv, dk):
        v_ref[:, c0:c0 + dk] = project(2 * nqk + c0).astype(BF16)


def _ret_in(h2d, w_in, cos, sin, zf, zb, seq):
    t, d = h2d.shape
    tm = PROJ_ROW_TILE
    tpb = seq // tm
    nqk = RET_HEADS * (d // RET_HEADS)
    nv = (w_in.shape[1] - 2 * nqk) // 2
    row = lambda width: pl.BlockSpec((tm, width), lambda i: (i, 0))
    pos = pl.BlockSpec((tm, cos.shape[1]), lambda i: (i % tpb, 0))
    return pl.pallas_call(
        _ret_in_kernel,
        grid=(t // tm,),
        in_specs=[row(d), _resident((d, 2 * nqk + nv)), pos, pos,
                  _resident(zf.shape), _resident(zb.shape)],
        out_specs=[row(nqk), row(nqk), row(nqk), row(nqk), row(nv)],
        out_shape=[jax.ShapeDtypeStruct((t, nqk), BF16)] * 4
        + [jax.ShapeDtypeStruct((t, nv), BF16)],
        compiler_params=_params(1),
        name="ret_in_proj",
    )(h2d, w_in, cos, sin, zf, zb)


def _state_update(st_ref, h, kz_h, v_h, cd):
    upd = lax.dot_general(kz_h, v_h, (((0,), (0,)), ((), ())), preferred_element_type=F32)
    st_ref[h] = st_ref[h] * cd + upd


def _ret_scan_kernel(cdf_ref, cdb_ref, q_ref, k_ref, kzf_ref, kzb_ref, v_ref, h_ref, wg_ref,
                     xif_ref, xib_ref, dm_ref, o_ref, st_ref, snap_ref, sg_ref):
    j = pl.program_id(1)
    ns = pl.num_programs(1) // 2
    L = RET_CHUNK
    n_sub = RET_STEP_CHUNKS

    @pl.when((j == 0) | (j == ns))
    def _():
        st_ref[...] = jnp.zeros(st_ref.shape, F32)

    dk = q_ref.shape[2] // RET_HEADS
    dv = v_ref.shape[2] // RET_HEADS

    @pl.when(j < ns)
    def _():
        for ci in reversed(range(n_sub)):
            rows = slice(ci * L, (ci + 1) * L)
            c = (ns - 1 - j) * n_sub + ci
            for h in range(RET_HEADS):
                snap_ref[c, h] = st_ref[h].astype(BF16)
                _state_update(st_ref, h, kzb_ref[0, rows, h * dk:(h + 1) * dk],
                              v_ref[0, rows, h * dv:(h + 1) * dv], cdb_ref[h])

    @pl.when(j >= ns)
    def _():
        sg_ref[...] = _silu(jnp.dot(h_ref[0], wg_ref[...], preferred_element_type=F32))

        def scores(ci, h):
            rows = slice(ci * L, (ci + 1) * L)
            return lax.dot_general(q_ref[0, rows, h * dk:(h + 1) * dk],
                                   k_ref[0, rows, h * dk:(h + 1) * dk],
                                   (((1,), (1,)), ((), ())), preferred_element_type=F32)

        def finish(ci, h, s):
            rows = slice(ci * L, (ci + 1) * L)
            c = (j - ns) * n_sub + ci
            q_h = q_ref[0, rows, h * dk:(h + 1) * dk]
            v_h = v_ref[0, rows, h * dv:(h + 1) * dv]
            y = jnp.dot((s * dm_ref[h]).astype(BF16), v_h, preferred_element_type=F32)
            y = y + jnp.dot(q_h, st_ref[h].astype(BF16), preferred_element_type=F32) * xif_ref[h]
            y = y + jnp.dot(q_h, snap_ref[c, h], preferred_element_type=F32) * xib_ref[h]
            _state_update(st_ref, h, kzf_ref[0, rows, h * dk:(h + 1) * dk], v_h, cdf_ref[h])
            yn = y * lax.rsqrt(jnp.mean(y * y, axis=-1, keepdims=True) + EPS)
            o_ref[0, rows, h * dv:(h + 1) * dv] = (
                sg_ref[rows, h * dv:(h + 1) * dv] * yn).astype(BF16)

        units = [(ci, h) for ci in range(n_sub) for h in range(RET_HEADS)]
        ahead = scores(*units[0])
        for idx, (ci, h) in enumerate(units):
            s = ahead
            if idx + 1 < len(units):
                ahead = scores(*units[idx + 1])
            finish(ci, h, s)


def _retention(q, k, kzf, kzb, v, h, w_in, xi_f, xi_b, dmat, cd_f, cd_b):
    bsz, seq, nqk = q.shape
    nv = v.shape[2]
    d = h.shape[2]
    gate_col = (2 * nqk + nv) // nv
    L = RET_CHUNK
    rows = RET_STEP_CHUNKS * L
    ns = seq // rows
    dk = nqk // RET_HEADS
    dv = nv // RET_HEADS
    smem = pl.BlockSpec(memory_space=pltpu.SMEM)
    both = lambda j: jnp.where(j < ns, ns - 1 - j, j - ns)
    bwd_only = lambda j: jnp.where(j < ns, ns - 1 - j, 0)
    fwd_only = lambda j: jnp.where(j < ns, 0, j - ns)
    blk = lambda width, at: pl.BlockSpec((1, rows, width), lambda b, j: (b, at(j), 0))
    return pl.pallas_call(
        _ret_scan_kernel,
        grid=(bsz, 2 * ns),
        in_specs=[smem, smem, blk(nqk, fwd_only), blk(nqk, fwd_only), blk(nqk, fwd_only),
                  blk(nqk, bwd_only), blk(nv, both), blk(d, fwd_only),
                  pl.BlockSpec((d, nv), lambda b, j: (0, gate_col),
                               pipeline_mode=pl.Buffered(1)),
                  _resident(xi_f.shape), _resident(xi_b.shape), _resident(dmat.shape)],
        out_specs=blk(nv, fwd_only),
        out_shape=jax.ShapeDtypeStruct((bsz, seq, nv), BF16),
        scratch_shapes=[pltpu.VMEM((RET_HEADS, dk, dv), F32),
                        pltpu.VMEM((seq // L, RET_HEADS, dk, dv), BF16),
                        pltpu.VMEM((rows, nv), F32)],
        compiler_params=_params(2),
        name="ret_scan",
    )(cd_f, cd_b, q, k, kzf, kzb, v, h, w_in, xi_f, xi_b, dmat)


def _decay_tables(decay_fwd, decay_bwd, dk):
    L = RET_CHUNK
    lg_f = jax.nn.log_sigmoid(decay_fwd.astype(F32))
    lg_b = jax.nn.log_sigmoid(decay_bwd.astype(F32))
    idx = jnp.arange(L, dtype=F32)
    diff = idx[:, None] - idx[None, :]
    dm = jnp.where((diff >= 0)[None],
                   jnp.exp(lg_f[:, None, None] * jnp.maximum(diff, 0.0)[None]),
                   jnp.exp(lg_b[:, None, None] * jnp.maximum(-diff, 0.0)[None]))
    xi_f = jnp.exp(lg_f[:, None] * (idx + 1.0)[None])[:, :, None]
    xi_b = jnp.exp(lg_b[:, None] * (L - idx)[None])[:, :, None]
    zeta_f = jnp.exp(lg_f[:, None] * (L - 1.0 - idx)[None])
    zeta_b = jnp.exp(lg_b[:, None] * idx[None])
    zf = jnp.repeat(zeta_f.T, dk, axis=1)
    zb = jnp.repeat(zeta_b.T, dk, axis=1)
    return dm, xi_f, xi_b, zf, zb, jnp.exp(lg_f * L), jnp.exp(lg_b * L)


def _rope_tables(seq, dk):
    inv = ROPE_BASE ** (-jnp.arange(0, dk, 2, dtype=F32) / dk)
    ang = jnp.arange(seq, dtype=F32)[:, None] * inv[None]
    return jnp.cos(ang), jnp.sin(ang)


def kernel(x, c, rel_bias, att_w_qkv, att_w_o, att_sink, ret_w_in, ret_w_o, ret_decay_fwd,
           ret_decay_bwd, ada_w, ada_b, mix_norm_pre, mix_norm_post, ffn_norm_pre,
           ffn_norm_post, ffn_w_in, ffn_w_out):
    bsz, seq, d = x.shape
    t = bsz * seq
    assert seq % ROW_TILE == 0 and ROW_TILE % RET_CHUNK == 0 and d % RET_HEADS == 0
    assert seq % PROJ_ROW_TILE == 0 and PROJ_ROW_TILE % RET_CHUNK == 0
    assert ada_w.shape[0] == 2 and ffn_w_out.shape[1] % FFN_CHUNK == 0
    rows = -(-bsz // 8) * 8
    vec = lambda a: a.reshape(1, d)

    c_pad = jnp.pad(c, ((0, rows - bsz), (0, 0)))
    mod = _ada_mod(c_pad, ada_w, ada_b)
    modv = mod.reshape(2 * rows * 6, 1, d)

    x2d = x.reshape(t, d)

    nq = Q_HEADS * HEAD_DIM
    nkv = KV_HEADS * HEAD_DIM
    heads = [(2 * p + e) * GROUP + g for p in range(KV_HEADS // 2) for g in range(GROUP)
             for e in range(2)]
    w_qkv = att_w_qkv[0].astype(BF16)
    w_q = w_qkv[:, :nq].reshape(d, Q_HEADS, HEAD_DIM)[:, jnp.array(heads)].reshape(d, nq)
    w_q = w_q * jnp.asarray(HEAD_DIM ** -0.5, BF16)
    w_k = w_qkv[:, nq:nq + nkv]
    w_vt = w_qkv[:, nq + nkv:].T
    q, k, vt = _qkv_proj(x2d, vec(mix_norm_pre[0]), modv, w_q, w_k, w_vt, seq, rows)
    att, w_ao, w_f0in, w_f0out = _attention(
        q.reshape(bsz, seq, nq), k.reshape(bsz, seq, nkv), vt, rel_bias.astype(F32),
        att_sink[0].astype(F32), [(att_w_o, 0, None), (ffn_w_in, 0, None), (ffn_w_out, 0, None)])

    dk = d // RET_HEADS
    nqk = RET_HEADS * dk
    kscale = jnp.concatenate([jnp.ones((nqk,), F32), jnp.full((nqk,), dk ** -0.5, F32),
                              jnp.ones((ret_w_in.shape[2] - 2 * nqk,), F32)])[None]
    x1, h1, w_rin, w_ro, w_f1in, w_f1out = _post_ffn(
        att.reshape(t, nq), x2d, w_ao, vec(mix_norm_post[0]), vec(ffn_norm_pre[0]), w_f0in,
        w_f0out, vec(ffn_norm_post[0]), modv, 0, seq, rows, g_next=vec(mix_norm_pre[1]),
        cast_jobs=[(ret_w_in, 0, kscale), (ret_w_o, 0, None), (ffn_w_in, 1, None),
                   (ffn_w_out, 1, None)])
    dm, xi_f, xi_b, zf, zb, cd_f, cd_b = _decay_tables(ret_decay_fwd[0], ret_decay_bwd[0], dk)
    cos, sin = _rope_tables(seq, dk)
    q, k, kzf, kzb, v = _ret_in(h1, w_rin, cos, sin, zf, zb, seq)
    r3 = lambda a: a.reshape(bsz, seq, a.shape[1])
    gated = _retention(r3(q), r3(k), r3(kzf), r3(kzb), r3(v), r3(h1), w_rin, xi_f, xi_b, dm,
                       cd_f, cd_b)
    (x2,) = _post_ffn(gated.reshape(t, -1), x1, w_ro, vec(mix_norm_post[1]),
                      vec(ffn_norm_pre[1]), w_f1in, w_f1out, vec(ffn_norm_post[1]), modv, 1,
                      seq, rows)
    return x2.reshape(bsz, seq, d)
```

```python
import functools
import math

import jax
import jax.numpy as jnp
import numpy as np
from jax import lax
from jax.experimental import pallas as pl
from jax.experimental.pallas import tpu as pltpu

F32 = jnp.float32
BF16 = jnp.bfloat16

EPS = 1e-6
NEG = -1e30
LOG2E = math.log2(math.e)

Q_HEADS = 16
KV_HEADS = 4
GROUP = Q_HEADS // KV_HEADS
HEAD_DIM = 64
ATT_BLOCK = 128
ATT_STEP_BLOCKS = 8
REL_BUCKETS = 32
RET_HEADS = 4
RET_CHUNK = 256
RET_STEP_CHUNKS = 2
ROPE_BASE = 10000.0
FFN_CHUNK = 256
ROW_TILE = 512
PROJ_ROW_TILE = 1024
VMEM_LIMIT = 56 * 1024 * 1024


def _silu(x):
    hx = 0.5 * x
    return hx * jnp.tanh(hx) + hx


def _rms(xf, g):
    ms = jnp.mean(xf * xf, axis=-1, keepdims=True)
    return (xf * lax.rsqrt(ms + EPS)) * g


def _resident(shape):
    zeros = (0,) * len(shape)
    return pl.BlockSpec(shape, lambda *_: zeros, pipeline_mode=pl.Buffered(1))


def _params(n_axes, vmem=VMEM_LIMIT):
    return pltpu.CompilerParams(
        dimension_semantics=("arbitrary",) * n_axes, vmem_limit_bytes=vmem)


BF16_SUBLANES = 16


def _cast_plan(rows, n_steps):
    per = BF16_SUBLANES
    while rows % per or rows // per > n_steps:
        per += BF16_SUBLANES
    return per, rows // per


def _cast_io(jobs, n_steps, step_of):
    in_specs, out_specs, out_shapes, args, plan = [], [], [], [], []
    for w, layer, scale in jobs:
        _, rows, cols = w.shape
        per, n_cast = _cast_plan(rows, n_steps)
        blk = lambda *idx, n_cast=n_cast: jnp.minimum(step_of(*idx), n_cast - 1)
        in_specs.append(pl.BlockSpec(
            (1, per, cols), lambda *idx, blk=blk, layer=layer: (layer, blk(*idx), 0)))
        args.append(w)
        if scale is not None:
            in_specs.append(_resident(scale.shape))
            args.append(scale)
        out_specs.append(pl.BlockSpec((per, cols), lambda *idx, blk=blk: (blk(*idx), 0)))
        out_shapes.append(jax.ShapeDtypeStruct((rows, cols), BF16))
        plan.append((n_cast, scale is not None))
    return in_specs, out_specs, out_shapes, args, tuple(plan)


def _run_casts(step, plan, in_refs, out_refs):
    in_refs = list(in_refs)
    for (n_cast, scaled), dst in zip(plan, out_refs):
        src = in_refs.pop(0)
        scale = in_refs.pop(0) if scaled else None

        @pl.when(step < n_cast)
        def _(src=src, scale=scale, dst=dst):
            v = src[0]
            if scale is not None:
                v = v * scale[...]
            dst[...] = v.astype(BF16)


def _n_cast_inputs(plan):
    return sum(2 if scaled else 1 for _, scaled in plan)


def _ada_kernel(c_ref, w_ref, b_ref, o_ref):
    ca = _silu(c_ref[...])
    o_ref[0] = jnp.dot(ca.astype(BF16), w_ref[0].astype(BF16),
                       preferred_element_type=F32) + b_ref[0]


def _ada_mod(c_pad, ada_w, ada_b):
    depth, d, n = ada_w.shape
    rows = c_pad.shape[0]
    tn = 1536
    return pl.pallas_call(
        _ada_kernel,
        grid=(depth, n // tn),
        in_specs=[
            pl.BlockSpec((rows, d), lambda i, j: (0, 0)),
            pl.BlockSpec((1, d, tn), lambda i, j: (i, 0, j)),
            pl.BlockSpec((1, 1, tn), lambda i, j: (i, 0, j)),
        ],
        out_specs=pl.BlockSpec((1, rows, tn), lambda i, j: (i, 0, j)),
        out_shape=jax.ShapeDtypeStruct((depth, rows, n), F32),
        compiler_params=_params(2),
        name="ada_mod",
    )(c_pad, ada_w, ada_b.reshape(depth, 1, n))


def _mod_spec(layer, slot, tiles_per_batch, n_batch_rows):
    base = layer * n_batch_rows * 6 + slot

    def index(t):
        return (base + (t // tiles_per_batch) * 6, 0, 0)
    return index


def _qkv_kernel(x_ref, g_ref, sc_ref, sh_ref, wq_ref, wk_ref, wvt_ref, q_ref, k_ref, vt_ref):
    h = _rms(x_ref[...], g_ref[...]) * (1.0 + sc_ref[0]) + sh_ref[0]
    hb = h.astype(BF16)
    q_ref[...] = jnp.dot(hb, wq_ref[...], preferred_element_type=F32).astype(BF16)
    k_ref[...] = jnp.dot(hb, wk_ref[...], preferred_element_type=F32).astype(BF16)
    vt_ref[0] = lax.dot_general(wvt_ref[...], hb, (((1,), (1,)), ((), ())),
                                preferred_element_type=F32).astype(BF16)


def _qkv_proj(x2d, g_pre, modv, w_q, w_k, w_vt, seq, n_batch_rows):
    t, d = x2d.shape
    nq = w_q.shape[1]
    nkv = w_k.shape[1]
    tm = PROJ_ROW_TILE
    tpb = seq // tm
    vec = lambda idx: pl.BlockSpec((1, 1, d), idx)
    return pl.pallas_call(
        _qkv_kernel,
        grid=(t // tm,),
        in_specs=[
            pl.BlockSpec((tm, d), lambda i: (i, 0)),
            _resident((1, d)),
            vec(_mod_spec(0, 1, tpb, n_batch_rows)),
            vec(_mod_spec(0, 0, tpb, n_batch_rows)),
            _resident(w_q.shape), _resident(w_k.shape), _resident(w_vt.shape),
        ],
        out_specs=[
            pl.BlockSpec((tm, nq), lambda i: (i, 0)),
            pl.BlockSpec((tm, nkv), lambda i: (i, 0)),
            pl.BlockSpec((1, nkv, tm), lambda i: (i // tpb, 0, i % tpb)),
        ],
        out_shape=[
            jax.ShapeDtypeStruct((t, nq), BF16),
            jax.ShapeDtypeStruct((t, nkv), BF16),
            jax.ShapeDtypeStruct((t // seq, nkv, seq), BF16),
        ],
        compiler_params=_params(1),
        name="qkv_proj",
    )(x2d, g_pre, modv, modv, w_q, w_k, w_vt)


T5_THRESHOLDS = (12, 16, 23, 32, 46, 64, 91)


def _t5_bucket(rel, where, astype_i32):
    n = abs(rel)
    large = 8
    for thr in T5_THRESHOLDS:
        large = large + astype_i32(n >= thr)
    return where(rel > 0, 16, 0) + where(n < 8, n, large)


def _buckets_present(first_key, n_keys):
    j = np.arange(first_key, first_key + n_keys)[:, None]
    t = np.arange(ATT_BLOCK)[None, :]
    bucket = _t5_bucket(j - ATT_BLOCK - t, np.where, lambda m: m.astype(np.int32))
    return sorted(int(b) for b in np.unique(bucket))


def _att_tables(rb_ref, b2_ref, m2_ref):
    L = ATT_BLOCK
    rows = 64
    for r in range(3 * L // rows):
        j = lax.broadcasted_iota(jnp.int32, (rows, L), 0) + r * rows
        t = lax.broadcasted_iota(jnp.int32, (rows, L), 1)
        rel = j - L - t
        bucket = _t5_bucket(rel, jnp.where, lambda m: m.astype(jnp.int32))
        in_win = jnp.abs(rel) <= L
        visible = (in_win & (j >= L), in_win, in_win & (j < 2 * L))
        piece = slice(r * rows, (r + 1) * rows)
        for kind in range(3):
            m2_ref[kind, piece, :] = jnp.where(visible[kind], LOG2E, 0.0).astype(F32)

        present = _buckets_present(r * rows, rows)

        def body(hq, carry):
            acc = jnp.zeros((rows, L), F32)
            for b in present:
                acc = jnp.where(bucket == b, rb_ref[b, hq], acc)
            acc = acc * LOG2E
            for kind in range(3):
                b2_ref[kind, hq, piece, :] = jnp.where(visible[kind], acc, NEG)
            return carry
        lax.fori_loop(0, Q_HEADS, body, 0)


def _att_kernel(*refs, cast_plan):
    rb_ref, sink_ref, q_ref, kp_ref, kc_ref, kn_ref, vp_ref, vc_ref, vn_ref = refs[:9]
    n_cast_in = _n_cast_inputs(cast_plan)
    cast_in = refs[9:9 + n_cast_in]
    o_ref = refs[9 + n_cast_in]
    cast_out = refs[10 + n_cast_in:10 + n_cast_in + len(cast_plan)]
    b2_ref, m2_ref = refs[10 + n_cast_in + len(cast_plan):]
    b = pl.program_id(0)
    i = pl.program_id(1)
    ns = pl.num_programs(1)
    _run_casts(b * ns + i, cast_plan, cast_in, cast_out)

    @pl.when((b == 0) & (i == 0))
    def _():
        _att_tables(rb_ref, b2_ref, m2_ref)

    L = ATT_BLOCK
    dh = HEAD_DIM
    U = ATT_STEP_BLOCKS
    q = q_ref[0]
    kb = jnp.concatenate([kp_ref[0], kc_ref[0], kn_ref[0]], axis=0)
    vt = jnp.concatenate([vp_ref[0], vc_ref[0], vn_ref[0]], axis=1)
    lane = lax.broadcasted_iota(jnp.int32, (L, 2 * dh), 1)
    zero = jnp.zeros((L, 2 * dh), BF16)
    ones = jnp.ones((BF16_SUBLANES, 3 * L), BF16)
    kinds = [1] * U
    kinds[0] = jnp.where(i == 0, 0, 1)
    kinds[U - 1] = jnp.where(i == ns - 1, 2, 1)

    def scores(u, h):
        p, e = divmod(h, 2)
        k_pair = kb[u * L:(u + 3) * L, p * 2 * dh:(p + 1) * 2 * dh]
        mine = (lane < dh) if e == 0 else (lane >= dh)
        qz = jnp.concatenate(
            [jnp.where(mine, q[u * L:(u + 1) * L,
                               (p * GROUP + g) * 2 * dh:(p * GROUP + g + 1) * 2 * dh], zero)
             for g in range(GROUP)], axis=0)
        return lax.dot_general(k_pair, qz, (((1,), (1,)), ((), ())),
                               preferred_element_type=F32)

    units = [(u, h) for u in range(U) for h in range(KV_HEADS)]
    o_rows = []
    st_next = scores(*units[0])
    for idx, (u, h) in enumerate(units):
        st = st_next
        if idx + 1 < len(units):
            st_next = scores(*units[idx + 1])
        m2 = m2_ref[kinds[u]]
        es, sinks = [], []
        for g in range(GROUP):
            hq = h * GROUP + g
            sk2 = jnp.full((1, L), sink_ref[hq], F32) * LOG2E
            l2 = st[:, g * L:(g + 1) * L] * m2 + b2_ref[kinds[u], hq]
            m = jnp.maximum(jnp.max(l2, axis=0, keepdims=True), sk2)
            es.append(jnp.exp2(l2 - m).astype(BF16))
            sinks.append(jnp.exp2(sk2 - m))
        et = jnp.concatenate(es, axis=1)
        va = jnp.concatenate([vt[h * dh:(h + 1) * dh, u * L:(u + 3) * L], ones], axis=0)
        ot = jnp.dot(va, et, preferred_element_type=F32)
        den = ot[dh:dh + 1, :] + jnp.concatenate(sinks, axis=1)
        ot = ot[:dh, :] * (1.0 / den)
        o_rows += [ot[:, g * L:(g + 1) * L] for g in range(GROUP)]
        if h == KV_HEADS - 1:
            ot_all = jnp.concatenate(o_rows, axis=0)
            o_ref[0, u * L:(u + 1) * L, :] = ot_all.T.astype(BF16)
            o_rows = []


def _attention(q, k, vt, rel_bias, sink, cast_jobs):
    bsz, seq, dq = q.shape
    dkv = k.shape[2]
    L = ATT_BLOCK
    nb = seq // L
    U = ATT_STEP_BLOCKS
    assert U >= 2 and nb % U == 0
    ns = nb // U
    smem = pl.BlockSpec(memory_space=pltpu.SMEM)
    prev = lambda i: jnp.maximum(U * i - 1, 0)
    nxt = lambda i: jnp.minimum(U * i + U, nb - 1)
    c_in, c_out, c_shapes, c_args, plan = _cast_io(cast_jobs, bsz * ns, lambda b, i: b * ns + i)
    return pl.pallas_call(
        functools.partial(_att_kernel, cast_plan=plan),
        grid=(bsz, ns),
        in_specs=[
            smem, smem,
            pl.BlockSpec((1, U * L, dq), lambda b, i: (b, i, 0)),
            pl.BlockSpec((1, L, dkv), lambda b, i: (b, prev(i), 0)),
            pl.BlockSpec((1, U * L, dkv), lambda b, i: (b, i, 0)),
            pl.BlockSpec((1, L, dkv), lambda b, i: (b, nxt(i), 0)),
            pl.BlockSpec((1, dkv, L), lambda b, i: (b, 0, prev(i))),
            pl.BlockSpec((1, dkv, U * L), lambda b, i: (b, 0, i)),
            pl.BlockSpec((1, dkv, L), lambda b, i: (b, 0, nxt(i))),
        ] + c_in,
        out_specs=[pl.BlockSpec((1, U * L, dq), lambda b, i: (b, i, 0))] + c_out,
        out_shape=[jax.ShapeDtypeStruct((bsz, seq, dq), BF16)] + c_shapes,
        scratch_shapes=[
            pltpu.VMEM((3, Q_HEADS, 3 * L, L), F32),
            pltpu.VMEM((3, 3 * L, L), F32),
        ],
        compiler_params=_params(2),
        name="swa_attention",
    )(rel_bias, sink, q, k, k, k, vt, vt, vt, *c_args)


def _post_ffn_kernel(*refs, emit_next, cast_plan):
    (a_ref, x_ref, wo_ref, gpost_ref, g1_ref, gpre_ref, sc2_ref, sh2_ref,
     win_ref, wout_ref, gfpost_ref, g2_ref) = refs[:12]
    rest = list(refs[12:])
    if emit_next:
        gn_ref, scn_ref, shn_ref = rest[:3]
        rest = rest[3:]
    n_cast_in = _n_cast_inputs(cast_plan)
    cast_in, rest = rest[:n_cast_in], rest[n_cast_in:]
    xo_ref = rest.pop(0)
    if emit_next:
        ho_ref = rest.pop(0)
    cast_out = rest[:len(cast_plan)]
    hb_ref, x1_ref, acc_ref, y_ref = rest[len(cast_plan):]
    s = pl.program_id(0)
    n = pl.num_programs(0) - 2
    _run_casts(s, cast_plan, cast_in, cast_out)
    hidden = wout_ref.shape[0]

    def advance():
        hb_ref[1] = hb_ref[0]
        x1_ref[2] = x1_ref[1]
        x1_ref[1] = x1_ref[0]

    n_chunks = hidden // FFN_CHUNK
    n_slices = 8
    rows_of = lambda r: slice(r * (x_ref.shape[0] // n_slices), (r + 1) * (x_ref.shape[0] // n_slices))

    def a_matmul():
        y_ref[...] = jnp.dot(a_ref[...], wo_ref[...], preferred_element_type=F32)

    def after(v, dep):
        if dep is None:
            return v
        zero = (pltpu.bitcast(dep, jnp.uint32) >> 16) >> 16
        return v * pltpu.bitcast(zero | jnp.uint32(0x3F800000), F32)

    def a_norm(r, dep=None):
        rows = rows_of(r)
        x1 = x_ref[rows, :] + g1_ref[0] * _rms(after(y_ref[rows, :], dep), gpost_ref[...])
        h = _rms(x1, gpre_ref[...]) * (1.0 + sc2_ref[0]) + sh2_ref[0]
        hb_ref[0, rows, :] = h.astype(BF16)
        x1_ref[0, rows, :] = x1

    def b_chunk(c, hb, acc):
        lo = c * FFN_CHUNK
        a = jnp.dot(hb, win_ref[:, lo:lo + FFN_CHUNK], preferred_element_type=F32)
        b = jnp.dot(hb, win_ref[:, hidden + lo:hidden + lo + FFN_CHUNK],
                    preferred_element_type=F32)
        act = (_silu(a) * b).astype(BF16)
        return acc + jnp.dot(act, wout_ref[lo:lo + FFN_CHUNK, :], preferred_element_type=F32)

    def c_slice(r, dep=None):
        rows = rows_of(r)
        x2 = x1_ref[2, rows, :] + g2_ref[0] * _rms(after(acc_ref[rows, :], dep),
                                                   gfpost_ref[...])
        xo_ref[rows, :] = x2
        if emit_next:
            hn = _rms(x2, gn_ref[...]) * (1.0 + scn_ref[0]) + shn_ref[0]
            ho_ref[rows, :] = hn.astype(BF16)

    @pl.when(s == 0)
    def _():
        x1_ref[...] = jnp.zeros(x1_ref.shape, F32)
        acc_ref[...] = jnp.zeros(acc_ref.shape, F32)
        a_matmul()
        for r in range(n_slices):
            a_norm(r)

    @pl.when((s >= 1) & (s <= n))
    def _():
        advance()
        a_matmul()
        hb = hb_ref[1]
        acc = jnp.zeros((hb.shape[0], wout_ref.shape[1]), F32)
        for c in range(n_chunks):
            acc = b_chunk(c, hb, acc)
            dep = acc[0:1, :]
            if c < n_slices:
                c_slice(c, dep)
            if c >= n_chunks - n_slices - 1 and c < n_chunks - 1:
                a_norm(c - (n_chunks - n_slices - 1), dep)
        acc_ref[...] = acc

    @pl.when(s == n + 1)
    def _():
        advance()
        for r in range(n_slices):
            c_slice(r)


def _post_ffn(a2d, x2d, w_o, g_post, g_pre, w_in, w_out, g_fpost, modv, layer,
              seq, n_batch_rows, g_next=None, cast_jobs=()):
    t, d = x2d.shape
    kin = a2d.shape[1]
    tm = ROW_TILE
    tpb = seq // tm
    n = t // tm
    emit_next = g_next is not None
    tile_a = lambda s: jnp.minimum(s, n - 1)
    tile_c = lambda s: jnp.clip(s - 2, 0, n - 1)

    def vec(lyr, slot, tile_of):
        idx = _mod_spec(lyr, slot, tpb, n_batch_rows)
        return pl.BlockSpec((1, 1, d), lambda s: idx(tile_of(s)))
    row = lambda width, tile_of: pl.BlockSpec((tm, width), lambda s: (tile_of(s), 0))
    in_specs = [
        row(kin, tile_a), row(d, tile_a), _resident(w_o.shape), _resident((1, d)),
        vec(layer, 2, tile_a), _resident((1, d)), vec(layer, 4, tile_a), vec(layer, 3, tile_a),
        _resident(w_in.shape), _resident(w_out.shape), _resident((1, d)), vec(layer, 5, tile_c),
    ]
    args = [a2d, x2d, w_o, g_post, modv, g_pre, modv, modv, w_in, w_out, g_fpost, modv]
    out_specs = [row(d, tile_c)]
    out_shape = [jax.ShapeDtypeStruct((t, d), F32)]
    if emit_next:
        in_specs += [_resident((1, d)), vec(layer + 1, 1, tile_c), vec(layer + 1, 0, tile_c)]
        args += [g_next, modv, modv]
        out_specs.append(row(d, tile_c))
        out_shape.append(jax.ShapeDtypeStruct((t, d), BF16))
    c_in, c_out, c_shapes, c_args, plan = _cast_io(cast_jobs, n, lambda s: s)
    return pl.pallas_call(
        functools.partial(_post_ffn_kernel, emit_next=emit_next, cast_plan=plan),
        grid=(n + 2,),
        in_specs=in_specs + c_in,
        out_specs=out_specs + c_out,
        out_shape=out_shape + c_shapes,
        scratch_shapes=[pltpu.VMEM((2, tm, d), BF16), pltpu.VMEM((3, tm, d), F32),
                        pltpu.VMEM((tm, d), F32), pltpu.VMEM((tm, d), F32)],
        compiler_params=_params(1),
        name="post_ffn_next" if emit_next else "post_ffn",
    )(*args, *c_args)


def _ret_in_kernel(h_ref, w_ref, cos_ref, sin_ref, zf_ref, zb_ref,
                   q_ref, k_ref, kzf_ref, kzb_ref, v_ref):
    hb = h_ref[...]
    tm = hb.shape[0]
    cos = cos_ref[...]
    sin = sin_ref[...]
    dk = w_ref.shape[0] // RET_HEADS
    half = dk // 2
    nqk = RET_HEADS * dk
    nv = w_ref.shape[1] - 2 * nqk

    def project(col):
        return jnp.dot(hb, w_ref[:, col:col + dk], preferred_element_type=F32)

    def rotated(y, off):
        a = y[:, off:off + half] * cos
        b = y[:, half - off:dk - off] * sin
        return a - b if off == 0 else a + b

    zf = zf_ref[...][None]
    zb = zb_ref[...][None]
    for h in range(RET_HEADS):
        yk = project(nqk + h * dk)
        for off in (0, half):
            lo = h * dk + off
            rk = rotated(yk, off)
            k_ref[:, lo:lo + half] = rk.astype(BF16)
            rk3 = rk.reshape(tm // RET_CHUNK, RET_CHUNK, half)
            kzf_ref[:, lo:lo + half] = (rk3 * zf[:, :, lo:lo + half]).reshape(tm, half).astype(BF16)
            kzb_ref[:, lo:lo + half] = (rk3 * zb[:, :, lo:lo + half]).reshape(tm, half).astype(BF16)
    for h in range(RET_HEADS):
        yq = project(h * dk)
        for off in (0, half):
            lo = h * dk + off
            q_ref[:, lo:lo + half] = rotated(yq, off).astype(BF16)
    for c0 in range(0, nv, dk):
        v_ref[:, c0:c0 + dk] = project(2 * nqk + c0).astype(BF16)


def _ret_in(h2d, w_in, cos, sin, zf, zb, seq):
    t, d = h2d.shape
    tm = PROJ_ROW_TILE
    tpb = seq // tm
    nqk = RET_HEADS * (d // RET_HEADS)
    nv = (w_in.shape[1] - 2 * nqk) // 2
    row = lambda width: pl.BlockSpec((tm, width), lambda i: (i, 0))
    pos = pl.BlockSpec((tm, cos.shape[1]), lambda i: (i % tpb, 0))
    return pl.pallas_call(
        _ret_in_kernel,
        grid=(t // tm,),
        in_specs=[row(d), _resident((d, 2 * nqk + nv)), pos, pos,
                  _resident(zf.shape), _resident(zb.shape)],
        out_specs=[row(nqk), row(nqk), row(nqk), row(nqk), row(nv)],
        out_shape=[jax.ShapeDtypeStruct((t, nqk), BF16)] * 4
        + [jax.ShapeDtypeStruct((t, nv), BF16)],
        compiler_params=_params(1),
        name="ret_in_proj",
    )(h2d, w_in, cos, sin, zf, zb)


def _state_update(st_ref, h, kz_h, v_h, cd):
    upd = lax.dot_general(kz_h, v_h, (((0,), (0,)), ((), ())), preferred_element_type=F32)
    st_ref[h] = st_ref[h] * cd + upd


def _ret_scan_kernel(cdf_ref, cdb_ref, q_ref, k_ref, kzf_ref, kzb_ref, v_ref, h_ref, wg_ref,
                     xif_ref, xib_ref, dm_ref, o_ref, st_ref, snap_ref, sg_ref):
    j = pl.program_id(1)
    ns = pl.num_programs(1) // 2
    L = RET_CHUNK
    n_sub = RET_STEP_CHUNKS

    @pl.when((j == 0) | (j == ns))
    def _():
        st_ref[...] = jnp.zeros(st_ref.shape, F32)

    dk = q_ref.shape[2] // RET_HEADS
    dv = v_ref.shape[2] // RET_HEADS

    @pl.when(j < ns)
    def _():
        for ci in reversed(range(n_sub)):
            rows = slice(ci * L, (ci + 1) * L)
            c = (ns - 1 - j) * n_sub + ci
            for h in range(RET_HEADS):
                snap_ref[c, h] = st_ref[h].astype(BF16)
                _state_update(st_ref, h, kzb_ref[0, rows, h * dk:(h + 1) * dk],
                              v_ref[0, rows, h * dv:(h + 1) * dv], cdb_ref[h])

    @pl.when(j >= ns)
    def _():
        sg_ref[...] = _silu(jnp.dot(h_ref[0], wg_ref[...], preferred_element_type=F32))

        def scores(ci, h):
            rows = slice(ci * L, (ci + 1) * L)
            return lax.dot_general(q_ref[0, rows, h * dk:(h + 1) * dk],
                                   k_ref[0, rows, h * dk:(h + 1) * dk],
                                   (((1,), (1,)), ((), ())), preferred_element_type=F32)

        def finish(ci, h, s):
            rows = slice(ci * L, (ci + 1) * L)
            c = (j - ns) * n_sub + ci
            q_h = q_ref[0, rows, h * dk:(h + 1) * dk]
            v_h = v_ref[0, rows, h * dv:(h + 1) * dv]
            y = jnp.dot((s * dm_ref[h]).astype(BF16), v_h, preferred_element_type=F32)
            y = y + jnp.dot(q_h, st_ref[h].astype(BF16), preferred_element_type=F32) * xif_ref[h]
            y = y + jnp.dot(q_h, snap_ref[c, h], preferred_element_type=F32) * xib_ref[h]
            _state_update(st_ref, h, kzf_ref[0, rows, h * dk:(h + 1) * dk], v_h, cdf_ref[h])
            yn = y * lax.rsqrt(jnp.mean(y * y, axis=-1, keepdims=True) + EPS)
            o_ref[0, rows, h * dv:(h + 1) * dv] = (
                sg_ref[rows, h * dv:(h + 1) * dv] * yn).astype(BF16)

        units = [(ci, h) for ci in range(n_sub) for h in range(RET_HEADS)]
        ahead = scores(*units[0])
        for idx, (ci, h) in enumerate(units):
            s = ahead
            if idx + 1 < len(units):
                ahead = scores(*units[idx + 1])
            finish(ci, h, s)


def _retention(q, k, kzf, kzb, v, h, w_in, xi_f, xi_b, dmat, cd_f, cd_b):
    bsz, seq, nqk = q.shape
    nv = v.shape[2]
    d = h.shape[2]
    gate_col = (2 * nqk + nv) // nv
    L = RET_CHUNK
    rows = RET_STEP_CHUNKS * L
    ns = seq // rows
    dk = nqk // RET_HEADS
    dv = nv // RET_HEADS
    smem = pl.BlockSpec(memory_space=pltpu.SMEM)
    both = lambda j: jnp.where(j < ns, ns - 1 - j, j - ns)
    bwd_only = lambda j: jnp.where(j < ns, ns - 1 - j, 0)
    fwd_only = lambda j: jnp.where(j < ns, 0, j - ns)
    blk = lambda width, at: pl.BlockSpec((1, rows, width), lambda b, j: (b, at(j), 0))
    return pl.pallas_call(
        _ret_scan_kernel,
        grid=(bsz, 2 * ns),
        in_specs=[smem, smem, blk(nqk, fwd_only), blk(nqk, fwd_only), blk(nqk, fwd_only),
                  blk(nqk, bwd_only), blk(nv, both), blk(d, fwd_only),
                  pl.BlockSpec((d, nv), lambda b, j: (0, gate_col),
                               pipeline_mode=pl.Buffered(1)),
                  _resident(xi_f.shape), _resident(xi_b.shape), _resident(dmat.shape)],
        out_specs=blk(nv, fwd_only),
        out_shape=jax.ShapeDtypeStruct((bsz, seq, nv), BF16),
        scratch_shapes=[pltpu.VMEM((RET_HEADS, dk, dv), F32),
                        pltpu.VMEM((seq // L, RET_HEADS, dk, dv), BF16),
                        pltpu.VMEM((rows, nv), F32)],
        compiler_params=_params(2),
        name="ret_scan",
    )(cd_f, cd_b, q, k, kzf, kzb, v, h, w_in, xi_f, xi_b, dmat)


def _decay_tables(decay_fwd, decay_bwd, dk):
    L = RET_CHUNK
    lg_f = jax.nn.log_sigmoid(decay_fwd.astype(F32))
    lg_b = jax.nn.log_sigmoid(decay_bwd.astype(F32))
    idx = jnp.arange(L, dtype=F32)
    diff = idx[:, None] - idx[None, :]
    dm = jnp.where((diff >= 0)[None],
                   jnp.exp(lg_f[:, None, None] * jnp.maximum(diff, 0.0)[None]),
                   jnp.exp(lg_b[:, None, None] * jnp.maximum(-diff, 0.0)[None]))
    xi_f = jnp.exp(lg_f[:, None] * (idx + 1.0)[None])[:, :, None]
    xi_b = jnp.exp(lg_b[:, None] * (L - idx)[None])[:, :, None]
    zeta_f = jnp.exp(lg_f[:, None] * (L - 1.0 - idx)[None])
    zeta_b = jnp.exp(lg_b[:, None] * idx[None])
    zf = jnp.repeat(zeta_f.T, dk, axis=1)
    zb = jnp.repeat(zeta_b.T, dk, axis=1)
    return dm, xi_f, xi_b, zf, zb, jnp.exp(lg_f * L), jnp.exp(lg_b * L)


def _rope_tables(seq, dk):
    inv = ROPE_BASE ** (-jnp.arange(0, dk, 2, dtype=F32) / dk)
    ang = jnp.arange(seq, dtype=F32)[:, None] * inv[None]
    return jnp.cos(ang), jnp.sin(ang)


def kernel(x, c, rel_bias, att_w_qkv, att_w_o, att_sink, ret_w_in, ret_w_o, ret_decay_fwd,
           ret_decay_bwd, ada_w, ada_b, mix_norm_pre, mix_norm_post, ffn_norm_pre,
           ffn_norm_post, ffn_w_in, ffn_w_out):
    bsz, seq, d = x.shape
    t = bsz * seq
    assert seq % ROW_TILE == 0 and ROW_TILE % RET_CHUNK == 0 and d % RET_HEADS == 0
    assert seq % PROJ_ROW_TILE == 0 and PROJ_ROW_TILE % RET_CHUNK == 0
    assert ada_w.shape[0] == 2 and ffn_w_out.shape[1] % FFN_CHUNK == 0
    rows = -(-bsz // 8) * 8
    vec = lambda a: a.reshape(1, d)

    c_pad = jnp.pad(c, ((0, rows - bsz), (0, 0)))
    mod = _ada_mod(c_pad, ada_w, ada_b)
    modv = mod.reshape(2 * rows * 6, 1, d)

    x2d = x.reshape(t, d)

    nq = Q_HEADS * HEAD_DIM
    nkv = KV_HEADS * HEAD_DIM
    heads = [(2 * p + e) * GROUP + g for p in range(KV_HEADS // 2) for g in range(GROUP)
             for e in range(2)]
    w_qkv = att_w_qkv[0].astype(BF16)
    w_q = w_qkv[:, :nq].reshape(d, Q_HEADS, HEAD_DIM)[:, jnp.array(heads)].reshape(d, nq)
    w_q = w_q * jnp.asarray(HEAD_DIM ** -0.5, BF16)
    w_k = w_qkv[:, nq:nq + nkv]
    w_vt = w_qkv[:, nq + nkv:].T
    q, k, vt = _qkv_proj(x2d, vec(mix_norm_pre[0]), modv, w_q, w_k, w_vt, seq, rows)
    att, w_ao, w_f0in, w_f0out = _attention(
        q.reshape(bsz, seq, nq), k.reshape(bsz, seq, nkv), vt, rel_bias.astype(F32),
        att_sink[0].astype(F32), [(att_w_o, 0, None), (ffn_w_in, 0, None), (ffn_w_out, 0, None)])

    dk = d // RET_HEADS
    nqk = RET_HEADS * dk
    kscale = jnp.concatenate([jnp.ones((nqk,), F32), jnp.full((nqk,), dk ** -0.5, F32),
                              jnp.ones((ret_w_in.shape[2] - 2 * nqk,), F32)])[None]
    x1, h1, w_rin, w_ro, w_f1in, w_f1out = _post_ffn(
        att.reshape(t, nq), x2d, w_ao, vec(mix_norm_post[0]), vec(ffn_norm_pre[0]), w_f0in,
        w_f0out, vec(ffn_norm_post[0]), modv, 0, seq, rows, g_next=vec(mix_norm_pre[1]),
        cast_jobs=[(ret_w_in, 0, kscale), (ret_w_o, 0, None), (ffn_w_in, 1, None),
                   (ffn_w_out, 1, None)])
    dm, xi_f, xi_b, zf, zb, cd_f, cd_b = _decay_tables(ret_decay_fwd[0], ret_decay_bwd[0], dk)
    cos, sin = _rope_tables(seq, dk)
    q, k, kzf, kzb, v = _ret_in(h1, w_rin, cos, sin, zf, zb, seq)
    r3 = lambda a: a.reshape(bsz, seq, a.shape[1])
    gated = _retention(r3(q), r3(k), r3(kzf), r3(kzb), r3(v), r3(h1), w_rin, xi_f, xi_b, dm,
                       cd_f, cd_b)
    (x2,) = _post_ffn(gated.reshape(t, -1), x1, w_ro, vec(mix_norm_post[1]),
                      vec(ffn_norm_pre[1]), w_f1in, w_f1out, vec(ffn_norm_post[1]), modv, 1,
                      seq, rows)
    return x2.reshape(bsz, seq, d)
```

```python
import functools
import math

import jax
import jax.numpy as jnp
from jax import lax
from jax.experimental import pallas as pl
from jax.experimental.pallas import tpu as pltpu

F32 = jnp.float32
BF16 = jnp.bfloat16

EPS = 1e-6
NEG = -1e30
LOG2E = math.log2(math.e)

Q_HEADS = 16
KV_HEADS = 4
GROUP = Q_HEADS // KV_HEADS
HEAD_DIM = 64
ATT_BLOCK = 128
ATT_STEP_BLOCKS = 8
REL_BUCKETS = 32
RET_HEADS = 4
RET_CHUNK = 256
RET_STEP_CHUNKS = 2
ROPE_BASE = 10000.0
FFN_CHUNK = 256
ROW_TILE = 512
PROJ_ROW_TILE = 1024
VMEM_LIMIT = 56 * 1024 * 1024


def _silu(x):
    hx = 0.5 * x
    return hx * jnp.tanh(hx) + hx


def _rms(xf, g):
    ms = jnp.mean(xf * xf, axis=-1, keepdims=True)
    return (xf * lax.rsqrt(ms + EPS)) * g


def _resident(shape):
    zeros = (0,) * len(shape)
    return pl.BlockSpec(shape, lambda *_: zeros, pipeline_mode=pl.Buffered(1))


def _params(n_axes, vmem=VMEM_LIMIT):
    return pltpu.CompilerParams(
        dimension_semantics=("arbitrary",) * n_axes, vmem_limit_bytes=vmem)


BF16_SUBLANES = 16


def _cast_plan(rows, n_steps):
    per = BF16_SUBLANES
    while rows % per or rows // per > n_steps:
        per += BF16_SUBLANES
    return per, rows // per


def _cast_io(jobs, n_steps, step_of):
    in_specs, out_specs, out_shapes, args, plan = [], [], [], [], []
    for w, layer, scale in jobs:
        _, rows, cols = w.shape
        per, n_cast = _cast_plan(rows, n_steps)
        blk = lambda *idx, n_cast=n_cast: jnp.minimum(step_of(*idx), n_cast - 1)
        in_specs.append(pl.BlockSpec(
            (1, per, cols), lambda *idx, blk=blk, layer=layer: (layer, blk(*idx), 0)))
        args.append(w)
        if scale is not None:
            in_specs.append(_resident(scale.shape))
            args.append(scale)
        out_specs.append(pl.BlockSpec((per, cols), lambda *idx, blk=blk: (blk(*idx), 0)))
        out_shapes.append(jax.ShapeDtypeStruct((rows, cols), BF16))
        plan.append((n_cast, scale is not None))
    return in_specs, out_specs, out_shapes, args, tuple(plan)


def _run_casts(step, plan, in_refs, out_refs):
    in_refs = list(in_refs)
    for (n_cast, scaled), dst in zip(plan, out_refs):
        src = in_refs.pop(0)
        scale = in_refs.pop(0) if scaled else None

        @pl.when(step < n_cast)
        def _(src=src, scale=scale, dst=dst):
            v = src[0]
            if scale is not None:
                v = v * scale[...]
            dst[...] = v.astype(BF16)


def _n_cast_inputs(plan):
    return sum(2 if scaled else 1 for _, scaled in plan)


def _ada_kernel(c_ref, w_ref, b_ref, o_ref):
    ca = _silu(c_ref[...])
    o_ref[0] = jnp.dot(ca.astype(BF16), w_ref[0].astype(BF16),
                       preferred_element_type=F32) + b_ref[0]


def _ada_mod(c_pad, ada_w, ada_b):
    depth, d, n = ada_w.shape
    rows = c_pad.shape[0]
    tn = 1536
    return pl.pallas_call(
        _ada_kernel,
        grid=(depth, n // tn),
        in_specs=[
            pl.BlockSpec((rows, d), lambda i, j: (0, 0)),
            pl.BlockSpec((1, d, tn), lambda i, j: (i, 0, j)),
            pl.BlockSpec((1, 1, tn), lambda i, j: (i, 0, j)),
        ],
        out_specs=pl.BlockSpec((1, rows, tn), lambda i, j: (i, 0, j)),
        out_shape=jax.ShapeDtypeStruct((depth, rows, n), F32),
        compiler_params=_params(2),
        name="ada_mod",
    )(c_pad, ada_w, ada_b.reshape(depth, 1, n))


def _mod_spec(layer, slot, tiles_per_batch, n_batch_rows):
    base = layer * n_batch_rows * 6 + slot

    def index(t):
        return (base + (t // tiles_per_batch) * 6, 0, 0)
    return index


def _qkv_kernel(x_ref, g_ref, sc_ref, sh_ref, wq_ref, wk_ref, wvt_ref, q_ref, k_ref, vt_ref):
    h = _rms(x_ref[...], g_ref[...]) * (1.0 + sc_ref[0]) + sh_ref[0]
    hb = h.astype(BF16)
    q_ref[...] = jnp.dot(hb, wq_ref[...], preferred_element_type=F32).astype(BF16)
    k_ref[...] = jnp.dot(hb, wk_ref[...], preferred_element_type=F32).astype(BF16)
    vt_ref[0] = lax.dot_general(wvt_ref[...], hb, (((1,), (1,)), ((), ())),
                                preferred_element_type=F32).astype(BF16)


def _qkv_proj(x2d, g_pre, modv, w_q, w_k, w_vt, seq, n_batch_rows):
    t, d = x2d.shape
    nq = w_q.shape[1]
    nkv = w_k.shape[1]
    tm = PROJ_ROW_TILE
    tpb = seq // tm
    vec = lambda idx: pl.BlockSpec((1, 1, d), idx)
    return pl.pallas_call(
        _qkv_kernel,
        grid=(t // tm,),
        in_specs=[
            pl.BlockSpec((tm, d), lambda i: (i, 0)),
            _resident((1, d)),
            vec(_mod_spec(0, 1, tpb, n_batch_rows)),
            vec(_mod_spec(0, 0, tpb, n_batch_rows)),
            _resident(w_q.shape), _resident(w_k.shape), _resident(w_vt.shape),
        ],
        out_specs=[
            pl.BlockSpec((tm, nq), lambda i: (i, 0)),
            pl.BlockSpec((tm, nkv), lambda i: (i, 0)),
            pl.BlockSpec((1, nkv, tm), lambda i: (i // tpb, 0, i % tpb)),
        ],
        out_shape=[
            jax.ShapeDtypeStruct((t, nq), BF16),
            jax.ShapeDtypeStruct((t, nkv), BF16),
            jax.ShapeDtypeStruct((t // seq, nkv, seq), BF16),
        ],
        compiler_params=_params(1),
        name="qkv_proj",
    )(x2d, g_pre, modv, modv, w_q, w_k, w_vt)


def _att_tables(rb_ref, b2_ref, m2_ref):
    L = ATT_BLOCK
    rows = 64
    for r in range(3 * L // rows):
        j = lax.broadcasted_iota(jnp.int32, (rows, L), 0) + r * rows
        t = lax.broadcasted_iota(jnp.int32, (rows, L), 1)
        rel = j - L - t
        n = jnp.abs(rel)
        large = jnp.full((rows, L), 8, jnp.int32)
        for thr in (12, 16, 23, 32, 46, 64, 91):
            large = large + (n >= thr).astype(jnp.int32)
        bucket = jnp.where(rel > 0, 16, 0) + jnp.where(n < 8, n, large)
        in_win = n <= L
        visible = (in_win & (j >= L), in_win, in_win & (j < 2 * L))
        piece = slice(r * rows, (r + 1) * rows)
        for kind in range(3):
            m2_ref[kind, piece, :] = jnp.where(visible[kind], LOG2E, 0.0).astype(F32)

        def body(hq, carry):
            acc = jnp.zeros((rows, L), F32)
            for b in range(REL_BUCKETS):
                acc = jnp.where(bucket == b, rb_ref[b, hq], acc)
            acc = acc * LOG2E
            for kind in range(3):
                b2_ref[kind, hq, piece, :] = jnp.where(visible[kind], acc, NEG)
            return carry
        lax.fori_loop(0, Q_HEADS, body, 0)


def _att_kernel(*refs, cast_plan):
    rb_ref, sink_ref, q_ref, kp_ref, kc_ref, kn_ref, vp_ref, vc_ref, vn_ref = refs[:9]
    n_cast_in = _n_cast_inputs(cast_plan)
    cast_in = refs[9:9 + n_cast_in]
    o_ref = refs[9 + n_cast_in]
    cast_out = refs[10 + n_cast_in:10 + n_cast_in + len(cast_plan)]
    b2_ref, m2_ref = refs[10 + n_cast_in + len(cast_plan):]
    b = pl.program_id(0)
    i = pl.program_id(1)
    ns = pl.num_programs(1)
    _run_casts(b * ns + i, cast_plan, cast_in, cast_out)

    @pl.when((b == 0) & (i == 0))
    def _():
        _att_tables(rb_ref, b2_ref, m2_ref)

    L = ATT_BLOCK
    dh = HEAD_DIM
    U = ATT_STEP_BLOCKS
    q = q_ref[0]
    kb = jnp.concatenate([kp_ref[0], kc_ref[0], kn_ref[0]], axis=0)
    vt = jnp.concatenate([vp_ref[0], vc_ref[0], vn_ref[0]], axis=1)
    lane = lax.broadcasted_iota(jnp.int32, (L, 2 * dh), 1)
    zero = jnp.zeros((L, 2 * dh), BF16)
    ones = jnp.ones((BF16_SUBLANES, 3 * L), BF16)
    kinds = [1] * U
    kinds[0] = jnp.where(i == 0, 0, 1)
    kinds[U - 1] = jnp.where(i == ns - 1, 2, 1)

    def scores(u, h):
        p, e = divmod(h, 2)
        k_pair = kb[u * L:(u + 3) * L, p * 2 * dh:(p + 1) * 2 * dh]
        mine = (lane < dh) if e == 0 else (lane >= dh)
        qz = jnp.concatenate(
            [jnp.where(mine, q[u * L:(u + 1) * L,
                               (p * GROUP + g) * 2 * dh:(p * GROUP + g + 1) * 2 * dh], zero)
             for g in range(GROUP)], axis=0)
        return lax.dot_general(k_pair, qz, (((1,), (1,)), ((), ())),
                               preferred_element_type=F32)

    units = [(u, h) for u in range(U) for h in range(KV_HEADS)]
    o_rows = []
    st_next = scores(*units[0])
    for idx, (u, h) in enumerate(units):
        st = st_next
        if idx + 1 < len(units):
            st_next = scores(*units[idx + 1])
        m2 = m2_ref[kinds[u]]
        es, sinks = [], []
        for g in range(GROUP):
            hq = h * GROUP + g
            sk2 = jnp.full((1, L), sink_ref[hq], F32) * LOG2E
            l2 = st[:, g * L:(g + 1) * L] * m2 + b2_ref[kinds[u], hq]
            m = jnp.maximum(jnp.max(l2, axis=0, keepdims=True), sk2)
            es.append(jnp.exp2(l2 - m).astype(BF16))
            sinks.append(jnp.exp2(sk2 - m))
        et = jnp.concatenate(es, axis=1)
        va = jnp.concatenate([vt[h * dh:(h + 1) * dh, u * L:(u + 3) * L], ones], axis=0)
        ot = jnp.dot(va, et, preferred_element_type=F32)
        den = ot[dh:dh + 1, :] + jnp.concatenate(sinks, axis=1)
        ot = ot[:dh, :] * (1.0 / den)
        o_rows += [ot[:, g * L:(g + 1) * L] for g in range(GROUP)]
        if h == KV_HEADS - 1:
            ot_all = jnp.concatenate(o_rows, axis=0)
            o_ref[0, u * L:(u + 1) * L, :] = ot_all.T.astype(BF16)
            o_rows = []


def _attention(q, k, vt, rel_bias, sink, cast_jobs):
    bsz, seq, dq = q.shape
    dkv = k.shape[2]
    L = ATT_BLOCK
    nb = seq // L
    U = ATT_STEP_BLOCKS
    ns = nb // U
    smem = pl.BlockSpec(memory_space=pltpu.SMEM)
    prev = lambda i: jnp.maximum(U * i - 1, 0)
    nxt = lambda i: jnp.minimum(U * i + U, nb - 1)
    c_in, c_out, c_shapes, c_args, plan = _cast_io(cast_jobs, bsz * ns, lambda b, i: b * ns + i)
    return pl.pallas_call(
        functools.partial(_att_kernel, cast_plan=plan),
        grid=(bsz, ns),
        in_specs=[
            smem, smem,
            pl.BlockSpec((1, U * L, dq), lambda b, i: (b, i, 0)),
            pl.BlockSpec((1, L, dkv), lambda b, i: (b, prev(i), 0)),
            pl.BlockSpec((1, U * L, dkv), lambda b, i: (b, i, 0)),
            pl.BlockSpec((1, L, dkv), lambda b, i: (b, nxt(i), 0)),
            pl.BlockSpec((1, dkv, L), lambda b, i: (b, 0, prev(i))),
            pl.BlockSpec((1, dkv, U * L), lambda b, i: (b, 0, i)),
            pl.BlockSpec((1, dkv, L), lambda b, i: (b, 0, nxt(i))),
        ] + c_in,
        out_specs=[pl.BlockSpec((1, U * L, dq), lambda b, i: (b, i, 0))] + c_out,
        out_shape=[jax.ShapeDtypeStruct((bsz, seq, dq), BF16)] + c_shapes,
        scratch_shapes=[
            pltpu.VMEM((3, Q_HEADS, 3 * L, L), F32),
            pltpu.VMEM((3, 3 * L, L), F32),
        ],
        compiler_params=_params(2),
        name="swa_attention",
    )(rel_bias, sink, q, k, k, k, vt, vt, vt, *c_args)


def _post_ffn_kernel(*refs, emit_next, cast_plan):
    (a_ref, x_ref, wo_ref, gpost_ref, g1_ref, gpre_ref, sc2_ref, sh2_ref,
     win_hbm, wout_hbm, gfpost_ref, g2_ref) = refs[:12]
    rest = list(refs[12:])
    if emit_next:
        gn_ref, scn_ref, shn_ref = rest[:3]
        rest = rest[3:]
    n_cast_in = _n_cast_inputs(cast_plan)
    cast_in, rest = rest[:n_cast_in], rest[n_cast_in:]
    xo_ref = rest.pop(0)
    if emit_next:
        ho_ref = rest.pop(0)
    cast_out = rest[:len(cast_plan)]
    hb_ref, x1_ref, acc_ref, y_ref, win_ref, wout_ref, w_sem = rest[len(cast_plan):]
    s = pl.program_id(0)
    n = pl.num_programs(0) - 2

    def weight_copies():
        return (pltpu.make_async_copy(win_hbm, win_ref, w_sem.at[0]),
                pltpu.make_async_copy(wout_hbm, wout_ref, w_sem.at[1]))

    @pl.when(s == 0)
    def _():
        for cp in weight_copies():
            cp.start()

    @pl.when(s == 1)
    def _():
        for cp in weight_copies():
            cp.wait()

    _run_casts(s, cast_plan, cast_in, cast_out)
    hidden = wout_ref.shape[0]

    def advance():
        hb_ref[1] = hb_ref[0]
        x1_ref[2] = x1_ref[1]
        x1_ref[1] = x1_ref[0]

    n_chunks = hidden // FFN_CHUNK
    n_slices = 8
    rows_of = lambda r: slice(r * (x_ref.shape[0] // n_slices), (r + 1) * (x_ref.shape[0] // n_slices))

    def a_matmul():
        y_ref[...] = jnp.dot(a_ref[...], wo_ref[...], preferred_element_type=F32)

    def after(v, dep):
        if dep is None:
            return v
        zero = (pltpu.bitcast(dep, jnp.uint32) >> 16) >> 16
        return v * pltpu.bitcast(zero | jnp.uint32(0x3F800000), F32)

    def a_norm(r, dep=None):
        rows = rows_of(r)
        x1 = x_ref[rows, :] + g1_ref[0] * _rms(after(y_ref[rows, :], dep), gpost_ref[...])
        h = _rms(x1, gpre_ref[...]) * (1.0 + sc2_ref[0]) + sh2_ref[0]
        hb_ref[0, rows, :] = h.astype(BF16)
        x1_ref[0, rows, :] = x1

    def b_chunk(c, hb, acc):
        lo = c * FFN_CHUNK
        a = jnp.dot(hb, win_ref[:, lo:lo + FFN_CHUNK], preferred_element_type=F32)
        b = jnp.dot(hb, win_ref[:, hidden + lo:hidden + lo + FFN_CHUNK],
                    preferred_element_type=F32)
        act = (_silu(a) * b).astype(BF16)
        return acc + jnp.dot(act, wout_ref[lo:lo + FFN_CHUNK, :], preferred_element_type=F32)

    def c_slice(r, dep=None):
        rows = rows_of(r)
        x2 = x1_ref[2, rows, :] + g2_ref[0] * _rms(after(acc_ref[rows, :], dep),
                                                   gfpost_ref[...])
        xo_ref[rows, :] = x2
        if emit_next:
            hn = _rms(x2, gn_ref[...]) * (1.0 + scn_ref[0]) + shn_ref[0]
            ho_ref[rows, :] = hn.astype(BF16)

    @pl.when(s == 0)
    def _():
        x1_ref[...] = jnp.zeros(x1_ref.shape, F32)
        acc_ref[...] = jnp.zeros(acc_ref.shape, F32)
        a_matmul()
        for r in range(n_slices):
            a_norm(r)

    @pl.when((s >= 1) & (s <= n))
    def _():
        advance()
        a_matmul()
        hb = hb_ref[1]
        acc = jnp.zeros((hb.shape[0], wout_ref.shape[1]), F32)
        for c in range(n_chunks):
            acc = b_chunk(c, hb, acc)
            dep = acc[0:1, :]
            if c < n_slices:
                c_slice(c, dep)
            if c >= n_chunks - n_slices - 1 and c < n_chunks - 1:
                a_norm(c - (n_chunks - n_slices - 1), dep)
        acc_ref[...] = acc

    @pl.when(s == n + 1)
    def _():
        advance()
        for r in range(n_slices):
            c_slice(r)


def _post_ffn(a2d, x2d, w_o, g_post, g_pre, w_in, w_out, g_fpost, modv, layer,
              seq, n_batch_rows, g_next=None, cast_jobs=()):
    t, d = x2d.shape
    kin = a2d.shape[1]
    tm = ROW_TILE
    tpb = seq // tm
    n = t // tm
    emit_next = g_next is not None
    tile_a = lambda s: jnp.minimum(s, n - 1)
    tile_c = lambda s: jnp.clip(s - 2, 0, n - 1)

    def vec(lyr, slot, tile_of):
        idx = _mod_spec(lyr, slot, tpb, n_batch_rows)
        return pl.BlockSpec((1, 1, d), lambda s: idx(tile_of(s)))
    row = lambda width, tile_of: pl.BlockSpec((tm, width), lambda s: (tile_of(s), 0))
    in_specs = [
        row(kin, tile_a), row(d, tile_a), _resident(w_o.shape), _resident((1, d)),
        vec(layer, 2, tile_a), _resident((1, d)), vec(layer, 4, tile_a), vec(layer, 3, tile_a),
        pl.BlockSpec(memory_space=pl.ANY), pl.BlockSpec(memory_space=pl.ANY),
        _resident((1, d)), vec(layer, 5, tile_c),
    ]
    args = [a2d, x2d, w_o, g_post, modv, g_pre, modv, modv, w_in, w_out, g_fpost, modv]
    out_specs = [row(d, tile_c)]
    out_shape = [jax.ShapeDtypeStruct((t, d), F32)]
    if emit_next:
        in_specs += [_resident((1, d)), vec(layer + 1, 1, tile_c), vec(layer + 1, 0, tile_c)]
        args += [g_next, modv, modv]
        out_specs.append(row(d, tile_c))
        out_shape.append(jax.ShapeDtypeStruct((t, d), BF16))
    c_in, c_out, c_shapes, c_args, plan = _cast_io(cast_jobs, n, lambda s: s)
    return pl.pallas_call(
        functools.partial(_post_ffn_kernel, emit_next=emit_next, cast_plan=plan),
        grid=(n + 2,),
        in_specs=in_specs + c_in,
        out_specs=out_specs + c_out,
        out_shape=out_shape + c_shapes,
        scratch_shapes=[pltpu.VMEM((2, tm, d), BF16), pltpu.VMEM((3, tm, d), F32),
                        pltpu.VMEM((tm, d), F32), pltpu.VMEM((tm, d), F32),
                        pltpu.VMEM(w_in.shape, BF16), pltpu.VMEM(w_out.shape, BF16),
                        pltpu.SemaphoreType.DMA((2,))],
        compiler_params=_params(1),
        name="post_ffn_next" if emit_next else "post_ffn",
    )(*args, *c_args)


def _ret_in_kernel(h_ref, w_ref, cos_ref, sin_ref, zf_ref, zb_ref,
                   q_ref, k_ref, kzf_ref, kzb_ref, v_ref):
    hb = h_ref[...]
    tm = hb.shape[0]
    cos = cos_ref[...]
    sin = sin_ref[...]
    dk = w_ref.shape[0] // RET_HEADS
    half = dk // 2
    nqk = RET_HEADS * dk
    nv = w_ref.shape[1] - 2 * nqk

    def project(col):
        return jnp.dot(hb, w_ref[:, col:col + dk], preferred_element_type=F32)

    def rotated(y, off):
        a = y[:, off:off + half] * cos
        b = y[:, half - off:dk - off] * sin
        return a - b if off == 0 else a + b

    zf = zf_ref[...][None]
    zb = zb_ref[...][None]
    for h in range(RET_HEADS):
        yk = project(nqk + h * dk)
        for off in (0, half):
            lo = h * dk + off
            rk = rotated(yk, off)
            k_ref[:, lo:lo + half] = rk.astype(BF16)
            rk3 = rk.reshape(tm // RET_CHUNK, RET_CHUNK, half)
            kzf_ref[:, lo:lo + half] = (rk3 * zf[:, :, lo:lo + half]).reshape(tm, half).astype(BF16)
            kzb_ref[:, lo:lo + half] = (rk3 * zb[:, :, lo:lo + half]).reshape(tm, half).astype(BF16)
    for h in range(RET_HEADS):
        yq = project(h * dk)
        for off in (0, half):
            lo = h * dk + off
            q_ref[:, lo:lo + half] = rotated(yq, off).astype(BF16)
    for c0 in range(0, nv, dk):
        v_ref[:, c0:c0 + dk] = project(2 * nqk + c0).astype(BF16)


def _ret_in(h2d, w_in, cos, sin, zf, zb, seq):
    t, d = h2d.shape
    tm = PROJ_ROW_TILE
    tpb = seq // tm
    nqk = RET_HEADS * (d // RET_HEADS)
    nv = (w_in.shape[1] - 2 * nqk) // 2
    row = lambda width: pl.BlockSpec((tm, width), lambda i: (i, 0))
    pos = pl.BlockSpec((tm, cos.shape[1]), lambda i: (i % tpb, 0))
    return pl.pallas_call(
        _ret_in_kernel,
        grid=(t // tm,),
        in_specs=[row(d), _resident((d, 2 * nqk + nv)), pos, pos,
                  _resident(zf.shape), _resident(zb.shape)],
        out_specs=[row(nqk), row(nqk), row(nqk), row(nqk), row(nv)],
        out_shape=[jax.ShapeDtypeStruct((t, nqk), BF16)] * 4
        + [jax.ShapeDtypeStruct((t, nv), BF16)],
        compiler_params=_params(1),
        name="ret_in_proj",
    )(h2d, w_in, cos, sin, zf, zb)


def _state_update(st_ref, h, kz_h, v_h, cd):
    upd = lax.dot_general(kz_h, v_h, (((0,), (0,)), ((), ())), preferred_element_type=F32)
    st_ref[h] = st_ref[h] * cd + upd


def _ret_scan_kernel(cdf_ref, cdb_ref, q_ref, k_ref, kzf_ref, kzb_ref, v_ref, h_ref, wg_ref,
                     xif_ref, xib_ref, dm_ref, o_ref, st_ref, snap_ref, sg_ref):
    j = pl.program_id(1)
    ns = pl.num_programs(1) // 2
    L = RET_CHUNK
    n_sub = RET_STEP_CHUNKS

    @pl.when((j == 0) | (j == ns))
    def _():
        st_ref[...] = jnp.zeros(st_ref.shape, F32)

    dk = q_ref.shape[2] // RET_HEADS
    dv = v_ref.shape[2] // RET_HEADS

    @pl.when(j < ns)
    def _():
        for ci in reversed(range(n_sub)):
            rows = slice(ci * L, (ci + 1) * L)
            c = (ns - 1 - j) * n_sub + ci
            for h in range(RET_HEADS):
                snap_ref[c, h] = st_ref[h].astype(BF16)
                _state_update(st_ref, h, kzb_ref[0, rows, h * dk:(h + 1) * dk],
                              v_ref[0, rows, h * dv:(h + 1) * dv], cdb_ref[h])

    @pl.when(j >= ns)
    def _():
        sg_ref[...] = _silu(jnp.dot(h_ref[0], wg_ref[...], preferred_element_type=F32))

        def scores(ci, h):
            rows = slice(ci * L, (ci + 1) * L)
            return lax.dot_general(q_ref[0, rows, h * dk:(h + 1) * dk],
                                   k_ref[0, rows, h * dk:(h + 1) * dk],
                                   (((1,), (1,)), ((), ())), preferred_element_type=F32)

        def finish(ci, h, s):
            rows = slice(ci * L, (ci + 1) * L)
            c = (j - ns) * n_sub + ci
            q_h = q_ref[0, rows, h * dk:(h + 1) * dk]
            v_h = v_ref[0, rows, h * dv:(h + 1) * dv]
            y = jnp.dot((s * dm_ref[h]).astype(BF16), v_h, preferred_element_type=F32)
            y = y + jnp.dot(q_h, st_ref[h].astype(BF16), preferred_element_type=F32) * xif_ref[h]
            y = y + jnp.dot(q_h, snap_ref[c, h], preferred_element_type=F32) * xib_ref[h]
            _state_update(st_ref, h, kzf_ref[0, rows, h * dk:(h + 1) * dk], v_h, cdf_ref[h])
            yn = y * lax.rsqrt(jnp.mean(y * y, axis=-1, keepdims=True) + EPS)
            o_ref[0, rows, h * dv:(h + 1) * dv] = (
                sg_ref[rows, h * dv:(h + 1) * dv] * yn).astype(BF16)

        units = [(ci, h) for ci in range(n_sub) for h in range(RET_HEADS)]
        ahead = scores(*units[0])
        for idx, (ci, h) in enumerate(units):
            s = ahead
            if idx + 1 < len(units):
                ahead = scores(*units[idx + 1])
            finish(ci, h, s)


def _retention(q, k, kzf, kzb, v, h, w_in, xi_f, xi_b, dmat, cd_f, cd_b):
    bsz, seq, nqk = q.shape
    nv = v.shape[2]
    d = h.shape[2]
    gate_col = (2 * nqk + nv) // nv
    L = RET_CHUNK
    rows = RET_STEP_CHUNKS * L
    ns = seq // rows
    dk = nqk // RET_HEADS
    dv = nv // RET_HEADS
    smem = pl.BlockSpec(memory_space=pltpu.SMEM)
    both = lambda j: jnp.where(j < ns, ns - 1 - j, j - ns)
    bwd_only = lambda j: jnp.where(j < ns, ns - 1 - j, 0)
    fwd_only = lambda j: jnp.where(j < ns, 0, j - ns)
    blk = lambda width, at: pl.BlockSpec((1, rows, width), lambda b, j: (b, at(j), 0))
    return pl.pallas_call(
        _ret_scan_kernel,
        grid=(bsz, 2 * ns),
        in_specs=[smem, smem, blk(nqk, fwd_only), blk(nqk, fwd_only), blk(nqk, fwd_only),
                  blk(nqk, bwd_only), blk(nv, both), blk(d, fwd_only),
                  pl.BlockSpec((d, nv), lambda b, j: (0, gate_col),
                               pipeline_mode=pl.Buffered(1)),
                  _resident(xi_f.shape), _resident(xi_b.shape), _resident(dmat.shape)],
        out_specs=blk(nv, fwd_only),
        out_shape=jax.ShapeDtypeStruct((bsz, seq, nv), BF16),
        scratch_shapes=[pltpu.VMEM((RET_HEADS, dk, dv), F32),
                        pltpu.VMEM((seq // L, RET_HEADS, dk, dv), BF16),
                        pltpu.VMEM((rows, nv), F32)],
        compiler_params=_params(2),
        name="ret_scan",
    )(cd_f, cd_b, q, k, kzf, kzb, v, h, w_in, xi_f, xi_b, dmat)


def _decay_tables(decay_fwd, decay_bwd, dk):
    L = RET_CHUNK
    lg_f = jax.nn.log_sigmoid(decay_fwd.astype(F32))
    lg_b = jax.nn.log_sigmoid(decay_bwd.astype(F32))
    idx = jnp.arange(L, dtype=F32)
    diff = idx[:, None] - idx[None, :]
    dm = jnp.where((diff >= 0)[None],
                   jnp.exp(lg_f[:, None, None] * jnp.maximum(diff, 0.0)[None]),
                   jnp.exp(lg_b[:, None, None] * jnp.maximum(-diff, 0.0)[None]))
    xi_f = jnp.exp(lg_f[:, None] * (idx + 1.0)[None])[:, :, None]
    xi_b = jnp.exp(lg_b[:, None] * (L - idx)[None])[:, :, None]
    zeta_f = jnp.exp(lg_f[:, None] * (L - 1.0 - idx)[None])
    zeta_b = jnp.exp(lg_b[:, None] * idx[None])
    zf = jnp.repeat(zeta_f.T, dk, axis=1)
    zb = jnp.repeat(zeta_b.T, dk, axis=1)
    return dm, xi_f, xi_b, zf, zb, jnp.exp(lg_f * L), jnp.exp(lg_b * L)


def _rope_tables(seq, dk):
    inv = ROPE_BASE ** (-jnp.arange(0, dk, 2, dtype=F32) / dk)
    ang = jnp.arange(seq, dtype=F32)[:, None] * inv[None]
    return jnp.cos(ang), jnp.sin(ang)


def kernel(x, c, rel_bias, att_w_qkv, att_w_o, att_sink, ret_w_in, ret_w_o, ret_decay_fwd,
           ret_decay_bwd, ada_w, ada_b, mix_norm_pre, mix_norm_post, ffn_norm_pre,
           ffn_norm_post, ffn_w_in, ffn_w_out):
    bsz, seq, d = x.shape
    t = bsz * seq
    assert seq % ROW_TILE == 0 and ROW_TILE % RET_CHUNK == 0 and d % RET_HEADS == 0
    assert seq % PROJ_ROW_TILE == 0 and PROJ_ROW_TILE % RET_CHUNK == 0
    assert ada_w.shape[0] == 2 and ffn_w_out.shape[1] % FFN_CHUNK == 0
    rows = -(-bsz // 8) * 8
    vec = lambda a: a.reshape(1, d)

    c_pad = jnp.pad(c, ((0, rows - bsz), (0, 0)))
    mod = _ada_mod(c_pad, ada_w, ada_b)
    modv = mod.reshape(2 * rows * 6, 1, d)

    x2d = x.reshape(t, d)

    nq = Q_HEADS * HEAD_DIM
    nkv = KV_HEADS * HEAD_DIM
    heads = [(2 * p + e) * GROUP + g for p in range(KV_HEADS // 2) for g in range(GROUP)
             for e in range(2)]
    w_qkv = att_w_qkv[0].astype(BF16)
    w_q = w_qkv[:, :nq].reshape(d, Q_HEADS, HEAD_DIM)[:, jnp.array(heads)].reshape(d, nq)
    w_q = w_q * jnp.asarray(HEAD_DIM ** -0.5, BF16)
    w_k = w_qkv[:, nq:nq + nkv]
    w_vt = w_qkv[:, nq + nkv:].T
    q, k, vt = _qkv_proj(x2d, vec(mix_norm_pre[0]), modv, w_q, w_k, w_vt, seq, rows)
    att, w_ao, w_f0in, w_f0out = _attention(
        q.reshape(bsz, seq, nq), k.reshape(bsz, seq, nkv), vt, rel_bias.astype(F32),
        att_sink[0].astype(F32), [(att_w_o, 0, None), (ffn_w_in, 0, None), (ffn_w_out, 0, None)])

    dk = d // RET_HEADS
    nqk = RET_HEADS * dk
    kscale = jnp.concatenate([jnp.ones((nqk,), F32), jnp.full((nqk,), dk ** -0.5, F32),
                              jnp.ones((ret_w_in.shape[2] - 2 * nqk,), F32)])[None]
    x1, h1, w_rin, w_ro, w_f1in, w_f1out = _post_ffn(
        att.reshape(t, nq), x2d, w_ao, vec(mix_norm_post[0]), vec(ffn_norm_pre[0]), w_f0in,
        w_f0out, vec(ffn_norm_post[0]), modv, 0, seq, rows, g_next=vec(mix_norm_pre[1]),
        cast_jobs=[(ret_w_in, 0, kscale), (ret_w_o, 0, None), (ffn_w_in, 1, None),
                   (ffn_w_out, 1, None)])
    dm, xi_f, xi_b, zf, zb, cd_f, cd_b = _decay_tables(ret_decay_fwd[0], ret_decay_bwd[0], dk)
    cos, sin = _rope_tables(seq, dk)
    q, k, kzf, kzb, v = _ret_in(h1, w_rin, cos, sin, zf, zb, seq)
    r3 = lambda a: a.reshape(bsz, seq, a.shape[1])
    gated = _retention(r3(q), r3(k), r3(kzf), r3(kzb), r3(v), r3(h1), w_rin, xi_f, xi_b, dm,
                       cd_f, cd_b)
    (x2,) = _post_ffn(gated.reshape(t, -1), x1, w_ro, vec(mix_norm_post[1]),
                      vec(ffn_norm_pre[1]), w_f1in, w_f1out, vec(ffn_norm_post[1]), modv, 1,
                      seq, rows)
    return x2.reshape(bsz, seq, d)
```

```python
import functools
import math

import jax
import jax.numpy as jnp
from jax import lax
from jax.experimental import pallas as pl
from jax.experimental.pallas import tpu as pltpu

F32 = jnp.float32
BF16 = jnp.bfloat16

EPS = 1e-6
NEG = -1e30
LOG2E = math.log2(math.e)

Q_HEADS = 16
KV_HEADS = 4
GROUP = Q_HEADS // KV_HEADS
HEAD_DIM = 64
ATT_BLOCK = 128
ATT_STEP_BLOCKS = 8
REL_BUCKETS = 32
RET_HEADS = 4
RET_CHUNK = 256
RET_STEP_CHUNKS = 2
ROPE_BASE = 10000.0
FFN_CHUNK = 256
ROW_TILE = 512
PROJ_ROW_TILE = 1024
VMEM_LIMIT = 56 * 1024 * 1024


def _silu(x):
    hx = 0.5 * x
    return hx * jnp.tanh(hx) + hx


def _rms(xf, g):
    ms = jnp.mean(xf * xf, axis=-1, keepdims=True)
    return (xf * lax.rsqrt(ms + EPS)) * g


def _resident(shape):
    zeros = (0,) * len(shape)
    return pl.BlockSpec(shape, lambda *_: zeros, pipeline_mode=pl.Buffered(1))


def _params(n_axes, vmem=VMEM_LIMIT):
    return pltpu.CompilerParams(
        dimension_semantics=("arbitrary",) * n_axes, vmem_limit_bytes=vmem)


BF16_SUBLANES = 16


def _cast_plan(rows, n_steps):
    per = BF16_SUBLANES
    while rows % per or rows // per > n_steps:
        per += BF16_SUBLANES
    return per, rows // per


def _cast_io(jobs, n_steps, step_of):
    in_specs, out_specs, out_shapes, args, plan = [], [], [], [], []
    for w, layer, scale in jobs:
        _, rows, cols = w.shape
        per, n_cast = _cast_plan(rows, n_steps)
        blk = lambda *idx, n_cast=n_cast: jnp.minimum(step_of(*idx), n_cast - 1)
        in_specs.append(pl.BlockSpec(
            (1, per, cols), lambda *idx, blk=blk, layer=layer: (layer, blk(*idx), 0)))
        args.append(w)
        if scale is not None:
            in_specs.append(_resident(scale.shape))
            args.append(scale)
        out_specs.append(pl.BlockSpec((per, cols), lambda *idx, blk=blk: (blk(*idx), 0)))
        out_shapes.append(jax.ShapeDtypeStruct((rows, cols), BF16))
        plan.append((n_cast, scale is not None))
    return in_specs, out_specs, out_shapes, args, tuple(plan)


def _run_casts(step, plan, in_refs, out_refs):
    in_refs = list(in_refs)
    for (n_cast, scaled), dst in zip(plan, out_refs):
        src = in_refs.pop(0)
        scale = in_refs.pop(0) if scaled else None

        @pl.when(step < n_cast)
        def _(src=src, scale=scale, dst=dst):
            v = src[0]
            if scale is not None:
                v = v * scale[...]
            dst[...] = v.astype(BF16)


def _n_cast_inputs(plan):
    return sum(2 if scaled else 1 for _, scaled in plan)


def _ada_kernel(c_ref, w_ref, b_ref, o_ref):
    ca = _silu(c_ref[...])
    o_ref[0] = jnp.dot(ca.astype(BF16), w_ref[0].astype(BF16),
                       preferred_element_type=F32) + b_ref[0]


def _ada_mod(c_pad, ada_w, ada_b):
    depth, d, n = ada_w.shape
    rows = c_pad.shape[0]
    tn = 1536
    return pl.pallas_call(
        _ada_kernel,
        grid=(depth, n // tn),
        in_specs=[
            pl.BlockSpec((rows, d), lambda i, j: (0, 0)),
            pl.BlockSpec((1, d, tn), lambda i, j: (i, 0, j)),
            pl.BlockSpec((1, 1, tn), lambda i, j: (i, 0, j)),
        ],
        out_specs=pl.BlockSpec((1, rows, tn), lambda i, j: (i, 0, j)),
        out_shape=jax.ShapeDtypeStruct((depth, rows, n), F32),
        compiler_params=_params(2),
        name="ada_mod",
    )(c_pad, ada_w, ada_b.reshape(depth, 1, n))


def _mod_spec(layer, slot, tiles_per_batch, n_batch_rows):
    base = layer * n_batch_rows * 6 + slot

    def index(t):
        return (base + (t // tiles_per_batch) * 6, 0, 0)
    return index


def _qkv_kernel(x_ref, g_ref, sc_ref, sh_ref, wq_ref, wk_ref, wvt_ref, q_ref, k_ref, vt_ref):
    h = _rms(x_ref[...], g_ref[...]) * (1.0 + sc_ref[0]) + sh_ref[0]
    hb = h.astype(BF16)
    q_ref[...] = jnp.dot(hb, wq_ref[...], preferred_element_type=F32).astype(BF16)
    k_ref[...] = jnp.dot(hb, wk_ref[...], preferred_element_type=F32).astype(BF16)
    vt_ref[0] = lax.dot_general(wvt_ref[...], hb, (((1,), (1,)), ((), ())),
                                preferred_element_type=F32).astype(BF16)


def _qkv_proj(x2d, g_pre, modv, w_q, w_k, w_vt, seq, n_batch_rows):
    t, d = x2d.shape
    nq = w_q.shape[1]
    nkv = w_k.shape[1]
    tm = PROJ_ROW_TILE
    tpb = seq // tm
    vec = lambda idx: pl.BlockSpec((1, 1, d), idx)
    return pl.pallas_call(
        _qkv_kernel,
        grid=(t // tm,),
        in_specs=[
            pl.BlockSpec((tm, d), lambda i: (i, 0)),
            _resident((1, d)),
            vec(_mod_spec(0, 1, tpb, n_batch_rows)),
            vec(_mod_spec(0, 0, tpb, n_batch_rows)),
            _resident(w_q.shape), _resident(w_k.shape), _resident(w_vt.shape),
        ],
        out_specs=[
            pl.BlockSpec((tm, nq), lambda i: (i, 0)),
            pl.BlockSpec((tm, nkv), lambda i: (i, 0)),
            pl.BlockSpec((1, nkv, tm), lambda i: (i // tpb, 0, i % tpb)),
        ],
        out_shape=[
            jax.ShapeDtypeStruct((t, nq), BF16),
            jax.ShapeDtypeStruct((t, nkv), BF16),
            jax.ShapeDtypeStruct((t // seq, nkv, seq), BF16),
        ],
        compiler_params=_params(1),
        name="qkv_proj",
    )(x2d, g_pre, modv, modv, w_q, w_k, w_vt)


def _att_tables(rb_ref, b2_ref, m2_ref):
    L = ATT_BLOCK
    rows = 64
    for r in range(3 * L // rows):
        j = lax.broadcasted_iota(jnp.int32, (rows, L), 0) + r * rows
        t = lax.broadcasted_iota(jnp.int32, (rows, L), 1)
        rel = j - L - t
        n = jnp.abs(rel)
        large = jnp.full((rows, L), 8, jnp.int32)
        for thr in (12, 16, 23, 32, 46, 64, 91):
            large = large + (n >= thr).astype(jnp.int32)
        bucket = jnp.where(rel > 0, 16, 0) + jnp.where(n < 8, n, large)
        in_win = n <= L
        visible = (in_win & (j >= L), in_win, in_win & (j < 2 * L))
        piece = slice(r * rows, (r + 1) * rows)
        for kind in range(3):
            m2_ref[kind, piece, :] = jnp.where(visible[kind], LOG2E, 0.0).astype(F32)

        def body(hq, carry):
            acc = jnp.zeros((rows, L), F32)
            for b in range(REL_BUCKETS):
                acc = jnp.where(bucket == b, rb_ref[b, hq], acc)
            acc = acc * LOG2E
            for kind in range(3):
                b2_ref[kind, hq, piece, :] = jnp.where(visible[kind], acc, NEG)
            return carry
        lax.fori_loop(0, Q_HEADS, body, 0)


def _att_kernel(*refs, cast_plan):
    rb_ref, sink_ref, q_ref, kp_ref, kc_ref, kn_ref, vp_ref, vc_ref, vn_ref = refs[:9]
    n_cast_in = _n_cast_inputs(cast_plan)
    cast_in = refs[9:9 + n_cast_in]
    o_ref = refs[9 + n_cast_in]
    cast_out = refs[10 + n_cast_in:10 + n_cast_in + len(cast_plan)]
    b2_ref, m2_ref = refs[10 + n_cast_in + len(cast_plan):]
    b = pl.program_id(0)
    i = pl.program_id(1)
    ns = pl.num_programs(1)
    _run_casts(b * ns + i, cast_plan, cast_in, cast_out)

    @pl.when((b == 0) & (i == 0))
    def _():
        _att_tables(rb_ref, b2_ref, m2_ref)

    L = ATT_BLOCK
    dh = HEAD_DIM
    U = ATT_STEP_BLOCKS
    q = q_ref[0]
    kb = jnp.concatenate([kp_ref[0], kc_ref[0], kn_ref[0]], axis=0)
    vt = jnp.concatenate([vp_ref[0], vc_ref[0], vn_ref[0]], axis=1)
    lane = lax.broadcasted_iota(jnp.int32, (L, 2 * dh), 1)
    zero = jnp.zeros((L, 2 * dh), BF16)
    ones = jnp.ones((BF16_SUBLANES, 3 * L), BF16)
    kinds = [1] * U
    kinds[0] = jnp.where(i == 0, 0, 1)
    kinds[U - 1] = jnp.where(i == ns - 1, 2, 1)

    def scores(u, h):
        p, e = divmod(h, 2)
        k_pair = kb[u * L:(u + 3) * L, p * 2 * dh:(p + 1) * 2 * dh]
        mine = (lane < dh) if e == 0 else (lane >= dh)
        qz = jnp.concatenate(
            [jnp.where(mine, q[u * L:(u + 1) * L,
                               (p * GROUP + g) * 2 * dh:(p * GROUP + g + 1) * 2 * dh], zero)
             for g in range(GROUP)], axis=0)
        return lax.dot_general(k_pair, qz, (((1,), (1,)), ((), ())),
                               preferred_element_type=F32)

    units = [(u, h) for u in range(U) for h in range(KV_HEADS)]
    o_rows = []
    st_next = scores(*units[0])
    for idx, (u, h) in enumerate(units):
        st = st_next
        if idx + 1 < len(units):
            st_next = scores(*units[idx + 1])
        m2 = m2_ref[kinds[u]]
        es, sinks = [], []
        for g in range(GROUP):
            hq = h * GROUP + g
            sk2 = jnp.full((1, L), sink_ref[hq], F32) * LOG2E
            l2 = st[:, g * L:(g + 1) * L] * m2 + b2_ref[kinds[u], hq]
            m = jnp.maximum(jnp.max(l2, axis=0, keepdims=True), sk2)
            es.append(jnp.exp2(l2 - m).astype(BF16))
            sinks.append(jnp.exp2(sk2 - m))
        et = jnp.concatenate(es, axis=1)
        va = jnp.concatenate([vt[h * dh:(h + 1) * dh, u * L:(u + 3) * L], ones], axis=0)
        ot = jnp.dot(va, et, preferred_element_type=F32)
        den = ot[dh:dh + 1, :] + jnp.concatenate(sinks, axis=1)
        ot = ot[:dh, :] * (1.0 / den)
        o_rows += [ot[:, g * L:(g + 1) * L] for g in range(GROUP)]
        if h == KV_HEADS - 1:
            ot_all = jnp.concatenate(o_rows, axis=0)
            o_ref[0, u * L:(u + 1) * L, :] = ot_all.T.astype(BF16)
            o_rows = []


def _attention(q, k, vt, rel_bias, sink, cast_jobs):
    bsz, seq, dq = q.shape
    dkv = k.shape[2]
    L = ATT_BLOCK
    nb = seq // L
    U = ATT_STEP_BLOCKS
    ns = nb // U
    smem = pl.BlockSpec(memory_space=pltpu.SMEM)
    prev = lambda i: jnp.maximum(U * i - 1, 0)
    nxt = lambda i: jnp.minimum(U * i + U, nb - 1)
    c_in, c_out, c_shapes, c_args, plan = _cast_io(cast_jobs, bsz * ns, lambda b, i: b * ns + i)
    return pl.pallas_call(
        functools.partial(_att_kernel, cast_plan=plan),
        grid=(bsz, ns),
        in_specs=[
            smem, smem,
            pl.BlockSpec((1, U * L, dq), lambda b, i: (b, i, 0)),
            pl.BlockSpec((1, L, dkv), lambda b, i: (b, prev(i), 0)),
            pl.BlockSpec((1, U * L, dkv), lambda b, i: (b, i, 0)),
            pl.BlockSpec((1, L, dkv), lambda b, i: (b, nxt(i), 0)),
            pl.BlockSpec((1, dkv, L), lambda b, i: (b, 0, prev(i))),
            pl.BlockSpec((1, dkv, U * L), lambda b, i: (b, 0, i)),
            pl.BlockSpec((1, dkv, L), lambda b, i: (b, 0, nxt(i))),
        ] + c_in,
        out_specs=[pl.BlockSpec((1, U * L, dq), lambda b, i: (b, i, 0))] + c_out,
        out_shape=[jax.ShapeDtypeStruct((bsz, seq, dq), BF16)] + c_shapes,
        scratch_shapes=[
            pltpu.VMEM((3, Q_HEADS, 3 * L, L), F32),
            pltpu.VMEM((3, 3 * L, L), F32),
        ],
        compiler_params=_params(2),
        name="swa_attention",
    )(rel_bias, sink, q, k, k, k, vt, vt, vt, *c_args)


def _post_ffn_kernel(*refs, emit_next, cast_plan):
    (a_ref, x_ref, wo_ref, gpost_ref, g1_ref, gpre_ref, sc2_ref, sh2_ref,
     win_ref, wout_ref, gfpost_ref, g2_ref) = refs[:12]
    rest = list(refs[12:])
    if emit_next:
        gn_ref, scn_ref, shn_ref = rest[:3]
        rest = rest[3:]
    n_cast_in = _n_cast_inputs(cast_plan)
    cast_in, rest = rest[:n_cast_in], rest[n_cast_in:]
    xo_ref = rest.pop(0)
    if emit_next:
        ho_ref = rest.pop(0)
    cast_out = rest[:len(cast_plan)]
    hb_ref, x1_ref, acc_ref, y_ref = rest[len(cast_plan):]
    s = pl.program_id(0)
    n = pl.num_programs(0) - 2
    _run_casts(s, cast_plan, cast_in, cast_out)
    hidden = wout_ref.shape[0]

    def advance():
        hb_ref[1] = hb_ref[0]
        x1_ref[2] = x1_ref[1]
        x1_ref[1] = x1_ref[0]

    n_chunks = hidden // FFN_CHUNK
    n_slices = 8
    rows_of = lambda r: slice(r * (x_ref.shape[0] // n_slices), (r + 1) * (x_ref.shape[0] // n_slices))

    def a_matmul():
        y_ref[...] = jnp.dot(a_ref[...], wo_ref[...], preferred_element_type=F32)

    def after(v, dep):
        if dep is None:
            return v
        zero = (pltpu.bitcast(dep, jnp.uint32) >> 16) >> 16
        return v * pltpu.bitcast(zero | jnp.uint32(0x3F800000), F32)

    def a_norm(r, dep=None):
        rows = rows_of(r)
        x1 = x_ref[rows, :] + g1_ref[0] * _rms(after(y_ref[rows, :], dep), gpost_ref[...])
        h = _rms(x1, gpre_ref[...]) * (1.0 + sc2_ref[0]) + sh2_ref[0]
        hb_ref[0, rows, :] = h.astype(BF16)
        x1_ref[0, rows, :] = x1

    def b_chunk(c, hb, acc):
        lo = c * FFN_CHUNK
        a = jnp.dot(hb, win_ref[:, lo:lo + FFN_CHUNK], preferred_element_type=F32)
        b = jnp.dot(hb, win_ref[:, hidden + lo:hidden + lo + FFN_CHUNK],
                    preferred_element_type=F32)
        act = (_silu(a) * b).astype(BF16)
        return acc + jnp.dot(act, wout_ref[lo:lo + FFN_CHUNK, :], preferred_element_type=F32)

    def c_slice(r, dep=None):
        rows = rows_of(r)
        x2 = x1_ref[2, rows, :] + g2_ref[0] * _rms(after(acc_ref[rows, :], dep),
                                                   gfpost_ref[...])
        xo_ref[rows, :] = x2
        if emit_next:
            hn = _rms(x2, gn_ref[...]) * (1.0 + scn_ref[0]) + shn_ref[0]
            ho_ref[rows, :] = hn.astype(BF16)

    @pl.when(s == 0)
    def _():
        x1_ref[...] = jnp.zeros(x1_ref.shape, F32)
        acc_ref[...] = jnp.zeros(acc_ref.shape, F32)
        a_matmul()
        for r in range(n_slices):
            a_norm(r)

    @pl.when((s >= 1) & (s <= n))
    def _():
        advance()
        a_matmul()
        hb = hb_ref[1]
        acc = jnp.zeros((hb.shape[0], wout_ref.shape[1]), F32)
        for c in range(n_chunks):
            acc = b_chunk(c, hb, acc)
            dep = acc[0:1, :]
            if c < n_slices:
                c_slice(c, dep)
            if c >= n_chunks - n_slices - 1 and c < n_chunks - 1:
                a_norm(c - (n_chunks - n_slices - 1), dep)
        acc_ref[...] = acc

    @pl.when(s == n + 1)
    def _():
        advance()
        for r in range(n_slices):
            c_slice(r)


def _post_ffn(a2d, x2d, w_o, g_post, g_pre, w_in, w_out, g_fpost, modv, layer,
              seq, n_batch_rows, g_next=None, cast_jobs=()):
    t, d = x2d.shape
    kin = a2d.shape[1]
    tm = ROW_TILE
    tpb = seq // tm
    n = t // tm
    emit_next = g_next is not None
    tile_a = lambda s: jnp.minimum(s, n - 1)
    tile_c = lambda s: jnp.clip(s - 2, 0, n - 1)

    def vec(lyr, slot, tile_of):
        idx = _mod_spec(lyr, slot, tpb, n_batch_rows)
        return pl.BlockSpec((1, 1, d), lambda s: idx(tile_of(s)))
    row = lambda width, tile_of: pl.BlockSpec((tm, width), lambda s: (tile_of(s), 0))
    in_specs = [
        row(kin, tile_a), row(d, tile_a), _resident(w_o.shape), _resident((1, d)),
        vec(layer, 2, tile_a), _resident((1, d)), vec(layer, 4, tile_a), vec(layer, 3, tile_a),
        _resident(w_in.shape), _resident(w_out.shape), _resident((1, d)), vec(layer, 5, tile_c),
    ]
    args = [a2d, x2d, w_o, g_post, modv, g_pre, modv, modv, w_in, w_out, g_fpost, modv]
    out_specs = [row(d, tile_c)]
    out_shape = [jax.ShapeDtypeStruct((t, d), F32)]
    if emit_next:
        in_specs += [_resident((1, d)), vec(layer + 1, 1, tile_c), vec(layer + 1, 0, tile_c)]
        args += [g_next, modv, modv]
        out_specs.append(row(d, tile_c))
        out_shape.append(jax.ShapeDtypeStruct((t, d), BF16))
    c_in, c_out, c_shapes, c_args, plan = _cast_io(cast_jobs, n, lambda s: s)
    return pl.pallas_call(
        functools.partial(_post_ffn_kernel, emit_next=emit_next, cast_plan=plan),
        grid=(n + 2,),
        in_specs=in_specs + c_in,
        out_specs=out_specs + c_out,
        out_shape=out_shape + c_shapes,
        scratch_shapes=[pltpu.VMEM((2, tm, d), BF16), pltpu.VMEM((3, tm, d), F32),
                        pltpu.VMEM((tm, d), F32), pltpu.VMEM((tm, d), F32)],
        compiler_params=_params(1),
        name="post_ffn_next" if emit_next else "post_ffn",
    )(*args, *c_args)


def _ret_in_kernel(h_ref, w_ref, cos_ref, sin_ref, zf_ref, zb_ref,
                   q_ref, k_ref, kzf_ref, kzb_ref, v_ref):
    hb = h_ref[...]
    tm = hb.shape[0]
    cos = cos_ref[...]
    sin = sin_ref[...]
    dk = w_ref.shape[0] // RET_HEADS
    half = dk // 2
    nqk = RET_HEADS * dk
    nv = w_ref.shape[1] - 2 * nqk

    def project(col):
        return jnp.dot(hb, w_ref[:, col:col + dk], preferred_element_type=F32)

    def rotated(y, off):
        a = y[:, off:off + half] * cos
        b = y[:, half - off:dk - off] * sin
        return a - b if off == 0 else a + b

    zf = zf_ref[...][None]
    zb = zb_ref[...][None]
    for h in range(RET_HEADS):
        yk = project(nqk + h * dk)
        for off in (0, half):
            lo = h * dk + off
            rk = rotated(yk, off)
            k_ref[:, lo:lo + half] = rk.astype(BF16)
            rk3 = rk.reshape(tm // RET_CHUNK, RET_CHUNK, half)
            kzf_ref[:, lo:lo + half] = (rk3 * zf[:, :, lo:lo + half]).reshape(tm, half).astype(BF16)
            kzb_ref[:, lo:lo + half] = (rk3 * zb[:, :, lo:lo + half]).reshape(tm, half).astype(BF16)
    for h in range(RET_HEADS):
        yq = project(h * dk)
        for off in (0, half):
            lo = h * dk + off
            q_ref[:, lo:lo + half] = rotated(yq, off).astype(BF16)
    for c0 in range(0, nv, dk):
        v_ref[:, c0:c0 + dk] = project(2 * nqk + c0).astype(BF16)


def _ret_in(h2d, w_in, cos, sin, zf, zb, seq):
    t, d = h2d.shape
    tm = PROJ_ROW_TILE
    tpb = seq // tm
    nqk = RET_HEADS * (d // RET_HEADS)
    nv = (w_in.shape[1] - 2 * nqk) // 2
    row = lambda width: pl.BlockSpec((tm, width), lambda i: (i, 0))
    pos = pl.BlockSpec((tm, cos.shape[1]), lambda i: (i % tpb, 0))
    return pl.pallas_call(
        _ret_in_kernel,
        grid=(t // tm,),
        in_specs=[row(d), _resident((d, 2 * nqk + nv)), pos, pos,
                  _resident(zf.shape), _resident(zb.shape)],
        out_specs=[row(nqk), row(nqk), row(nqk), row(nqk), row(nv)],
        out_shape=[jax.ShapeDtypeStruct((t, nqk), BF16)] * 4
        + [jax.ShapeDtypeStruct((t, nv), BF16)],
        compiler_params=_params(1),
        name="ret_in_proj",
    )(h2d, w_in, cos, sin, zf, zb)


def _state_update(st_ref, h, kz_h, v_h, cd):
    upd = lax.dot_general(kz_h, v_h, (((0,), (0,)), ((), ())), preferred_element_type=F32)
    st_ref[h] = st_ref[h] * cd + upd


def _ret_scan_kernel(cdf_ref, cdb_ref, q_ref, k_ref, kzf_ref, kzb_hbm, v_ref, h_ref, wg_ref,
                     xif_ref, xib_ref, dm_ref, o_ref, st_ref, snap_ref, sg_ref, kz_ring, kz_sem):
    j = pl.program_id(1)
    ns = pl.num_programs(1) // 2
    L = RET_CHUNK
    n_sub = RET_STEP_CHUNKS

    @pl.when((j == 0) | (j == ns))
    def _():
        st_ref[...] = jnp.zeros(st_ref.shape, F32)

    dk = q_ref.shape[2] // RET_HEADS
    dv = v_ref.shape[2] // RET_HEADS

    n_slots = kz_ring.shape[0]
    step_rows = n_sub * L

    def kz_copy(step):
        slot = step % n_slots
        first = (ns - 1 - step) * step_rows
        return pltpu.make_async_copy(kzb_hbm.at[pl.program_id(0), pl.ds(first, step_rows), :],
                                     kz_ring.at[slot], kz_sem.at[slot])

    @pl.when(j == 0)
    def _():
        for ahead in range(n_slots - 1):
            kz_copy(ahead).start()

    @pl.when(j < ns)
    def _():
        @pl.when(j + (n_slots - 1) < ns)
        def _():
            kz_copy(j + (n_slots - 1)).start()

        kz_copy(j).wait()
        slot = j % n_slots
        for ci in reversed(range(n_sub)):
            rows = slice(ci * L, (ci + 1) * L)
            c = (ns - 1 - j) * n_sub + ci
            for h in range(RET_HEADS):
                snap_ref[c, h] = st_ref[h].astype(BF16)
                _state_update(st_ref, h, kz_ring[slot, rows, h * dk:(h + 1) * dk],
                              v_ref[0, rows, h * dv:(h + 1) * dv], cdb_ref[h])

    @pl.when(j >= ns)
    def _():
        sg_ref[...] = _silu(jnp.dot(h_ref[0], wg_ref[...], preferred_element_type=F32))

        def scores(ci, h):
            rows = slice(ci * L, (ci + 1) * L)
            return lax.dot_general(q_ref[0, rows, h * dk:(h + 1) * dk],
                                   k_ref[0, rows, h * dk:(h + 1) * dk],
                                   (((1,), (1,)), ((), ())), preferred_element_type=F32)

        def finish(ci, h, s):
            rows = slice(ci * L, (ci + 1) * L)
            c = (j - ns) * n_sub + ci
            q_h = q_ref[0, rows, h * dk:(h + 1) * dk]
            v_h = v_ref[0, rows, h * dv:(h + 1) * dv]
            y = jnp.dot((s * dm_ref[h]).astype(BF16), v_h, preferred_element_type=F32)
            y = y + jnp.dot(q_h, st_ref[h].astype(BF16), preferred_element_type=F32) * xif_ref[h]
            y = y + jnp.dot(q_h, snap_ref[c, h], preferred_element_type=F32) * xib_ref[h]
            _state_update(st_ref, h, kzf_ref[0, rows, h * dk:(h + 1) * dk], v_h, cdf_ref[h])
            yn = y * lax.rsqrt(jnp.mean(y * y, axis=-1, keepdims=True) + EPS)
            o_ref[0, rows, h * dv:(h + 1) * dv] = (
                sg_ref[rows, h * dv:(h + 1) * dv] * yn).astype(BF16)

        units = [(ci, h) for ci in range(n_sub) for h in range(RET_HEADS)]
        ahead = scores(*units[0])
        for idx, (ci, h) in enumerate(units):
            s = ahead
            if idx + 1 < len(units):
                ahead = scores(*units[idx + 1])
            finish(ci, h, s)


def _retention(q, k, kzf, kzb, v, h, w_in, xi_f, xi_b, dmat, cd_f, cd_b):
    bsz, seq, nqk = q.shape
    nv = v.shape[2]
    d = h.shape[2]
    gate_col = (2 * nqk + nv) // nv
    L = RET_CHUNK
    rows = RET_STEP_CHUNKS * L
    ns = seq // rows
    dk = nqk // RET_HEADS
    dv = nv // RET_HEADS
    smem = pl.BlockSpec(memory_space=pltpu.SMEM)
    assert ns >= 2
    both = lambda j: jnp.where(j < ns, ns - 1 - j, j - ns)
    fwd_only = lambda j: jnp.where(j < ns, 0, j - ns)
    blk = lambda width, at: pl.BlockSpec((1, rows, width), lambda b, j: (b, at(j), 0))
    return pl.pallas_call(
        _ret_scan_kernel,
        grid=(bsz, 2 * ns),
        in_specs=[smem, smem, blk(nqk, fwd_only), blk(nqk, fwd_only), blk(nqk, fwd_only),
                  pl.BlockSpec(memory_space=pl.ANY), blk(nv, both), blk(d, fwd_only),
                  pl.BlockSpec((d, nv), lambda b, j: (0, gate_col),
                               pipeline_mode=pl.Buffered(1)),
                  _resident(xi_f.shape), _resident(xi_b.shape), _resident(dmat.shape)],
        out_specs=blk(nv, fwd_only),
        out_shape=jax.ShapeDtypeStruct((bsz, seq, nv), BF16),
        scratch_shapes=[pltpu.VMEM((RET_HEADS, dk, dv), F32),
                        pltpu.VMEM((seq // L, RET_HEADS, dk, dv), BF16),
                        pltpu.VMEM((rows, nv), F32),
                        pltpu.VMEM((3, rows, nqk), BF16), pltpu.SemaphoreType.DMA((3,))],
        compiler_params=_params(2),
        name="ret_scan",
    )(cd_f, cd_b, q, k, kzf, kzb, v, h, w_in, xi_f, xi_b, dmat)


def _decay_tables(decay_fwd, decay_bwd, dk):
    L = RET_CHUNK
    lg_f = jax.nn.log_sigmoid(decay_fwd.astype(F32))
    lg_b = jax.nn.log_sigmoid(decay_bwd.astype(F32))
    idx = jnp.arange(L, dtype=F32)
    diff = idx[:, None] - idx[None, :]
    dm = jnp.where((diff >= 0)[None],
                   jnp.exp(lg_f[:, None, None] * jnp.maximum(diff, 0.0)[None]),
                   jnp.exp(lg_b[:, None, None] * jnp.maximum(-diff, 0.0)[None]))
    xi_f = jnp.exp(lg_f[:, None] * (idx + 1.0)[None])[:, :, None]
    xi_b = jnp.exp(lg_b[:, None] * (L - idx)[None])[:, :, None]
    zeta_f = jnp.exp(lg_f[:, None] * (L - 1.0 - idx)[None])
    zeta_b = jnp.exp(lg_b[:, None] * idx[None])
    zf = jnp.repeat(zeta_f.T, dk, axis=1)
    zb = jnp.repeat(zeta_b.T, dk, axis=1)
    return dm, xi_f, xi_b, zf, zb, jnp.exp(lg_f * L), jnp.exp(lg_b * L)


def _rope_tables(seq, dk):
    inv = ROPE_BASE ** (-jnp.arange(0, dk, 2, dtype=F32) / dk)
    ang = jnp.arange(seq, dtype=F32)[:, None] * inv[None]
    return jnp.cos(ang), jnp.sin(ang)


def kernel(x, c, rel_bias, att_w_qkv, att_w_o, att_sink, ret_w_in, ret_w_o, ret_decay_fwd,
           ret_decay_bwd, ada_w, ada_b, mix_norm_pre, mix_norm_post, ffn_norm_pre,
           ffn_norm_post, ffn_w_in, ffn_w_out):
    bsz, seq, d = x.shape
    t = bsz * seq
    assert seq % ROW_TILE == 0 and ROW_TILE % RET_CHUNK == 0 and d % RET_HEADS == 0
    assert seq % PROJ_ROW_TILE == 0 and PROJ_ROW_TILE % RET_CHUNK == 0
    assert ada_w.shape[0] == 2 and ffn_w_out.shape[1] % FFN_CHUNK == 0
    rows = -(-bsz // 8) * 8
    vec = lambda a: a.reshape(1, d)

    c_pad = jnp.pad(c, ((0, rows - bsz), (0, 0)))
    mod = _ada_mod(c_pad, ada_w, ada_b)
    modv = mod.reshape(2 * rows * 6, 1, d)

    x2d = x.reshape(t, d)

    nq = Q_HEADS * HEAD_DIM
    nkv = KV_HEADS * HEAD_DIM
    heads = [(2 * p + e) * GROUP + g for p in range(KV_HEADS // 2) for g in range(GROUP)
             for e in range(2)]
    w_qkv = att_w_qkv[0].astype(BF16)
    w_q = w_qkv[:, :nq].reshape(d, Q_HEADS, HEAD_DIM)[:, jnp.array(heads)].reshape(d, nq)
    w_q = w_q * jnp.asarray(HEAD_DIM ** -0.5, BF16)
    w_k = w_qkv[:, nq:nq + nkv]
    w_vt = w_qkv[:, nq + nkv:].T
    q, k, vt = _qkv_proj(x2d, vec(mix_norm_pre[0]), modv, w_q, w_k, w_vt, seq, rows)
    att, w_ao, w_f0in, w_f0out = _attention(
        q.reshape(bsz, seq, nq), k.reshape(bsz, seq, nkv), vt, rel_bias.astype(F32),
        att_sink[0].astype(F32), [(att_w_o, 0, None), (ffn_w_in, 0, None), (ffn_w_out, 0, None)])

    dk = d // RET_HEADS
    nqk = RET_HEADS * dk
    kscale = jnp.concatenate([jnp.ones((nqk,), F32), jnp.full((nqk,), dk ** -0.5, F32),
                              jnp.ones((ret_w_in.shape[2] - 2 * nqk,), F32)])[None]
    x1, h1, w_rin, w_ro, w_f1in, w_f1out = _post_ffn(
        att.reshape(t, nq), x2d, w_ao, vec(mix_norm_post[0]), vec(ffn_norm_pre[0]), w_f0in,
        w_f0out, vec(ffn_norm_post[0]), modv, 0, seq, rows, g_next=vec(mix_norm_pre[1]),
        cast_jobs=[(ret_w_in, 0, kscale), (ret_w_o, 0, None), (ffn_w_in, 1, None),
                   (ffn_w_out, 1, None)])
    dm, xi_f, xi_b, zf, zb, cd_f, cd_b = _decay_tables(ret_decay_fwd[0], ret_decay_bwd[0], dk)
    cos, sin = _rope_tables(seq, dk)
    q, k, kzf, kzb, v = _ret_in(h1, w_rin, cos, sin, zf, zb, seq)
    r3 = lambda a: a.reshape(bsz, seq, a.shape[1])
    gated = _retention(r3(q), r3(k), r3(kzf), r3(kzb), r3(v), r3(h1), w_rin, xi_f, xi_b, dm,
                       cd_f, cd_b)
    (x2,) = _post_ffn(gated.reshape(t, -1), x1, w_ro, vec(mix_norm_post[1]),
                      vec(ffn_norm_pre[1]), w_f1in, w_f1out, vec(ffn_norm_post[1]), modv, 1,
                      seq, rows)
    return x2.reshape(bsz, seq, d)
```
